```python
import jax
import jax.numpy as jnp
from jax import lax
import numpy as np

D_MODEL = 2048
BATCH = 4
SEQ = 4096
DEPTH = 2

D_FF = ((8 * D_MODEL // 3 + 255) // 256) * 256
EPS = 1e-6
POOL_WIDTH = D_MODEL // 4
POOL_WINDOWS = (2, 4, 8, 16)
POOL_GROUP = POOL_WIDTH // len(POOL_WINDOWS)
GLA_HEADS = 4
GLA_WIDTH = D_MODEL // 2
GLA_DK = GLA_WIDTH // 2 // GLA_HEADS
GLA_DV = GLA_WIDTH // GLA_HEADS
GLA_GATE_RANK = 16
GLA_TAU = 16.0
GLA_CHUNK = 64
SSM_WIDTH = D_MODEL // 4
SSM_GROUP = 16
SSM_GROUPS = SSM_WIDTH // SSM_GROUP
SSM_STATE = 64
DT_MIN = 1e-3
DT_MAX = 1e-1
N_BRANCH = 3
MIX_WIDTH = POOL_WIDTH + GLA_WIDTH + SSM_WIDTH
IN_SIZES = (POOL_WIDTH, GLA_HEADS * GLA_DK, GLA_HEADS * GLA_DK, GLA_WIDTH, GLA_WIDTH,
            GLA_GATE_RANK, SSM_WIDTH, N_BRANCH * D_MODEL)
IN_WIDTH = sum(IN_SIZES)

kernel_name = "hybrid_pool_gla_s5_macaron"


def rms_norm(x, g):
    xf = x.astype(jnp.float32)
    y = xf * lax.rsqrt(jnp.mean(xf * xf, axis=-1, keepdims=True) + EPS)
    return (y * g.astype(jnp.float32)).astype(x.dtype)


def swiglu(x, w_gate, w_up, w_down):
    return (jax.nn.silu(x @ w_gate) * (x @ w_up)) @ w_down


def split_columns(z):
    idx = np.cumsum(IN_SIZES)[:-1].tolist()
    return jnp.split(z, idx, axis=-1)


def pool_mixer(x, w_group, scale):
    B, S, _ = x.shape
    xf = x.astype(jnp.float32).reshape(B, S, len(POOL_WINDOWS), POOL_GROUP)
    csum = jnp.cumsum(xf, axis=1)
    pos = jnp.arange(1, S + 1, dtype=jnp.float32)
    means = []
    for i, w in enumerate(POOL_WINDOWS):
        c = csum[:, :, i]
        lagged = jnp.pad(c, ((0, 0), (w, 0), (0, 0)))[:, :S]
        means.append((c - lagged) / jnp.minimum(pos, float(w))[:, None])
    mean = jnp.stack(means, axis=2)
    mixed = jnp.einsum('bsgc,gcd->bsgd', mean - xf, w_group.astype(jnp.float32))
    return (mixed.reshape(B, S, POOL_WIDTH) * scale.astype(jnp.float32)).astype(x.dtype)


def gla_mixer(q, k, v, r, gate_lr, w_gate2, gate_bias, norm_gain):
    B, S, _ = q.shape
    H, C = GLA_HEADS, GLA_CHUNK
    N = S // C
    f32 = jnp.float32
    log_a = jax.nn.log_sigmoid((gate_lr @ w_gate2 + gate_bias).astype(f32)) / GLA_TAU

    def chunks(t, d):
        return t.reshape(B, N, C, H, d).transpose(0, 3, 1, 2, 4)

    qc = chunks(q.astype(f32), GLA_DK) * (GLA_DK ** -0.5)
    kc = chunks(k.astype(f32), GLA_DK)
    vc = chunks(v.astype(f32), GLA_DV)
    bc = jnp.cumsum(chunks(log_a, GLA_DK), axis=3)
    b_last = bc[:, :, :, C - 1:C, :]
    b_mid = bc[:, :, :, C // 2 - 1:C // 2, :]
    causal = jnp.tril(jnp.ones((C, C), dtype=bool))
    scores = jnp.einsum('bhnid,bhnjd->bhnij', qc * jnp.exp(bc - b_mid), kc * jnp.exp(b_mid - bc))
    scores = jnp.where(causal, scores, 0.0)
    o_intra = jnp.einsum('bhnij,bhnjv->bhniv', scores, vc)
    q_dec = qc * jnp.exp(bc)
    kv = jnp.einsum('bhncd,bhncv->nbhdv', kc * jnp.exp(b_last - bc), vc)
    chunk_decay = jnp.exp(b_last[:, :, :, 0, :]).transpose(2, 0, 1, 3)

    def step(state, inp):
        dec, kv_n = inp
        return dec[..., None] * state + kv_n, state

    _, states = lax.scan(step, jnp.zeros((B, H, GLA_DK, GLA_DV), f32), (chunk_decay, kv))
    o_inter = jnp.einsum('bhncd,nbhdv->bhncv', q_dec, states)
    o = (o_intra + o_inter).transpose(0, 2, 3, 1, 4).reshape(B, S, H, GLA_DV)
    o = o * lax.rsqrt(jnp.mean(o * o, axis=-1, keepdims=True) + EPS)
    o = o.reshape(B, S, GLA_WIDTH) * norm_gain.astype(f32)
    return (o * jax.nn.silu(r.astype(f32))).astype(q.dtype)


def s5_mixer(x, a_re, a_im, log_dt, b_re, b_im, c_re, c_im, d, w_glu):
    B, S, _ = x.shape
    f32 = jnp.float32
    a_re, a_im = a_re.astype(f32), a_im.astype(f32)
    b_re, b_im = b_re.astype(f32), b_im.astype(f32)
    xf = x.astype(f32)
    xg = xf.reshape(B, S, SSM_GROUPS, SSM_GROUP)
    dt = jnp.exp(log_dt.astype(f32))[:, None]
    mag = jnp.exp(dt * a_re)
    ab_re = mag * jnp.cos(dt * a_im)
    ab_im = mag * jnp.sin(dt * a_im)
    den = a_re * a_re + a_im * a_im
    f_re = ((ab_re - 1.0) * a_re + ab_im * a_im) / den
    f_im = (ab_im * a_re - (ab_re - 1.0) * a_im) / den
    bb_re = f_re[..., None] * b_re - f_im[..., None] * b_im
    bb_im = f_re[..., None] * b_im + f_im[..., None] * b_re
    u_re = jnp.einsum('bsgh,gph->bsgp', xg, bb_re)
    u_im = jnp.einsum('bsgh,gph->bsgp', xg, bb_im)

    def combine(e1, e2):
        a1r, a1i, h1r, h1i = e1
        a2r, a2i, h2r, h2i = e2
        return (a2r * a1r - a2i * a1i, a2r * a1i + a2i * a1r,
                a2r * h1r - a2i * h1i + h2r, a2r * h1i + a2i * h1r + h2i)

    def scan_one(ur, ui):
        ar = jnp.broadcast_to(ab_re, ur.shape)
        ai = jnp.broadcast_to(ab_im, ur.shape)
        _, _, hr, hi = lax.associative_scan(combine, (ar, ai, ur, ui), axis=0)
        return hr, hi

    h_re, h_im = jax.vmap(scan_one)(u_re, u_im)
    y = (jnp.einsum('bsgp,ghp->bsgh', h_re, c_re.astype(f32))
         - jnp.einsum('bsgp,ghp->bsgh', h_im, c_im.astype(f32)))
    y = y.reshape(B, S, SSM_WIDTH) + d.astype(f32) * xf
    y = jax.nn.gelu(y)
    y = y * jax.nn.sigmoid(y @ w_glu.astype(f32))
    return y.astype(x.dtype)


def _normal(key, shape, std):
    return jax.random.normal(key, shape, jnp.float32) * std


def setup_inputs(seed: int = 0) -> dict:
    key = jax.random.key(seed)
    k = jax.random.split(key, 32)
    L, D, F = DEPTH, D_MODEL, D_FF
    G, P, HG = SSM_GROUPS, SSM_STATE, SSM_GROUP
    nq = len(POOL_WINDOWS)
    a_im = jnp.pi * jnp.arange(P, dtype=jnp.float32)
    return {
        'x': _normal(k[0], (BATCH, SEQ, D), 1.0),
        'ffn1_norm': 1.0 + _normal(k[1], (L, D), 0.02),
        'ffn1_w_gate': _normal(k[2], (L, D, F), D ** -0.5),
        'ffn1_w_up': _normal(k[3], (L, D, F), D ** -0.5),
        'ffn1_w_down': _normal(k[4], (L, F, D), F ** -0.5),
        'mix_norm': 1.0 + _normal(k[5], (L, D), 0.02),
        'w_in': _normal(k[6], (L, D, IN_WIDTH), D ** -0.5),
        'pool_w': _normal(k[7], (L, nq, POOL_GROUP, POOL_GROUP), POOL_GROUP ** -0.5),
        'pool_scale': 1.0 + _normal(k[8], (L, POOL_WIDTH), 0.1),
        'gla_w_gate2': _normal(k[9], (L, GLA_GATE_RANK, GLA_HEADS * GLA_DK), GLA_GATE_RANK ** -0.5),
        'gla_gate_bias': _normal(k[10], (L, GLA_HEADS * GLA_DK), 0.01),
        'gla_norm': 1.0 + _normal(k[11], (L, GLA_WIDTH), 0.02),
        'ssm_a_re': -0.5 + _normal(k[12], (L, G, P), 0.01),
        'ssm_a_im': a_im + _normal(k[13], (L, G, P), 0.01),
        'ssm_log_dt': jax.random.uniform(k[14], (L, G), jnp.float32,
                                         float(np.log(DT_MIN)), float(np.log(DT_MAX))),
        'ssm_b_re': _normal(k[15], (L, G, P, HG), (2 * HG) ** -0.5),
        'ssm_b_im': _normal(k[16], (L, G, P, HG), (2 * HG) ** -0.5),
        'ssm_c_re': _normal(k[17], (L, G, HG, P), (2 * P) ** -0.5),
        'ssm_c_im': _normal(k[18], (L, G, HG, P), (2 * P) ** -0.5),
        'ssm_d': _normal(k[19], (L, SSM_WIDTH), 1.0),
        'ssm_w_glu': _normal(k[20], (L, SSM_WIDTH, SSM_WIDTH), SSM_WIDTH ** -0.5),
        'w_branch': _normal(k[21], (L, MIX_WIDTH, D), (MIX_WIDTH // 2) ** -0.5),
        'w_out': _normal(k[22], (L, D, D), D ** -0.5),
        'ffn2_norm': 1.0 + _normal(k[23], (L, D), 0.02),
        'ffn2_w_gate': _normal(k[24], (L, D, F), D ** -0.5),
        'ffn2_w_up': _normal(k[25], (L, D, F), D ** -0.5),
        'ffn2_w_down': _normal(k[26], (L, F, D), F ** -0.5),
        'final_norm': 1.0 + _normal(k[27], (D,), 0.02),
    }


def reference(x, ffn1_norm, ffn1_w_gate, ffn1_w_up, ffn1_w_down, mix_norm, w_in,
              pool_w, pool_scale, gla_w_gate2, gla_gate_bias, gla_norm,
              ssm_a_re, ssm_a_im, ssm_log_dt, ssm_b_re, ssm_b_im, ssm_c_re, ssm_c_im,
              ssm_d, ssm_w_glu, w_branch, w_out,
              ffn2_norm, ffn2_w_gate, ffn2_w_up, ffn2_w_down, final_norm):
    B, S, D = x.shape
    p0, p1 = POOL_WIDTH, POOL_WIDTH + GLA_WIDTH
    for l in range(DEPTH):
        x = x + 0.5 * swiglu(rms_norm(x, ffn1_norm[l]), ffn1_w_gate[l], ffn1_w_up[l], ffn1_w_down[l])
        u = rms_norm(x, mix_norm[l])
        z = u @ w_in[l]
        p_in, q, k, v, r, g_lr, s_in, gate_logits = split_columns(z)
        y_pool = pool_mixer(p_in, pool_w[l], pool_scale[l])
        y_gla = gla_mixer(q, k, v, r, g_lr, gla_w_gate2[l], gla_gate_bias[l], gla_norm[l])
        y_ssm = s5_mixer(s_in, ssm_a_re[l], ssm_a_im[l], ssm_log_dt[l], ssm_b_re[l], ssm_b_im[l],
                         ssm_c_re[l], ssm_c_im[l], ssm_d[l], ssm_w_glu[l])
        gates = jax.nn.sigmoid(gate_logits.astype(jnp.float32)).astype(x.dtype).reshape(B, S, N_BRANCH, D)
        wb = w_branch[l]
        merged = (gates[:, :, 0] * (y_pool @ wb[:p0])
                  + gates[:, :, 1] * (y_gla @ wb[p0:p1])
                  + gates[:, :, 2] * (y_ssm @ wb[p1:]))
        x = x + merged @ w_out[l]
        x = x + 0.5 * swiglu(rms_norm(x, ffn2_norm[l]), ffn2_w_gate[l], ffn2_w_up[l], ffn2_w_down[l])
    return rms_norm(x, final_norm)
```

```python
import functools

import jax
import jax.numpy as jnp
import numpy as np
from jax import lax
from jax.experimental import pallas as pl
from jax.experimental.pallas import tpu as pltpu

F32 = jnp.float32
BF16 = jnp.bfloat16

EPS = 1e-6
POOL_WINDOWS = (2, 4, 8, 16)
POOL_GROUP = 128
POOL_HALO = 16
GLA_HEADS = 4
GLA_DK = 128
GLA_DV = 256
GLA_RANK = 16
GLA_TAU = 16.0
GLA_CHUNK = 64
SSM_GROUP = 16
SSM_STATE = 64
SSM_CHUNK = 16
LANES = 128
FFN_DOWN_CHUNK = 512
FFN_ROW_CHUNK = 256

VMEM_LIMIT = 60 * 1024 * 1024


def _cparams(sem):
    return pltpu.CompilerParams(dimension_semantics=sem, vmem_limit_bytes=VMEM_LIMIT)


def _rms(xf, g):
    return xf * lax.rsqrt(jnp.mean(xf * xf, axis=-1, keepdims=True) + EPS) * g


def _dot(a, b):
    return jnp.dot(a, b, preferred_element_type=F32)


def _ffn_body(x_ref, g_ref, wg_ref, wu_ref, wd_ref, *rest, final):
    if final:
        fg_ref, o_ref, xn_ref = rest
    else:
        o_ref, xn_ref = rest
    j = pl.program_id(1)

    row_chunks = [slice(r0, r0 + FFN_ROW_CHUNK) for r0 in range(0, o_ref.shape[0], FFN_ROW_CHUNK)]

    @pl.when(j == 0)
    def _():
        for rs in row_chunks:
            xn_ref[rs, :] = _rms(x_ref[rs, :], g_ref[...]).astype(BF16)
            o_ref[rs, :] = jnp.zeros((FFN_ROW_CHUNK, o_ref.shape[1]), F32)

    for rs in row_chunks:
        xn = xn_ref[rs, :]
        a = _dot(xn, wg_ref[...])
        b = _dot(xn, wu_ref[...])
        h = (jax.nn.silu(a) * b).astype(BF16)
        for c0 in range(0, o_ref.shape[1], FFN_DOWN_CHUNK):
            cs = slice(c0, c0 + FFN_DOWN_CHUNK)
            o_ref[rs, cs] += _dot(h, wd_ref[:, cs])

    @pl.when(j == pl.num_programs(1) - 1)
    def _():
        def epilogue(r, carry):
            rs = pl.ds(pl.multiple_of(r * FFN_ROW_CHUNK, FFN_ROW_CHUNK), FFN_ROW_CHUNK)
            y = x_ref[rs, :] + 0.5 * o_ref[rs, :]
            if final:
                y = _rms(y, fg_ref[...])
            o_ref[rs, :] = y
            return carry
        lax.fori_loop(0, len(row_chunks), epilogue, 0)


def _ffn(x, g, wg, wu, wd, final_g=None, *, tm=1024, tf=512):
    m, d = x.shape
    f = wg.shape[1]
    final = final_g is not None
    in_specs = [
        pl.BlockSpec((tm, d), lambda i, j: (i, 0)),
        pl.BlockSpec((1, d), lambda i, j: (0, 0)),
        pl.BlockSpec((d, tf), lambda i, j: (0, j)),
        pl.BlockSpec((d, tf), lambda i, j: (0, j)),
        pl.BlockSpec((tf, d), lambda i, j: (j, 0)),
    ]
    args = [x, g.reshape(1, d), wg, wu, wd]
    if final:
        in_specs.append(pl.BlockSpec((1, d), lambda i, j: (0, 0)))
        args.append(final_g.reshape(1, d))
    return pl.pallas_call(
        functools.partial(_ffn_body, final=final),
        grid=(m // tm, f // tf),
        in_specs=in_specs,
        out_specs=pl.BlockSpec((tm, d), lambda i, j: (i, 0)),
        out_shape=jax.ShapeDtypeStruct((m, d), F32),
        scratch_shapes=[pltpu.VMEM((tm, d), BF16)],
        compiler_params=_cparams(("parallel", "arbitrary")),
        name="ffn_final" if final else "ffn",
    )(*args)


def _inproj_body(x_ref, g_ref, w_ref, wl_ref, z_ref, glr_ref, un_ref, *, gate_block0):
    j = pl.program_id(1)

    @pl.when(j == 0)
    def _():
        un = _rms(x_ref[...], g_ref[...]).astype(BF16)
        un_ref[...] = un
        glr_ref[...] = _dot(un, wl_ref[...])

    acc = _dot(un_ref[...], w_ref[...])

    @pl.when(j < gate_block0)
    def _():
        z_ref[...] = acc.astype(BF16)

    @pl.when(j >= gate_block0)
    def _():
        z_ref[...] = jax.nn.sigmoid(acc).astype(BF16)


def _inproj(x, g, w_main, w_glr, gate_col0, *, tm=1024, tn=1024):
    m, d = x.shape
    n = w_main.shape[1]
    return pl.pallas_call(
        functools.partial(_inproj_body, gate_block0=gate_col0 // tn),
        grid=(m // tm, n // tn),
        in_specs=[
            pl.BlockSpec((tm, d), lambda i, j: (i, 0)),
            pl.BlockSpec((1, d), lambda i, j: (0, 0)),
            pl.BlockSpec((d, tn), lambda i, j: (0, j)),
            pl.BlockSpec((d, LANES), lambda i, j: (0, 0)),
        ],
        out_specs=[
            pl.BlockSpec((tm, tn), lambda i, j: (i, j)),
            pl.BlockSpec((tm, LANES), lambda i, j: (i, 0)),
        ],
        out_shape=[
            jax.ShapeDtypeStruct((m, n), BF16),
            jax.ShapeDtypeStruct((m, LANES), F32),
        ],
        scratch_shapes=[pltpu.VMEM((tm, d), BF16)],
        compiler_params=_cparams(("parallel", "arbitrary")),
        name="inproj",
    )(x, g.reshape(1, d), w_main, w_glr)


def _pool_body(p_ref, halo_ref, w_ref, sc_ref, o_ref, *, ts):
    t = pl.program_id(1)
    x = p_ref[...].astype(F32)
    halo = jnp.where(t > 0, halo_ref[...].astype(F32), 0.0)
    e = jnp.concatenate([halo, x], axis=0)
    s2 = e + pltpu.roll(e, 1, 0)
    s4 = s2 + pltpu.roll(s2, 2, 0)
    s8 = s4 + pltpu.roll(s4, 4, 0)
    s16 = s8 + pltpu.roll(s8, 8, 0)
    pos = (t * ts + 1 + lax.broadcasted_iota(jnp.int32, (ts, 1), 0)).astype(F32)
    outs = []
    for gi, (w, s) in enumerate(zip(POOL_WINDOWS, (s2, s4, s8, s16))):
        cols = slice(gi * POOL_GROUP, (gi + 1) * POOL_GROUP)
        mean = s[POOL_HALO:, cols] / jnp.minimum(pos, float(w))
        outs.append(_dot((mean - x[:, cols]).astype(BF16), w_ref[gi]))
    o_ref[...] = (jnp.concatenate(outs, axis=1) * sc_ref[...]).astype(BF16)


def _pool(z, w, scale, batch, seq, *, ts=512):
    width = len(POOL_WINDOWS) * POOL_GROUP
    nt = seq // ts
    hb = ts // POOL_HALO
    return pl.pallas_call(
        functools.partial(_pool_body, ts=ts),
        grid=(batch, nt),
        in_specs=[
            pl.BlockSpec((ts, width), lambda b, t: (b * nt + t, 0)),
            pl.BlockSpec((POOL_HALO, width),
                         lambda b, t: (jnp.maximum((b * nt + t) * hb - 1, 0), 0)),
            pl.BlockSpec((len(POOL_WINDOWS), POOL_GROUP, POOL_GROUP), lambda b, t: (0, 0, 0)),
            pl.BlockSpec((1, width), lambda b, t: (0, 0)),
        ],
        out_specs=pl.BlockSpec((ts, width), lambda b, t: (b * nt + t, 0)),
        out_shape=jax.ShapeDtypeStruct((batch * seq, width), BF16),
        compiler_params=_cparams(("parallel", "arbitrary")),
        name="pool",
    )(z, z, w, scale.reshape(1, width))


def _gla_body(q_ref, k_ref, v_ref, r_ref, glr_ref, w2_ref, b_ref, gain_ref, o_ref, st_ref, *, ts):
    c = GLA_CHUNK
    nc = ts // c
    t = pl.program_id(2)

    @pl.when(t == 0)
    def _():
        st_ref[...] = jnp.zeros_like(st_ref)

    logits = _dot(glr_ref[...].astype(BF16), w2_ref[0]) + b_ref[0]
    log_a = (jnp.minimum(logits, 0.0) - jnp.log1p(jnp.exp(-jnp.abs(logits)))) * (1.0 / GLA_TAU)
    rowc = lax.broadcasted_iota(jnp.int32, (ts, 1), 0) % c
    bc = log_a
    k = 1
    while k < c:
        bc = bc + jnp.where(rowc >= k, pltpu.roll(bc, k, 0), 0.0)
        k *= 2
    bc3 = bc.reshape(nc, c, GLA_DK)
    b_last = bc3[:, c - 1:c, :]
    b_mid = bc3[:, c // 2 - 1:c // 2, :]
    q3 = q_ref[...].astype(F32).reshape(nc, c, GLA_DK) * (GLA_DK ** -0.5)
    k3 = k_ref[...].astype(F32).reshape(nc, c, GLA_DK)
    v3 = v_ref[...].reshape(nc, c, GLA_DV)
    qe = (q3 * jnp.exp(bc3 - b_mid)).astype(BF16)
    ke = (k3 * jnp.exp(b_mid - bc3)).astype(BF16)
    scores = jnp.einsum('nid,njd->nij', qe, ke, preferred_element_type=F32)
    causal = (lax.broadcasted_iota(jnp.int32, (c, c), 0)
              >= lax.broadcasted_iota(jnp.int32, (c, c), 1))
    scores = jnp.where(causal[None], scores, 0.0).astype(BF16)
    o_intra = jnp.einsum('nij,njv->niv', scores, v3, preferred_element_type=F32)
    qd = (q3 * jnp.exp(bc3)).astype(BF16)
    kd = (k3 * jnp.exp(b_last - bc3)).astype(BF16)
    dec = jnp.exp(b_last)
    st = st_ref[...]
    outs = []
    for n in range(nc):
        o_inter = lax.dot_general(qd[n], st.astype(BF16), (((1,), (1,)), ((), ())),
                                  preferred_element_type=F32)
        outs.append(o_intra[n] + o_inter)
        kv_t = lax.dot_general(v3[n], kd[n], (((0,), (0,)), ((), ())),
                               preferred_element_type=F32)
        st = dec[n] * st + kv_t
    st_ref[...] = st
    o = jnp.concatenate(outs, axis=0)
    o = o * lax.rsqrt(jnp.mean(o * o, axis=-1, keepdims=True) + EPS) * gain_ref[0]
    o_ref[...] = (o * jax.nn.silu(r_ref[...].astype(F32))).astype(BF16)


def _gla(z, glr, w2, bias, gain, batch, seq, cols, *, ts=512):
    h = GLA_HEADS
    nt = seq // ts
    q0, k0, v0, r0 = cols
    row = lambda b, hh, t: b * nt + t
    return pl.pallas_call(
        functools.partial(_gla_body, ts=ts),
        grid=(batch, h, nt),
        in_specs=[
            pl.BlockSpec((ts, GLA_DK), lambda b, hh, t: (row(b, hh, t), q0 // GLA_DK + hh)),
            pl.BlockSpec((ts, GLA_DK), lambda b, hh, t: (row(b, hh, t), k0 // GLA_DK + hh)),
            pl.BlockSpec((ts, GLA_DV), lambda b, hh, t: (row(b, hh, t), v0 // GLA_DV + hh)),
            pl.BlockSpec((ts, GLA_DV), lambda b, hh, t: (row(b, hh, t), r0 // GLA_DV + hh)),
            pl.BlockSpec((ts, LANES), lambda b, hh, t: (row(b, hh, t), 0)),
            pl.BlockSpec((1, LANES, GLA_DK), lambda b, hh, t: (hh, 0, 0)),
            pl.BlockSpec((1, 1, GLA_DK), lambda b, hh, t: (hh, 0, 0)),
            pl.BlockSpec((1, 1, GLA_DV), lambda b, hh, t: (hh, 0, 0)),
        ],
        out_specs=pl.BlockSpec((ts, GLA_DV), lambda b, hh, t: (row(b, hh, t), hh)),
        out_shape=jax.ShapeDtypeStruct((batch * seq, h * GLA_DV), BF16),
        scratch_shapes=[pltpu.VMEM((GLA_DV, GLA_DK), F32)],
        compiler_params=_cparams(("parallel", "parallel", "arbitrary")),
        name="gla",
    )(z, z, z, z, glr, w2, bias, gain)


def _ssm_body(x_ref, m_ref, p_ref, q_ref, ar_ref, ai_ref, o_ref, *, gb, chunks_per_seq):
    rows = x_ref.shape[1]
    rowc = lax.broadcasted_iota(jnp.int32, (rows, 1), 0) % chunks_per_seq
    for g in range(gb):
        x = x_ref[g]
        hst = _dot(x, p_ref[g])
        k, si = 1, 0
        while k < chunks_per_seq:
            hs = jnp.where(rowc >= k, pltpu.roll(hst, k, 0), 0.0)
            hst = hst + ar_ref[g, si:si + 1, :] * hs + ai_ref[g, si:si + 1, :] * pltpu.roll(hs, SSM_STATE, 1)
            k *= 2
            si += 1
        hprev = jnp.where(rowc >= 1, pltpu.roll(hst, 1, 0), 0.0)
        y = _dot(x, m_ref[g]) + _dot(hprev.astype(BF16), q_ref[g])
        o_ref[g] = y.astype(BF16)


def _ssm_core(xg, mg, pg, qg, ar, ai, chunks_per_seq, *, gb=4):
    g, rows, w = xg.shape
    ns = 2 * SSM_STATE
    nsteps = ar.shape[1]
    return pl.pallas_call(
        functools.partial(_ssm_body, gb=gb, chunks_per_seq=chunks_per_seq),
        grid=(g // gb,),
        in_specs=[
            pl.BlockSpec((gb, rows, w), lambda i: (i, 0, 0)),
            pl.BlockSpec((gb, w, w), lambda i: (i, 0, 0)),
            pl.BlockSpec((gb, w, ns), lambda i: (i, 0, 0)),
            pl.BlockSpec((gb, ns, w), lambda i: (i, 0, 0)),
            pl.BlockSpec((gb, nsteps, ns), lambda i: (i, 0, 0)),
            pl.BlockSpec((gb, nsteps, ns), lambda i: (i, 0, 0)),
        ],
        out_specs=pl.BlockSpec((gb, rows, w), lambda i: (i, 0, 0)),
        out_shape=jax.ShapeDtypeStruct((g, rows, w), BF16),
        compiler_params=_cparams(("parallel",)),
        name="ssm_core",
    )(xg, mg, pg, qg, ar, ai)


def _ssm_operators(a_re, a_im, log_dt, b_re, b_im, c_re, c_im, chunks_per_seq):
    t = SSM_CHUNK
    g, p = a_re.shape
    hg = b_re.shape[-1]
    dt = jnp.exp(log_dt)[:, None]
    lam_re, lam_im = dt * a_re, dt * a_im

    def powers(n):
        n = jnp.asarray(n, F32)[:, None, None]
        mag = jnp.exp(n * lam_re)
        return mag * jnp.cos(n * lam_im), mag * jnp.sin(n * lam_im)

    ab_re, ab_im = powers([1.0])
    ab_re, ab_im = ab_re[0], ab_im[0]
    den = a_re * a_re + a_im * a_im
    f_re = ((ab_re - 1.0) * a_re + ab_im * a_im) / den
    f_im = (ab_im * a_re - (ab_re - 1.0) * a_im) / den
    bb_re = f_re[..., None] * b_re - f_im[..., None] * b_im
    bb_im = f_re[..., None] * b_im + f_im[..., None] * b_re
    pw_re, pw_im = powers(np.arange(t + 1))
    cp_re = c_re[None] * pw_re[:, :, None, :] - c_im[None] * pw_im[:, :, None, :]
    cp_im = c_re[None] * pw_im[:, :, None, :] + c_im[None] * pw_re[:, :, None, :]
    ktau = (jnp.einsum('tghp,gpk->tghk', cp_re[:t], bb_re)
            - jnp.einsum('tghp,gpk->tghk', cp_im[:t], bb_im))
    lag = np.arange(t)[None, :] - np.arange(t)[:, None]
    km = jnp.where((lag >= 0)[:, :, None, None, None], ktau[np.maximum(lag, 0)], 0.0)
    mg = km.transpose(2, 0, 4, 1, 3).reshape(g, t * hg, t * hg)
    rp_re, rp_im = pw_re[t - 1 - np.arange(t)], pw_im[t - 1 - np.arange(t)]
    pin_re = rp_re[..., None] * bb_re[None] - rp_im[..., None] * bb_im[None]
    pin_im = rp_re[..., None] * bb_im[None] + rp_im[..., None] * bb_re[None]
    pg = jnp.concatenate([pin_re, pin_im], axis=2).transpose(1, 0, 3, 2).reshape(g, t * hg, 2 * p)
    qg = jnp.concatenate([cp_re[1:], -cp_im[1:]], axis=3)
    qg = qg.transpose(1, 3, 0, 2).reshape(g, 2 * p, t * hg)
    nsteps = int(np.log2(chunks_per_seq))
    sr, si = powers(t * 2.0 ** np.arange(nsteps))
    ar = jnp.concatenate([sr, sr], axis=2).transpose(1, 0, 2)
    ai = jnp.concatenate([-si, si], axis=2).transpose(1, 0, 2)
    return mg.astype(BF16), pg.astype(BF16), qg.astype(BF16), ar, ai


def _merge_body(x_ref, yp_ref, yg_ref, ys_ref, sin_ref, g0_ref, g1_ref, g2_ref,
                d_ref, wglu_ref, wbp_ref, wbg_ref, wbs_ref, wo_ref, o_ref):
    y = ys_ref[...].astype(F32) + d_ref[...] * sin_ref[...].astype(F32)
    y = jax.nn.gelu(y)
    y = y * jax.nn.sigmoid(_dot(y.astype(BF16), wglu_ref[...]))
    m = g0_ref[...].astype(F32) * _dot(yp_ref[...], wbp_ref[...])
    m = m + g1_ref[...].astype(F32) * _dot(yg_ref[...], wbg_ref[...])
    m = m + g2_ref[...].astype(F32) * _dot(y.astype(BF16), wbs_ref[...])
    o_ref[...] = x_ref[...] + _dot(m.astype(BF16), wo_ref[...])


def _merge(x, y_pool, y_gla, y_ssm, z, ssm_col0, gate_col0, d_skip, w_glu, wb, w_out, *, tm=256):
    m, d = x.shape
    wp, wg, ws = y_pool.shape[1], y_gla.shape[1], y_ssm.shape[1]
    gb = gate_col0 // d
    const = lambda shape: pl.BlockSpec(shape, lambda i: (0, 0), pipeline_mode=pl.Buffered(1))
    return pl.pallas_call(
        _merge_body,
        grid=(m // tm,),
        in_specs=[
            pl.BlockSpec((tm, d), lambda i: (i, 0)),
            pl.BlockSpec((tm, wp), lambda i: (i, 0)),
            pl.BlockSpec((tm, wg), lambda i: (i, 0)),
            pl.BlockSpec((tm, ws), lambda i: (i, 0)),
            pl.BlockSpec((tm, ws), lambda i: (i, ssm_col0 // ws)),
            pl.BlockSpec((tm, d), lambda i: (i, gb)),
            pl.BlockSpec((tm, d), lambda i: (i, gb + 1)),
            pl.BlockSpec((tm, d), lambda i: (i, gb + 2)),
            const((1, ws)),
            const((ws, ws)),
            const((wp, d)),
            const((wg, d)),
            const((ws, d)),
            const((d, d)),
        ],
        out_specs=pl.BlockSpec((tm, d), lambda i: (i, 0)),
        out_shape=jax.ShapeDtypeStruct((m, d), F32),
        compiler_params=_cparams(("parallel",)),
        name="merge",
    )(x, y_pool, y_gla, y_ssm, z, z, z, z, d_skip.reshape(1, ws), w_glu,
      wb[:wp], wb[wp:wp + wg], wb[wp + wg:], w_out)


def kernel(x, ffn1_norm, ffn1_w_gate, ffn1_w_up, ffn1_w_down, mix_norm, w_in, pool_w, pool_scale, gla_w_gate2, gla_gate_bias, gla_norm, ssm_a_re, ssm_a_im, ssm_log_dt, ssm_b_re, ssm_b_im, ssm_c_re, ssm_c_im, ssm_d, ssm_w_glu, w_branch, w_out, ffn2_norm, ffn2_w_gate, ffn2_w_up, ffn2_w_down, final_norm):
    batch, seq, d = x.shape
    depth = w_in.shape[0]
    m = batch * seq
    pool_width = pool_scale.shape[1]
    qk_width = GLA_HEADS * GLA_DK
    v_width = GLA_HEADS * GLA_DV
    ssm_width = ssm_d.shape[1]
    n_groups = ssm_width // SSM_GROUP
    q0 = pool_width
    k0 = q0 + qk_width
    v0 = k0 + qk_width
    r0 = v0 + v_width
    glr0 = r0 + v_width
    ssm_src0 = glr0 + GLA_RANK
    ssm0 = glr0
    gate0 = ssm0 + ssm_width
    chunks_per_seq = seq // SSM_CHUNK

    bf = lambda a: a.astype(BF16)
    w_main = bf(jnp.concatenate([w_in[:, :, :glr0], w_in[:, :, ssm_src0:]], axis=2))
    w_glr = bf(jnp.pad(w_in[:, :, glr0:ssm_src0], ((0, 0), (0, 0), (0, LANES - GLA_RANK))))
    f1g, f1u, f1d = bf(ffn1_w_gate), bf(ffn1_w_up), bf(ffn1_w_down)
    f2g, f2u, f2d = bf(ffn2_w_gate), bf(ffn2_w_up), bf(ffn2_w_down)
    wbr, wo, wglu, pw = bf(w_branch), bf(w_out), bf(ssm_w_glu), bf(pool_w)
    w2 = gla_w_gate2.reshape(depth, GLA_RANK, GLA_HEADS, GLA_DK).transpose(0, 2, 1, 3)
    w2 = bf(jnp.pad(w2, ((0, 0), (0, 0), (0, LANES - GLA_RANK), (0, 0))))
    gbias = gla_gate_bias.reshape(depth, GLA_HEADS, 1, GLA_DK)
    ggain = gla_norm.reshape(depth, GLA_HEADS, 1, GLA_DV)

    xf = x.reshape(m, d)
    for l in range(depth):
        xf = _ffn(xf, ffn1_norm[l], f1g[l], f1u[l], f1d[l])
        z, glr = _inproj(xf, mix_norm[l], w_main[l], w_glr[l], gate0)
        y_pool = _pool(z, pw[l], pool_scale[l], batch, seq)
        y_gla = _gla(z, glr, w2[l], gbias[l], ggain[l], batch, seq, (q0, k0, v0, r0))
        mg, pg, qg, ar, ai = _ssm_operators(ssm_a_re[l], ssm_a_im[l], ssm_log_dt[l], ssm_b_re[l],
                                            ssm_b_im[l], ssm_c_re[l], ssm_c_im[l], chunks_per_seq)
        xs = z[:, ssm0:ssm0 + ssm_width].reshape(m // SSM_CHUNK, SSM_CHUNK, n_groups, SSM_GROUP)
        xs = xs.transpose(2, 0, 1, 3).reshape(n_groups, m // SSM_CHUNK, SSM_CHUNK * SSM_GROUP)
        ys = _ssm_core(xs, mg, pg, qg, ar, ai, chunks_per_seq)
        ys = ys.reshape(n_groups, m // SSM_CHUNK, SSM_CHUNK, SSM_GROUP).transpose(1, 2, 0, 3)
        ys = ys.reshape(m, ssm_width)
        xf = _merge(xf, y_pool, y_gla, ys, z, ssm0, gate0, ssm_d[l], wglu[l], wbr[l], wo[l])
        last = l == depth - 1
        xf = _ffn(xf, ffn2_norm[l], f2g[l], f2u[l], f2d[l], final_norm if last else None)
    return xf.reshape(batch, seq, d)
```

```python
import functools

import jax
import jax.numpy as jnp
import numpy as np
from jax import lax
from jax.experimental import pallas as pl
from jax.experimental.pallas import tpu as pltpu

F32 = jnp.float32
BF16 = jnp.bfloat16

EPS = 1e-6
POOL_WINDOWS = (2, 4, 8, 16)
POOL_GROUP = 128
POOL_HALO = 16
GLA_HEADS = 4
GLA_DK = 128
GLA_DV = 256
GLA_RANK = 16
GLA_TAU = 16.0
GLA_CHUNK = 64
SSM_GROUP = 16
SSM_STATE = 64
SSM_CHUNK = 16
LANES = 128
FFN_DOWN_CHUNK = 512
FFN_ROW_CHUNK = 256
BRANCH_ROWS = 512

VMEM_LIMIT = 60 * 1024 * 1024


def _cparams(sem):
    return pltpu.CompilerParams(dimension_semantics=sem, vmem_limit_bytes=VMEM_LIMIT)


def _rms(xf, g):
    return xf * lax.rsqrt(jnp.mean(xf * xf, axis=-1, keepdims=True) + EPS) * g


def _dot(a, b):
    return jnp.dot(a, b, preferred_element_type=F32)


def _ffn_body(x_ref, g_ref, wg_ref, wu_ref, wd_ref, *rest, final):
    if final:
        fg_ref, o_ref, xn_ref = rest
    else:
        o_ref, xn_ref = rest
    j = pl.program_id(1)

    row_chunks = [slice(r0, r0 + FFN_ROW_CHUNK) for r0 in range(0, o_ref.shape[0], FFN_ROW_CHUNK)]

    @pl.when(j == 0)
    def _():
        for rs in row_chunks:
            xn_ref[rs, :] = _rms(x_ref[rs, :], g_ref[...]).astype(BF16)
            o_ref[rs, :] = jnp.zeros((FFN_ROW_CHUNK, o_ref.shape[1]), F32)

    for rs in row_chunks:
        xn = xn_ref[rs, :]
        a = _dot(xn, wg_ref[...])
        b = _dot(xn, wu_ref[...])
        h = (jax.nn.silu(a) * b).astype(BF16)
        for c0 in range(0, o_ref.shape[1], FFN_DOWN_CHUNK):
            cs = slice(c0, c0 + FFN_DOWN_CHUNK)
            o_ref[rs, cs] += _dot(h, wd_ref[:, cs])

    @pl.when(j == pl.num_programs(1) - 1)
    def _():
        def epilogue(r, carry):
            rs = pl.ds(pl.multiple_of(r * FFN_ROW_CHUNK, FFN_ROW_CHUNK), FFN_ROW_CHUNK)
            y = x_ref[rs, :] + 0.5 * o_ref[rs, :]
            if final:
                y = _rms(y, fg_ref[...])
            o_ref[rs, :] = y
            return carry
        lax.fori_loop(0, len(row_chunks), epilogue, 0)


def _ffn(x, g, wg, wu, wd, layer, final_g=None, *, tm=1024, tf=512):
    m, d = x.shape
    f = wg.shape[2]
    final = final_g is not None
    in_specs = [
        pl.BlockSpec((tm, d), lambda i, j: (i, 0)),
        pl.BlockSpec((None, 1, d), lambda i, j: (layer, 0, 0)),
        pl.BlockSpec((None, d, tf), lambda i, j: (layer, 0, j)),
        pl.BlockSpec((None, d, tf), lambda i, j: (layer, 0, j)),
        pl.BlockSpec((None, tf, d), lambda i, j: (layer, j, 0)),
    ]
    args = [x, g, wg, wu, wd]
    if final:
        in_specs.append(pl.BlockSpec((1, d), lambda i, j: (0, 0)))
        args.append(final_g.reshape(1, d))
    return pl.pallas_call(
        functools.partial(_ffn_body, final=final),
        grid=(m // tm, f // tf),
        in_specs=in_specs,
        out_specs=pl.BlockSpec((tm, d), lambda i, j: (i, 0)),
        out_shape=jax.ShapeDtypeStruct((m, d), F32),
        scratch_shapes=[pltpu.VMEM((tm, d), BF16)],
        compiler_params=_cparams(("parallel", "arbitrary")),
        name="ffn_final" if final else "ffn",
    )(*args)


def _inproj_body(x_ref, g_ref, w_ref, wl_ref, z_ref, glr_ref, un_ref, *, gate_block0):
    j = pl.program_id(1)

    @pl.when(j == 0)
    def _():
        un = _rms(x_ref[...], g_ref[...]).astype(BF16)
        un_ref[...] = un
        glr_ref[...] = _dot(un, wl_ref[...])

    acc = _dot(un_ref[...], w_ref[...])
    z_ref[...] = jnp.where(j >= gate_block0, jax.nn.sigmoid(acc), acc).astype(BF16)


def _inproj(x, g, w_main, w_glr, layer, gate_col0, *, tm=1024, tn=1024):
    m, d = x.shape
    n = w_main.shape[2]
    return pl.pallas_call(
        functools.partial(_inproj_body, gate_block0=gate_col0 // tn),
        grid=(m // tm, n // tn),
        in_specs=[
            pl.BlockSpec((tm, d), lambda i, j: (i, 0)),
            pl.BlockSpec((None, 1, d), lambda i, j: (layer, 0, 0)),
            pl.BlockSpec((None, d, tn), lambda i, j: (layer, 0, j)),
            pl.BlockSpec((None, d, LANES), lambda i, j: (layer, 0, 0)),
        ],
        out_specs=[
            pl.BlockSpec((tm, tn), lambda i, j: (i, j)),
            pl.BlockSpec((tm, LANES), lambda i, j: (i, 0)),
        ],
        out_shape=[
            jax.ShapeDtypeStruct((m, n), BF16),
            jax.ShapeDtypeStruct((m, LANES), F32),
        ],
        scratch_shapes=[pltpu.VMEM((tm, d), BF16)],
        compiler_params=_cparams(("parallel", "arbitrary")),
        name="inproj",
    )(x, g, w_main, w_glr)


def _pool_body(p_ref, halo_ref, w_ref, sc_ref, o_ref, *, ts):
    t = pl.program_id(1)
    x = p_ref[...].astype(F32)
    halo = jnp.where(t > 0, halo_ref[...].astype(F32), 0.0)
    e = jnp.concatenate([halo, x], axis=0)
    s2 = e + pltpu.roll(e, 1, 0)
    s4 = s2 + pltpu.roll(s2, 2, 0)
    s8 = s4 + pltpu.roll(s4, 4, 0)
    s16 = s8 + pltpu.roll(s8, 8, 0)
    pos = (t * ts + 1 + lax.broadcasted_iota(jnp.int32, (ts, 1), 0)).astype(F32)
    outs = []
    for gi, (w, s) in enumerate(zip(POOL_WINDOWS, (s2, s4, s8, s16))):
        cols = slice(gi * POOL_GROUP, (gi + 1) * POOL_GROUP)
        mean = s[POOL_HALO:, cols] / jnp.minimum(pos, float(w))
        outs.append(_dot((mean - x[:, cols]).astype(BF16), w_ref[gi]))
    o_ref[...] = (jnp.concatenate(outs, axis=1) * sc_ref[...]).astype(BF16)


def _pool(z, w, scale, layer, batch, seq, *, ts=512):
    width = len(POOL_WINDOWS) * POOL_GROUP
    nt = seq // ts
    hb = ts // POOL_HALO
    return pl.pallas_call(
        functools.partial(_pool_body, ts=ts),
        grid=(batch, nt),
        in_specs=[
            pl.BlockSpec((ts, width), lambda b, t: (b * nt + t, 0)),
            pl.BlockSpec((POOL_HALO, width),
                         lambda b, t: (jnp.maximum((b * nt + t) * hb - 1, 0), 0)),
            pl.BlockSpec((None, len(POOL_WINDOWS), POOL_GROUP, POOL_GROUP),
                         lambda b, t: (layer, 0, 0, 0)),
            pl.BlockSpec((None, 1, width), lambda b, t: (layer, 0, 0)),
        ],
        out_specs=pl.BlockSpec((ts, width), lambda b, t: (b * nt + t, 0)),
        out_shape=jax.ShapeDtypeStruct((batch * seq, width), BF16),
        compiler_params=_cparams(("parallel", "arbitrary")),
        name="pool",
    )(z, z, w, scale)


def _gla_head(q, k, v, r, glr, w2, bias, gain, st, ts):
    c = GLA_CHUNK
    nc = ts // c
    logits = _dot(glr, w2) + bias
    log_a = (jnp.minimum(logits, 0.0) - jnp.log1p(jnp.exp(-jnp.abs(logits)))) * (1.0 / GLA_TAU)
    rowc = lax.broadcasted_iota(jnp.int32, (ts, 1), 0) % c
    bc = log_a
    step = 1
    while step < c:
        bc = bc + jnp.where(rowc >= step, pltpu.roll(bc, step, 0), 0.0)
        step *= 2
    bc3 = bc.reshape(nc, c, GLA_DK)
    b_last = bc3[:, c - 1:c, :]
    b_mid = bc3[:, c // 2 - 1:c // 2, :]
    q3 = q.astype(F32).reshape(nc, c, GLA_DK) * (GLA_DK ** -0.5)
    k3 = k.astype(F32).reshape(nc, c, GLA_DK)
    v3 = v.reshape(nc, c, GLA_DV)
    qe = (q3 * jnp.exp(bc3 - b_mid)).astype(BF16)
    ke = (k3 * jnp.exp(b_mid - bc3)).astype(BF16)
    scores = jnp.einsum('nid,njd->nij', qe, ke, preferred_element_type=F32)
    causal = (lax.broadcasted_iota(jnp.int32, (c, c), 0)
              >= lax.broadcasted_iota(jnp.int32, (c, c), 1))
    scores = jnp.where(causal[None], scores, 0.0).astype(BF16)
    o_intra = jnp.einsum('nij,njv->niv', scores, v3, preferred_element_type=F32)
    qd = (q3 * jnp.exp(bc3)).astype(BF16)
    kd = (k3 * jnp.exp(b_last - bc3)).astype(BF16)
    dec = jnp.exp(b_last)
    outs = []
    for n in range(nc):
        o_inter = lax.dot_general(qd[n], st.astype(BF16), (((1,), (1,)), ((), ())),
                                  preferred_element_type=F32)
        outs.append(o_intra[n] + o_inter)
        kv_t = lax.dot_general(v3[n], kd[n], (((0,), (0,)), ((), ())),
                               preferred_element_type=F32)
        st = dec[n] * st + kv_t
    o = jnp.concatenate(outs, axis=0)
    o = o * lax.rsqrt(jnp.mean(o * o, axis=-1, keepdims=True) + EPS) * gain
    return (o * jax.nn.silu(r.astype(F32))).astype(BF16), st


def _gla_body(q_ref, k_ref, v0_ref, v1_ref, r0_ref, r1_ref, glr_ref, w2_ref, b_ref, gain_ref,
              o_ref, st_ref, *, ts):
    @pl.when(pl.program_id(1) == 0)
    def _():
        st_ref[...] = jnp.zeros_like(st_ref)

    glr = glr_ref[...].astype(BF16)
    vr_refs = ((v0_ref, r0_ref), (v1_ref, r1_ref))
    heads_per_block = v0_ref.shape[1] // GLA_DV
    for h in range(GLA_HEADS):
        v_ref, r_ref = vr_refs[h // heads_per_block]
        vs = slice((h % heads_per_block) * GLA_DV, (h % heads_per_block + 1) * GLA_DV)
        ks = slice(h * GLA_DK, (h + 1) * GLA_DK)
        o, st = _gla_head(q_ref[:, ks], k_ref[:, ks], v_ref[:, vs], r_ref[:, vs], glr,
                          w2_ref[h], b_ref[h], gain_ref[h], st_ref[h], ts)
        st_ref[h] = st
        o_ref[:, h * GLA_DV:(h + 1) * GLA_DV] = o


def _gla(z, glr, w2, bias, gain, layer, batch, seq, cols, *, ts=512):
    h = GLA_HEADS
    nt = seq // ts
    q0, k0, v0, r0 = cols
    wb = h * GLA_DK
    zspec = lambda c0: pl.BlockSpec((ts, wb), lambda b, t: (b * nt + t, c0 // wb))
    wspec = lambda last2: pl.BlockSpec((None, h) + last2, lambda b, t: (layer, 0, 0, 0))
    return pl.pallas_call(
        functools.partial(_gla_body, ts=ts),
        grid=(batch, nt),
        in_specs=[
            zspec(q0), zspec(k0), zspec(v0), zspec(v0 + wb), zspec(r0), zspec(r0 + wb),
            pl.BlockSpec((ts, LANES), lambda b, t: (b * nt + t, 0)),
            wspec((LANES, GLA_DK)), wspec((1, GLA_DK)), wspec((1, GLA_DV)),
        ],
        out_specs=pl.BlockSpec((ts, h * GLA_DV), lambda b, t: (b * nt + t, 0)),
        out_shape=jax.ShapeDtypeStruct((batch * seq, h * GLA_DV), BF16),
        scratch_shapes=[pltpu.VMEM((h, GLA_DV, GLA_DK), F32)],
        compiler_params=_cparams(("parallel", "arbitrary")),
        name="gla",
    )(z, z, z, z, z, z, glr, w2, bias, gain)


def _ssm_body(x_ref, m_ref, p_ref, q_ref, ar_ref, ai_ref, o_ref, *, gb, chunks_per_seq):
    rows = x_ref.shape[1]
    rowc = lax.broadcasted_iota(jnp.int32, (rows, 1), 0) % chunks_per_seq
    for g in range(gb):
        x = x_ref[g]
        hst = _dot(x, p_ref[g])
        k, si = 1, 0
        while k < chunks_per_seq:
            hs = jnp.where(rowc >= k, pltpu.roll(hst, k, 0), 0.0)
            hst = hst + ar_ref[g, si:si + 1, :] * hs + ai_ref[g, si:si + 1, :] * pltpu.roll(hs, SSM_STATE, 1)
            k *= 2
            si += 1
        hprev = jnp.where(rowc >= 1, pltpu.roll(hst, 1, 0), 0.0)
        y = _dot(x, m_ref[g]) + _dot(hprev.astype(BF16), q_ref[g])
        o_ref[g] = y.astype(BF16)


def _ssm_core(xg, mg, pg, qg, ar, ai, chunks_per_seq, *, gb=4):
    g, rows, w = xg.shape
    ns = 2 * SSM_STATE
    nsteps = ar.shape[1]
    return pl.pallas_call(
        functools.partial(_ssm_body, gb=gb, chunks_per_seq=chunks_per_seq),
        grid=(g // gb,),
        in_specs=[
            pl.BlockSpec((gb, rows, w), lambda i: (i, 0, 0)),
            pl.BlockSpec((gb, w, w), lambda i: (i, 0, 0)),
            pl.BlockSpec((gb, w, ns), lambda i: (i, 0, 0)),
            pl.BlockSpec((gb, ns, w), lambda i: (i, 0, 0)),
            pl.BlockSpec((gb, nsteps, ns), lambda i: (i, 0, 0)),
            pl.BlockSpec((gb, nsteps, ns), lambda i: (i, 0, 0)),
        ],
        out_specs=pl.BlockSpec((gb, rows, w), lambda i: (i, 0, 0)),
        out_shape=jax.ShapeDtypeStruct((g, rows, w), BF16),
        compiler_params=_cparams(("parallel",)),
        name="ssm_core",
    )(xg, mg, pg, qg, ar, ai)


def _ssm_operators(a_re, a_im, log_dt, b_re, b_im, c_re, c_im, chunks_per_seq):
    t = SSM_CHUNK
    g, p = a_re.shape
    hg = b_re.shape[-1]
    dt = jnp.exp(log_dt)[:, None]
    lam_re, lam_im = dt * a_re, dt * a_im

    def powers(n):
        n = jnp.asarray(n, F32)[:, None, None]
        mag = jnp.exp(n * lam_re)
        return mag * jnp.cos(n * lam_im), mag * jnp.sin(n * lam_im)

    ab_re, ab_im = powers([1.0])
    ab_re, ab_im = ab_re[0], ab_im[0]
    den = a_re * a_re + a_im * a_im
    f_re = ((ab_re - 1.0) * a_re + ab_im * a_im) / den
    f_im = (ab_im * a_re - (ab_re - 1.0) * a_im) / den
    bb_re = f_re[..., None] * b_re - f_im[..., None] * b_im
    bb_im = f_re[..., None] * b_im + f_im[..., None] * b_re
    pw_re, pw_im = powers(np.arange(t + 1))
    cp_re = c_re[None] * pw_re[:, :, None, :] - c_im[None] * pw_im[:, :, None, :]
    cp_im = c_re[None] * pw_im[:, :, None, :] + c_im[None] * pw_re[:, :, None, :]
    ktau = (jnp.einsum('tghp,gpk->tghk', cp_re[:t], bb_re)
            - jnp.einsum('tghp,gpk->tghk', cp_im[:t], bb_im))
    lag = np.arange(t)[None, :] - np.arange(t)[:, None]
    km = jnp.where((lag >= 0)[:, :, None, None, None], ktau[np.maximum(lag, 0)], 0.0)
    mg = km.transpose(2, 0, 4, 1, 3).reshape(g, t * hg, t * hg)
    rp_re, rp_im = pw_re[t - 1 - np.arange(t)], pw_im[t - 1 - np.arange(t)]
    pin_re = rp_re[..., None] * bb_re[None] - rp_im[..., None] * bb_im[None]
    pin_im = rp_re[..., None] * bb_im[None] + rp_im[..., None] * bb_re[None]
    pg = jnp.concatenate([pin_re, pin_im], axis=2).transpose(1, 0, 3, 2).reshape(g, t * hg, 2 * p)
    qg = jnp.concatenate([cp_re[1:], -cp_im[1:]], axis=3)
    qg = qg.transpose(1, 3, 0, 2).reshape(g, 2 * p, t * hg)
    nsteps = int(np.log2(chunks_per_seq))
    sr, si = powers(t * 2.0 ** np.arange(nsteps))
    ar = jnp.concatenate([sr, sr], axis=2).transpose(1, 0, 2)
    ai = jnp.concatenate([-si, si], axis=2).transpose(1, 0, 2)
    return mg.astype(BF16), pg.astype(BF16), qg.astype(BF16), ar, ai


def _merge_body(x_ref, yp_ref, yg_ref, ys_ref, sin_ref, g0_ref, g1_ref, g2_ref,
                d_ref, wglu_ref, wbp_ref, wbg0_ref, wbg1_ref, wbs_ref, wo_ref, o_ref):
    y = ys_ref[...].astype(F32) + d_ref[...] * sin_ref[...].astype(F32)
    y = jax.nn.gelu(y)
    y = y * jax.nn.sigmoid(_dot(y.astype(BF16), wglu_ref[...]))
    m = g0_ref[...].astype(F32) * _dot(yp_ref[...], wbp_ref[...])
    half = wbg0_ref.shape[0]
    gla_proj = _dot(yg_ref[:, :half], wbg0_ref[...]) + _dot(yg_ref[:, half:], wbg1_ref[...])
    m = m + g1_ref[...].astype(F32) * gla_proj
    m = m + g2_ref[...].astype(F32) * _dot(y.astype(BF16), wbs_ref[...])
    o_ref[...] = x_ref[...] + _dot(m.astype(BF16), wo_ref[...])


def _merge(x, y_pool, y_gla, y_ssm, z, ssm_col0, gate_col0, d_skip, w_glu, wb, w_out, layer, *, tm=256):
    m, d = x.shape
    wp, wg, ws = y_pool.shape[1], y_gla.shape[1], y_ssm.shape[1]
    br = BRANCH_ROWS
    assert wp == br and ws == br and wg == 2 * br
    gb = gate_col0 // d
    one = pl.Buffered(1)
    wspec = lambda shape, r: pl.BlockSpec((None,) + shape, lambda i: (layer, r, 0), pipeline_mode=one)
    return pl.pallas_call(
        _merge_body,
        grid=(m // tm,),
        in_specs=[
            pl.BlockSpec((tm, d), lambda i: (i, 0)),
            pl.BlockSpec((tm, wp), lambda i: (i, 0)),
            pl.BlockSpec((tm, wg), lambda i: (i, 0)),
            pl.BlockSpec((tm, ws), lambda i: (i, 0)),
            pl.BlockSpec((tm, ws), lambda i: (i, ssm_col0 // ws)),
            pl.BlockSpec((tm, d), lambda i: (i, gb)),
            pl.BlockSpec((tm, d), lambda i: (i, gb + 1)),
            pl.BlockSpec((tm, d), lambda i: (i, gb + 2)),
            wspec((1, ws), 0),
            wspec((ws, ws), 0),
            wspec((br, d), 0),
            wspec((br, d), 1),
            wspec((br, d), 2),
            wspec((br, d), 3),
            wspec((d, d), 0),
        ],
        out_specs=pl.BlockSpec((tm, d), lambda i: (i, 0)),
        out_shape=jax.ShapeDtypeStruct((m, d), F32),
        compiler_params=_cparams(("parallel",)),
        name="merge",
    )(x, y_pool, y_gla, y_ssm, z, z, z, z, d_skip, w_glu, wb, wb, wb, wb, w_out)


def kernel(x, ffn1_norm, ffn1_w_gate, ffn1_w_up, ffn1_w_down, mix_norm, w_in, pool_w, pool_scale, gla_w_gate2, gla_gate_bias, gla_norm, ssm_a_re, ssm_a_im, ssm_log_dt, ssm_b_re, ssm_b_im, ssm_c_re, ssm_c_im, ssm_d, ssm_w_glu, w_branch, w_out, ffn2_norm, ffn2_w_gate, ffn2_w_up, ffn2_w_down, final_norm):
    batch, seq, d = x.shape
    depth = w_in.shape[0]
    m = batch * seq
    pool_width = pool_scale.shape[1]
    qk_width = GLA_HEADS * GLA_DK
    v_width = GLA_HEADS * GLA_DV
    ssm_width = ssm_d.shape[1]
    n_groups = ssm_width // SSM_GROUP
    q0 = pool_width
    k0 = q0 + qk_width
    v0 = k0 + qk_width
    r0 = v0 + v_width
    glr0 = r0 + v_width
    ssm_src0 = glr0 + GLA_RANK
    ssm0 = glr0
    gate0 = ssm0 + ssm_width
    chunks_per_seq = seq // SSM_CHUNK

    bf = lambda a: a.astype(BF16)
    row3 = lambda a: a.reshape(depth, 1, a.shape[-1])
    w_main = jnp.concatenate([bf(w_in[:, :, :glr0]), bf(w_in[:, :, ssm_src0:])], axis=2)
    w_glr = bf(jnp.pad(w_in[:, :, glr0:ssm_src0], ((0, 0), (0, 0), (0, LANES - GLA_RANK))))
    f1g, f1u, f1d = bf(ffn1_w_gate), bf(ffn1_w_up), bf(ffn1_w_down)
    f2g, f2u, f2d = bf(ffn2_w_gate), bf(ffn2_w_up), bf(ffn2_w_down)
    wbr, wo, wglu, pw = bf(w_branch), bf(w_out), bf(ssm_w_glu), bf(pool_w)
    n1, nm, n2 = row3(ffn1_norm), row3(mix_norm), row3(ffn2_norm)
    pscale, dskip = row3(pool_scale), row3(ssm_d)
    w2 = gla_w_gate2.reshape(depth, GLA_RANK, GLA_HEADS, GLA_DK).transpose(0, 2, 1, 3)
    w2 = bf(jnp.pad(w2, ((0, 0), (0, 0), (0, LANES - GLA_RANK), (0, 0))))
    gbias = gla_gate_bias.reshape(depth, GLA_HEADS, 1, GLA_DK)
    ggain = gla_norm.reshape(depth, GLA_HEADS, 1, GLA_DV)
    ssm_ops = jax.vmap(functools.partial(_ssm_operators, chunks_per_seq=chunks_per_seq))(
        ssm_a_re, ssm_a_im, ssm_log_dt, ssm_b_re, ssm_b_im, ssm_c_re, ssm_c_im)

    xf = x.reshape(m, d)
    for l in range(depth):
        xf = _ffn(xf, n1, f1g, f1u, f1d, l)
        z, glr = _inproj(xf, nm, w_main, w_glr, l, gate0)
        y_pool = _pool(z, pw, pscale, l, batch, seq)
        y_gla = _gla(z, glr, w2, gbias, ggain, l, batch, seq, (q0, k0, v0, r0))
        mg, pg, qg, ar, ai = (op[l] for op in ssm_ops)
        xs = z[:, ssm0:ssm0 + ssm_width].reshape(m // SSM_CHUNK, SSM_CHUNK, n_groups, SSM_GROUP)
        xs = xs.transpose(2, 0, 1, 3).reshape(n_groups, m // SSM_CHUNK, SSM_CHUNK * SSM_GROUP)
        ys = _ssm_core(xs, mg, pg, qg, ar, ai, chunks_per_seq)
        ys = ys.reshape(n_groups, m // SSM_CHUNK, SSM_CHUNK, SSM_GROUP).transpose(1, 2, 0, 3)
        ys = ys.reshape(m, ssm_width)
        xf = _merge(xf, y_pool, y_gla, ys, z, ssm0, gate0, dskip, wglu, wbr, wo, l)
        xf = _ffn(xf, n2, f2g, f2u, f2d, l, final_norm if l == depth - 1 else None)
    return xf.reshape(batch, seq, d)
```

```python
import functools

import jax
import jax.numpy as jnp
import numpy as np
from jax import lax
from jax.experimental import pallas as pl
from jax.experimental.pallas import tpu as pltpu

F32 = jnp.float32
BF16 = jnp.bfloat16

EPS = 1e-6
POOL_WINDOWS = (2, 4, 8, 16)
POOL_GROUP = 128
POOL_HALO = 16
GLA_HEADS = 4
GLA_DK = 128
GLA_DV = 256
GLA_RANK = 16
GLA_TAU = 16.0
GLA_CHUNK = 64
SSM_GROUP = 16
SSM_STATE = 64
SSM_CHUNK = 16
LANES = 128
FFN_DOWN_CHUNK = 512
FFN_ROW_CHUNK = 256
BRANCH_ROWS = 512

VMEM_LIMIT = 60 * 1024 * 1024


def _cparams(sem):
    return pltpu.CompilerParams(dimension_semantics=sem, vmem_limit_bytes=VMEM_LIMIT)


def _rms(xf, g):
    return xf * lax.rsqrt(jnp.mean(xf * xf, axis=-1, keepdims=True) + EPS) * g


def _dot(a, b):
    return jnp.dot(a, b, preferred_element_type=F32)


def _ffn_body(x_ref, g_ref, wg_ref, wu_ref, wd_ref, *rest, final):
    if final:
        fg_ref, o_ref, xn_ref = rest
    else:
        o_ref, xn_ref = rest
    j = pl.program_id(1)

    row_chunks = [slice(r0, r0 + FFN_ROW_CHUNK) for r0 in range(0, o_ref.shape[0], FFN_ROW_CHUNK)]

    @pl.when(j == 0)
    def _():
        for rs in row_chunks:
            xn_ref[rs, :] = _rms(x_ref[rs, :], g_ref[...]).astype(BF16)
            o_ref[rs, :] = jnp.zeros((FFN_ROW_CHUNK, o_ref.shape[1]), F32)

    for rs in row_chunks:
        xn = xn_ref[rs, :]
        a = _dot(xn, wg_ref[...])
        b = _dot(xn, wu_ref[...])
        h = (jax.nn.silu(a) * b).astype(BF16)
        for c0 in range(0, o_ref.shape[1], FFN_DOWN_CHUNK):
            cs = slice(c0, c0 + FFN_DOWN_CHUNK)
            o_ref[rs, cs] += _dot(h, wd_ref[:, cs])

    @pl.when(j == pl.num_programs(1) - 1)
    def _():
        def epilogue(r, carry):
            rs = pl.ds(pl.multiple_of(r * FFN_ROW_CHUNK, FFN_ROW_CHUNK), FFN_ROW_CHUNK)
            y = x_ref[rs, :] + 0.5 * o_ref[rs, :]
            if final:
                y = _rms(y, fg_ref[...])
            o_ref[rs, :] = y
            return carry
        lax.fori_loop(0, len(row_chunks), epilogue, 0)


def _ffn(x, g, wg, wu, wd, layer, final_g=None, *, tm=1024, tf=512):
    m, d = x.shape
    f = wg.shape[2]
    final = final_g is not None
    in_specs = [
        pl.BlockSpec((tm, d), lambda i, j: (i, 0)),
        pl.BlockSpec((None, 1, d), lambda i, j: (layer, 0, 0)),
        pl.BlockSpec((None, d, tf), lambda i, j: (layer, 0, j)),
        pl.BlockSpec((None, d, tf), lambda i, j: (layer, 0, j)),
        pl.BlockSpec((None, tf, d), lambda i, j: (layer, j, 0)),
    ]
    args = [x, g, wg, wu, wd]
    if final:
        in_specs.append(pl.BlockSpec((1, d), lambda i, j: (0, 0)))
        args.append(final_g.reshape(1, d))
    return pl.pallas_call(
        functools.partial(_ffn_body, final=final),
        grid=(m // tm, f // tf),
        in_specs=in_specs,
        out_specs=pl.BlockSpec((tm, d), lambda i, j: (i, 0)),
        out_shape=jax.ShapeDtypeStruct((m, d), F32),
        scratch_shapes=[pltpu.VMEM((tm, d), BF16)],
        compiler_params=_cparams(("parallel", "arbitrary")),
        name="ffn_final" if final else "ffn",
    )(*args)


def _repack_body(a_ref, b_ref, w_ref, wl_ref, *, shift, first_shifted):
    j = pl.program_id(2)
    tn = a_ref.shape[1]

    @pl.when(j < first_shifted)
    def _():
        w_ref[...] = a_ref[...].astype(BF16)

    @pl.when(j >= first_shifted)
    def _():
        full = jnp.concatenate([a_ref[...], b_ref[...]], axis=1)
        w_ref[...] = pltpu.roll(full, full.shape[1] - shift, 1)[:, :tn].astype(BF16)

    @pl.when(j == first_shifted)
    def _():
        head = a_ref[:, :LANES]
        lane = lax.broadcasted_iota(jnp.int32, head.shape, 1)
        wl_ref[...] = jnp.where(lane < shift, head, 0.0).astype(BF16)


def _repack_win(w_in, drop0, drop_width, *, td=1024, tn=512):
    depth, d, n_src = w_in.shape
    n = n_src - drop_width
    assert drop0 % tn == 0 and n % tn == 0 and drop_width < LANES
    sub = tn // LANES
    return pl.pallas_call(
        functools.partial(_repack_body, shift=drop_width, first_shifted=drop0 // tn),
        grid=(depth, d // td, n // tn),
        in_specs=[
            pl.BlockSpec((None, td, tn), lambda l, i, j: (l, i, j)),
            pl.BlockSpec((None, td, LANES), lambda l, i, j: (l, i, (j + 1) * sub)),
        ],
        out_specs=[
            pl.BlockSpec((None, td, tn), lambda l, i, j: (l, i, j)),
            pl.BlockSpec((None, td, LANES), lambda l, i, j: (l, i, 0)),
        ],
        out_shape=[
            jax.ShapeDtypeStruct((depth, d, n), BF16),
            jax.ShapeDtypeStruct((depth, d, LANES), BF16),
        ],
        compiler_params=_cparams(("parallel", "parallel", "arbitrary")),
        name="repack_win",
    )(w_in, w_in)


def _inproj_body(x_ref, g_ref, w_ref, wl_ref, z_ref, glr_ref, xc_ref, un_ref, ph_ref, *,
                 gate_block0, ssm_block, ssm_off):
    j = pl.program_id(1)

    @pl.when(j == 0)
    def _():
        un = _rms(x_ref[...], g_ref[...]).astype(BF16)
        un_ref[...] = un
        glr_ref[...] = _dot(un, wl_ref[...])

    acc = _dot(un_ref[...], w_ref[...])
    z_ref[...] = jnp.where(j >= gate_block0, jax.nn.sigmoid(acc), acc).astype(BF16)

    @pl.when(j == ssm_block)
    def _():
        ncol = ph_ref.shape[0]
        per_half = ncol // 2
        half = per_half * LANES
        for c in range(ncol):
            ph_ref[c] = acc[:, ssm_off + c * LANES:ssm_off + (c + 1) * LANES]
        rows = ph_ref.shape[1] // SSM_CHUNK
        for s in range(SSM_CHUNK):
            for c in range(ncol):
                piece = ph_ref[c, pl.ds(s, rows, stride=SSM_CHUNK), :].astype(BF16)
                l0 = s * half + (c % per_half) * LANES
                xc_ref[c // per_half, :, l0:l0 + LANES] = piece


def _inproj(x, g, w_main, w_glr, layer, ssm_col0, ssm_width, gate_col0, *, tm=1024, tn=1024):
    m, d = x.shape
    n = w_main.shape[2]
    body = functools.partial(_inproj_body, gate_block0=gate_col0 // tn,
                             ssm_block=ssm_col0 // tn, ssm_off=ssm_col0 % tn)
    return pl.pallas_call(
        body,
        grid=(m // tm, n // tn),
        in_specs=[
            pl.BlockSpec((tm, d), lambda i, j: (i, 0)),
            pl.BlockSpec((None, 1, d), lambda i, j: (layer, 0, 0)),
            pl.BlockSpec((None, d, tn), lambda i, j: (layer, 0, j)),
            pl.BlockSpec((None, d, LANES), lambda i, j: (layer, 0, 0)),
        ],
        out_specs=[
            pl.BlockSpec((tm, tn), lambda i, j: (i, j)),
            pl.BlockSpec((tm, LANES), lambda i, j: (i, 0)),
            pl.BlockSpec((2, tm // SSM_CHUNK, SSM_CHUNK * ssm_width // 2), lambda i, j: (0, i, 0)),
        ],
        out_shape=[
            jax.ShapeDtypeStruct((m, n), BF16),
            jax.ShapeDtypeStruct((m, LANES), F32),
            jax.ShapeDtypeStruct((2, m // SSM_CHUNK, SSM_CHUNK * ssm_width // 2), BF16),
        ],
        scratch_shapes=[pltpu.VMEM((tm, d), BF16), pltpu.VMEM((ssm_width // LANES, tm, LANES), F32)],
        compiler_params=_cparams(("parallel", "arbitrary")),
        name="inproj",
    )(x, g, w_main, w_glr)


def _pool_body(p_ref, halo_ref, w_ref, sc_ref, o_ref, *, ts):
    t = pl.program_id(1)
    x = p_ref[...].astype(F32)
    halo = jnp.where(t > 0, halo_ref[...].astype(F32), 0.0)
    e = jnp.concatenate([halo, x], axis=0)
    s2 = e + pltpu.roll(e, 1, 0)
    s4 = s2 + pltpu.roll(s2, 2, 0)
    s8 = s4 + pltpu.roll(s4, 4, 0)
    s16 = s8 + pltpu.roll(s8, 8, 0)
    pos = (t * ts + 1 + lax.broadcasted_iota(jnp.int32, (ts, 1), 0)).astype(F32)
    outs = []
    for gi, (w, s) in enumerate(zip(POOL_WINDOWS, (s2, s4, s8, s16))):
        cols = slice(gi * POOL_GROUP, (gi + 1) * POOL_GROUP)
        mean = s[POOL_HALO:, cols] / jnp.minimum(pos, float(w))
        outs.append(_dot((mean - x[:, cols]).astype(BF16), w_ref[gi]))
    o_ref[...] = (jnp.concatenate(outs, axis=1) * sc_ref[...]).astype(BF16)


def _pool(z, w, scale, layer, batch, seq, *, ts=512):
    width = len(POOL_WINDOWS) * POOL_GROUP
    nt = seq // ts
    hb = ts // POOL_HALO
    return pl.pallas_call(
        functools.partial(_pool_body, ts=ts),
        grid=(batch, nt),
        in_specs=[
            pl.BlockSpec((ts, width), lambda b, t: (b * nt + t, 0)),
            pl.BlockSpec((POOL_HALO, width),
                         lambda b, t: (jnp.maximum((b * nt + t) * hb - 1, 0), 0)),
            pl.BlockSpec((None, len(POOL_WINDOWS), POOL_GROUP, POOL_GROUP),
                         lambda b, t: (layer, 0, 0, 0)),
            pl.BlockSpec((None, 1, width), lambda b, t: (layer, 0, 0)),
        ],
        out_specs=pl.BlockSpec((ts, width), lambda b, t: (b * nt + t, 0)),
        out_shape=jax.ShapeDtypeStruct((batch * seq, width), BF16),
        compiler_params=_cparams(("parallel", "arbitrary")),
        name="pool",
    )(z, z, w, scale)


def _gla_head(q, k, v, r, glr, w2, bias, gain, st, ts):
    c = GLA_CHUNK
    nc = ts // c
    logits = _dot(glr, w2) + bias
    log_a = (jnp.minimum(logits, 0.0) - jnp.log1p(jnp.exp(-jnp.abs(logits)))) * (1.0 / GLA_TAU)
    rowc = lax.broadcasted_iota(jnp.int32, (ts, 1), 0) % c
    bc = log_a
    step = 1
    while step < c:
        bc = bc + jnp.where(rowc >= step, pltpu.roll(bc, step, 0), 0.0)
        step *= 2
    bc3 = bc.reshape(nc, c, GLA_DK)
    b_last = bc3[:, c - 1:c, :]
    b_mid = bc3[:, c // 2 - 1:c // 2, :]
    q3 = q.astype(F32).reshape(nc, c, GLA_DK) * (GLA_DK ** -0.5)
    k3 = k.astype(F32).reshape(nc, c, GLA_DK)
    v3 = v.reshape(nc, c, GLA_DV)
    qe = (q3 * jnp.exp(bc3 - b_mid)).astype(BF16)
    ke = (k3 * jnp.exp(b_mid - bc3)).astype(BF16)
    scores = jnp.einsum('nid,njd->nij', qe, ke, preferred_element_type=F32)
    causal = (lax.broadcasted_iota(jnp.int32, (c, c), 0)
              >= lax.broadcasted_iota(jnp.int32, (c, c), 1))
    scores = jnp.where(causal[None], scores, 0.0).astype(BF16)
    o_intra = jnp.einsum('nij,njv->niv', scores, v3, preferred_element_type=F32)
    qd = (q3 * jnp.exp(bc3)).astype(BF16)
    kd = (k3 * jnp.exp(b_last - bc3)).astype(BF16)
    dec = jnp.exp(b_last)
    outs = []
    for n in range(nc):
        o_inter = lax.dot_general(qd[n], st.astype(BF16), (((1,), (1,)), ((), ())),
                                  preferred_element_type=F32)
        outs.append(o_intra[n] + o_inter)
        kv_t = lax.dot_general(v3[n], kd[n], (((0,), (0,)), ((), ())),
                               preferred_element_type=F32)
        st = dec[n] * st + kv_t
    o = jnp.concatenate(outs, axis=0)
    o = o * lax.rsqrt(jnp.mean(o * o, axis=-1, keepdims=True) + EPS) * gain
    return (o * jax.nn.silu(r.astype(F32))).astype(BF16), st


def _gla_body(q_ref, k_ref, v0_ref, v1_ref, r0_ref, r1_ref, glr_ref, w2_ref, b_ref, gain_ref,
              o_ref, st_ref, *, ts):
    @pl.when(pl.program_id(1) == 0)
    def _():
        st_ref[...] = jnp.zeros_like(st_ref)

    glr = glr_ref[...].astype(BF16)
    vr_refs = ((v0_ref, r0_ref), (v1_ref, r1_ref))
    heads_per_block = v0_ref.shape[1] // GLA_DV
    for h in range(GLA_HEADS):
        v_ref, r_ref = vr_refs[h // heads_per_block]
        vs = slice((h % heads_per_block) * GLA_DV, (h % heads_per_block + 1) * GLA_DV)
        ks = slice(h * GLA_DK, (h + 1) * GLA_DK)
        o, st = _gla_head(q_ref[:, ks], k_ref[:, ks], v_ref[:, vs], r_ref[:, vs], glr,
                          w2_ref[h], b_ref[h], gain_ref[h], st_ref[h], ts)
        st_ref[h] = st
        o_ref[:, h * GLA_DV:(h + 1) * GLA_DV] = o


def _gla(z, glr, w2, bias, gain, layer, batch, seq, cols, *, ts=512):
    h = GLA_HEADS
    nt = seq // ts
    q0, k0, v0, r0 = cols
    wb = h * GLA_DK
    zspec = lambda c0: pl.BlockSpec((ts, wb), lambda b, t: (b * nt + t, c0 // wb))
    wspec = lambda last2: pl.BlockSpec((None, h) + last2, lambda b, t: (layer, 0, 0, 0))
    return pl.pallas_call(
        functools.partial(_gla_body, ts=ts),
        grid=(batch, nt),
        in_specs=[
            zspec(q0), zspec(k0), zspec(v0), zspec(v0 + wb), zspec(r0), zspec(r0 + wb),
            pl.BlockSpec((ts, LANES), lambda b, t: (b * nt + t, 0)),
            wspec((LANES, GLA_DK)), wspec((1, GLA_DK)), wspec((1, GLA_DV)),
        ],
        out_specs=pl.BlockSpec((ts, h * GLA_DV), lambda b, t: (b * nt + t, 0)),
        out_shape=jax.ShapeDtypeStruct((batch * seq, h * GLA_DV), BF16),
        scratch_shapes=[pltpu.VMEM((h, GLA_DV, GLA_DK), F32)],
        compiler_params=_cparams(("parallel", "arbitrary")),
        name="gla",
    )(z, z, z, z, z, z, glr, w2, bias, gain)


SSM_HALF_GROUPS = 16
SSM_BLOCK_GROUPS = 2
SSM_TSTEP = 4


def _ssm_body(x_ref, kst_ref, p_ref, q_ref, ar_ref, ai_ref, o_ref, hp_ref, *,
              chunks_per_seq, n_state_blocks):
    u = pl.program_id(0)
    rows = x_ref.shape[1]
    hw = SSM_HALF_GROUPS * SSM_GROUP
    blocks_per_half = n_state_blocks // 2
    ns = 2 * SSM_STATE

    @pl.when(u < n_state_blocks)
    def _():
        hb = u // blocks_per_half
        rowc = lax.broadcasted_iota(jnp.int32, (rows, 1), 0) % chunks_per_seq
        hst = _dot(x_ref[hb], p_ref[...])
        k, si = 1, 0
        while k < chunks_per_seq:
            hs = jnp.where(rowc >= k, pltpu.roll(hst, k, 0), 0.0)
            sw = jnp.concatenate([pltpu.roll(hs[:, b0:b0 + ns], SSM_STATE, 1)
                                  for b0 in range(0, hs.shape[1], ns)], axis=1)
            hst = hst + ar_ref[si:si + 1, :] * hs + ai_ref[si:si + 1, :] * sw
            k *= 2
            si += 1
        hp_ref[u] = jnp.where(rowc >= 1, pltpu.roll(hst, 1, 0), 0.0).astype(BF16)

    @pl.when(u >= n_state_blocks)
    def _():
        ob = u - n_state_blocks
        hb = ob // SSM_CHUNK
        t = ob % SSM_CHUNK
        bw = hp_ref.shape[2]
        yq = _dot(hp_ref[hb * blocks_per_half], q_ref[0:bw, :])
        for kk in range(1, blocks_per_half):
            yq = yq + _dot(hp_ref[hb * blocks_per_half + kk], q_ref[kk * bw:(kk + 1) * bw, :])
        start = pl.multiple_of((SSM_CHUNK - 1 - t) * hw, hw)
        for v in range(SSM_CHUNK // SSM_TSTEP):
            kext = (v + 1) * SSM_TSTEP * hw

            @pl.when(t // SSM_TSTEP == v)
            def _():
                intra = _dot(x_ref[hb, :, 0:kext], kst_ref[hb, pl.ds(start, kext), :])
                o_ref[...] = (yq + intra).astype(BF16)


def _ssm_core(xcat, kst, pfull, qfull, ar, ai, layer, chunks_per_seq):
    _, rows, xw = xcat.shape
    nb = pfull.shape[1]
    bw = pfull.shape[3]
    hw = SSM_HALF_GROUPS * SSM_GROUP
    nsteps = ar.shape[2]
    nout = 2 * SSM_CHUNK
    one = pl.Buffered(1)
    return pl.pallas_call(
        functools.partial(_ssm_body, chunks_per_seq=chunks_per_seq, n_state_blocks=nb),
        grid=(nb + nout,),
        in_specs=[
            pl.BlockSpec((2, rows, xw), lambda u: (0, 0, 0), pipeline_mode=one),
            pl.BlockSpec((None, 2, kst.shape[2], hw), lambda u: (layer, 0, 0, 0), pipeline_mode=one),
            pl.BlockSpec((None, None, xw, bw), lambda u: (layer, jnp.minimum(u, nb - 1), 0, 0)),
            pl.BlockSpec((None, None, qfull.shape[2], hw),
                         lambda u: (layer, jnp.maximum(u - nb, 0), 0, 0)),
            pl.BlockSpec((None, None, nsteps, bw), lambda u: (layer, jnp.minimum(u, nb - 1), 0, 0)),
            pl.BlockSpec((None, None, nsteps, bw), lambda u: (layer, jnp.minimum(u, nb - 1), 0, 0)),
        ],
        out_specs=pl.BlockSpec((None, rows, hw), lambda u: (jnp.maximum(u - nb, 0), 0, 0)),
        out_shape=jax.ShapeDtypeStruct((nout, rows, hw), BF16),
        scratch_shapes=[pltpu.VMEM((nb, rows, bw), BF16)],
        compiler_params=_cparams(("arbitrary",)),
        name="ssm_core",
    )(xcat, kst, pfull, qfull, ar, ai)


def _ssm_operators(a_re, a_im, log_dt, b_re, b_im, c_re, c_im, chunks_per_seq):
    t = SSM_CHUNK
    g, p = a_re.shape
    hg = b_re.shape[-1]
    hgc = SSM_HALF_GROUPS
    bgc = SSM_BLOCK_GROUPS
    dt = jnp.exp(log_dt)[:, None]
    lam_re, lam_im = dt * a_re, dt * a_im

    def powers(n):
        n = jnp.asarray(n, F32)[:, None, None]
        mag = jnp.exp(n * lam_re)
        return mag * jnp.cos(n * lam_im), mag * jnp.sin(n * lam_im)

    ab_re, ab_im = powers([1.0])
    ab_re, ab_im = ab_re[0], ab_im[0]
    den = a_re * a_re + a_im * a_im
    f_re = ((ab_re - 1.0) * a_re + ab_im * a_im) / den
    f_im = (ab_im * a_re - (ab_re - 1.0) * a_im) / den
    bb_re = f_re[..., None] * b_re - f_im[..., None] * b_im
    bb_im = f_re[..., None] * b_im + f_im[..., None] * b_re
    pw_re, pw_im = powers(np.arange(t + 1))
    cp_re = c_re[None] * pw_re[:, :, None, :] - c_im[None] * pw_im[:, :, None, :]
    cp_im = c_re[None] * pw_im[:, :, None, :] + c_im[None] * pw_re[:, :, None, :]
    ktau = (jnp.einsum('tghp,gpk->tghk', cp_re[:t], bb_re)
            - jnp.einsum('tghp,gpk->tghk', cp_im[:t], bb_im))
    eye_h = jnp.eye(hgc, dtype=F32)
    kbd = jnp.einsum('tzghk,gv->tzgkvh', ktau.reshape(t, 2, hgc, hg, hg), eye_h)
    kbd = kbd.reshape(t, 2, hgc * hg, hgc * hg)[::-1].transpose(1, 0, 2, 3)
    kst = jnp.concatenate([kbd, jnp.zeros((2, SSM_TSTEP - 1) + kbd.shape[2:], F32)], axis=1)
    kst = kst.reshape(2, (t + SSM_TSTEP - 1) * hgc * hg, hgc * hg)
    rp_re, rp_im = pw_re[t - 1 - np.arange(t)], pw_im[t - 1 - np.arange(t)]
    pin_re = rp_re[..., None] * bb_re[None] - rp_im[..., None] * bb_im[None]
    pin_im = rp_re[..., None] * bb_im[None] + rp_im[..., None] * bb_re[None]
    pin = jnp.concatenate([pin_re, pin_im], axis=2)
    nblk = g // bgc
    blk_per_half = hgc // bgc
    sel = (np.arange(hgc)[None, None, :]
           == (np.arange(blk_per_half)[:, None, None] * bgc + np.arange(bgc)[None, :, None]))
    sel = jnp.asarray(sel, F32)
    pin_r = pin.reshape(t, 2, blk_per_half, bgc, 2 * p, hg)
    pfull = jnp.einsum('szbgqk,bgv->zbsvkgq', pin_r, sel)
    pfull = pfull.reshape(nblk, t * hgc * hg, bgc * 2 * p)
    qg = jnp.concatenate([cp_re[1:], -cp_im[1:]], axis=3)
    qfull = jnp.einsum('tzghq,gv->ztgqvh', qg.reshape(t, 2, hgc, hg, 2 * p), eye_h)
    qfull = qfull.reshape(2 * t, hgc * 2 * p, hgc * hg)
    nsteps = int(np.log2(chunks_per_seq))
    sr, si = powers(t * 2.0 ** np.arange(nsteps))
    blk = lambda a: a.reshape(nsteps, nblk, bgc * 2 * p).transpose(1, 0, 2)
    ar = blk(jnp.concatenate([sr, sr], axis=2))
    ai = blk(jnp.concatenate([-si, si], axis=2))
    return kst.astype(BF16), pfull.astype(BF16), qfull.astype(BF16), ar, ai


def _merge_body(x_ref, yp_ref, yg_ref, yc_ref, sin_ref, g0_ref, g1_ref, g2_ref,
                d_ref, wglu_ref, wbp_ref, wbg0_ref, wbg1_ref, wbs_ref, wo_ref, o_ref, ys_ref):
    per_half = yc_ref.shape[2] // LANES
    rows = yc_ref.shape[1]
    for ob in range(yc_ref.shape[0]):
        hb, t = divmod(ob, SSM_CHUNK)
        plane = yc_ref[ob].astype(F32)
        for c in range(per_half):
            ys_ref[hb * per_half + c, pl.ds(t, rows, stride=SSM_CHUNK), :] = plane[:, c * LANES:(c + 1) * LANES]
    ys = jnp.concatenate([ys_ref[c] for c in range(ys_ref.shape[0])], axis=1)
    y = ys + d_ref[...] * sin_ref[...].astype(F32)
    y = jax.nn.gelu(y)
    y = y * jax.nn.sigmoid(_dot(y.astype(BF16), wglu_ref[...]))
    m = g0_ref[...].astype(F32) * _dot(yp_ref[...], wbp_ref[...])
    half = wbg0_ref.shape[0]
    gla_proj = _dot(yg_ref[:, :half], wbg0_ref[...]) + _dot(yg_ref[:, half:], wbg1_ref[...])
    m = m + g1_ref[...].astype(F32) * gla_proj
    m = m + g2_ref[...].astype(F32) * _dot(y.astype(BF16), wbs_ref[...])
    o_ref[...] = x_ref[...] + _dot(m.astype(BF16), wo_ref[...])


def _merge(x, y_pool, y_gla, ycat, z, ssm_col0, gate_col0, d_skip, w_glu, wb, w_out, layer, *, tm=256):
    m, d = x.shape
    wp, wg, ws = y_pool.shape[1], y_gla.shape[1], 2 * ycat.shape[2]
    br = BRANCH_ROWS
    assert wp == br and ws == br and wg == 2 * br
    gb = gate_col0 // d
    one = pl.Buffered(1)
    wspec = lambda shape, r: pl.BlockSpec((None,) + shape, lambda i: (layer, r, 0), pipeline_mode=one)
    return pl.pallas_call(
        _merge_body,
        grid=(m // tm,),
        in_specs=[
            pl.BlockSpec((tm, d), lambda i: (i, 0)),
            pl.BlockSpec((tm, wp), lambda i: (i, 0)),
            pl.BlockSpec((tm, wg), lambda i: (i, 0)),
            pl.BlockSpec((ycat.shape[0], tm // SSM_CHUNK, ws // 2), lambda i: (0, i, 0)),
            pl.BlockSpec((tm, ws), lambda i: (i, ssm_col0 // ws)),
            pl.BlockSpec((tm, d), lambda i: (i, gb)),
            pl.BlockSpec((tm, d), lambda i: (i, gb + 1)),
            pl.BlockSpec((tm, d), lambda i: (i, gb + 2)),
            wspec((1, ws), 0),
            wspec((ws, ws), 0),
            wspec((br, d), 0),
            wspec((br, d), 1),
            wspec((br, d), 2),
            wspec((br, d), 3),
            wspec((d, d), 0),
        ],
        out_specs=pl.BlockSpec((tm, d), lambda i: (i, 0)),
        out_shape=jax.ShapeDtypeStruct((m, d), F32),
        scratch_shapes=[pltpu.VMEM((ws // LANES, tm, LANES), F32)],
        compiler_params=_cparams(("parallel",)),
        name="merge",
    )(x, y_pool, y_gla, ycat, z, z, z, z, d_skip, w_glu, wb, wb, wb, wb, w_out)


def kernel(x, ffn1_norm, ffn1_w_gate, ffn1_w_up, ffn1_w_down, mix_norm, w_in, pool_w, pool_scale, gla_w_gate2, gla_gate_bias, gla_norm, ssm_a_re, ssm_a_im, ssm_log_dt, ssm_b_re, ssm_b_im, ssm_c_re, ssm_c_im, ssm_d, ssm_w_glu, w_branch, w_out, ffn2_norm, ffn2_w_gate, ffn2_w_up, ffn2_w_down, final_norm):
    batch, seq, d = x.shape
    depth = w_in.shape[0]
    m = batch * seq
    pool_width = pool_scale.shape[1]
    qk_width = GLA_HEADS * GLA_DK
    v_width = GLA_HEADS * GLA_DV
    ssm_width = ssm_d.shape[1]
    n_groups = ssm_width // SSM_GROUP
    q0 = pool_width
    k0 = q0 + qk_width
    v0 = k0 + qk_width
    r0 = v0 + v_width
    glr0 = r0 + v_width
    ssm_src0 = glr0 + GLA_RANK
    ssm0 = glr0
    gate0 = ssm0 + ssm_width
    chunks_per_seq = seq // SSM_CHUNK

    bf = lambda a: a.astype(BF16)
    row3 = lambda a: a.reshape(depth, 1, a.shape[-1])
    w_main, w_glr = _repack_win(w_in, glr0, GLA_RANK)
    f1g, f1u, f1d = bf(ffn1_w_gate), bf(ffn1_w_up), bf(ffn1_w_down)
    f2g, f2u, f2d = bf(ffn2_w_gate), bf(ffn2_w_up), bf(ffn2_w_down)
    wbr, wo, wglu, pw = bf(w_branch), bf(w_out), bf(ssm_w_glu), bf(pool_w)
    n1, nm, n2 = row3(ffn1_norm), row3(mix_norm), row3(ffn2_norm)
    pscale, dskip = row3(pool_scale), row3(ssm_d)
    w2 = gla_w_gate2.reshape(depth, GLA_RANK, GLA_HEADS, GLA_DK).transpose(0, 2, 1, 3)
    w2 = bf(jnp.pad(w2, ((0, 0), (0, 0), (0, LANES - GLA_RANK), (0, 0))))
    gbias = gla_gate_bias.reshape(depth, GLA_HEADS, 1, GLA_DK)
    ggain = gla_norm.reshape(depth, GLA_HEADS, 1, GLA_DV)
    ssm_ops = jax.vmap(functools.partial(_ssm_operators, chunks_per_seq=chunks_per_seq))(
        ssm_a_re, ssm_a_im, ssm_log_dt, ssm_b_re, ssm_b_im, ssm_c_re, ssm_c_im)

    xf = x.reshape(m, d)
    for l in range(depth):
        xf = _ffn(xf, n1, f1g, f1u, f1d, l)
        z, glr, xcat = _inproj(xf, nm, w_main, w_glr, l, ssm0, ssm_width, gate0)
        y_pool = _pool(z, pw, pscale, l, batch, seq)
        y_gla = _gla(z, glr, w2, gbias, ggain, l, batch, seq, (q0, k0, v0, r0))
        ycat = _ssm_core(xcat, *ssm_ops, l, chunks_per_seq)
        xf = _merge(xf, y_pool, y_gla, ycat, z, ssm0, gate0, dskip, wglu, wbr, wo, l)
        xf = _ffn(xf, n2, f2g, f2u, f2d, l, final_norm if l == depth - 1 else None)
    return xf.reshape(batch, seq, d)
```

```python
import functools

import jax
import jax.numpy as jnp
import numpy as np
from jax import lax
from jax.experimental import pallas as pl
from jax.experimental.pallas import tpu as pltpu

F32 = jnp.float32
BF16 = jnp.bfloat16

EPS = 1e-6
POOL_WINDOWS = (2, 4, 8, 16)
POOL_GROUP = 128
POOL_HALO = 16
GLA_HEADS = 4
GLA_DK = 128
GLA_DV = 256
GLA_RANK = 16
GLA_TAU = 16.0
GLA_CHUNK = 64
SSM_GROUP = 16
SSM_STATE = 64
SSM_CHUNK = 16
LANES = 128
FFN_DOWN_CHUNK = 512
FFN_ROW_CHUNK = 256
BRANCH_ROWS = 512

VMEM_LIMIT = 60 * 1024 * 1024


def _cparams(sem):
    return pltpu.CompilerParams(dimension_semantics=sem, vmem_limit_bytes=VMEM_LIMIT)


def _rms(xf, g):
    return xf * lax.rsqrt(jnp.mean(xf * xf, axis=-1, keepdims=True) + EPS) * g


def _dot(a, b):
    return jnp.dot(a, b, preferred_element_type=F32)


def _ffn_body(x_ref, g_ref, wg_ref, wu_ref, wd_ref, *rest, final):
    if final:
        fg_ref, o_ref, xn_ref = rest
    else:
        o_ref, xn_ref = rest
    j = pl.program_id(1)

    row_chunks = [slice(r0, r0 + FFN_ROW_CHUNK) for r0 in range(0, o_ref.shape[0], FFN_ROW_CHUNK)]

    @pl.when(j == 0)
    def _():
        for rs in row_chunks:
            xn_ref[rs, :] = _rms(x_ref[rs, :], g_ref[...]).astype(BF16)
            o_ref[rs, :] = jnp.zeros((FFN_ROW_CHUNK, o_ref.shape[1]), F32)

    for rs in row_chunks:
        xn = xn_ref[rs, :]
        a = _dot(xn, wg_ref[...])
        b = _dot(xn, wu_ref[...])
        h = (jax.nn.silu(a) * b).astype(BF16)
        for c0 in range(0, o_ref.shape[1], FFN_DOWN_CHUNK):
            cs = slice(c0, c0 + FFN_DOWN_CHUNK)
            o_ref[rs, cs] += _dot(h, wd_ref[:, cs])

    @pl.when(j == pl.num_programs(1) - 1)
    def _():
        def epilogue(r, carry):
            rs = pl.ds(pl.multiple_of(r * FFN_ROW_CHUNK, FFN_ROW_CHUNK), FFN_ROW_CHUNK)
            y = x_ref[rs, :] + 0.5 * o_ref[rs, :]
            if final:
                y = _rms(y, fg_ref[...])
            o_ref[rs, :] = y
            return carry
        lax.fori_loop(0, len(row_chunks), epilogue, 0)


def _ffn(x, g, wg, wu, wd, layer, final_g=None, *, tm=1024, tf=512):
    m, d = x.shape
    f = wg.shape[2]
    final = final_g is not None
    in_specs = [
        pl.BlockSpec((tm, d), lambda i, j: (i, 0)),
        pl.BlockSpec((None, 1, d), lambda i, j: (layer, 0, 0)),
        pl.BlockSpec((None, d, tf), lambda i, j: (layer, 0, j)),
        pl.BlockSpec((None, d, tf), lambda i, j: (layer, 0, j)),
        pl.BlockSpec((None, tf, d), lambda i, j: (layer, j, 0)),
    ]
    args = [x, g, wg, wu, wd]
    if final:
        in_specs.append(pl.BlockSpec((1, d), lambda i, j: (0, 0)))
        args.append(final_g.reshape(1, d))
    return pl.pallas_call(
        functools.partial(_ffn_body, final=final),
        grid=(m // tm, f // tf),
        in_specs=in_specs,
        out_specs=pl.BlockSpec((tm, d), lambda i, j: (i, 0)),
        out_shape=jax.ShapeDtypeStruct((m, d), F32),
        scratch_shapes=[pltpu.VMEM((tm, d), BF16)],
        compiler_params=_cparams(("parallel", "arbitrary")),
        name="ffn_final" if final else "ffn",
    )(*args)


def _repack_body(a_ref, b_ref, w_ref, wl_ref, *, shift, first_shifted):
    j = pl.program_id(2)
    tn = a_ref.shape[1]

    @pl.when(j < first_shifted)
    def _():
        w_ref[...] = a_ref[...].astype(BF16)

    @pl.when(j >= first_shifted)
    def _():
        full = jnp.concatenate([a_ref[...], b_ref[...]], axis=1)
        w_ref[...] = pltpu.roll(full, full.shape[1] - shift, 1)[:, :tn].astype(BF16)

    @pl.when(j == first_shifted)
    def _():
        head = a_ref[:, :LANES]
        lane = lax.broadcasted_iota(jnp.int32, head.shape, 1)
        wl_ref[...] = jnp.where(lane < shift, head, 0.0).astype(BF16)


def _repack_win(w_in, drop0, drop_width, *, td=1024, tn=512):
    depth, d, n_src = w_in.shape
    n = n_src - drop_width
    assert drop0 % tn == 0 and n % tn == 0 and drop_width < LANES
    sub = tn // LANES
    return pl.pallas_call(
        functools.partial(_repack_body, shift=drop_width, first_shifted=drop0 // tn),
        grid=(depth, d // td, n // tn),
        in_specs=[
            pl.BlockSpec((None, td, tn), lambda l, i, j: (l, i, j)),
            pl.BlockSpec((None, td, LANES), lambda l, i, j: (l, i, (j + 1) * sub)),
        ],
        out_specs=[
            pl.BlockSpec((None, td, tn), lambda l, i, j: (l, i, j)),
            pl.BlockSpec((None, td, LANES), lambda l, i, j: (l, i, 0)),
        ],
        out_shape=[
            jax.ShapeDtypeStruct((depth, d, n), BF16),
            jax.ShapeDtypeStruct((depth, d, LANES), BF16),
        ],
        compiler_params=_cparams(("parallel", "parallel", "arbitrary")),
        name="repack_win",
    )(w_in, w_in)


def _inproj_body(x_ref, g_ref, w_ref, wl_ref, z_ref, glr_ref, xc_ref, un_ref, ph_ref, *,
                 gate_block0, ssm_block, ssm_off):
    j = pl.program_id(1)

    @pl.when(j == 0)
    def _():
        un = _rms(x_ref[...], g_ref[...]).astype(BF16)
        un_ref[...] = un
        glr_ref[...] = _dot(un, wl_ref[...])

    acc = _dot(un_ref[...], w_ref[...])
    z_ref[...] = jnp.where(j >= gate_block0, jax.nn.sigmoid(acc), acc).astype(BF16)

    @pl.when(j == ssm_block)
    def _():
        ncol = ph_ref.shape[0]
        per_half = ncol // 2
        half = per_half * LANES
        for c in range(ncol):
            ph_ref[c] = acc[:, ssm_off + c * LANES:ssm_off + (c + 1) * LANES]
        rows = ph_ref.shape[1] // SSM_CHUNK
        for s in range(SSM_CHUNK):
            for c in range(ncol):
                piece = ph_ref[c, pl.ds(s, rows, stride=SSM_CHUNK), :].astype(BF16)
                l0 = s * half + (c % per_half) * LANES
                xc_ref[c // per_half, :, l0:l0 + LANES] = piece


def _inproj(x, g, w_main, w_glr, layer, ssm_col0, ssm_width, gate_col0, *, tm=1024, tn=1024):
    m, d = x.shape
    n = w_main.shape[2]
    body = functools.partial(_inproj_body, gate_block0=gate_col0 // tn,
                             ssm_block=ssm_col0 // tn, ssm_off=ssm_col0 % tn)
    return pl.pallas_call(
        body,
        grid=(m // tm, n // tn),
        in_specs=[
            pl.BlockSpec((tm, d), lambda i, j: (i, 0)),
            pl.BlockSpec((None, 1, d), lambda i, j: (layer, 0, 0)),
            pl.BlockSpec((None, d, tn), lambda i, j: (layer, 0, j)),
            pl.BlockSpec((None, d, LANES), lambda i, j: (layer, 0, 0)),
        ],
        out_specs=[
            pl.BlockSpec((tm, tn), lambda i, j: (i, j)),
            pl.BlockSpec((tm, LANES), lambda i, j: (i, 0)),
            pl.BlockSpec((2, tm // SSM_CHUNK, SSM_CHUNK * ssm_width // 2), lambda i, j: (0, i, 0)),
        ],
        out_shape=[
            jax.ShapeDtypeStruct((m, n), BF16),
            jax.ShapeDtypeStruct((m, LANES), F32),
            jax.ShapeDtypeStruct((2, m // SSM_CHUNK, SSM_CHUNK * ssm_width // 2), BF16),
        ],
        scratch_shapes=[pltpu.VMEM((tm, d), BF16), pltpu.VMEM((ssm_width // LANES, tm, LANES), F32)],
        compiler_params=_cparams(("parallel", "arbitrary")),
        name="inproj",
    )(x, g, w_main, w_glr)


def _pool_body(p_ref, halo_ref, w_ref, sc_ref, o_ref, *, ts):
    t = pl.program_id(1)
    x = p_ref[...].astype(F32)
    halo = jnp.where(t > 0, halo_ref[...].astype(F32), 0.0)
    e = jnp.concatenate([halo, x], axis=0)
    s2 = e + pltpu.roll(e, 1, 0)
    s4 = s2 + pltpu.roll(s2, 2, 0)
    s8 = s4 + pltpu.roll(s4, 4, 0)
    s16 = s8 + pltpu.roll(s8, 8, 0)
    pos = (t * ts + 1 + lax.broadcasted_iota(jnp.int32, (ts, 1), 0)).astype(F32)
    outs = []
    for gi, (w, s) in enumerate(zip(POOL_WINDOWS, (s2, s4, s8, s16))):
        cols = slice(gi * POOL_GROUP, (gi + 1) * POOL_GROUP)
        mean = s[POOL_HALO:, cols] / jnp.minimum(pos, float(w))
        outs.append(_dot((mean - x[:, cols]).astype(BF16), w_ref[gi]))
    o_ref[...] = (jnp.concatenate(outs, axis=1) * sc_ref[...]).astype(BF16)


def _pool(z, w, scale, layer, batch, seq, *, ts=512):
    width = len(POOL_WINDOWS) * POOL_GROUP
    nt = seq // ts
    hb = ts // POOL_HALO
    return pl.pallas_call(
        functools.partial(_pool_body, ts=ts),
        grid=(batch, nt),
        in_specs=[
            pl.BlockSpec((ts, width), lambda b, t: (b * nt + t, 0)),
            pl.BlockSpec((POOL_HALO, width),
                         lambda b, t: (jnp.maximum((b * nt + t) * hb - 1, 0), 0)),
            pl.BlockSpec((None, len(POOL_WINDOWS), POOL_GROUP, POOL_GROUP),
                         lambda b, t: (layer, 0, 0, 0)),
            pl.BlockSpec((None, 1, width), lambda b, t: (layer, 0, 0)),
        ],
        out_specs=pl.BlockSpec((ts, width), lambda b, t: (b * nt + t, 0)),
        out_shape=jax.ShapeDtypeStruct((batch * seq, width), BF16),
        compiler_params=_cparams(("parallel", "arbitrary")),
        name="pool",
    )(z, z, w, scale)


def _gla_head(q, k, v, r, glr, w2, bias, gain, st, ts):
    c = GLA_CHUNK
    nc = ts // c
    logits = _dot(glr, w2) + bias
    log_a = (jnp.minimum(logits, 0.0) - jnp.log1p(jnp.exp(-jnp.abs(logits)))) * (1.0 / GLA_TAU)
    rowc = lax.broadcasted_iota(jnp.int32, (ts, 1), 0) % c
    bc = log_a
    step = 1
    while step < c:
        bc = bc + jnp.where(rowc >= step, pltpu.roll(bc, step, 0), 0.0)
        step *= 2
    bc3 = bc.reshape(nc, c, GLA_DK)
    b_last = bc3[:, c - 1:c, :]
    b_mid = bc3[:, c // 2 - 1:c // 2, :]
    q3 = q.astype(F32).reshape(nc, c, GLA_DK) * (GLA_DK ** -0.5)
    k3 = k.astype(F32).reshape(nc, c, GLA_DK)
    v3 = v.reshape(nc, c, GLA_DV)
    qe = (q3 * jnp.exp(bc3 - b_mid)).astype(BF16)
    ke = (k3 * jnp.exp(b_mid - bc3)).astype(BF16)
    scores = jnp.einsum('nid,njd->nij', qe, ke, preferred_element_type=F32)
    causal = (lax.broadcasted_iota(jnp.int32, (c, c), 0)
              >= lax.broadcasted_iota(jnp.int32, (c, c), 1))
    scores = jnp.where(causal[None], scores, 0.0).astype(BF16)
    o_intra = jnp.einsum('nij,njv->niv', scores, v3, preferred_element_type=F32)
    qd = (q3 * jnp.exp(bc3)).astype(BF16)
    kd = (k3 * jnp.exp(b_last - bc3)).astype(BF16)
    dec = jnp.exp(b_last)
    outs = []
    for n in range(nc):
        o_inter = lax.dot_general(qd[n], st.astype(BF16), (((1,), (1,)), ((), ())),
                                  preferred_element_type=F32)
        outs.append(o_intra[n] + o_inter)
        kv_t = lax.dot_general(v3[n], kd[n], (((0,), (0,)), ((), ())),
                               preferred_element_type=F32)
        st = dec[n] * st + kv_t
    o = jnp.concatenate(outs, axis=0)
    o = o * lax.rsqrt(jnp.mean(o * o, axis=-1, keepdims=True) + EPS) * gain
    return (o * jax.nn.silu(r.astype(F32))).astype(BF16), st


def _gla_body(q_ref, k_ref, v0_ref, v1_ref, r0_ref, r1_ref, glr_ref, w2_ref, b_ref, gain_ref,
              o_ref, st_ref, *, ts):
    @pl.when(pl.program_id(1) == 0)
    def _():
        st_ref[...] = jnp.zeros_like(st_ref)

    glr = glr_ref[...].astype(BF16)
    vr_refs = ((v0_ref, r0_ref), (v1_ref, r1_ref))
    heads_per_block = v0_ref.shape[1] // GLA_DV
    for h in range(GLA_HEADS):
        v_ref, r_ref = vr_refs[h // heads_per_block]
        vs = slice((h % heads_per_block) * GLA_DV, (h % heads_per_block + 1) * GLA_DV)
        ks = slice(h * GLA_DK, (h + 1) * GLA_DK)
        o, st = _gla_head(q_ref[:, ks], k_ref[:, ks], v_ref[:, vs], r_ref[:, vs], glr,
                          w2_ref[h], b_ref[h], gain_ref[h], st_ref[h], ts)
        st_ref[h] = st
        o_ref[:, h * GLA_DV:(h + 1) * GLA_DV] = o


def _gla(z, glr, w2, bias, gain, layer, batch, seq, cols, *, ts=512):
    h = GLA_HEADS
    nt = seq // ts
    q0, k0, v0, r0 = cols
    wb = h * GLA_DK
    zspec = lambda c0: pl.BlockSpec((ts, wb), lambda b, t: (b * nt + t, c0 // wb))
    wspec = lambda last2: pl.BlockSpec((None, h) + last2, lambda b, t: (layer, 0, 0, 0))
    return pl.pallas_call(
        functools.partial(_gla_body, ts=ts),
        grid=(batch, nt),
        in_specs=[
            zspec(q0), zspec(k0), zspec(v0), zspec(v0 + wb), zspec(r0), zspec(r0 + wb),
            pl.BlockSpec((ts, LANES), lambda b, t: (b * nt + t, 0)),
            wspec((LANES, GLA_DK)), wspec((1, GLA_DK)), wspec((1, GLA_DV)),
        ],
        out_specs=pl.BlockSpec((ts, h * GLA_DV), lambda b, t: (b * nt + t, 0)),
        out_shape=jax.ShapeDtypeStruct((batch * seq, h * GLA_DV), BF16),
        scratch_shapes=[pltpu.VMEM((h, GLA_DV, GLA_DK), F32)],
        compiler_params=_cparams(("parallel", "arbitrary")),
        name="gla",
    )(z, z, z, z, z, z, glr, w2, bias, gain)


SSM_HALF_GROUPS = 16
SSM_BLOCK_GROUPS = 2
SSM_TSTEP = 4


def _ssm_body(x_ref, kc_ref, pc_ref, qc_ref, ar_ref, ai_ref, o_ref, hp_ref, kst_ref, v_ref, *,
              chunks_per_seq, n_state_blocks):
    u = pl.program_id(0)
    rows = x_ref.shape[1]
    hw = SSM_HALF_GROUPS * SSM_GROUP
    blocks_per_half = n_state_blocks // 2
    ns = 2 * SSM_STATE
    bw = SSM_BLOCK_GROUPS * ns
    lane_group = lax.broadcasted_iota(jnp.int32, (1, hw), 1) // SSM_GROUP

    def p_tile(n):
        pc = pc_ref[n // blocks_per_half]
        row_group = (lax.broadcasted_iota(jnp.int32, (pc.shape[0], 1), 0) // SSM_GROUP) % SSM_HALF_GROUPS
        g0 = (n % blocks_per_half) * SSM_BLOCK_GROUPS
        return jnp.concatenate([jnp.where(row_group == g0 + gg, pc, 0.0)
                                for gg in range(SSM_BLOCK_GROUPS)], axis=1).astype(BF16)

    @pl.when(u == 0)
    def _():
        row_group = lax.broadcasted_iota(jnp.int32, (hw, 1), 0) // SSM_GROUP
        for hb in range(2):
            for b in range(SSM_CHUNK):
                kc = kc_ref[hb, SSM_CHUNK - 1 - b]
                tiled = jnp.broadcast_to(kc[None], (SSM_HALF_GROUPS,) + kc.shape).reshape(hw, hw)
                kst_ref[hb, b * hw:(b + 1) * hw, :] = jnp.where(row_group == lane_group, tiled, 0.0).astype(BF16)
            kst_ref[hb, SSM_CHUNK * hw:, :] = jnp.zeros((kst_ref.shape[1] - SSM_CHUNK * hw, hw), BF16)
        v_ref[...] = _dot(x_ref[0], p_tile(0))

    @pl.when(u < n_state_blocks)
    def _():
        rowc = lax.broadcasted_iota(jnp.int32, (rows, 1), 0) % chunks_per_seq
        hst = v_ref[...]
        nxt = jnp.minimum(u + 1, n_state_blocks - 1)
        v_next = _dot(x_ref[nxt // blocks_per_half], p_tile(nxt))
        k, si = 1, 0
        while k < chunks_per_seq:
            hs = jnp.where(rowc >= k, pltpu.roll(hst, k, 0), 0.0)
            sw = jnp.concatenate([pltpu.roll(hs[:, b0:b0 + ns], SSM_STATE, 1)
                                  for b0 in range(0, bw, ns)], axis=1)
            hst = hst + ar_ref[u, si:si + 1, :] * hs + ai_ref[u, si:si + 1, :] * sw
            k *= 2
            si += 1
        hp_ref[u] = jnp.where(rowc >= 1, pltpu.roll(hst, 1, 0), 0.0).astype(BF16)
        v_ref[...] = v_next

    @pl.when(u >= n_state_blocks)
    def _():
        ob = u - n_state_blocks
        hb = ob // SSM_CHUNK
        t = ob % SSM_CHUNK
        qc = qc_ref[hb, t]
        grp = lax.broadcasted_iota(jnp.int32, (SSM_HALF_GROUPS, 1, 1), 0)
        q_tile = jnp.where(grp == lane_group[None], qc[None], 0.0).reshape(SSM_HALF_GROUPS * ns, hw)
        q_tile = q_tile.astype(BF16)
        yq = _dot(hp_ref[hb * blocks_per_half], q_tile[0:bw, :])
        for kk in range(1, blocks_per_half):
            yq = yq + _dot(hp_ref[hb * blocks_per_half + kk], q_tile[kk * bw:(kk + 1) * bw, :])
        start = pl.multiple_of((SSM_CHUNK - 1 - t) * hw, hw)
        for v in range(SSM_CHUNK // SSM_TSTEP):
            kext = (v + 1) * SSM_TSTEP * hw

            @pl.when(t // SSM_TSTEP == v)
            def _():
                intra = _dot(x_ref[hb, :, 0:kext], kst_ref[hb, pl.ds(start, kext), :])
                o_ref[...] = (yq + intra).astype(BF16)


def _ssm_core(xcat, kc, pc, qc, ar, ai, layer, chunks_per_seq):
    _, rows, xw = xcat.shape
    nb = ar.shape[1]
    bw = ar.shape[3]
    hw = SSM_HALF_GROUPS * SSM_GROUP
    nout = 2 * SSM_CHUNK
    one = pl.Buffered(1)
    whole = lambda a: pl.BlockSpec((None,) + a.shape[1:], lambda u: (layer,) + (0,) * (a.ndim - 1),
                                   pipeline_mode=one)
    return pl.pallas_call(
        functools.partial(_ssm_body, chunks_per_seq=chunks_per_seq, n_state_blocks=nb),
        grid=(nb + nout,),
        in_specs=[
            pl.BlockSpec((2, rows, xw), lambda u: (0, 0, 0), pipeline_mode=one),
            whole(kc), whole(pc), whole(qc), whole(ar), whole(ai),
        ],
        out_specs=pl.BlockSpec((None, rows, hw), lambda u: (jnp.maximum(u - nb, 0), 0, 0)),
        out_shape=jax.ShapeDtypeStruct((nout, rows, hw), BF16),
        scratch_shapes=[
            pltpu.VMEM((nb, rows, bw), BF16),
            pltpu.VMEM((2, (SSM_CHUNK + SSM_TSTEP - 1) * hw, hw), BF16),
            pltpu.VMEM((rows, bw), F32),
        ],
        compiler_params=_cparams(("arbitrary",)),
        name="ssm_core",
    )(xcat, kc, pc, qc, ar, ai)


def _ssm_operators(a_re, a_im, log_dt, b_re, b_im, c_re, c_im, chunks_per_seq):
    t = SSM_CHUNK
    g, p = a_re.shape
    hg = b_re.shape[-1]
    hgc = SSM_HALF_GROUPS
    bgc = SSM_BLOCK_GROUPS
    dt = jnp.exp(log_dt)[:, None]
    lam_re, lam_im = dt * a_re, dt * a_im

    def powers(n):
        n = jnp.asarray(n, F32)[:, None, None]
        mag = jnp.exp(n * lam_re)
        return mag * jnp.cos(n * lam_im), mag * jnp.sin(n * lam_im)

    ab_re, ab_im = powers([1.0])
    ab_re, ab_im = ab_re[0], ab_im[0]
    den = a_re * a_re + a_im * a_im
    f_re = ((ab_re - 1.0) * a_re + ab_im * a_im) / den
    f_im = (ab_im * a_re - (ab_re - 1.0) * a_im) / den
    bb_re = f_re[..., None] * b_re - f_im[..., None] * b_im
    bb_im = f_re[..., None] * b_im + f_im[..., None] * b_re
    pw_re, pw_im = powers(np.arange(t + 1))
    cp_re = c_re[None] * pw_re[:, :, None, :] - c_im[None] * pw_im[:, :, None, :]
    cp_im = c_re[None] * pw_im[:, :, None, :] + c_im[None] * pw_re[:, :, None, :]
    ktau = (jnp.einsum('tghp,gpk->tkgh', cp_re[:t], bb_re)
            - jnp.einsum('tghp,gpk->tkgh', cp_im[:t], bb_im))
    kc = ktau.reshape(t, hg, 2, hgc * hg).transpose(2, 0, 1, 3)
    bt_re, bt_im = bb_re.transpose(0, 2, 1), bb_im.transpose(0, 2, 1)
    rp_re = pw_re[t - 1 - np.arange(t)][:, :, None, :]
    rp_im = pw_im[t - 1 - np.arange(t)][:, :, None, :]
    pin = jnp.concatenate([rp_re * bt_re[None] - rp_im * bt_im[None],
                           rp_re * bt_im[None] + rp_im * bt_re[None]], axis=3)
    pc = pin.reshape(t, 2, hgc, hg, 2 * p).transpose(1, 0, 2, 3, 4).reshape(2, t * hgc * hg, 2 * p)
    ct_re, ct_im = c_re.transpose(2, 0, 1)[None], c_im.transpose(2, 0, 1)[None]
    pt_re = pw_re[1:].transpose(0, 2, 1)[..., None]
    pt_im = pw_im[1:].transpose(0, 2, 1)[..., None]
    qg = jnp.concatenate([ct_re * pt_re - ct_im * pt_im,
                          -(ct_re * pt_im + ct_im * pt_re)], axis=1)
    qc = qg.reshape(t, 2 * p, 2, hgc * hg).transpose(2, 0, 1, 3)
    nblk = g // bgc
    nsteps = int(np.log2(chunks_per_seq))
    sr, si = powers(t * 2.0 ** np.arange(nsteps))
    blk = lambda a: a.reshape(nsteps, nblk, bgc * 2 * p).transpose(1, 0, 2)
    ar = blk(jnp.concatenate([sr, sr], axis=2))
    ai = blk(jnp.concatenate([-si, si], axis=2))
    return kc, pc, qc, ar, ai


def _merge_body(x_ref, yp_ref, yg_ref, yc_ref, sin_ref, g0_ref, g1_ref, g2_ref,
                d_ref, wglu_ref, wbp_ref, wbg0_ref, wbg1_ref, wbs_ref, wo_ref, o_ref, ys_ref):
    per_half = yc_ref.shape[2] // LANES
    rows = yc_ref.shape[1]
    for ob in range(yc_ref.shape[0]):
        hb, t = divmod(ob, SSM_CHUNK)
        plane = yc_ref[ob].astype(F32)
        for c in range(per_half):
            ys_ref[hb * per_half + c, pl.ds(t, rows, stride=SSM_CHUNK), :] = plane[:, c * LANES:(c + 1) * LANES]
    ys = jnp.concatenate([ys_ref[c] for c in range(ys_ref.shape[0])], axis=1)
    y = ys + d_ref[...] * sin_ref[...].astype(F32)
    y = jax.nn.gelu(y)
    y = y * jax.nn.sigmoid(_dot(y.astype(BF16), wglu_ref[...]))
    m = g0_ref[...].astype(F32) * _dot(yp_ref[...], wbp_ref[...])
    half = wbg0_ref.shape[0]
    gla_proj = _dot(yg_ref[:, :half], wbg0_ref[...]) + _dot(yg_ref[:, half:], wbg1_ref[...])
    m = m + g1_ref[...].astype(F32) * gla_proj
    m = m + g2_ref[...].astype(F32) * _dot(y.astype(BF16), wbs_ref[...])
    o_ref[...] = x_ref[...] + _dot(m.astype(BF16), wo_ref[...])


def _merge(x, y_pool, y_gla, ycat, z, ssm_col0, gate_col0, d_skip, w_glu, wb, w_out, layer, *, tm=256):
    m, d = x.shape
    wp, wg, ws = y_pool.shape[1], y_gla.shape[1], 2 * ycat.shape[2]
    br = BRANCH_ROWS
    assert wp == br and ws == br and wg == 2 * br
    gb = gate_col0 // d
    one = pl.Buffered(1)
    wspec = lambda shape, r: pl.BlockSpec((None,) + shape, lambda i: (layer, r, 0), pipeline_mode=one)
    return pl.pallas_call(
        _merge_body,
        grid=(m // tm,),
        in_specs=[
            pl.BlockSpec((tm, d), lambda i: (i, 0)),
            pl.BlockSpec((tm, wp), lambda i: (i, 0)),
            pl.BlockSpec((tm, wg), lambda i: (i, 0)),
            pl.BlockSpec((ycat.shape[0], tm // SSM_CHUNK, ws // 2), lambda i: (0, i, 0)),
            pl.BlockSpec((tm, ws), lambda i: (i, ssm_col0 // ws)),
            pl.BlockSpec((tm, d), lambda i: (i, gb)),
            pl.BlockSpec((tm, d), lambda i: (i, gb + 1)),
            pl.BlockSpec((tm, d), lambda i: (i, gb + 2)),
            wspec((1, ws), 0),
            wspec((ws, ws), 0),
            wspec((br, d), 0),
            wspec((br, d), 1),
            wspec((br, d), 2),
            wspec((br, d), 3),
            wspec((d, d), 0),
        ],
        out_specs=pl.BlockSpec((tm, d), lambda i: (i, 0)),
        out_shape=jax.ShapeDtypeStruct((m, d), F32),
        scratch_shapes=[pltpu.VMEM((ws // LANES, tm, LANES), F32)],
        compiler_params=_cparams(("parallel",)),
        name="merge",
    )(x, y_pool, y_gla, ycat, z, z, z, z, d_skip, w_glu, wb, wb, wb, wb, w_out)


def kernel(x, ffn1_norm, ffn1_w_gate, ffn1_w_up, ffn1_w_down, mix_norm, w_in, pool_w, pool_scale, gla_w_gate2, gla_gate_bias, gla_norm, ssm_a_re, ssm_a_im, ssm_log_dt, ssm_b_re, ssm_b_im, ssm_c_re, ssm_c_im, ssm_d, ssm_w_glu, w_branch, w_out, ffn2_norm, ffn2_w_gate, ffn2_w_up, ffn2_w_down, final_norm):
    batch, seq, d = x.shape
    depth = w_in.shape[0]
    m = batch * seq
    pool_width = pool_scale.shape[1]
    qk_width = GLA_HEADS * GLA_DK
    v_width = GLA_HEADS * GLA_DV
    ssm_width = ssm_d.shape[1]
    n_groups = ssm_width // SSM_GROUP
    q0 = pool_width
    k0 = q0 + qk_width
    v0 = k0 + qk_width
    r0 = v0 + v_width
    glr0 = r0 + v_width
    ssm_src0 = glr0 + GLA_RANK
    ssm0 = glr0
    gate0 = ssm0 + ssm_width
    chunks_per_seq = seq // SSM_CHUNK

    bf = lambda a: a.astype(BF16)
    row3 = lambda a: a.reshape(depth, 1, a.shape[-1])
    w_main, w_glr = _repack_win(w_in, glr0, GLA_RANK)
    f1g, f1u, f1d = bf(ffn1_w_gate), bf(ffn1_w_up), bf(ffn1_w_down)
    f2g, f2u, f2d = bf(ffn2_w_gate), bf(ffn2_w_up), bf(ffn2_w_down)
    wbr, wo, wglu, pw = bf(w_branch), bf(w_out), bf(ssm_w_glu), bf(pool_w)
    n1, nm, n2 = row3(ffn1_norm), row3(mix_norm), row3(ffn2_norm)
    pscale, dskip = row3(pool_scale), row3(ssm_d)
    w2 = gla_w_gate2.reshape(depth, GLA_RANK, GLA_HEADS, GLA_DK).transpose(0, 2, 1, 3)
    w2 = bf(jnp.pad(w2, ((0, 0), (0, 0), (0, LANES - GLA_RANK), (0, 0))))
    gbias = gla_gate_bias.reshape(depth, GLA_HEADS, 1, GLA_DK)
    ggain = gla_norm.reshape(depth, GLA_HEADS, 1, GLA_DV)
    ssm_ops = jax.vmap(functools.partial(_ssm_operators, chunks_per_seq=chunks_per_seq))(
        ssm_a_re, ssm_a_im, ssm_log_dt, ssm_b_re, ssm_b_im, ssm_c_re, ssm_c_im)

    xf = x.reshape(m, d)
    for l in range(depth):
        xf = _ffn(xf, n1, f1g, f1u, f1d, l)
        z, glr, xcat = _inproj(xf, nm, w_main, w_glr, l, ssm0, ssm_width, gate0)
        y_pool = _pool(z, pw, pscale, l, batch, seq)
        y_gla = _gla(z, glr, w2, gbias, ggain, l, batch, seq, (q0, k0, v0, r0))
        ycat = _ssm_core(xcat, *ssm_ops, l, chunks_per_seq)
        xf = _merge(xf, y_pool, y_gla, ycat, z, ssm0, gate0, dskip, wglu, wbr, wo, l)
        xf = _ffn(xf, n2, f2g, f2u, f2d, l, final_norm if l == depth - 1 else None)
    return xf.reshape(batch, seq, d)
```

```python
import functools

import jax
import jax.numpy as jnp
import numpy as np
from jax import lax
from jax.experimental import pallas as pl
from jax.experimental.pallas import tpu as pltpu

F32 = jnp.float32
BF16 = jnp.bfloat16

EPS = 1e-6
POOL_WINDOWS = (2, 4, 8, 16)
POOL_GROUP = 128
POOL_HALO = 16
GLA_HEADS = 4
GLA_DK = 128
GLA_DV = 256
GLA_RANK = 16
GLA_TAU = 16.0
GLA_CHUNK = 64
SSM_GROUP = 16
SSM_STATE = 64
SSM_CHUNK = 16
LANES = 128
FFN_DOWN_CHUNK = 512
FFN_ROW_CHUNK = 512
FFN_EPILOGUE_ROWS = 256
BRANCH_ROWS = 512

VMEM_LIMIT = 60 * 1024 * 1024


def _cparams(sem):
    return pltpu.CompilerParams(dimension_semantics=sem, vmem_limit_bytes=VMEM_LIMIT)


def _rms(xf, g):
    return xf * lax.rsqrt(jnp.mean(xf * xf, axis=-1, keepdims=True) + EPS) * g


def _dot(a, b):
    return jnp.dot(a, b, preferred_element_type=F32)


def _dot_nt(a, b_t):
    return lax.dot_general(a, b_t, (((1,), (1,)), ((), ())), preferred_element_type=F32)


def _ffn_body(x_ref, g_ref, wg_ref, wu_ref, wd_ref, *rest, final):
    if final:
        fg_ref, o_ref, xn_ref = rest
    else:
        o_ref, xn_ref = rest
    j = pl.program_id(1)

    row_chunks = [slice(r0, r0 + FFN_ROW_CHUNK) for r0 in range(0, o_ref.shape[0], FFN_ROW_CHUNK)]

    @pl.when(j == 0)
    def _():
        for rs in row_chunks:
            xn_ref[rs, :] = _rms(x_ref[rs, :], g_ref[...]).astype(BF16)
            o_ref[rs, :] = jnp.zeros((FFN_ROW_CHUNK, o_ref.shape[1]), F32)

    for rs in row_chunks:
        xn = xn_ref[rs, :]
        a = _dot(xn, wg_ref[...])
        b = _dot(xn, wu_ref[...])
        h = (jax.nn.silu(a) * b).astype(BF16)
        for c0 in range(0, o_ref.shape[1], FFN_DOWN_CHUNK):
            cs = slice(c0, c0 + FFN_DOWN_CHUNK)
            o_ref[rs, cs] += _dot(h, wd_ref[:, cs])

    @pl.when(j == pl.num_programs(1) - 1)
    def _():
        def epilogue(r, carry):
            rs = pl.ds(pl.multiple_of(r * FFN_EPILOGUE_ROWS, FFN_EPILOGUE_ROWS), FFN_EPILOGUE_ROWS)
            y = x_ref[rs, :] + 0.5 * o_ref[rs, :]
            if final:
                y = _rms(y, fg_ref[...])
            o_ref[rs, :] = y
            return carry
        lax.fori_loop(0, o_ref.shape[0] // FFN_EPILOGUE_ROWS, epilogue, 0)


def _ffn(x, g, wg, wu, wd, layer, final_g=None, *, tm=1024, tf=512):
    m, d = x.shape
    f = wg.shape[2]
    final = final_g is not None
    in_specs = [
        pl.BlockSpec((tm, d), lambda i, j: (i, 0)),
        pl.BlockSpec((None, 1, d), lambda i, j: (layer, 0, 0)),
        pl.BlockSpec((None, d, tf), lambda i, j: (layer, 0, j)),
        pl.BlockSpec((None, d, tf), lambda i, j: (layer, 0, j)),
        pl.BlockSpec((None, tf, d), lambda i, j: (layer, j, 0)),
    ]
    args = [x, g, wg, wu, wd]
    if final:
        in_specs.append(pl.BlockSpec((1, d), lambda i, j: (0, 0)))
        args.append(final_g.reshape(1, d))
    return pl.pallas_call(
        functools.partial(_ffn_body, final=final),
        grid=(m // tm, f // tf),
        in_specs=in_specs,
        out_specs=pl.BlockSpec((tm, d), lambda i, j: (i, 0)),
        out_shape=jax.ShapeDtypeStruct((m, d), F32),
        scratch_shapes=[pltpu.VMEM((tm, d), BF16)],
        compiler_params=_cparams(("parallel", "arbitrary")),
        name="ffn_final" if final else "ffn",
    )(*args)


def _repack_body(a_ref, b_ref, w_ref, wl_ref):
    w_ref[...] = a_ref[0].astype(BF16)
    pad = jnp.zeros((wl_ref.shape[0] - b_ref.shape[1], b_ref.shape[2]), F32)
    wl_ref[...] = jnp.concatenate([b_ref[0], pad], axis=0).astype(BF16)


def _repack_win(w_in_t, drop0, drop_width, *, td=1024, tn=512):
    depth, n_src, d = w_in_t.shape
    n = n_src - drop_width
    assert drop0 % tn == 0 and n % tn == 0 and drop_width % 8 == 0
    first_shifted = drop0 // tn
    src_row = lambda j: (j * (tn // 8) + jnp.where(j >= first_shifted, drop_width // 8, 0)) * 8
    return pl.pallas_call(
        _repack_body,
        grid=(depth, d // td, n // tn),
        in_specs=[
            pl.BlockSpec((pl.Element(1), pl.Element(tn), pl.Element(td)),
                         lambda l, i, j: (l, src_row(j), i * td)),
            pl.BlockSpec((pl.Element(1), pl.Element(drop_width), pl.Element(td)),
                         lambda l, i, j: (l, drop0, i * td)),
        ],
        out_specs=[
            pl.BlockSpec((None, tn, td), lambda l, i, j: (l, j, i)),
            pl.BlockSpec((None, LANES, td), lambda l, i, j: (l, 0, i)),
        ],
        out_shape=[
            jax.ShapeDtypeStruct((depth, n, d), BF16),
            jax.ShapeDtypeStruct((depth, LANES, d), BF16),
        ],
        compiler_params=_cparams(("parallel", "parallel", "arbitrary")),
        name="repack_win",
    )(w_in_t, w_in_t)


def _inproj_body(x_ref, g_ref, w_ref, wl_ref, z_ref, glr_ref, xc_ref, un_ref, ph_ref, *,
                 gate_block0, ssm_block, ssm_off):
    j = pl.program_id(1)

    @pl.when(j == 0)
    def _():
        un = _rms(x_ref[...], g_ref[...]).astype(BF16)
        un_ref[...] = un
        glr_ref[...] = _dot_nt(un, wl_ref[...])

    acc = _dot_nt(un_ref[...], w_ref[...])
    z_ref[...] = jnp.where(j >= gate_block0, jax.nn.sigmoid(acc), acc).astype(BF16)

    @pl.when(j == ssm_block)
    def _():
        ncol = ph_ref.shape[0]
        per_half = ncol // 2
        half = per_half * LANES
        for c in range(ncol):
            ph_ref[c] = acc[:, ssm_off + c * LANES:ssm_off + (c + 1) * LANES]
        rows = ph_ref.shape[1] // SSM_CHUNK
        for s in range(SSM_CHUNK):
            for c in range(ncol):
                piece = ph_ref[c, pl.ds(s, rows, stride=SSM_CHUNK), :].astype(BF16)
                l0 = s * half + (c % per_half) * LANES
                xc_ref[c // per_half, :, l0:l0 + LANES] = piece


def _inproj(x, g, w_main, w_glr, layer, ssm_col0, ssm_width, gate_col0, *, tm=1024, tn=1024):
    m, d = x.shape
    n = w_main.shape[1]
    body = functools.partial(_inproj_body, gate_block0=gate_col0 // tn,
                             ssm_block=ssm_col0 // tn, ssm_off=ssm_col0 % tn)
    return pl.pallas_call(
        body,
        grid=(m // tm, n // tn),
        in_specs=[
            pl.BlockSpec((tm, d), lambda i, j: (i, 0)),
            pl.BlockSpec((None, 1, d), lambda i, j: (layer, 0, 0)),
            pl.BlockSpec((None, tn, d), lambda i, j: (layer, j, 0)),
            pl.BlockSpec((None, LANES, d), lambda i, j: (layer, 0, 0)),
        ],
        out_specs=[
            pl.BlockSpec((tm, tn), lambda i, j: (i, j)),
            pl.BlockSpec((tm, LANES), lambda i, j: (i, 0)),
            pl.BlockSpec((2, tm // SSM_CHUNK, SSM_CHUNK * ssm_width // 2), lambda i, j: (0, i, 0)),
        ],
        out_shape=[
            jax.ShapeDtypeStruct((m, n), BF16),
            jax.ShapeDtypeStruct((m, LANES), F32),
            jax.ShapeDtypeStruct((2, m // SSM_CHUNK, SSM_CHUNK * ssm_width // 2), BF16),
        ],
        scratch_shapes=[pltpu.VMEM((tm, d), BF16), pltpu.VMEM((ssm_width // LANES, tm, LANES), F32)],
        compiler_params=_cparams(("parallel", "arbitrary")),
        name="inproj",
    )(x, g, w_main, w_glr)


def _pool_body(p_ref, halo_ref, w_ref, sc_ref, o_ref, *, ts):
    t = pl.program_id(1)
    x = p_ref[...].astype(F32)
    halo = jnp.where(t > 0, halo_ref[...].astype(F32), 0.0)
    e = jnp.concatenate([halo, x], axis=0)
    s2 = e + pltpu.roll(e, 1, 0)
    s4 = s2 + pltpu.roll(s2, 2, 0)
    s8 = s4 + pltpu.roll(s4, 4, 0)
    s16 = s8 + pltpu.roll(s8, 8, 0)
    pos = (t * ts + 1 + lax.broadcasted_iota(jnp.int32, (ts, 1), 0)).astype(F32)
    outs = []
    for gi, (w, s) in enumerate(zip(POOL_WINDOWS, (s2, s4, s8, s16))):
        cols = slice(gi * POOL_GROUP, (gi + 1) * POOL_GROUP)
        mean = s[POOL_HALO:, cols] / jnp.minimum(pos, float(w))
        outs.append(_dot((mean - x[:, cols]).astype(BF16), w_ref[gi]))
    o_ref[...] = (jnp.concatenate(outs, axis=1) * sc_ref[...]).astype(BF16)


def _pool(z, w, scale, layer, batch, seq, *, ts=512):
    width = len(POOL_WINDOWS) * POOL_GROUP
    nt = seq // ts
    hb = ts // POOL_HALO
    return pl.pallas_call(
        functools.partial(_pool_body, ts=ts),
        grid=(batch, nt),
        in_specs=[
            pl.BlockSpec((ts, width), lambda b, t: (b * nt + t, 0)),
            pl.BlockSpec((POOL_HALO, width),
                         lambda b, t: (jnp.maximum((b * nt + t) * hb - 1, 0), 0)),
            pl.BlockSpec((None, len(POOL_WINDOWS), POOL_GROUP, POOL_GROUP),
                         lambda b, t: (layer, 0, 0, 0)),
            pl.BlockSpec((None, 1, width), lambda b, t: (layer, 0, 0)),
        ],
        out_specs=pl.BlockSpec((ts, width), lambda b, t: (b * nt + t, 0)),
        out_shape=jax.ShapeDtypeStruct((batch * seq, width), BF16),
        compiler_params=_cparams(("parallel", "arbitrary")),
        name="pool",
    )(z, z, w, scale)


def _gla_head(q, k, v, r, glr, w2, bias, gain, st, ts):
    c = GLA_CHUNK
    nc = ts // c
    logits = _dot(glr, w2) + bias
    log_a = (jnp.minimum(logits, 0.0) - jnp.log1p(jnp.exp(-jnp.abs(logits)))) * (1.0 / GLA_TAU)
    rowc = lax.broadcasted_iota(jnp.int32, (ts, 1), 0) % c
    bc = log_a
    step = 1
    while step < c:
        bc = bc + jnp.where(rowc >= step, pltpu.roll(bc, step, 0), 0.0)
        step *= 2
    bc3 = bc.reshape(nc, c, GLA_DK)
    b_last = bc3[:, c - 1:c, :]
    b_mid = bc3[:, c // 2 - 1:c // 2, :]
    q3 = q.astype(F32).reshape(nc, c, GLA_DK) * (GLA_DK ** -0.5)
    k3 = k.astype(F32).reshape(nc, c, GLA_DK)
    v3 = v.reshape(nc, c, GLA_DV)
    qe = (q3 * jnp.exp(bc3 - b_mid)).astype(BF16)
    ke = (k3 * jnp.exp(b_mid - bc3)).astype(BF16)
    scores = jnp.einsum('nid,njd->nij', qe, ke, preferred_element_type=F32)
    causal = (lax.broadcasted_iota(jnp.int32, (c, c), 0)
              >= lax.broadcasted_iota(jnp.int32, (c, c), 1))
    scores = jnp.where(causal[None], scores, 0.0).astype(BF16)
    o_intra = jnp.einsum('nij,njv->niv', scores, v3, preferred_element_type=F32)
    qd = (q3 * jnp.exp(bc3)).astype(BF16)
    kd = (k3 * jnp.exp(b_last - bc3)).astype(BF16)
    dec = jnp.exp(b_last)
    outs = []
    for n in range(nc):
        o_inter = lax.dot_general(qd[n], st.astype(BF16), (((1,), (1,)), ((), ())),
                                  preferred_element_type=F32)
        outs.append(o_intra[n] + o_inter)
        kv_t = lax.dot_general(v3[n], kd[n], (((0,), (0,)), ((), ())),
                               preferred_element_type=F32)
        st = dec[n] * st + kv_t
    o = jnp.concatenate(outs, axis=0)
    o = o * lax.rsqrt(jnp.mean(o * o, axis=-1, keepdims=True) + EPS) * gain
    return (o * jax.nn.silu(r.astype(F32))).astype(BF16), st


def _gla_body(q_ref, k_ref, v0_ref, v1_ref, r0_ref, r1_ref, glr_ref, w2_ref, b_ref, gain_ref,
              o_ref, st_ref, *, ts):
    @pl.when(pl.program_id(1) == 0)
    def _():
        st_ref[...] = jnp.zeros_like(st_ref)

    glr = glr_ref[...].astype(BF16)
    vr_refs = ((v0_ref, r0_ref), (v1_ref, r1_ref))
    heads_per_block = v0_ref.shape[1] // GLA_DV
    for h in range(GLA_HEADS):
        v_ref, r_ref = vr_refs[h // heads_per_block]
        vs = slice((h % heads_per_block) * GLA_DV, (h % heads_per_block + 1) * GLA_DV)
        ks = slice(h * GLA_DK, (h + 1) * GLA_DK)
        o, st = _gla_head(q_ref[:, ks], k_ref[:, ks], v_ref[:, vs], r_ref[:, vs], glr,
                          w2_ref[h], b_ref[h], gain_ref[h], st_ref[h], ts)
        st_ref[h] = st
        o_ref[:, h * GLA_DV:(h + 1) * GLA_DV] = o


def _gla(z, glr, w2, bias, gain, layer, batch, seq, cols, *, ts=512):
    h = GLA_HEADS
    nt = seq // ts
    q0, k0, v0, r0 = cols
    wb = h * GLA_DK
    zspec = lambda c0: pl.BlockSpec((ts, wb), lambda b, t: (b * nt + t, c0 // wb))
    wspec = lambda last2: pl.BlockSpec((None, h) + last2, lambda b, t: (layer, 0, 0, 0))
    return pl.pallas_call(
        functools.partial(_gla_body, ts=ts),
        grid=(batch, nt),
        in_specs=[
            zspec(q0), zspec(k0), zspec(v0), zspec(v0 + wb), zspec(r0), zspec(r0 + wb),
            pl.BlockSpec((ts, LANES), lambda b, t: (b * nt + t, 0)),
            wspec((LANES, GLA_DK)), wspec((1, GLA_DK)), wspec((1, GLA_DV)),
        ],
        out_specs=pl.BlockSpec((ts, h * GLA_DV), lambda b, t: (b * nt + t, 0)),
        out_shape=jax.ShapeDtypeStruct((batch * seq, h * GLA_DV), BF16),
        scratch_shapes=[pltpu.VMEM((h, GLA_DV, GLA_DK), F32)],
        compiler_params=_cparams(("parallel", "arbitrary")),
        name="gla",
    )(z, z, z, z, z, z, glr, w2, bias, gain)


SSM_HALF_GROUPS = 16
SSM_BLOCK_GROUPS = 2
SSM_TSTEP = 4


def _ssm_body(x_ref, kc_ref, pc_ref, qc_ref, ar_ref, ai_ref, o_ref, hp_ref, kst_ref, v_ref, *,
              chunks_per_seq, n_state_blocks):
    u = pl.program_id(0)
    rows = x_ref.shape[1]
    hw = SSM_HALF_GROUPS * SSM_GROUP
    blocks_per_half = n_state_blocks // 2
    ns = 2 * SSM_STATE
    bw = SSM_BLOCK_GROUPS * ns
    lane_group = lax.broadcasted_iota(jnp.int32, (1, hw), 1) // SSM_GROUP

    def p_tile(n):
        pc = pc_ref[n // blocks_per_half]
        row_group = (lax.broadcasted_iota(jnp.int32, (pc.shape[0], 1), 0) // SSM_GROUP) % SSM_HALF_GROUPS
        g0 = (n % blocks_per_half) * SSM_BLOCK_GROUPS
        return jnp.concatenate([jnp.where(row_group == g0 + gg, pc, 0.0)
                                for gg in range(SSM_BLOCK_GROUPS)], axis=1).astype(BF16)

    @pl.when(u == 0)
    def _():
        row_group = lax.broadcasted_iota(jnp.int32, (hw, 1), 0) // SSM_GROUP
        for hb in range(2):
            for b in range(SSM_CHUNK):
                kc = kc_ref[hb, SSM_CHUNK - 1 - b]
                tiled = jnp.broadcast_to(kc[None], (SSM_HALF_GROUPS,) + kc.shape).reshape(hw, hw)
                kst_ref[hb, b * hw:(b + 1) * hw, :] = jnp.where(row_group == lane_group, tiled, 0.0).astype(BF16)
            kst_ref[hb, SSM_CHUNK * hw:, :] = jnp.zeros((kst_ref.shape[1] - SSM_CHUNK * hw, hw), BF16)
        v_ref[...] = _dot(x_ref[0], p_tile(0))

    @pl.when(u < n_state_blocks)
    def _():
        rowc = lax.broadcasted_iota(jnp.int32, (rows, 1), 0) % chunks_per_seq
        hst = v_ref[...]
        nxt = jnp.minimum(u + 1, n_state_blocks - 1)
        v_next = _dot(x_ref[nxt // blocks_per_half], p_tile(nxt))
        k, si = 1, 0
        while k < chunks_per_seq:
            hs = jnp.where(rowc >= k, pltpu.roll(hst, k, 0), 0.0)
            sw = jnp.concatenate([pltpu.roll(hs[:, b0:b0 + ns], SSM_STATE, 1)
                                  for b0 in range(0, bw, ns)], axis=1)
            hst = hst + ar_ref[u, si:si + 1, :] * hs + ai_ref[u, si:si + 1, :] * sw
            k *= 2
            si += 1
        hp_ref[u] = jnp.where(rowc >= 1, pltpu.roll(hst, 1, 0), 0.0).astype(BF16)
        v_ref[...] = v_next

    @pl.when(u >= n_state_blocks)
    def _():
        ob = u - n_state_blocks
        hb = ob // SSM_CHUNK
        t = ob % SSM_CHUNK
        qc = qc_ref[hb, t]
        grp = lax.broadcasted_iota(jnp.int32, (SSM_HALF_GROUPS, 1, 1), 0)
        q_tile = jnp.where(grp == lane_group[None], qc[None], 0.0).reshape(SSM_HALF_GROUPS * ns, hw)
        q_tile = q_tile.astype(BF16)
        yq = _dot(hp_ref[hb * blocks_per_half], q_tile[0:bw, :])
        for kk in range(1, blocks_per_half):
            yq = yq + _dot(hp_ref[hb * blocks_per_half + kk], q_tile[kk * bw:(kk + 1) * bw, :])
        start = pl.multiple_of((SSM_CHUNK - 1 - t) * hw, hw)
        for v in range(SSM_CHUNK // SSM_TSTEP):
            kext = (v + 1) * SSM_TSTEP * hw

            @pl.when(t // SSM_TSTEP == v)
            def _():
                intra = _dot(x_ref[hb, :, 0:kext], kst_ref[hb, pl.ds(start, kext), :])
                o_ref[...] = (yq + intra).astype(BF16)


def _ssm_core(xcat, kc, pc, qc, ar, ai, layer, chunks_per_seq):
    _, rows, xw = xcat.shape
    nb = ar.shape[1]
    bw = ar.shape[3]
    hw = SSM_HALF_GROUPS * SSM_GROUP
    nout = 2 * SSM_CHUNK
    one = pl.Buffered(1)
    whole = lambda a: pl.BlockSpec((None,) + a.shape[1:], lambda u: (layer,) + (0,) * (a.ndim - 1),
                                   pipeline_mode=one)
    return pl.pallas_call(
        functools.partial(_ssm_body, chunks_per_seq=chunks_per_seq, n_state_blocks=nb),
        grid=(nb + nout,),
        in_specs=[
            pl.BlockSpec((2, rows, xw), lambda u: (0, 0, 0), pipeline_mode=one),
            whole(kc), whole(pc), whole(qc), whole(ar), whole(ai),
        ],
        out_specs=pl.BlockSpec((None, rows, hw), lambda u: (jnp.maximum(u - nb, 0), 0, 0)),
        out_shape=jax.ShapeDtypeStruct((nout, rows, hw), BF16),
        scratch_shapes=[
            pltpu.VMEM((nb, rows, bw), BF16),
            pltpu.VMEM((2, (SSM_CHUNK + SSM_TSTEP - 1) * hw, hw), BF16),
            pltpu.VMEM((rows, bw), F32),
        ],
        compiler_params=_cparams(("arbitrary",)),
        name="ssm_core",
    )(xcat, kc, pc, qc, ar, ai)


def _ssm_operators(a_re, a_im, log_dt, b_re, b_im, c_re, c_im, chunks_per_seq):
    t = SSM_CHUNK
    g, p = a_re.shape
    hg = b_re.shape[-1]
    hgc = SSM_HALF_GROUPS
    bgc = SSM_BLOCK_GROUPS
    dt = jnp.exp(log_dt)[:, None]
    lam_re, lam_im = dt * a_re, dt * a_im

    def powers(n):
        n = jnp.asarray(n, F32)[:, None, None]
        mag = jnp.exp(n * lam_re)
        return mag * jnp.cos(n * lam_im), mag * jnp.sin(n * lam_im)

    ab_re, ab_im = powers([1.0])
    ab_re, ab_im = ab_re[0], ab_im[0]
    den = a_re * a_re + a_im * a_im
    f_re = ((ab_re - 1.0) * a_re + ab_im * a_im) / den
    f_im = (ab_im * a_re - (ab_re - 1.0) * a_im) / den
    bb_re = f_re[..., None] * b_re - f_im[..., None] * b_im
    bb_im = f_re[..., None] * b_im + f_im[..., None] * b_re
    pw_re, pw_im = powers(np.arange(t + 1))
    cp_re = c_re[None] * pw_re[:, :, None, :] - c_im[None] * pw_im[:, :, None, :]
    cp_im = c_re[None] * pw_im[:, :, None, :] + c_im[None] * pw_re[:, :, None, :]
    ktau = (jnp.einsum('tghp,gpk->tkgh', cp_re[:t], bb_re)
            - jnp.einsum('tghp,gpk->tkgh', cp_im[:t], bb_im))
    kc = ktau.reshape(t, hg, 2, hgc * hg).transpose(2, 0, 1, 3)
    bt_re, bt_im = bb_re.transpose(0, 2, 1), bb_im.transpose(0, 2, 1)
    rp_re = pw_re[t - 1 - np.arange(t)][:, :, None, :]
    rp_im = pw_im[t - 1 - np.arange(t)][:, :, None, :]
    pin = jnp.concatenate([rp_re * bt_re[None] - rp_im * bt_im[None],
                           rp_re * bt_im[None] + rp_im * bt_re[None]], axis=3)
    pc = pin.reshape(t, 2, hgc, hg, 2 * p).transpose(1, 0, 2, 3, 4).reshape(2, t * hgc * hg, 2 * p)
    ct_re, ct_im = c_re.transpose(2, 0, 1)[None], c_im.transpose(2, 0, 1)[None]
    pt_re = pw_re[1:].transpose(0, 2, 1)[..., None]
    pt_im = pw_im[1:].transpose(0, 2, 1)[..., None]
    qg = jnp.concatenate([ct_re * pt_re - ct_im * pt_im,
                          -(ct_re * pt_im + ct_im * pt_re)], axis=1)
    qc = qg.reshape(t, 2 * p, 2, hgc * hg).transpose(2, 0, 1, 3)
    nblk = g // bgc
    nsteps = int(np.log2(chunks_per_seq))
    sr, si = powers(t * 2.0 ** np.arange(nsteps))
    blk = lambda a: a.reshape(nsteps, nblk, bgc * 2 * p).transpose(1, 0, 2)
    ar = blk(jnp.concatenate([sr, sr], axis=2))
    ai = blk(jnp.concatenate([-si, si], axis=2))
    return kc, pc, qc, ar, ai


def _merge_body(x_ref, yp_ref, yg_ref, yc_ref, sin_ref, g0_ref, g1_ref, g2_ref,
                d_ref, wglu_ref, wbp_ref, wbg0_ref, wbg1_ref, wbs_ref, wo_ref, o_ref, ys_ref):
    per_half = yc_ref.shape[2] // LANES
    rows = yc_ref.shape[1]
    for ob in range(yc_ref.shape[0]):
        hb, t = divmod(ob, SSM_CHUNK)
        plane = yc_ref[ob].astype(F32)
        for c in range(per_half):
            ys_ref[hb * per_half + c, pl.ds(t, rows, stride=SSM_CHUNK), :] = plane[:, c * LANES:(c + 1) * LANES]
    ys = jnp.concatenate([ys_ref[c] for c in range(ys_ref.shape[0])], axis=1)
    y = ys + d_ref[...] * sin_ref[...].astype(F32)
    y = jax.nn.gelu(y)
    y = y * jax.nn.sigmoid(_dot(y.astype(BF16), wglu_ref[...]))
    m = g0_ref[...].astype(F32) * _dot(yp_ref[...], wbp_ref[...])
    half = wbg0_ref.shape[0]
    gla_proj = _dot(yg_ref[:, :half], wbg0_ref[...]) + _dot(yg_ref[:, half:], wbg1_ref[...])
    m = m + g1_ref[...].astype(F32) * gla_proj
    m = m + g2_ref[...].astype(F32) * _dot(y.astype(BF16), wbs_ref[...])
    o_ref[...] = x_ref[...] + _dot(m.astype(BF16), wo_ref[...])


def _merge(x, y_pool, y_gla, ycat, z, ssm_col0, gate_col0, d_skip, w_glu, wb, w_out, layer, *, tm=256):
    m, d = x.shape
    wp, wg, ws = y_pool.shape[1], y_gla.shape[1], 2 * ycat.shape[2]
    br = BRANCH_ROWS
    assert wp == br and ws == br and wg == 2 * br
    gb = gate_col0 // d
    one = pl.Buffered(1)
    wspec = lambda shape, r: pl.BlockSpec((None,) + shape, lambda i: (layer, r, 0), pipeline_mode=one)
    return pl.pallas_call(
        _merge_body,
        grid=(m // tm,),
        in_specs=[
            pl.BlockSpec((tm, d), lambda i: (i, 0)),
            pl.BlockSpec((tm, wp), lambda i: (i, 0)),
            pl.BlockSpec((tm, wg), lambda i: (i, 0)),
            pl.BlockSpec((ycat.shape[0], tm // SSM_CHUNK, ws // 2), lambda i: (0, i, 0)),
            pl.BlockSpec((tm, ws), lambda i: (i, ssm_col0 // ws)),
            pl.BlockSpec((tm, d), lambda i: (i, gb)),
            pl.BlockSpec((tm, d), lambda i: (i, gb + 1)),
            pl.BlockSpec((tm, d), lambda i: (i, gb + 2)),
            wspec((1, ws), 0),
            wspec((ws, ws), 0),
            wspec((br, d), 0),
            wspec((br, d), 1),
            wspec((br, d), 2),
            wspec((br, d), 3),
            wspec((d, d), 0),
        ],
        out_specs=pl.BlockSpec((tm, d), lambda i: (i, 0)),
        out_shape=jax.ShapeDtypeStruct((m, d), F32),
        scratch_shapes=[pltpu.VMEM((ws // LANES, tm, LANES), F32)],
        compiler_params=_cparams(("parallel",)),
        name="merge",
    )(x, y_pool, y_gla, ycat, z, z, z, z, d_skip, w_glu, wb, wb, wb, wb, w_out)


def kernel(x, ffn1_norm, ffn1_w_gate, ffn1_w_up, ffn1_w_down, mix_norm, w_in, pool_w, pool_scale, gla_w_gate2, gla_gate_bias, gla_norm, ssm_a_re, ssm_a_im, ssm_log_dt, ssm_b_re, ssm_b_im, ssm_c_re, ssm_c_im, ssm_d, ssm_w_glu, w_branch, w_out, ffn2_norm, ffn2_w_gate, ffn2_w_up, ffn2_w_down, final_norm):
    batch, seq, d = x.shape
    depth = w_in.shape[0]
    m = batch * seq
    pool_width = pool_scale.shape[1]
    qk_width = GLA_HEADS * GLA_DK
    v_width = GLA_HEADS * GLA_DV
    ssm_width = ssm_d.shape[1]
    n_groups = ssm_width // SSM_GROUP
    q0 = pool_width
    k0 = q0 + qk_width
    v0 = k0 + qk_width
    r0 = v0 + v_width
    glr0 = r0 + v_width
    ssm_src0 = glr0 + GLA_RANK
    ssm0 = glr0
    gate0 = ssm0 + ssm_width
    chunks_per_seq = seq // SSM_CHUNK

    bf = lambda a: a.astype(BF16)
    row3 = lambda a: a.reshape(depth, 1, a.shape[-1])
    w_main, w_glr = _repack_win(jnp.swapaxes(w_in, 1, 2), glr0, GLA_RANK)
    f1g, f1u, f1d = bf(ffn1_w_gate), bf(ffn1_w_up), bf(ffn1_w_down)
    f2g, f2u, f2d = bf(ffn2_w_gate), bf(ffn2_w_up), bf(ffn2_w_down)
    wbr, wo, wglu, pw = bf(w_branch), bf(w_out), bf(ssm_w_glu), bf(pool_w)
    n1, nm, n2 = row3(ffn1_norm), row3(mix_norm), row3(ffn2_norm)
    pscale, dskip = row3(pool_scale), row3(ssm_d)
    w2 = gla_w_gate2.reshape(depth, GLA_RANK, GLA_HEADS, GLA_DK).transpose(0, 2, 1, 3)
    w2 = bf(jnp.pad(w2, ((0, 0), (0, 0), (0, LANES - GLA_RANK), (0, 0))))
    gbias = gla_gate_bias.reshape(depth, GLA_HEADS, 1, GLA_DK)
    ggain = gla_norm.reshape(depth, GLA_HEADS, 1, GLA_DV)
    ssm_ops = jax.vmap(functools.partial(_ssm_operators, chunks_per_seq=chunks_per_seq))(
        ssm_a_re, ssm_a_im, ssm_log_dt, ssm_b_re, ssm_b_im, ssm_c_re, ssm_c_im)

    xf = x.reshape(m, d)
    for l in range(depth):
        xf = _ffn(xf, n1, f1g, f1u, f1d, l)
        z, glr, xcat = _inproj(xf, nm, w_main, w_glr, l, ssm0, ssm_width, gate0)
        y_pool = _pool(z, pw, pscale, l, batch, seq)
        y_gla = _gla(z, glr, w2, gbias, ggain, l, batch, seq, (q0, k0, v0, r0))
        ycat = _ssm_core(xcat, *ssm_ops, l, chunks_per_seq)
        xf = _merge(xf, y_pool, y_gla, ycat, z, ssm0, gate0, dskip, wglu, wbr, wo, l)
        xf = _ffn(xf, n2, f2g, f2u, f2d, l, final_norm if l == depth - 1 else None)
    return xf.reshape(batch, seq, d)
```

```python
import functools

import jax
import jax.numpy as jnp
import numpy as np
from jax import lax
from jax.experimental import pallas as pl
from jax.experimental.pallas import tpu as pltpu

F32 = jnp.float32
BF16 = jnp.bfloat16

EPS = 1e-6
POOL_WINDOWS = (2, 4, 8, 16)
POOL_GROUP = 128
POOL_HALO = 16
GLA_HEADS = 4
GLA_DK = 128
GLA_DV = 256
GLA_RANK = 16
GLA_TAU = 16.0
GLA_CHUNK = 64
SSM_GROUP = 16
SSM_STATE = 64
SSM_CHUNK = 16
LANES = 128
FFN_DOWN_CHUNK = 512
FFN_ROW_CHUNK = 1024
FFN_EPILOGUE_ROWS = 256
BRANCH_ROWS = 512

VMEM_LIMIT = 60 * 1024 * 1024


def _cparams(sem):
    return pltpu.CompilerParams(dimension_semantics=sem, vmem_limit_bytes=VMEM_LIMIT)


def _rms(xf, g):
    return xf * lax.rsqrt(jnp.mean(xf * xf, axis=-1, keepdims=True) + EPS) * g


def _dot(a, b):
    return jnp.dot(a, b, preferred_element_type=F32)


def _dot_nt(a, b_t):
    return lax.dot_general(a, b_t, (((1,), (1,)), ((), ())), preferred_element_type=F32)


def _ffn_body(x_ref, g_ref, wg_ref, wu_ref, wd_ref, *rest, final, n_cast):
    rest = list(rest)
    cast_in = [rest.pop(0) for _ in range(n_cast)]
    fg_ref = rest.pop(0) if final else None
    o_ref = rest.pop(0)
    cast_out = [rest.pop(0) for _ in range(n_cast)]
    xn_ref, = rest
    j = pl.program_id(1)

    for src, dst in zip(cast_in, cast_out):
        dst[...] = src[...].astype(BF16)

    row_chunks = [slice(r0, r0 + FFN_ROW_CHUNK) for r0 in range(0, o_ref.shape[0], FFN_ROW_CHUNK)]

    @pl.when(j == 0)
    def _():
        for rs in row_chunks:
            xn_ref[rs, :] = _rms(x_ref[rs, :], g_ref[...]).astype(BF16)
            o_ref[rs, :] = jnp.zeros((FFN_ROW_CHUNK, o_ref.shape[1]), F32)

    for rs in row_chunks:
        xn = xn_ref[rs, :]
        a = _dot(xn, wg_ref[...])
        b = _dot(xn, wu_ref[...])
        h = (jax.nn.silu(a) * b).astype(BF16)
        for c0 in range(0, o_ref.shape[1], FFN_DOWN_CHUNK):
            cs = slice(c0, c0 + FFN_DOWN_CHUNK)
            o_ref[rs, cs] += _dot(h, wd_ref[:, cs])

    @pl.when(j == pl.num_programs(1) - 1)
    def _():
        def epilogue(r, carry):
            rs = pl.ds(pl.multiple_of(r * FFN_EPILOGUE_ROWS, FFN_EPILOGUE_ROWS), FFN_EPILOGUE_ROWS)
            y = x_ref[rs, :] + 0.5 * o_ref[rs, :]
            if final:
                y = _rms(y, fg_ref[...])
            o_ref[rs, :] = y
            return carry
        lax.fori_loop(0, o_ref.shape[0] // FFN_EPILOGUE_ROWS, epilogue, 0)


def _ffn(x, g, layer, wg, wu, wd, cast=None, final_g=None, *, tm=1024, tf=512):
    m, d = x.shape
    f = wg.shape[1]
    ni, nj = m // tm, f // tf
    final = final_g is not None
    in_specs = [
        pl.BlockSpec((tm, d), lambda i, j: (i, 0)),
        pl.BlockSpec((None, 1, d), lambda i, j: (layer, 0, 0)),
        pl.BlockSpec((d, tf), lambda i, j: (0, j)),
        pl.BlockSpec((d, tf), lambda i, j: (0, j)),
        pl.BlockSpec((tf, d), lambda i, j: (j, 0)),
    ]
    args = [x, g, wg, wu, wd]
    out_specs = [pl.BlockSpec((tm, d), lambda i, j: (i, 0))]
    out_shape = [jax.ShapeDtypeStruct((m, d), F32)]
    n_cast = 0
    if cast is not None:
        cl, cg, cu, cd = cast
        n_cast = 3
        rows_up = d // ni
        rows_dn = f // (ni * nj)
        assert d % ni == 0 and f % (ni * nj) == 0 and rows_up % 16 == 0 and rows_dn % 16 == 0
        for w in (cg, cu):
            in_specs.append(pl.BlockSpec((None, rows_up, tf), lambda i, j: (cl, i, j)))
            out_specs.append(pl.BlockSpec((rows_up, tf), lambda i, j: (i, j)))
            out_shape.append(jax.ShapeDtypeStruct((d, f), BF16))
            args.append(w)
        in_specs.append(pl.BlockSpec((None, rows_dn, d), lambda i, j: (cl, i * nj + j, 0)))
        out_specs.append(pl.BlockSpec((rows_dn, d), lambda i, j: (i * nj + j, 0)))
        out_shape.append(jax.ShapeDtypeStruct((f, d), BF16))
        args.append(cd)
    if final:
        in_specs.append(pl.BlockSpec((1, d), lambda i, j: (0, 0)))
        args.append(final_g.reshape(1, d))
    return pl.pallas_call(
        functools.partial(_ffn_body, final=final, n_cast=n_cast),
        grid=(ni, nj),
        in_specs=in_specs,
        out_specs=out_specs,
        out_shape=out_shape,
        scratch_shapes=[pltpu.VMEM((tm, d), BF16)],
        compiler_params=_cparams(("arbitrary", "arbitrary")),
        name="ffn_final" if final else "ffn",
    )(*args)


def _repack_body(a_ref, b_ref, w_ref, wl_ref):
    w_ref[...] = a_ref[0].astype(BF16)
    pad = jnp.zeros((wl_ref.shape[0] - b_ref.shape[1], b_ref.shape[2]), F32)
    wl_ref[...] = jnp.concatenate([b_ref[0], pad], axis=0).astype(BF16)


def _repack_win(w_in_t, drop0, drop_width, *, td=1024, tn=512):
    depth, n_src, d = w_in_t.shape
    n = n_src - drop_width
    assert drop0 % tn == 0 and n % tn == 0 and drop_width % 8 == 0
    first_shifted = drop0 // tn
    src_row = lambda j: (j * (tn // 8) + jnp.where(j >= first_shifted, drop_width // 8, 0)) * 8
    return pl.pallas_call(
        _repack_body,
        grid=(depth, d // td, n // tn),
        in_specs=[
            pl.BlockSpec((pl.Element(1), pl.Element(tn), pl.Element(td)),
                         lambda l, i, j: (l, src_row(j), i * td)),
            pl.BlockSpec((pl.Element(1), pl.Element(drop_width), pl.Element(td)),
                         lambda l, i, j: (l, drop0, i * td)),
        ],
        out_specs=[
            pl.BlockSpec((None, tn, td), lambda l, i, j: (l, j, i)),
            pl.BlockSpec((None, LANES, td), lambda l, i, j: (l, 0, i)),
        ],
        out_shape=[
            jax.ShapeDtypeStruct((depth, n, d), BF16),
            jax.ShapeDtypeStruct((depth, LANES, d), BF16),
        ],
        compiler_params=_cparams(("parallel", "parallel", "arbitrary")),
        name="repack_win",
    )(w_in_t, w_in_t)


def _inproj_body(x_ref, g_ref, w_ref, wl_ref, z_ref, glr_ref, xc_ref, un_ref, ph_ref, *,
                 gate_block0, ssm_block, ssm_off):
    j = pl.program_id(1)

    @pl.when(j == 0)
    def _():
        un = _rms(x_ref[...], g_ref[...]).astype(BF16)
        un_ref[...] = un
        glr_ref[...] = _dot_nt(un, wl_ref[...])

    acc = _dot_nt(un_ref[...], w_ref[...])
    z_ref[...] = jnp.where(j >= gate_block0, jax.nn.sigmoid(acc), acc).astype(BF16)

    @pl.when(j == ssm_block)
    def _():
        ncol = ph_ref.shape[0]
        per_half = ncol // 2
        half = per_half * LANES
        for c in range(ncol):
            ph_ref[c] = acc[:, ssm_off + c * LANES:ssm_off + (c + 1) * LANES]
        rows = ph_ref.shape[1] // SSM_CHUNK
        for s in range(SSM_CHUNK):
            for c in range(ncol):
                piece = ph_ref[c, pl.ds(s, rows, stride=SSM_CHUNK), :].astype(BF16)
                l0 = s * half + (c % per_half) * LANES
                xc_ref[c // per_half, :, l0:l0 + LANES] = piece


def _inproj(x, g, w_main, w_glr, layer, ssm_col0, ssm_width, gate_col0, *, tm=1024, tn=1024):
    m, d = x.shape
    n = w_main.shape[1]
    body = functools.partial(_inproj_body, gate_block0=gate_col0 // tn,
                             ssm_block=ssm_col0 // tn, ssm_off=ssm_col0 % tn)
    return pl.pallas_call(
        body,
        grid=(m // tm, n // tn),
        in_specs=[
            pl.BlockSpec((tm, d), lambda i, j: (i, 0)),
            pl.BlockSpec((None, 1, d), lambda i, j: (layer, 0, 0)),
            pl.BlockSpec((None, tn, d), lambda i, j: (layer, j, 0)),
            pl.BlockSpec((None, LANES, d), lambda i, j: (layer, 0, 0)),
        ],
        out_specs=[
            pl.BlockSpec((tm, tn), lambda i, j: (i, j)),
            pl.BlockSpec((tm, LANES), lambda i, j: (i, 0)),
            pl.BlockSpec((2, tm // SSM_CHUNK, SSM_CHUNK * ssm_width // 2), lambda i, j: (0, i, 0)),
        ],
        out_shape=[
            jax.ShapeDtypeStruct((m, n), BF16),
            jax.ShapeDtypeStruct((m, LANES), F32),
            jax.ShapeDtypeStruct((2, m // SSM_CHUNK, SSM_CHUNK * ssm_width // 2), BF16),
        ],
        scratch_shapes=[pltpu.VMEM((tm, d), BF16), pltpu.VMEM((ssm_width // LANES, tm, LANES), F32)],
        compiler_params=_cparams(("parallel", "arbitrary")),
        name="inproj",
    )(x, g, w_main, w_glr)


def _pool_body(p_ref, halo_ref, w_ref, sc_ref, o_ref, *, ts):
    t = pl.program_id(1)
    x = p_ref[...].astype(F32)
    halo = jnp.where(t > 0, halo_ref[...].astype(F32), 0.0)
    e = jnp.concatenate([halo, x], axis=0)
    s2 = e + pltpu.roll(e, 1, 0)
    s4 = s2 + pltpu.roll(s2, 2, 0)
    s8 = s4 + pltpu.roll(s4, 4, 0)
    s16 = s8 + pltpu.roll(s8, 8, 0)
    pos = (t * ts + 1 + lax.broadcasted_iota(jnp.int32, (ts, 1), 0)).astype(F32)
    outs = []
    for gi, (w, s) in enumerate(zip(POOL_WINDOWS, (s2, s4, s8, s16))):
        cols = slice(gi * POOL_GROUP, (gi + 1) * POOL_GROUP)
        mean = s[POOL_HALO:, cols] / jnp.minimum(pos, float(w))
        outs.append(_dot((mean - x[:, cols]).astype(BF16), w_ref[gi]))
    o_ref[...] = (jnp.concatenate(outs, axis=1) * sc_ref[...]).astype(BF16)


def _pool(z, w, scale, layer, batch, seq, *, ts=512):
    width = len(POOL_WINDOWS) * POOL_GROUP
    nt = seq // ts
    hb = ts // POOL_HALO
    return pl.pallas_call(
        functools.partial(_pool_body, ts=ts),
        grid=(batch, nt),
        in_specs=[
            pl.BlockSpec((ts, width), lambda b, t: (b * nt + t, 0)),
            pl.BlockSpec((POOL_HALO, width),
                         lambda b, t: (jnp.maximum((b * nt + t) * hb - 1, 0), 0)),
            pl.BlockSpec((None, len(POOL_WINDOWS), POOL_GROUP, POOL_GROUP),
                         lambda b, t: (layer, 0, 0, 0)),
            pl.BlockSpec((None, 1, width), lambda b, t: (layer, 0, 0)),
        ],
        out_specs=pl.BlockSpec((ts, width), lambda b, t: (b * nt + t, 0)),
        out_shape=jax.ShapeDtypeStruct((batch * seq, width), BF16),
        compiler_params=_cparams(("parallel", "arbitrary")),
        name="pool",
    )(z, z, w, scale)


def _gla_head(q, k, v, r, glr, w2, bias, gain, st, ts):
    c = GLA_CHUNK
    nc = ts // c
    logits = _dot(glr, w2) + bias
    log_a = (jnp.minimum(logits, 0.0) - jnp.log1p(jnp.exp(-jnp.abs(logits)))) * (1.0 / GLA_TAU)
    rowc = lax.broadcasted_iota(jnp.int32, (ts, 1), 0) % c
    bc = log_a
    step = 1
    while step < c:
        bc = bc + jnp.where(rowc >= step, pltpu.roll(bc, step, 0), 0.0)
        step *= 2
    bc3 = bc.reshape(nc, c, GLA_DK)
    b_last = bc3[:, c - 1:c, :]
    b_mid = bc3[:, c // 2 - 1:c // 2, :]
    q3 = q.astype(F32).reshape(nc, c, GLA_DK) * (GLA_DK ** -0.5)
    k3 = k.astype(F32).reshape(nc, c, GLA_DK)
    v3 = v.reshape(nc, c, GLA_DV)
    qe = (q3 * jnp.exp(bc3 - b_mid)).astype(BF16)
    ke = (k3 * jnp.exp(b_mid - bc3)).astype(BF16)
    scores = jnp.einsum('nid,njd->nij', qe, ke, preferred_element_type=F32)
    causal = (lax.broadcasted_iota(jnp.int32, (c, c), 0)
              >= lax.broadcasted_iota(jnp.int32, (c, c), 1))
    scores = jnp.where(causal[None], scores, 0.0).astype(BF16)
    o_intra = jnp.einsum('nij,njv->niv', scores, v3, preferred_element_type=F32)
    qd = (q3 * jnp.exp(bc3)).astype(BF16)
    kd = (k3 * jnp.exp(b_last - bc3)).astype(BF16)
    dec = jnp.exp(b_last)
    outs = []
    for n in range(nc):
        o_inter = lax.dot_general(qd[n], st.astype(BF16), (((1,), (1,)), ((), ())),
                                  preferred_element_type=F32)
        outs.append(o_intra[n] + o_inter)
        kv_t = lax.dot_general(v3[n], kd[n], (((0,), (0,)), ((), ())),
                               preferred_element_type=F32)
        st = dec[n] * st + kv_t
    o = jnp.concatenate(outs, axis=0)
    o = o * lax.rsqrt(jnp.mean(o * o, axis=-1, keepdims=True) + EPS) * gain
    return (o * jax.nn.silu(r.astype(F32))).astype(BF16), st


def _gla_body(q_ref, k_ref, v0_ref, v1_ref, r0_ref, r1_ref, glr_ref, w2_ref, b_ref, gain_ref,
              o_ref, st_ref, *, ts):
    @pl.when(pl.program_id(1) == 0)
    def _():
        st_ref[...] = jnp.zeros_like(st_ref)

    glr = glr_ref[...].astype(BF16)
    vr_refs = ((v0_ref, r0_ref), (v1_ref, r1_ref))
    heads_per_block = v0_ref.shape[1] // GLA_DV
    for h in range(GLA_HEADS):
        v_ref, r_ref = vr_refs[h // heads_per_block]
        vs = slice((h % heads_per_block) * GLA_DV, (h % heads_per_block + 1) * GLA_DV)
        ks = slice(h * GLA_DK, (h + 1) * GLA_DK)
        o, st = _gla_head(q_ref[:, ks], k_ref[:, ks], v_ref[:, vs], r_ref[:, vs], glr,
                          w2_ref[h], b_ref[h], gain_ref[h], st_ref[h], ts)
        st_ref[h] = st
        o_ref[:, h * GLA_DV:(h + 1) * GLA_DV] = o


def _gla(z, glr, w2, bias, gain, layer, batch, seq, cols, *, ts=512):
    h = GLA_HEADS
    nt = seq // ts
    q0, k0, v0, r0 = cols
    wb = h * GLA_DK
    zspec = lambda c0: pl.BlockSpec((ts, wb), lambda b, t: (b * nt + t, c0 // wb))
    wspec = lambda last2: pl.BlockSpec((None, h) + last2, lambda b, t: (layer, 0, 0, 0))
    return pl.pallas_call(
        functools.partial(_gla_body, ts=ts),
        grid=(batch, nt),
        in_specs=[
            zspec(q0), zspec(k0), zspec(v0), zspec(v0 + wb), zspec(r0), zspec(r0 + wb),
            pl.BlockSpec((ts, LANES), lambda b, t: (b * nt + t, 0)),
            wspec((LANES, GLA_DK)), wspec((1, GLA_DK)), wspec((1, GLA_DV)),
        ],
        out_specs=pl.BlockSpec((ts, h * GLA_DV), lambda b, t: (b * nt + t, 0)),
        out_shape=jax.ShapeDtypeStruct((batch * seq, h * GLA_DV), BF16),
        scratch_shapes=[pltpu.VMEM((h, GLA_DV, GLA_DK), F32)],
        compiler_params=_cparams(("parallel", "arbitrary")),
        name="gla",
    )(z, z, z, z, z, z, glr, w2, bias, gain)


SSM_HALF_GROUPS = 16
SSM_BLOCK_GROUPS = 2
SSM_TSTEP = 4


def _ssm_body(x_ref, kc_ref, pc_ref, qc_ref, ar_ref, ai_ref, o_ref, hp_ref, kst_ref, v_ref, *,
              chunks_per_seq, n_state_blocks):
    u = pl.program_id(0)
    rows = x_ref.shape[1]
    hw = SSM_HALF_GROUPS * SSM_GROUP
    blocks_per_half = n_state_blocks // 2
    ns = 2 * SSM_STATE
    bw = SSM_BLOCK_GROUPS * ns
    lane_group = lax.broadcasted_iota(jnp.int32, (1, hw), 1) // SSM_GROUP

    def p_tile(n):
        pc = pc_ref[n // blocks_per_half]
        row_group = (lax.broadcasted_iota(jnp.int32, (pc.shape[0], 1), 0) // SSM_GROUP) % SSM_HALF_GROUPS
        g0 = (n % blocks_per_half) * SSM_BLOCK_GROUPS
        return jnp.concatenate([jnp.where(row_group == g0 + gg, pc, 0.0)
                                for gg in range(SSM_BLOCK_GROUPS)], axis=1).astype(BF16)

    @pl.when(u == 0)
    def _():
        row_group = lax.broadcasted_iota(jnp.int32, (hw, 1), 0) // SSM_GROUP
        for hb in range(2):
            for b in range(SSM_CHUNK):
                kc = kc_ref[hb, SSM_CHUNK - 1 - b]
                tiled = jnp.broadcast_to(kc[None], (SSM_HALF_GROUPS,) + kc.shape).reshape(hw, hw)
                kst_ref[hb, b * hw:(b + 1) * hw, :] = jnp.where(row_group == lane_group, tiled, 0.0).astype(BF16)
            kst_ref[hb, SSM_CHUNK * hw:, :] = jnp.zeros((kst_ref.shape[1] - SSM_CHUNK * hw, hw), BF16)
        v_ref[...] = _dot(x_ref[0], p_tile(0))

    @pl.when(u < n_state_blocks)
    def _():
        rowc = lax.broadcasted_iota(jnp.int32, (rows, 1), 0) % chunks_per_seq
        hst = v_ref[...]
        nxt = jnp.minimum(u + 1, n_state_blocks - 1)
        v_next = _dot(x_ref[nxt // blocks_per_half], p_tile(nxt))
        k, si = 1, 0
        while k < chunks_per_seq:
            hs = jnp.where(rowc >= k, pltpu.roll(hst, k, 0), 0.0)
            sw = jnp.concatenate([pltpu.roll(hs[:, b0:b0 + ns], SSM_STATE, 1)
                                  for b0 in range(0, bw, ns)], axis=1)
            hst = hst + ar_ref[u, si:si + 1, :] * hs + ai_ref[u, si:si + 1, :] * sw
            k *= 2
            si += 1
        hp_ref[u] = jnp.where(rowc >= 1, pltpu.roll(hst, 1, 0), 0.0).astype(BF16)
        v_ref[...] = v_next

    @pl.when(u >= n_state_blocks)
    def _():
        ob = u - n_state_blocks
        hb = ob // SSM_CHUNK
        t = ob % SSM_CHUNK
        qc = qc_ref[hb, t]
        grp = lax.broadcasted_iota(jnp.int32, (SSM_HALF_GROUPS, 1, 1), 0)
        q_tile = jnp.where(grp == lane_group[None], qc[None], 0.0).reshape(SSM_HALF_GROUPS * ns, hw)
        q_tile = q_tile.astype(BF16)
        yq = _dot(hp_ref[hb * blocks_per_half], q_tile[0:bw, :])
        for kk in range(1, blocks_per_half):
            yq = yq + _dot(hp_ref[hb * blocks_per_half + kk], q_tile[kk * bw:(kk + 1) * bw, :])
        start = pl.multiple_of((SSM_CHUNK - 1 - t) * hw, hw)
        for v in range(SSM_CHUNK // SSM_TSTEP):
            kext = (v + 1) * SSM_TSTEP * hw

            @pl.when(t // SSM_TSTEP == v)
            def _():
                intra = _dot(x_ref[hb, :, 0:kext], kst_ref[hb, pl.ds(start, kext), :])
                o_ref[...] = (yq + intra).astype(BF16)


def _ssm_core(xcat, kc, pc, qc, ar, ai, layer, chunks_per_seq):
    _, rows, xw = xcat.shape
    nb = ar.shape[1]
    bw = ar.shape[3]
    hw = SSM_HALF_GROUPS * SSM_GROUP
    nout = 2 * SSM_CHUNK
    one = pl.Buffered(1)
    whole = lambda a: pl.BlockSpec((None,) + a.shape[1:], lambda u: (layer,) + (0,) * (a.ndim - 1),
                                   pipeline_mode=one)
    return pl.pallas_call(
        functools.partial(_ssm_body, chunks_per_seq=chunks_per_seq, n_state_blocks=nb),
        grid=(nb + nout,),
        in_specs=[
            pl.BlockSpec((2, rows, xw), lambda u: (0, 0, 0), pipeline_mode=one),
            whole(kc), whole(pc), whole(qc), whole(ar), whole(ai),
        ],
        out_specs=pl.BlockSpec((None, rows, hw), lambda u: (jnp.maximum(u - nb, 0), 0, 0)),
        out_shape=jax.ShapeDtypeStruct((nout, rows, hw), BF16),
        scratch_shapes=[
            pltpu.VMEM((nb, rows, bw), BF16),
            pltpu.VMEM((2, (SSM_CHUNK + SSM_TSTEP - 1) * hw, hw), BF16),
            pltpu.VMEM((rows, bw), F32),
        ],
        compiler_params=_cparams(("arbitrary",)),
        name="ssm_core",
    )(xcat, kc, pc, qc, ar, ai)


def _ssm_operators(a_re, a_im, log_dt, b_re, b_im, c_re, c_im, chunks_per_seq):
    t = SSM_CHUNK
    g, p = a_re.shape
    hg = b_re.shape[-1]
    hgc = SSM_HALF_GROUPS
    bgc = SSM_BLOCK_GROUPS
    dt = jnp.exp(log_dt)[:, None]
    lam_re, lam_im = dt * a_re, dt * a_im

    def powers(n):
        n = jnp.asarray(n, F32)[:, None, None]
        mag = jnp.exp(n * lam_re)
        return mag * jnp.cos(n * lam_im), mag * jnp.sin(n * lam_im)

    ab_re, ab_im = powers([1.0])
    ab_re, ab_im = ab_re[0], ab_im[0]
    den = a_re * a_re + a_im * a_im
    f_re = ((ab_re - 1.0) * a_re + ab_im * a_im) / den
    f_im = (ab_im * a_re - (ab_re - 1.0) * a_im) / den
    bb_re = f_re[..., None] * b_re - f_im[..., None] * b_im
    bb_im = f_re[..., None] * b_im + f_im[..., None] * b_re
    pw_re, pw_im = powers(np.arange(t + 1))
    cp_re = c_re[None] * pw_re[:, :, None, :] - c_im[None] * pw_im[:, :, None, :]
    cp_im = c_re[None] * pw_im[:, :, None, :] + c_im[None] * pw_re[:, :, None, :]
    ktau = (jnp.einsum('tghp,gpk->tkgh', cp_re[:t], bb_re)
            - jnp.einsum('tghp,gpk->tkgh', cp_im[:t], bb_im))
    kc = ktau.reshape(t, hg, 2, hgc * hg).transpose(2, 0, 1, 3)
    bt_re, bt_im = bb_re.transpose(0, 2, 1), bb_im.transpose(0, 2, 1)
    rp_re = pw_re[t - 1 - np.arange(t)][:, :, None, :]
    rp_im = pw_im[t - 1 - np.arange(t)][:, :, None, :]
    pin = jnp.concatenate([rp_re * bt_re[None] - rp_im * bt_im[None],
                           rp_re * bt_im[None] + rp_im * bt_re[None]], axis=3)
    pc = pin.reshape(t, 2, hgc, hg, 2 * p).transpose(1, 0, 2, 3, 4).reshape(2, t * hgc * hg, 2 * p)
    ct_re, ct_im = c_re.transpose(2, 0, 1)[None], c_im.transpose(2, 0, 1)[None]
    pt_re = pw_re[1:].transpose(0, 2, 1)[..., None]
    pt_im = pw_im[1:].transpose(0, 2, 1)[..., None]
    qg = jnp.concatenate([ct_re * pt_re - ct_im * pt_im,
                          -(ct_re * pt_im + ct_im * pt_re)], axis=1)
    qc = qg.reshape(t, 2 * p, 2, hgc * hg).transpose(2, 0, 1, 3)
    nblk = g // bgc
    nsteps = int(np.log2(chunks_per_seq))
    sr, si = powers(t * 2.0 ** np.arange(nsteps))
    blk = lambda a: a.reshape(nsteps, nblk, bgc * 2 * p).transpose(1, 0, 2)
    ar = blk(jnp.concatenate([sr, sr], axis=2))
    ai = blk(jnp.concatenate([-si, si], axis=2))
    return kc, pc, qc, ar, ai


def _merge_body(x_ref, yp_ref, yg_ref, yc_ref, sin_ref, g0_ref, g1_ref, g2_ref,
                d_ref, wglu_ref, wbp_ref, wbg0_ref, wbg1_ref, wbs_ref, wo_ref, o_ref, ys_ref):
    per_half = yc_ref.shape[2] // LANES
    rows = yc_ref.shape[1]
    for ob in range(yc_ref.shape[0]):
        hb, t = divmod(ob, SSM_CHUNK)
        plane = yc_ref[ob].astype(F32)
        for c in range(per_half):
            ys_ref[hb * per_half + c, pl.ds(t, rows, stride=SSM_CHUNK), :] = plane[:, c * LANES:(c + 1) * LANES]
    ys = jnp.concatenate([ys_ref[c] for c in range(ys_ref.shape[0])], axis=1)
    y = ys + d_ref[...] * sin_ref[...].astype(F32)
    y = jax.nn.gelu(y)
    y = y * jax.nn.sigmoid(_dot(y.astype(BF16), wglu_ref[...]))
    m = g0_ref[...].astype(F32) * _dot(yp_ref[...], wbp_ref[...])
    half = wbg0_ref.shape[0]
    gla_proj = _dot(yg_ref[:, :half], wbg0_ref[...]) + _dot(yg_ref[:, half:], wbg1_ref[...])
    m = m + g1_ref[...].astype(F32) * gla_proj
    m = m + g2_ref[...].astype(F32) * _dot(y.astype(BF16), wbs_ref[...])
    o_ref[...] = x_ref[...] + _dot(m.astype(BF16), wo_ref[...])


def _merge(x, y_pool, y_gla, ycat, z, ssm_col0, gate_col0, d_skip, w_glu, wb, w_out, layer, *, tm=256):
    m, d = x.shape
    wp, wg, ws = y_pool.shape[1], y_gla.shape[1], 2 * ycat.shape[2]
    br = BRANCH_ROWS
    assert wp == br and ws == br and wg == 2 * br
    gb = gate_col0 // d
    one = pl.Buffered(1)
    wspec = lambda shape, r: pl.BlockSpec((None,) + shape, lambda i: (layer, r, 0), pipeline_mode=one)
    return pl.pallas_call(
        _merge_body,
        grid=(m // tm,),
        in_specs=[
            pl.BlockSpec((tm, d), lambda i: (i, 0)),
            pl.BlockSpec((tm, wp), lambda i: (i, 0)),
            pl.BlockSpec((tm, wg), lambda i: (i, 0)),
            pl.BlockSpec((ycat.shape[0], tm // SSM_CHUNK, ws // 2), lambda i: (0, i, 0)),
            pl.BlockSpec((tm, ws), lambda i: (i, ssm_col0 // ws)),
            pl.BlockSpec((tm, d), lambda i: (i, gb)),
            pl.BlockSpec((tm, d), lambda i: (i, gb + 1)),
            pl.BlockSpec((tm, d), lambda i: (i, gb + 2)),
            wspec((1, ws), 0),
            wspec((ws, ws), 0),
            wspec((br, d), 0),
            wspec((br, d), 1),
            wspec((br, d), 2),
            wspec((br, d), 3),
            wspec((d, d), 0),
        ],
        out_specs=pl.BlockSpec((tm, d), lambda i: (i, 0)),
        out_shape=jax.ShapeDtypeStruct((m, d), F32),
        scratch_shapes=[pltpu.VMEM((ws // LANES, tm, LANES), F32)],
        compiler_params=_cparams(("parallel",)),
        name="merge",
    )(x, y_pool, y_gla, ycat, z, z, z, z, d_skip, w_glu, wb, wb, wb, wb, w_out)


def kernel(x, ffn1_norm, ffn1_w_gate, ffn1_w_up, ffn1_w_down, mix_norm, w_in, pool_w, pool_scale, gla_w_gate2, gla_gate_bias, gla_norm, ssm_a_re, ssm_a_im, ssm_log_dt, ssm_b_re, ssm_b_im, ssm_c_re, ssm_c_im, ssm_d, ssm_w_glu, w_branch, w_out, ffn2_norm, ffn2_w_gate, ffn2_w_up, ffn2_w_down, final_norm):
    batch, seq, d = x.shape
    depth = w_in.shape[0]
    m = batch * seq
    pool_width = pool_scale.shape[1]
    qk_width = GLA_HEADS * GLA_DK
    v_width = GLA_HEADS * GLA_DV
    ssm_width = ssm_d.shape[1]
    n_groups = ssm_width // SSM_GROUP
    q0 = pool_width
    k0 = q0 + qk_width
    v0 = k0 + qk_width
    r0 = v0 + v_width
    glr0 = r0 + v_width
    ssm_src0 = glr0 + GLA_RANK
    ssm0 = glr0
    gate0 = ssm0 + ssm_width
    chunks_per_seq = seq // SSM_CHUNK

    bf = lambda a: a.astype(BF16)
    row3 = lambda a: a.reshape(depth, 1, a.shape[-1])
    w_main, w_glr = _repack_win(jnp.swapaxes(w_in, 1, 2), glr0, GLA_RANK)
    ffn1_w = (ffn1_w_gate, ffn1_w_up, ffn1_w_down)
    ffn2_w = (ffn2_w_gate, ffn2_w_up, ffn2_w_down)
    f1 = [bf(w[0]) for w in ffn1_w]
    wbr, wo, wglu, pw = bf(w_branch), bf(w_out), bf(ssm_w_glu), bf(pool_w)
    n1, nm, n2 = row3(ffn1_norm), row3(mix_norm), row3(ffn2_norm)
    pscale, dskip = row3(pool_scale), row3(ssm_d)
    w2 = gla_w_gate2.reshape(depth, GLA_RANK, GLA_HEADS, GLA_DK).transpose(0, 2, 1, 3)
    w2 = bf(jnp.pad(w2, ((0, 0), (0, 0), (0, LANES - GLA_RANK), (0, 0))))
    gbias = gla_gate_bias.reshape(depth, GLA_HEADS, 1, GLA_DK)
    ggain = gla_norm.reshape(depth, GLA_HEADS, 1, GLA_DV)
    ssm_ops = jax.vmap(functools.partial(_ssm_operators, chunks_per_seq=chunks_per_seq))(
        ssm_a_re, ssm_a_im, ssm_log_dt, ssm_b_re, ssm_b_im, ssm_c_re, ssm_c_im)

    xf = x.reshape(m, d)
    for l in range(depth):
        xf, *f2 = _ffn(xf, n1, l, *f1, cast=(l,) + ffn2_w)
        z, glr, xcat = _inproj(xf, nm, w_main, w_glr, l, ssm0, ssm_width, gate0)
        y_pool = _pool(z, pw, pscale, l, batch, seq)
        y_gla = _gla(z, glr, w2, gbias, ggain, l, batch, seq, (q0, k0, v0, r0))
        ycat = _ssm_core(xcat, *ssm_ops, l, chunks_per_seq)
        xf = _merge(xf, y_pool, y_gla, ycat, z, ssm0, gate0, dskip, wglu, wbr, wo, l)
        last = l == depth - 1
        xf, *f1 = _ffn(xf, n2, l, *f2, cast=None if last else (l + 1,) + ffn1_w,
                       final_g=final_norm if last else None)
    return xf.reshape(batch, seq, d)
```

```python
import functools

import jax
import jax.numpy as jnp
import numpy as np
from jax import lax
from jax.experimental import pallas as pl
from jax.experimental.pallas import tpu as pltpu

F32 = jnp.float32
BF16 = jnp.bfloat16

EPS = 1e-6
POOL_WINDOWS = (2, 4, 8, 16)
POOL_GROUP = 128
POOL_HALO = 16
GLA_HEADS = 4
GLA_DK = 128
GLA_DV = 256
GLA_RANK = 16
GLA_TAU = 16.0
GLA_CHUNK = 64
SSM_GROUP = 16
SSM_STATE = 64
SSM_CHUNK = 16
LANES = 128
FFN_DOWN_CHUNK = 512
FFN_ROW_CHUNK = 1024
FFN_EPILOGUE_ROWS = 256
BRANCH_ROWS = 512

VMEM_LIMIT = 60 * 1024 * 1024


def _cparams(sem):
    return pltpu.CompilerParams(dimension_semantics=sem, vmem_limit_bytes=VMEM_LIMIT)


def _rms(xf, g):
    return xf * lax.rsqrt(jnp.mean(xf * xf, axis=-1, keepdims=True) + EPS) * g


def _dot(a, b):
    return jnp.dot(a, b, preferred_element_type=F32)


def _dot_nt(a, b_t):
    return lax.dot_general(a, b_t, (((1,), (1,)), ((), ())), preferred_element_type=F32)


def _ffn_body(x_ref, g_ref, wg_ref, wu_ref, wd_ref, *rest, final, n_cast):
    rest = list(rest)
    cast_in = [rest.pop(0) for _ in range(n_cast)]
    fg_ref = rest.pop(0) if final else None
    o_ref = rest.pop(0)
    cast_out = [rest.pop(0) for _ in range(n_cast)]
    xn_ref, = rest
    j = pl.program_id(1)

    for src, dst in zip(cast_in, cast_out):
        dst[...] = src[...].astype(BF16)

    row_chunks = [slice(r0, r0 + FFN_ROW_CHUNK) for r0 in range(0, o_ref.shape[0], FFN_ROW_CHUNK)]

    @pl.when(j == 0)
    def _():
        for rs in row_chunks:
            xn_ref[rs, :] = _rms(x_ref[rs, :], g_ref[...]).astype(BF16)
            o_ref[rs, :] = jnp.zeros((FFN_ROW_CHUNK, o_ref.shape[1]), F32)

    for rs in row_chunks:
        xn = xn_ref[rs, :]
        a = _dot(xn, wg_ref[...])
        b = _dot(xn, wu_ref[...])
        h = (jax.nn.silu(a) * b).astype(BF16)
        for c0 in range(0, o_ref.shape[1], FFN_DOWN_CHUNK):
            cs = slice(c0, c0 + FFN_DOWN_CHUNK)
            o_ref[rs, cs] += _dot(h, wd_ref[:, cs])

    @pl.when(j == pl.num_programs(1) - 1)
    def _():
        def epilogue(r, carry):
            rs = pl.ds(pl.multiple_of(r * FFN_EPILOGUE_ROWS, FFN_EPILOGUE_ROWS), FFN_EPILOGUE_ROWS)
            y = x_ref[rs, :] + 0.5 * o_ref[rs, :]
            if final:
                y = _rms(y, fg_ref[...])
            o_ref[rs, :] = y
            return carry
        lax.fori_loop(0, o_ref.shape[0] // FFN_EPILOGUE_ROWS, epilogue, 0)


def _ffn(x, g, layer, wg, wu, wd, cast=None, final_g=None, *, tm=1024, tf=512):
    m, d = x.shape
    f = wg.shape[1]
    ni, nj = m // tm, f // tf
    final = final_g is not None
    in_specs = [
        pl.BlockSpec((tm, d), lambda i, j: (i, 0)),
        pl.BlockSpec((None, 1, d), lambda i, j: (layer, 0, 0)),
        pl.BlockSpec((d, tf), lambda i, j: (0, j)),
        pl.BlockSpec((d, tf), lambda i, j: (0, j)),
        pl.BlockSpec((tf, d), lambda i, j: (j, 0)),
    ]
    args = [x, g, wg, wu, wd]
    out_specs = [pl.BlockSpec((tm, d), lambda i, j: (i, 0))]
    out_shape = [jax.ShapeDtypeStruct((m, d), F32)]
    n_cast = 0
    if cast is not None:
        cl, cg, cu, cd = cast
        n_cast = 3
        rows_up = d // ni
        rows_dn = f // (ni * nj)
        assert d % ni == 0 and f % (ni * nj) == 0 and rows_up % 16 == 0 and rows_dn % 16 == 0
        for w in (cg, cu):
            in_specs.append(pl.BlockSpec((None, rows_up, tf), lambda i, j: (cl, i, j)))
            out_specs.append(pl.BlockSpec((rows_up, tf), lambda i, j: (i, j)))
            out_shape.append(jax.ShapeDtypeStruct((d, f), BF16))
            args.append(w)
        in_specs.append(pl.BlockSpec((None, rows_dn, d), lambda i, j: (cl, i * nj + j, 0)))
        out_specs.append(pl.BlockSpec((rows_dn, d), lambda i, j: (i * nj + j, 0)))
        out_shape.append(jax.ShapeDtypeStruct((f, d), BF16))
        args.append(cd)
    if final:
        in_specs.append(pl.BlockSpec((1, d), lambda i, j: (0, 0)))
        args.append(final_g.reshape(1, d))
    return pl.pallas_call(
        functools.partial(_ffn_body, final=final, n_cast=n_cast),
        grid=(ni, nj),
        in_specs=in_specs,
        out_specs=out_specs,
        out_shape=out_shape,
        scratch_shapes=[pltpu.VMEM((tm, d), BF16)],
        compiler_params=_cparams(("arbitrary", "arbitrary")),
        name="ffn_final" if final else "ffn",
    )(*args)


def _repack_body(a_ref, b_ref, w_ref, wl_ref):
    w_ref[...] = a_ref[0].astype(BF16)
    pad = jnp.zeros((wl_ref.shape[0] - b_ref.shape[1], b_ref.shape[2]), F32)
    wl_ref[...] = jnp.concatenate([b_ref[0], pad], axis=0).astype(BF16)


def _repack_win(w_in_t, drop0, drop_width, *, td=1024, tn=512):
    depth, n_src, d = w_in_t.shape
    n = n_src - drop_width
    assert drop0 % tn == 0 and n % tn == 0 and drop_width % 8 == 0
    first_shifted = drop0 // tn
    src_row = lambda j: (j * (tn // 8) + jnp.where(j >= first_shifted, drop_width // 8, 0)) * 8
    return pl.pallas_call(
        _repack_body,
        grid=(depth, d // td, n // tn),
        in_specs=[
            pl.BlockSpec((pl.Element(1), pl.Element(tn), pl.Element(td)),
                         lambda l, i, j: (l, src_row(j), i * td)),
            pl.BlockSpec((pl.Element(1), pl.Element(drop_width), pl.Element(td)),
                         lambda l, i, j: (l, drop0, i * td)),
        ],
        out_specs=[
            pl.BlockSpec((None, tn, td), lambda l, i, j: (l, j, i)),
            pl.BlockSpec((None, LANES, td), lambda l, i, j: (l, 0, i)),
        ],
        out_shape=[
            jax.ShapeDtypeStruct((depth, n, d), BF16),
            jax.ShapeDtypeStruct((depth, LANES, d), BF16),
        ],
        compiler_params=_cparams(("parallel", "parallel", "arbitrary")),
        name="repack_win",
    )(w_in_t, w_in_t)


def _inproj_body(x_ref, g_ref, w_ref, wl_ref, z_ref, glr_ref, xc_ref, un_ref, ph_ref, *,
                 gate_block0, ssm_block, ssm_off):
    j = pl.program_id(1)

    @pl.when(j == 0)
    def _():
        un = _rms(x_ref[...], g_ref[...]).astype(BF16)
        un_ref[...] = un
        glr_ref[...] = _dot_nt(un, wl_ref[...])

    acc = _dot_nt(un_ref[...], w_ref[...])
    z_ref[...] = jnp.where(j >= gate_block0, jax.nn.sigmoid(acc), acc).astype(BF16)

    @pl.when(j == ssm_block)
    def _():
        ncol = ph_ref.shape[0]
        per_half = ncol // 2
        half = per_half * LANES
        for c in range(ncol):
            ph_ref[c] = acc[:, ssm_off + c * LANES:ssm_off + (c + 1) * LANES]
        rows = ph_ref.shape[1] // SSM_CHUNK
        for s in range(SSM_CHUNK):
            for c in range(ncol):
                piece = ph_ref[c, pl.ds(s, rows, stride=SSM_CHUNK), :].astype(BF16)
                l0 = s * half + (c % per_half) * LANES
                xc_ref[c // per_half, :, l0:l0 + LANES] = piece


def _inproj(x, g, w_main, w_glr, layer, ssm_col0, ssm_width, gate_col0, *, tm=1024, tn=1024):
    m, d = x.shape
    n = w_main.shape[1]
    body = functools.partial(_inproj_body, gate_block0=gate_col0 // tn,
                             ssm_block=ssm_col0 // tn, ssm_off=ssm_col0 % tn)
    return pl.pallas_call(
        body,
        grid=(m // tm, n // tn),
        in_specs=[
            pl.BlockSpec((tm, d), lambda i, j: (i, 0)),
            pl.BlockSpec((None, 1, d), lambda i, j: (layer, 0, 0)),
            pl.BlockSpec((None, tn, d), lambda i, j: (layer, j, 0)),
            pl.BlockSpec((None, LANES, d), lambda i, j: (layer, 0, 0)),
        ],
        out_specs=[
            pl.BlockSpec((tm, tn), lambda i, j: (i, j)),
            pl.BlockSpec((tm, LANES), lambda i, j: (i, 0)),
            pl.BlockSpec((2, tm // SSM_CHUNK, SSM_CHUNK * ssm_width // 2), lambda i, j: (0, i, 0)),
        ],
        out_shape=[
            jax.ShapeDtypeStruct((m, n), BF16),
            jax.ShapeDtypeStruct((m, LANES), F32),
            jax.ShapeDtypeStruct((2, m // SSM_CHUNK, SSM_CHUNK * ssm_width // 2), BF16),
        ],
        scratch_shapes=[pltpu.VMEM((tm, d), BF16), pltpu.VMEM((ssm_width // LANES, tm, LANES), F32)],
        compiler_params=_cparams(("parallel", "arbitrary")),
        name="inproj",
    )(x, g, w_main, w_glr)


def _pool_mix(x, halo, t, w_ref, scale):
    ts = x.shape[0]
    halo = jnp.where(t > 0, halo, 0.0)
    e = jnp.concatenate([halo, x], axis=0)
    s2 = e + pltpu.roll(e, 1, 0)
    s4 = s2 + pltpu.roll(s2, 2, 0)
    s8 = s4 + pltpu.roll(s4, 4, 0)
    s16 = s8 + pltpu.roll(s8, 8, 0)
    pos = (t * ts + 1 + lax.broadcasted_iota(jnp.int32, (ts, 1), 0)).astype(F32)
    outs = []
    for gi, (w, s) in enumerate(zip(POOL_WINDOWS, (s2, s4, s8, s16))):
        cols = slice(gi * POOL_GROUP, (gi + 1) * POOL_GROUP)
        mean = s[POOL_HALO:, cols] / jnp.minimum(pos, float(w))
        outs.append(_dot((mean - x[:, cols]).astype(BF16), w_ref[gi]))
    return jnp.concatenate(outs, axis=1) * scale


def _gla_head(q, k, v, glr, w2, bias, st, ts):
    c = GLA_CHUNK
    nc = ts // c
    logits = _dot(glr, w2) + bias
    log_a = (jnp.minimum(logits, 0.0) - jnp.log1p(jnp.exp(-jnp.abs(logits)))) * (1.0 / GLA_TAU)
    rowc = lax.broadcasted_iota(jnp.int32, (ts, 1), 0) % c
    bc = log_a
    step = 1
    while step < c:
        bc = bc + jnp.where(rowc >= step, pltpu.roll(bc, step, 0), 0.0)
        step *= 2
    bc3 = bc.reshape(nc, c, GLA_DK)
    b_last = bc3[:, c - 1:c, :]
    b_mid = bc3[:, c // 2 - 1:c // 2, :]
    q3 = q.astype(F32).reshape(nc, c, GLA_DK) * (GLA_DK ** -0.5)
    k3 = k.astype(F32).reshape(nc, c, GLA_DK)
    v3 = v.reshape(nc, c, GLA_DV)
    qe = (q3 * jnp.exp(bc3 - b_mid)).astype(BF16)
    ke = (k3 * jnp.exp(b_mid - bc3)).astype(BF16)
    scores = jnp.einsum('nid,njd->nij', qe, ke, preferred_element_type=F32)
    causal = (lax.broadcasted_iota(jnp.int32, (c, c), 0)
              >= lax.broadcasted_iota(jnp.int32, (c, c), 1))
    scores = jnp.where(causal[None], scores, 0.0).astype(BF16)
    o_intra = jnp.einsum('nij,njv->niv', scores, v3, preferred_element_type=F32)
    qd = (q3 * jnp.exp(bc3)).astype(BF16)
    kd = (k3 * jnp.exp(b_last - bc3)).astype(BF16)
    dec_t = jnp.transpose(jnp.exp(b_last.reshape(nc, GLA_DK)))
    outs = []
    for n in range(nc):
        outs.append(o_intra[n] + _dot(qd[n], st.astype(BF16)))
        kv = lax.dot_general(kd[n], v3[n], (((0,), (0,)), ((), ())), preferred_element_type=F32)
        st = dec_t[:, n:n + 1] * st + kv
    return jnp.concatenate(outs, axis=0).astype(BF16), st


def _gla_body(q_ref, k_ref, v0_ref, v1_ref, glr_ref, w2_ref, b_ref, o_ref, st_ref, *, ts):
    @pl.when(pl.program_id(1) == 0)
    def _():
        st_ref[...] = jnp.zeros_like(st_ref)

    glr = glr_ref[...].astype(BF16)
    v_refs = (v0_ref, v1_ref)
    heads_per_block = v0_ref.shape[1] // GLA_DV
    for h in range(GLA_HEADS):
        v_ref = v_refs[h // heads_per_block]
        vs = slice((h % heads_per_block) * GLA_DV, (h % heads_per_block + 1) * GLA_DV)
        ks = slice(h * GLA_DK, (h + 1) * GLA_DK)
        o, st = _gla_head(q_ref[:, ks], k_ref[:, ks], v_ref[:, vs], glr, w2_ref[h], b_ref[h], st_ref[h], ts)
        st_ref[h] = st
        o_ref[:, h * GLA_DV:(h + 1) * GLA_DV] = o


def _gla(z, glr, w2, bias, layer, batch, seq, cols, *, ts=512):
    h = GLA_HEADS
    nt = seq // ts
    q0, k0, v0 = cols
    wb = h * GLA_DK
    zspec = lambda c0: pl.BlockSpec((ts, wb), lambda b, t: (b * nt + t, c0 // wb))
    wspec = lambda last2: pl.BlockSpec((None, h) + last2, lambda b, t: (layer, 0, 0, 0))
    return pl.pallas_call(
        functools.partial(_gla_body, ts=ts),
        grid=(batch, nt),
        in_specs=[
            zspec(q0), zspec(k0), zspec(v0), zspec(v0 + wb),
            pl.BlockSpec((ts, LANES), lambda b, t: (b * nt + t, 0)),
            wspec((LANES, GLA_DK)), wspec((1, GLA_DK)),
        ],
        out_specs=pl.BlockSpec((ts, h * GLA_DV), lambda b, t: (b * nt + t, 0)),
        out_shape=jax.ShapeDtypeStruct((batch * seq, h * GLA_DV), BF16),
        scratch_shapes=[pltpu.VMEM((h, GLA_DK, GLA_DV), F32)],
        compiler_params=_cparams(("parallel", "arbitrary")),
        name="gla",
    )(z, z, z, z, glr, w2, bias)


SSM_HALF_GROUPS = 16
SSM_BLOCK_GROUPS = 2
SSM_TSTEP = 4


def _ssm_body(x_ref, kc_ref, pc_ref, qc_ref, ar_ref, ai_ref, o_ref, hp_ref, kst_ref, v_ref, *,
              chunks_per_seq, n_state_blocks):
    u = pl.program_id(0)
    rows = x_ref.shape[1]
    hw = SSM_HALF_GROUPS * SSM_GROUP
    blocks_per_half = n_state_blocks // 2
    ns = 2 * SSM_STATE
    bw = SSM_BLOCK_GROUPS * ns
    lane_group = lax.broadcasted_iota(jnp.int32, (1, hw), 1) // SSM_GROUP

    def p_tile(n):
        pc = pc_ref[n // blocks_per_half]
        row_group = (lax.broadcasted_iota(jnp.int32, (pc.shape[0], 1), 0) // SSM_GROUP) % SSM_HALF_GROUPS
        g0 = (n % blocks_per_half) * SSM_BLOCK_GROUPS
        return jnp.concatenate([jnp.where(row_group == g0 + gg, pc, 0.0)
                                for gg in range(SSM_BLOCK_GROUPS)], axis=1).astype(BF16)

    @pl.when(u == 0)
    def _():
        row_group = lax.broadcasted_iota(jnp.int32, (hw, 1), 0) // SSM_GROUP
        for hb in range(2):
            for b in range(SSM_CHUNK):
                kc = kc_ref[hb, SSM_CHUNK - 1 - b]
                tiled = jnp.broadcast_to(kc[None], (SSM_HALF_GROUPS,) + kc.shape).reshape(hw, hw)
                kst_ref[hb, b * hw:(b + 1) * hw, :] = jnp.where(row_group == lane_group, tiled, 0.0).astype(BF16)
            kst_ref[hb, SSM_CHUNK * hw:, :] = jnp.zeros((kst_ref.shape[1] - SSM_CHUNK * hw, hw), BF16)
        v_ref[...] = _dot(x_ref[0], p_tile(0))

    @pl.when(u < n_state_blocks)
    def _():
        rowc = lax.broadcasted_iota(jnp.int32, (rows, 1), 0) % chunks_per_seq
        hst = v_ref[...]
        nxt = jnp.minimum(u + 1, n_state_blocks - 1)
        v_next = _dot(x_ref[nxt // blocks_per_half], p_tile(nxt))
        k, si = 1, 0
        while k < chunks_per_seq:
            hs = jnp.where(rowc >= k, pltpu.roll(hst, k, 0), 0.0)
            sw = jnp.concatenate([pltpu.roll(hs[:, b0:b0 + ns], SSM_STATE, 1)
                                  for b0 in range(0, bw, ns)], axis=1)
            hst = hst + ar_ref[u, si:si + 1, :] * hs + ai_ref[u, si:si + 1, :] * sw
            k *= 2
            si += 1
        hp_ref[u] = jnp.where(rowc >= 1, pltpu.roll(hst, 1, 0), 0.0).astype(BF16)
        v_ref[...] = v_next

    @pl.when(u >= n_state_blocks)
    def _():
        ob = u - n_state_blocks
        hb = ob // SSM_CHUNK
        t = ob % SSM_CHUNK
        qc = qc_ref[hb, t]
        grp = lax.broadcasted_iota(jnp.int32, (SSM_HALF_GROUPS, 1, 1), 0)
        q_tile = jnp.where(grp == lane_group[None], qc[None], 0.0).reshape(SSM_HALF_GROUPS * ns, hw)
        q_tile = q_tile.astype(BF16)
        yq = _dot(hp_ref[hb * blocks_per_half], q_tile[0:bw, :])
        for kk in range(1, blocks_per_half):
            yq = yq + _dot(hp_ref[hb * blocks_per_half + kk], q_tile[kk * bw:(kk + 1) * bw, :])
        start = pl.multiple_of((SSM_CHUNK - 1 - t) * hw, hw)
        for v in range(SSM_CHUNK // SSM_TSTEP):
            kext = (v + 1) * SSM_TSTEP * hw

            @pl.when(t // SSM_TSTEP == v)
            def _():
                intra = _dot(x_ref[hb, :, 0:kext], kst_ref[hb, pl.ds(start, kext), :])
                o_ref[...] = (yq + intra).astype(BF16)


def _ssm_core(xcat, kc, pc, qc, ar, ai, layer, chunks_per_seq):
    _, rows, xw = xcat.shape
    nb = ar.shape[1]
    bw = ar.shape[3]
    hw = SSM_HALF_GROUPS * SSM_GROUP
    nout = 2 * SSM_CHUNK
    one = pl.Buffered(1)
    whole = lambda a: pl.BlockSpec((None,) + a.shape[1:], lambda u: (layer,) + (0,) * (a.ndim - 1),
                                   pipeline_mode=one)
    return pl.pallas_call(
        functools.partial(_ssm_body, chunks_per_seq=chunks_per_seq, n_state_blocks=nb),
        grid=(nb + nout,),
        in_specs=[
            pl.BlockSpec((2, rows, xw), lambda u: (0, 0, 0), pipeline_mode=one),
            whole(kc), whole(pc), whole(qc), whole(ar), whole(ai),
        ],
        out_specs=pl.BlockSpec((None, rows, hw), lambda u: (jnp.maximum(u - nb, 0), 0, 0)),
        out_shape=jax.ShapeDtypeStruct((nout, rows, hw), BF16),
        scratch_shapes=[
            pltpu.VMEM((nb, rows, bw), BF16),
            pltpu.VMEM((2, (SSM_CHUNK + SSM_TSTEP - 1) * hw, hw), BF16),
            pltpu.VMEM((rows, bw), F32),
        ],
        compiler_params=_cparams(("arbitrary",)),
        name="ssm_core",
    )(xcat, kc, pc, qc, ar, ai)


def _ssm_operators(a_re, a_im, log_dt, b_re, b_im, c_re, c_im, chunks_per_seq):
    t = SSM_CHUNK
    g, p = a_re.shape
    hg = b_re.shape[-1]
    hgc = SSM_HALF_GROUPS
    bgc = SSM_BLOCK_GROUPS
    dt = jnp.exp(log_dt)[:, None]
    lam_re, lam_im = dt * a_re, dt * a_im

    def powers(n):
        n = jnp.asarray(n, F32)[:, None, None]
        mag = jnp.exp(n * lam_re)
        return mag * jnp.cos(n * lam_im), mag * jnp.sin(n * lam_im)

    ab_re, ab_im = powers([1.0])
    ab_re, ab_im = ab_re[0], ab_im[0]
    den = a_re * a_re + a_im * a_im
    f_re = ((ab_re - 1.0) * a_re + ab_im * a_im) / den
    f_im = (ab_im * a_re - (ab_re - 1.0) * a_im) / den
    bb_re = f_re[..., None] * b_re - f_im[..., None] * b_im
    bb_im = f_re[..., None] * b_im + f_im[..., None] * b_re
    pw_re, pw_im = powers(np.arange(t + 1))
    cp_re = c_re[None] * pw_re[:, :, None, :] - c_im[None] * pw_im[:, :, None, :]
    cp_im = c_re[None] * pw_im[:, :, None, :] + c_im[None] * pw_re[:, :, None, :]
    ktau = (jnp.einsum('tghp,gpk->tkgh', cp_re[:t], bb_re)
            - jnp.einsum('tghp,gpk->tkgh', cp_im[:t], bb_im))
    kc = ktau.reshape(t, hg, 2, hgc * hg).transpose(2, 0, 1, 3)
    bt_re, bt_im = bb_re.transpose(0, 2, 1), bb_im.transpose(0, 2, 1)
    rp_re = pw_re[t - 1 - np.arange(t)][:, :, None, :]
    rp_im = pw_im[t - 1 - np.arange(t)][:, :, None, :]
    pin = jnp.concatenate([rp_re * bt_re[None] - rp_im * bt_im[None],
                           rp_re * bt_im[None] + rp_im * bt_re[None]], axis=3)
    pc = pin.reshape(t, 2, hgc, hg, 2 * p).transpose(1, 0, 2, 3, 4).reshape(2, t * hgc * hg, 2 * p)
    ct_re, ct_im = c_re.transpose(2, 0, 1)[None], c_im.transpose(2, 0, 1)[None]
    pt_re = pw_re[1:].transpose(0, 2, 1)[..., None]
    pt_im = pw_im[1:].transpose(0, 2, 1)[..., None]
    qg = jnp.concatenate([ct_re * pt_re - ct_im * pt_im,
                          -(ct_re * pt_im + ct_im * pt_re)], axis=1)
    qc = qg.reshape(t, 2 * p, 2, hgc * hg).transpose(2, 0, 1, 3)
    nblk = g // bgc
    nsteps = int(np.log2(chunks_per_seq))
    sr, si = powers(t * 2.0 ** np.arange(nsteps))
    blk = lambda a: a.reshape(nsteps, nblk, bgc * 2 * p).transpose(1, 0, 2)
    ar = blk(jnp.concatenate([sr, sr], axis=2))
    ai = blk(jnp.concatenate([-si, si], axis=2))
    return kc, pc, qc, ar, ai


def _merge_body(x_ref, pin_ref, halo_ref, og_ref, r0_ref, r1_ref, yc_ref, sin_ref, g0_ref, g1_ref, g2_ref,
                pw_ref, ps_ref, gain_ref, d_ref, wglu_ref, wbp_ref, wbg0_ref, wbg1_ref, wbs_ref, wo_ref,
                o_ref, ys_ref, *, tiles_per_seq):
    t = pl.program_id(0) % tiles_per_seq
    yp = _pool_mix(pin_ref[...].astype(F32), halo_ref[...].astype(F32), t, pw_ref, ps_ref[...])
    r_refs = (r0_ref, r1_ref)
    heads_per_block = r0_ref.shape[1] // GLA_DV
    yg_heads = []
    for h in range(GLA_HEADS):
        hs = slice(h * GLA_DV, (h + 1) * GLA_DV)
        rs = slice((h % heads_per_block) * GLA_DV, (h % heads_per_block + 1) * GLA_DV)
        o = og_ref[:, hs].astype(F32)
        o = o * lax.rsqrt(jnp.mean(o * o, axis=-1, keepdims=True) + EPS) * gain_ref[:, hs]
        yg_heads.append((o * jax.nn.silu(r_refs[h // heads_per_block][:, rs].astype(F32))).astype(BF16))
    yg = jnp.concatenate(yg_heads, axis=1)
    per_half = yc_ref.shape[2] // LANES
    rows = yc_ref.shape[1]
    for ob in range(yc_ref.shape[0]):
        hb, t = divmod(ob, SSM_CHUNK)
        plane = yc_ref[ob].astype(F32)
        for c in range(per_half):
            ys_ref[hb * per_half + c, pl.ds(t, rows, stride=SSM_CHUNK), :] = plane[:, c * LANES:(c + 1) * LANES]
    ys = jnp.concatenate([ys_ref[c] for c in range(ys_ref.shape[0])], axis=1)
    y = ys + d_ref[...] * sin_ref[...].astype(F32)
    y = jax.nn.gelu(y)
    y = y * jax.nn.sigmoid(_dot(y.astype(BF16), wglu_ref[...]))
    m = g0_ref[...].astype(F32) * _dot(yp.astype(BF16), wbp_ref[...])
    half = wbg0_ref.shape[0]
    gla_proj = _dot(yg[:, :half], wbg0_ref[...]) + _dot(yg[:, half:], wbg1_ref[...])
    m = m + g1_ref[...].astype(F32) * gla_proj
    m = m + g2_ref[...].astype(F32) * _dot(y.astype(BF16), wbs_ref[...])
    o_ref[...] = x_ref[...] + _dot(m.astype(BF16), wo_ref[...])


def _merge(x, o_gla, ycat, z, r_col0, ssm_col0, gate_col0, seq, pool_w, pool_scale, gla_gain,
           d_skip, w_glu, wb, w_out, layer, *, tm=256):
    m, d = x.shape
    wp, wg, ws = pool_scale.shape[2], o_gla.shape[1], 2 * ycat.shape[2]
    br = BRANCH_ROWS
    assert wp == br and ws == br and wg == 2 * br and seq % tm == 0 and tm % POOL_HALO == 0
    gb = gate_col0 // d
    hb = tm // POOL_HALO
    one = pl.Buffered(1)
    wspec = lambda shape, r: pl.BlockSpec((None,) + shape, lambda i: (layer, r, 0), pipeline_mode=one)
    return pl.pallas_call(
        functools.partial(_merge_body, tiles_per_seq=seq // tm),
        grid=(m // tm,),
        in_specs=[
            pl.BlockSpec((tm, d), lambda i: (i, 0)),
            pl.BlockSpec((tm, wp), lambda i: (i, 0)),
            pl.BlockSpec((POOL_HALO, wp), lambda i: (jnp.maximum(i * hb - 1, 0), 0)),
            pl.BlockSpec((tm, wg), lambda i: (i, 0)),
            pl.BlockSpec((tm, br), lambda i: (i, r_col0 // br)),
            pl.BlockSpec((tm, br), lambda i: (i, r_col0 // br + 1)),
            pl.BlockSpec((ycat.shape[0], tm // SSM_CHUNK, ws // 2), lambda i: (0, i, 0)),
            pl.BlockSpec((tm, ws), lambda i: (i, ssm_col0 // ws)),
            pl.BlockSpec((tm, d), lambda i: (i, gb)),
            pl.BlockSpec((tm, d), lambda i: (i, gb + 1)),
            pl.BlockSpec((tm, d), lambda i: (i, gb + 2)),
            pl.BlockSpec((None, len(POOL_WINDOWS), POOL_GROUP, POOL_GROUP), lambda i: (layer, 0, 0, 0),
                         pipeline_mode=one),
            wspec((1, wp), 0),
            wspec((1, wg), 0),
            wspec((1, ws), 0),
            wspec((ws, ws), 0),
            wspec((br, d), 0),
            wspec((br, d), 1),
            wspec((br, d), 2),
            wspec((br, d), 3),
            wspec((d, d), 0),
        ],
        out_specs=pl.BlockSpec((tm, d), lambda i: (i, 0)),
        out_shape=jax.ShapeDtypeStruct((m, d), F32),
        scratch_shapes=[pltpu.VMEM((ws // LANES, tm, LANES), F32)],
        compiler_params=_cparams(("parallel",)),
        name="merge",
    )(x, z, z, o_gla, z, z, ycat, z, z, z, z, pool_w, pool_scale, gla_gain, d_skip, w_glu,
      wb, wb, wb, wb, w_out)


def kernel(x, ffn1_norm, ffn1_w_gate, ffn1_w_up, ffn1_w_down, mix_norm, w_in, pool_w, pool_scale, gla_w_gate2, gla_gate_bias, gla_norm, ssm_a_re, ssm_a_im, ssm_log_dt, ssm_b_re, ssm_b_im, ssm_c_re, ssm_c_im, ssm_d, ssm_w_glu, w_branch, w_out, ffn2_norm, ffn2_w_gate, ffn2_w_up, ffn2_w_down, final_norm):
    batch, seq, d = x.shape
    depth = w_in.shape[0]
    m = batch * seq
    pool_width = pool_scale.shape[1]
    qk_width = GLA_HEADS * GLA_DK
    v_width = GLA_HEADS * GLA_DV
    ssm_width = ssm_d.shape[1]
    n_groups = ssm_width // SSM_GROUP
    q0 = pool_width
    k0 = q0 + qk_width
    v0 = k0 + qk_width
    r0 = v0 + v_width
    glr0 = r0 + v_width
    ssm_src0 = glr0 + GLA_RANK
    ssm0 = glr0
    gate0 = ssm0 + ssm_width
    chunks_per_seq = seq // SSM_CHUNK

    bf = lambda a: a.astype(BF16)
    row3 = lambda a: a.reshape(depth, 1, a.shape[-1])
    w_main, w_glr = _repack_win(jnp.swapaxes(w_in, 1, 2), glr0, GLA_RANK)
    ffn1_w = (ffn1_w_gate, ffn1_w_up, ffn1_w_down)
    ffn2_w = (ffn2_w_gate, ffn2_w_up, ffn2_w_down)
    f1 = [bf(w[0]) for w in ffn1_w]
    wbr, wo, wglu, pw = bf(w_branch), bf(w_out), bf(ssm_w_glu), bf(pool_w)
    n1, nm, n2 = row3(ffn1_norm), row3(mix_norm), row3(ffn2_norm)
    pscale, dskip = row3(pool_scale), row3(ssm_d)
    w2 = gla_w_gate2.reshape(depth, GLA_RANK, GLA_HEADS, GLA_DK).transpose(0, 2, 1, 3)
    w2 = bf(jnp.pad(w2, ((0, 0), (0, 0), (0, LANES - GLA_RANK), (0, 0))))
    gbias = gla_gate_bias.reshape(depth, GLA_HEADS, 1, GLA_DK)
    ggain = row3(gla_norm)
    ssm_ops = jax.vmap(functools.partial(_ssm_operators, chunks_per_seq=chunks_per_seq))(
        ssm_a_re, ssm_a_im, ssm_log_dt, ssm_b_re, ssm_b_im, ssm_c_re, ssm_c_im)

    xf = x.reshape(m, d)
    for l in range(depth):
        xf, *f2 = _ffn(xf, n1, l, *f1, cast=(l,) + ffn2_w)
        z, glr, xcat = _inproj(xf, nm, w_main, w_glr, l, ssm0, ssm_width, gate0)
        o_gla = _gla(z, glr, w2, gbias, l, batch, seq, (q0, k0, v0))
        ycat = _ssm_core(xcat, *ssm_ops, l, chunks_per_seq)
        xf = _merge(xf, o_gla, ycat, z, r0, ssm0, gate0, seq, pw, pscale, ggain, dskip, wglu, wbr, wo, l)
        last = l == depth - 1
        xf, *f1 = _ffn(xf, n2, l, *f2, cast=None if last else (l + 1,) + ffn1_w,
                       final_g=final_norm if last else None)
    return xf.reshape(batch, seq, d)
```

```python
import functools

import jax
import jax.numpy as jnp
import numpy as np
from jax import lax
from jax.experimental import pallas as pl
from jax.experimental.pallas import tpu as pltpu

F32 = jnp.float32
BF16 = jnp.bfloat16

EPS = 1e-6
POOL_WINDOWS = (2, 4, 8, 16)
POOL_GROUP = 128
POOL_HALO = 16
GLA_HEADS = 4
GLA_DK = 128
GLA_DV = 256
GLA_RANK = 16
GLA_TAU = 16.0
GLA_CHUNK = 64
SSM_GROUP = 16
SSM_STATE = 64
SSM_CHUNK = 16
LANES = 128
FFN_DOWN_CHUNK = 512
FFN_ROW_CHUNK = 1024
FFN_EPILOGUE_ROWS = 256
REPACK_ROWS = 64
BRANCH_ROWS = 512

VMEM_LIMIT = 60 * 1024 * 1024


def _cparams(sem):
    return pltpu.CompilerParams(dimension_semantics=sem, vmem_limit_bytes=VMEM_LIMIT)


def _rms(xf, g):
    return xf * lax.rsqrt(jnp.mean(xf * xf, axis=-1, keepdims=True) + EPS) * g


def _dot(a, b):
    return jnp.dot(a, b, preferred_element_type=F32)


def _dot_nt(a, b_t):
    return lax.dot_general(a, b_t, (((1,), (1,)), ((), ())), preferred_element_type=F32)


def _ffn_body(x_ref, g_ref, wg_ref, wu_ref, wd_ref, *rest, final, n_cast, repack):
    rest = list(rest)
    cast_in = [rest.pop(0) for _ in range(n_cast)]
    rp_in = [rest.pop(0) for _ in range(2 if repack else 0)]
    fg_ref = rest.pop(0) if final else None
    o_ref = rest.pop(0)
    cast_out = [rest.pop(0) for _ in range(n_cast)]
    rp_out = [rest.pop(0) for _ in range(2 if repack else 0)]
    xn_ref, = rest
    j = pl.program_id(1)

    for src, dst in zip(cast_in, cast_out):
        dst[...] = src[...].astype(BF16)
    if repack:
        (rows_ref, drop_ref), (w_ref, wl_ref) = rp_in, rp_out
        w_ref[...] = rows_ref[0].astype(BF16)
        pad = jnp.zeros((wl_ref.shape[0] - drop_ref.shape[1], drop_ref.shape[2]), F32)
        wl_ref[...] = jnp.concatenate([drop_ref[0], pad], axis=0).astype(BF16)

    row_chunks = [slice(r0, r0 + FFN_ROW_CHUNK) for r0 in range(0, o_ref.shape[0], FFN_ROW_CHUNK)]

    @pl.when(j == 0)
    def _():
        for rs in row_chunks:
            xn_ref[rs, :] = _rms(x_ref[rs, :], g_ref[...]).astype(BF16)
            o_ref[rs, :] = jnp.zeros((FFN_ROW_CHUNK, o_ref.shape[1]), F32)

    for rs in row_chunks:
        xn = xn_ref[rs, :]
        a = _dot(xn, wg_ref[...])
        b = _dot(xn, wu_ref[...])
        h = (jax.nn.silu(a) * b).astype(BF16)
        for c0 in range(0, o_ref.shape[1], FFN_DOWN_CHUNK):
            cs = slice(c0, c0 + FFN_DOWN_CHUNK)
            o_ref[rs, cs] += _dot(h, wd_ref[:, cs])

    @pl.when(j == pl.num_programs(1) - 1)
    def _():
        def epilogue(r, carry):
            rs = pl.ds(pl.multiple_of(r * FFN_EPILOGUE_ROWS, FFN_EPILOGUE_ROWS), FFN_EPILOGUE_ROWS)
            y = x_ref[rs, :] + 0.5 * o_ref[rs, :]
            if final:
                y = _rms(y, fg_ref[...])
            o_ref[rs, :] = y
            return carry
        lax.fori_loop(0, o_ref.shape[0] // FFN_EPILOGUE_ROWS, epilogue, 0)


def _ffn(x, g, layer, wg, wu, wd, cast=None, repack=None, final_g=None, *, tm=1024, tf=512):
    m, d = x.shape
    f = wg.shape[1]
    ni, nj = m // tm, f // tf
    final = final_g is not None
    in_specs = [
        pl.BlockSpec((tm, d), lambda i, j: (i, 0)),
        pl.BlockSpec((None, 1, d), lambda i, j: (layer, 0, 0)),
        pl.BlockSpec((d, tf), lambda i, j: (0, j)),
        pl.BlockSpec((d, tf), lambda i, j: (0, j)),
        pl.BlockSpec((tf, d), lambda i, j: (j, 0)),
    ]
    args = [x, g, wg, wu, wd]
    out_specs = [pl.BlockSpec((tm, d), lambda i, j: (i, 0))]
    out_shape = [jax.ShapeDtypeStruct((m, d), F32)]
    n_cast = 0
    if cast is not None:
        cl, cg, cu, cd = cast
        n_cast = 3
        rows_up = d // ni
        rows_dn = f // (ni * nj)
        assert d % ni == 0 and f % (ni * nj) == 0 and rows_up % 16 == 0 and rows_dn % 16 == 0
        for w in (cg, cu):
            in_specs.append(pl.BlockSpec((None, rows_up, tf), lambda i, j: (cl, i, j)))
            out_specs.append(pl.BlockSpec((rows_up, tf), lambda i, j: (i, j)))
            out_shape.append(jax.ShapeDtypeStruct((d, f), BF16))
            args.append(w)
        in_specs.append(pl.BlockSpec((None, rows_dn, d), lambda i, j: (cl, i * nj + j, 0)))
        out_specs.append(pl.BlockSpec((rows_dn, d), lambda i, j: (i * nj + j, 0)))
        out_shape.append(jax.ShapeDtypeStruct((f, d), BF16))
        args.append(cd)
    if repack is not None:
        rl, w_in_t, drop0, drop_width = repack
        n = w_in_t.shape[1] - drop_width
        rb = REPACK_ROWS
        nblk = n // rb
        assert n % rb == 0 and nblk <= ni * nj and drop0 % rb == 0 and drop_width % 8 == 0
        blk = lambda i, j: jnp.minimum(i * nj + j, nblk - 1)
        src_row = lambda b: (b * (rb // 8) + jnp.where(b >= drop0 // rb, drop_width // 8, 0)) * 8
        in_specs.append(pl.BlockSpec((pl.Element(1), pl.Element(rb), pl.Element(d)),
                                     lambda i, j: (rl, src_row(blk(i, j)), 0)))
        in_specs.append(pl.BlockSpec((pl.Element(1), pl.Element(drop_width), pl.Element(d)),
                                     lambda i, j: (rl, drop0, 0)))
        out_specs.append(pl.BlockSpec((rb, d), lambda i, j: (blk(i, j), 0)))
        out_specs.append(pl.BlockSpec((LANES, d), lambda i, j: (0, 0)))
        out_shape.append(jax.ShapeDtypeStruct((n, d), BF16))
        out_shape.append(jax.ShapeDtypeStruct((LANES, d), BF16))
        args += [w_in_t, w_in_t]
    if final:
        in_specs.append(pl.BlockSpec((1, d), lambda i, j: (0, 0)))
        args.append(final_g.reshape(1, d))
    return pl.pallas_call(
        functools.partial(_ffn_body, final=final, n_cast=n_cast, repack=repack is not None),
        grid=(ni, nj),
        in_specs=in_specs,
        out_specs=out_specs,
        out_shape=out_shape,
        scratch_shapes=[pltpu.VMEM((tm, d), BF16)],
        compiler_params=_cparams(("arbitrary", "arbitrary")),
        name="ffn_final" if final else "ffn",
    )(*args)


def _inproj_body(x_ref, g_ref, w_ref, wl_ref, z_ref, glr_ref, xc_ref, un_ref, ph_ref, *,
                 gate_block0, ssm_block, ssm_off):
    j = pl.program_id(1)

    @pl.when(j == 0)
    def _():
        un = _rms(x_ref[...], g_ref[...]).astype(BF16)
        un_ref[...] = un
        glr_ref[...] = _dot_nt(un, wl_ref[...])

    acc = _dot_nt(un_ref[...], w_ref[...])
    z_ref[...] = jnp.where(j >= gate_block0, jax.nn.sigmoid(acc), acc).astype(BF16)

    @pl.when(j == ssm_block)
    def _():
        ncol = ph_ref.shape[0]
        per_half = ncol // 2
        half = per_half * LANES
        for c in range(ncol):
            ph_ref[c] = acc[:, ssm_off + c * LANES:ssm_off + (c + 1) * LANES]
        rows = ph_ref.shape[1] // SSM_CHUNK
        for s in range(SSM_CHUNK):
            for c in range(ncol):
                piece = ph_ref[c, pl.ds(s, rows, stride=SSM_CHUNK), :].astype(BF16)
                l0 = s * half + (c % per_half) * LANES
                xc_ref[c // per_half, :, l0:l0 + LANES] = piece


def _inproj(x, g, w_main, w_glr, layer, ssm_col0, ssm_width, gate_col0, *, tm=1024, tn=1024):
    m, d = x.shape
    n = w_main.shape[0]
    body = functools.partial(_inproj_body, gate_block0=gate_col0 // tn,
                             ssm_block=ssm_col0 // tn, ssm_off=ssm_col0 % tn)
    return pl.pallas_call(
        body,
        grid=(m // tm, n // tn),
        in_specs=[
            pl.BlockSpec((tm, d), lambda i, j: (i, 0)),
            pl.BlockSpec((None, 1, d), lambda i, j: (layer, 0, 0)),
            pl.BlockSpec((tn, d), lambda i, j: (j, 0)),
            pl.BlockSpec((LANES, d), lambda i, j: (0, 0)),
        ],
        out_specs=[
            pl.BlockSpec((tm, tn), lambda i, j: (i, j)),
            pl.BlockSpec((tm, LANES), lambda i, j: (i, 0)),
            pl.BlockSpec((2, tm // SSM_CHUNK, SSM_CHUNK * ssm_width // 2), lambda i, j: (0, i, 0)),
        ],
        out_shape=[
            jax.ShapeDtypeStruct((m, n), BF16),
            jax.ShapeDtypeStruct((m, LANES), F32),
            jax.ShapeDtypeStruct((2, m // SSM_CHUNK, SSM_CHUNK * ssm_width // 2), BF16),
        ],
        scratch_shapes=[pltpu.VMEM((tm, d), BF16), pltpu.VMEM((ssm_width // LANES, tm, LANES), F32)],
        compiler_params=_cparams(("parallel", "arbitrary")),
        name="inproj",
    )(x, g, w_main, w_glr)


def _pool_mix(x, halo, t, w_ref, scale):
    ts = x.shape[0]
    halo = jnp.where(t > 0, halo, 0.0)
    e = jnp.concatenate([halo, x], axis=0)
    s2 = e + pltpu.roll(e, 1, 0)
    s4 = s2 + pltpu.roll(s2, 2, 0)
    s8 = s4 + pltpu.roll(s4, 4, 0)
    s16 = s8 + pltpu.roll(s8, 8, 0)
    pos = (t * ts + 1 + lax.broadcasted_iota(jnp.int32, (ts, 1), 0)).astype(F32)
    outs = []
    for gi, (w, s) in enumerate(zip(POOL_WINDOWS, (s2, s4, s8, s16))):
        cols = slice(gi * POOL_GROUP, (gi + 1) * POOL_GROUP)
        mean = s[POOL_HALO:, cols] / jnp.minimum(pos, float(w))
        outs.append(_dot((mean - x[:, cols]).astype(BF16), w_ref[gi]))
    return jnp.concatenate(outs, axis=1) * scale


def _gla_head(q, k, v, glr, w2, bias, st, ts):
    c = GLA_CHUNK
    nc = ts // c
    logits = _dot(glr, w2) + bias
    log_a = (jnp.minimum(logits, 0.0) - jnp.log1p(jnp.exp(-jnp.abs(logits)))) * (1.0 / GLA_TAU)
    rowc = lax.broadcasted_iota(jnp.int32, (ts, 1), 0) % c
    bc = log_a
    step = 1
    while step < c:
        bc = bc + jnp.where(rowc >= step, pltpu.roll(bc, step, 0), 0.0)
        step *= 2
    bc3 = bc.reshape(nc, c, GLA_DK)
    b_last = bc3[:, c - 1:c, :]
    b_mid = bc3[:, c // 2 - 1:c // 2, :]
    q3 = q.astype(F32).reshape(nc, c, GLA_DK) * (GLA_DK ** -0.5)
    k3 = k.astype(F32).reshape(nc, c, GLA_DK)
    v3 = v.reshape(nc, c, GLA_DV)
    qe = (q3 * jnp.exp(bc3 - b_mid)).astype(BF16)
    ke = (k3 * jnp.exp(b_mid - bc3)).astype(BF16)
    scores = jnp.einsum('nid,njd->nij', qe, ke, preferred_element_type=F32)
    causal = (lax.broadcasted_iota(jnp.int32, (c, c), 0)
              >= lax.broadcasted_iota(jnp.int32, (c, c), 1))
    scores = jnp.where(causal[None], scores, 0.0).astype(BF16)
    o_intra = jnp.einsum('nij,njv->niv', scores, v3, preferred_element_type=F32)
    qd = (q3 * jnp.exp(bc3)).astype(BF16)
    kd = (k3 * jnp.exp(b_last - bc3)).astype(BF16)
    dec_t = jnp.transpose(jnp.exp(b_last.reshape(nc, GLA_DK)))
    outs = []
    for n in range(nc):
        outs.append(o_intra[n] + _dot(qd[n], st.astype(BF16)))
        kv = lax.dot_general(kd[n], v3[n], (((0,), (0,)), ((), ())), preferred_element_type=F32)
        st = dec_t[:, n:n + 1] * st + kv
    return jnp.concatenate(outs, axis=0).astype(BF16), st


def _gla_body(q_ref, k_ref, v0_ref, v1_ref, glr_ref, w2_ref, b_ref, o_ref, st_ref, *, ts):
    @pl.when(pl.program_id(1) == 0)
    def _():
        st_ref[...] = jnp.zeros_like(st_ref)

    glr = glr_ref[...].astype(BF16)
    v_refs = (v0_ref, v1_ref)
    heads_per_block = v0_ref.shape[1] // GLA_DV
    for h in range(GLA_HEADS):
        v_ref = v_refs[h // heads_per_block]
        vs = slice((h % heads_per_block) * GLA_DV, (h % heads_per_block + 1) * GLA_DV)
        ks = slice(h * GLA_DK, (h + 1) * GLA_DK)
        o, st = _gla_head(q_ref[:, ks], k_ref[:, ks], v_ref[:, vs], glr, w2_ref[h], b_ref[h], st_ref[h], ts)
        st_ref[h] = st
        o_ref[:, h * GLA_DV:(h + 1) * GLA_DV] = o


def _gla(z, glr, w2, bias, layer, batch, seq, cols, *, ts=512):
    h = GLA_HEADS
    nt = seq // ts
    q0, k0, v0 = cols
    wb = h * GLA_DK
    zspec = lambda c0: pl.BlockSpec((ts, wb), lambda b, t: (b * nt + t, c0 // wb))
    wspec = lambda last2: pl.BlockSpec((None, h) + last2, lambda b, t: (layer, 0, 0, 0))
    return pl.pallas_call(
        functools.partial(_gla_body, ts=ts),
        grid=(batch, nt),
        in_specs=[
            zspec(q0), zspec(k0), zspec(v0), zspec(v0 + wb),
            pl.BlockSpec((ts, LANES), lambda b, t: (b * nt + t, 0)),
            wspec((LANES, GLA_DK)), wspec((1, GLA_DK)),
        ],
        out_specs=pl.BlockSpec((ts, h * GLA_DV), lambda b, t: (b * nt + t, 0)),
        out_shape=jax.ShapeDtypeStruct((batch * seq, h * GLA_DV), BF16),
        scratch_shapes=[pltpu.VMEM((h, GLA_DK, GLA_DV), F32)],
        compiler_params=_cparams(("parallel", "arbitrary")),
        name="gla",
    )(z, z, z, z, glr, w2, bias)


SSM_HALF_GROUPS = 16
SSM_BLOCK_GROUPS = 2
SSM_TSTEP = 4


def _ssm_body(x_ref, kc_ref, pc_ref, qc_ref, ar_ref, ai_ref, o_ref, hp_ref, kst_ref, v_ref, *,
              chunks_per_seq, n_state_blocks):
    u = pl.program_id(0)
    rows = x_ref.shape[1]
    hw = SSM_HALF_GROUPS * SSM_GROUP
    blocks_per_half = n_state_blocks // 2
    ns = 2 * SSM_STATE
    bw = SSM_BLOCK_GROUPS * ns
    lane_group = lax.broadcasted_iota(jnp.int32, (1, hw), 1) // SSM_GROUP

    def p_tile(n):
        pc = pc_ref[n // blocks_per_half]
        row_group = (lax.broadcasted_iota(jnp.int32, (pc.shape[0], 1), 0) // SSM_GROUP) % SSM_HALF_GROUPS
        g0 = (n % blocks_per_half) * SSM_BLOCK_GROUPS
        return jnp.concatenate([jnp.where(row_group == g0 + gg, pc, 0.0)
                                for gg in range(SSM_BLOCK_GROUPS)], axis=1).astype(BF16)

    @pl.when(u == 0)
    def _():
        row_group = lax.broadcasted_iota(jnp.int32, (hw, 1), 0) // SSM_GROUP
        for hb in range(2):
            for b in range(SSM_CHUNK):
                kc = kc_ref[hb, SSM_CHUNK - 1 - b]
                tiled = jnp.broadcast_to(kc[None], (SSM_HALF_GROUPS,) + kc.shape).reshape(hw, hw)
                kst_ref[hb, b * hw:(b + 1) * hw, :] = jnp.where(row_group == lane_group, tiled, 0.0).astype(BF16)
            kst_ref[hb, SSM_CHUNK * hw:, :] = jnp.zeros((kst_ref.shape[1] - SSM_CHUNK * hw, hw), BF16)
        v_ref[...] = _dot(x_ref[0], p_tile(0))

    @pl.when(u < n_state_blocks)
    def _():
        rowc = lax.broadcasted_iota(jnp.int32, (rows, 1), 0) % chunks_per_seq
        hst = v_ref[...]
        nxt = jnp.minimum(u + 1, n_state_blocks - 1)
        v_next = _dot(x_ref[nxt // blocks_per_half], p_tile(nxt))
        k, si = 1, 0
        while k < chunks_per_seq:
            hs = jnp.where(rowc >= k, pltpu.roll(hst, k, 0), 0.0)
            sw = jnp.concatenate([pltpu.roll(hs[:, b0:b0 + ns], SSM_STATE, 1)
                                  for b0 in range(0, bw, ns)], axis=1)
            hst = hst + ar_ref[u, si:si + 1, :] * hs + ai_ref[u, si:si + 1, :] * sw
            k *= 2
            si += 1
        hp_ref[u] = jnp.where(rowc >= 1, pltpu.roll(hst, 1, 0), 0.0).astype(BF16)
        v_ref[...] = v_next

    @pl.when(u >= n_state_blocks)
    def _():
        ob = u - n_state_blocks
        hb = ob // SSM_CHUNK
        t = ob % SSM_CHUNK
        qc = qc_ref[hb, t]
        grp = lax.broadcasted_iota(jnp.int32, (SSM_HALF_GROUPS, 1, 1), 0)
        q_tile = jnp.where(grp == lane_group[None], qc[None], 0.0).reshape(SSM_HALF_GROUPS * ns, hw)
        q_tile = q_tile.astype(BF16)
        yq = _dot(hp_ref[hb * blocks_per_half], q_tile[0:bw, :])
        for kk in range(1, blocks_per_half):
            yq = yq + _dot(hp_ref[hb * blocks_per_half + kk], q_tile[kk * bw:(kk + 1) * bw, :])
        start = pl.multiple_of((SSM_CHUNK - 1 - t) * hw, hw)
        for v in range(SSM_CHUNK // SSM_TSTEP):
            kext = (v + 1) * SSM_TSTEP * hw

            @pl.when(t // SSM_TSTEP == v)
            def _():
                intra = _dot(x_ref[hb, :, 0:kext], kst_ref[hb, pl.ds(start, kext), :])
                o_ref[...] = (yq + intra).astype(BF16)


def _ssm_core(xcat, kc, pc, qc, ar, ai, layer, chunks_per_seq):
    _, rows, xw = xcat.shape
    nb = ar.shape[1]
    bw = ar.shape[3]
    hw = SSM_HALF_GROUPS * SSM_GROUP
    nout = 2 * SSM_CHUNK
    one = pl.Buffered(1)
    whole = lambda a: pl.BlockSpec((None,) + a.shape[1:], lambda u: (layer,) + (0,) * (a.ndim - 1),
                                   pipeline_mode=one)
    return pl.pallas_call(
        functools.partial(_ssm_body, chunks_per_seq=chunks_per_seq, n_state_blocks=nb),
        grid=(nb + nout,),
        in_specs=[
            pl.BlockSpec((2, rows, xw), lambda u: (0, 0, 0), pipeline_mode=one),
            whole(kc), whole(pc), whole(qc), whole(ar), whole(ai),
        ],
        out_specs=pl.BlockSpec((None, rows, hw), lambda u: (jnp.maximum(u - nb, 0), 0, 0)),
        out_shape=jax.ShapeDtypeStruct((nout, rows, hw), BF16),
        scratch_shapes=[
            pltpu.VMEM((nb, rows, bw), BF16),
            pltpu.VMEM((2, (SSM_CHUNK + SSM_TSTEP - 1) * hw, hw), BF16),
            pltpu.VMEM((rows, bw), F32),
        ],
        compiler_params=_cparams(("arbitrary",)),
        name="ssm_core",
    )(xcat, kc, pc, qc, ar, ai)


def _ssm_operators(a_re, a_im, log_dt, b_re, b_im, c_re, c_im, chunks_per_seq):
    t = SSM_CHUNK
    g, p = a_re.shape
    hg = b_re.shape[-1]
    hgc = SSM_HALF_GROUPS
    bgc = SSM_BLOCK_GROUPS
    dt = jnp.exp(log_dt)[:, None]
    lam_re, lam_im = dt * a_re, dt * a_im

    def powers(n):
        n = jnp.asarray(n, F32)[:, None, None]
        mag = jnp.exp(n * lam_re)
        return mag * jnp.cos(n * lam_im), mag * jnp.sin(n * lam_im)

    ab_re, ab_im = powers([1.0])
    ab_re, ab_im = ab_re[0], ab_im[0]
    den = a_re * a_re + a_im * a_im
    f_re = ((ab_re - 1.0) * a_re + ab_im * a_im) / den
    f_im = (ab_im * a_re - (ab_re - 1.0) * a_im) / den
    bb_re = f_re[..., None] * b_re - f_im[..., None] * b_im
    bb_im = f_re[..., None] * b_im + f_im[..., None] * b_re
    pw_re, pw_im = powers(np.arange(t + 1))
    cp_re = c_re[None] * pw_re[:, :, None, :] - c_im[None] * pw_im[:, :, None, :]
    cp_im = c_re[None] * pw_im[:, :, None, :] + c_im[None] * pw_re[:, :, None, :]
    ktau = (jnp.einsum('tghp,gpk->tkgh', cp_re[:t], bb_re)
            - jnp.einsum('tghp,gpk->tkgh', cp_im[:t], bb_im))
    kc = ktau.reshape(t, hg, 2, hgc * hg).transpose(2, 0, 1, 3)
    bt_re, bt_im = bb_re.transpose(0, 2, 1), bb_im.transpose(0, 2, 1)
    rp_re = pw_re[t - 1 - np.arange(t)][:, :, None, :]
    rp_im = pw_im[t - 1 - np.arange(t)][:, :, None, :]
    pin = jnp.concatenate([rp_re * bt_re[None] - rp_im * bt_im[None],
                           rp_re * bt_im[None] + rp_im * bt_re[None]], axis=3)
    pc = pin.reshape(t, 2, hgc, hg, 2 * p).transpose(1, 0, 2, 3, 4).reshape(2, t * hgc * hg, 2 * p)
    ct_re, ct_im = c_re.transpose(2, 0, 1)[None], c_im.transpose(2, 0, 1)[None]
    pt_re = pw_re[1:].transpose(0, 2, 1)[..., None]
    pt_im = pw_im[1:].transpose(0, 2, 1)[..., None]
    qg = jnp.concatenate([ct_re * pt_re - ct_im * pt_im,
                          -(ct_re * pt_im + ct_im * pt_re)], axis=1)
    qc = qg.reshape(t, 2 * p, 2, hgc * hg).transpose(2, 0, 1, 3)
    nblk = g // bgc
    nsteps = int(np.log2(chunks_per_seq))
    sr, si = powers(t * 2.0 ** np.arange(nsteps))
    blk = lambda a: a.reshape(nsteps, nblk, bgc * 2 * p).transpose(1, 0, 2)
    ar = blk(jnp.concatenate([sr, sr], axis=2))
    ai = blk(jnp.concatenate([-si, si], axis=2))
    return kc, pc, qc, ar, ai


def _merge_body(x_ref, pin_ref, halo_ref, og_ref, r0_ref, r1_ref, yc_ref, sin_ref, g0_ref, g1_ref, g2_ref,
                pw_ref, ps_ref, gain_ref, d_ref, wglu_ref, wbp_ref, wbg0_ref, wbg1_ref, wbs_ref, wo_ref,
                o_ref, ys_ref, *, tiles_per_seq):
    t = pl.program_id(0) % tiles_per_seq
    yp = _pool_mix(pin_ref[...].astype(F32), halo_ref[...].astype(F32), t, pw_ref, ps_ref[...])
    r_refs = (r0_ref, r1_ref)
    heads_per_block = r0_ref.shape[1] // GLA_DV
    yg_heads = []
    for h in range(GLA_HEADS):
        hs = slice(h * GLA_DV, (h + 1) * GLA_DV)
        rs = slice((h % heads_per_block) * GLA_DV, (h % heads_per_block + 1) * GLA_DV)
        o = og_ref[:, hs].astype(F32)
        o = o * lax.rsqrt(jnp.mean(o * o, axis=-1, keepdims=True) + EPS) * gain_ref[:, hs]
        yg_heads.append((o * jax.nn.silu(r_refs[h // heads_per_block][:, rs].astype(F32))).astype(BF16))
    yg = jnp.concatenate(yg_heads, axis=1)
    per_half = yc_ref.shape[2] // LANES
    rows = yc_ref.shape[1]
    for ob in range(yc_ref.shape[0]):
        hb, t = divmod(ob, SSM_CHUNK)
        plane = yc_ref[ob].astype(F32)
        for c in range(per_half):
            ys_ref[hb * per_half + c, pl.ds(t, rows, stride=SSM_CHUNK), :] = plane[:, c * LANES:(c + 1) * LANES]
    ys = jnp.concatenate([ys_ref[c] for c in range(ys_ref.shape[0])], axis=1)
    y = ys + d_ref[...] * sin_ref[...].astype(F32)
    y = jax.nn.gelu(y)
    y = y * jax.nn.sigmoid(_dot(y.astype(BF16), wglu_ref[...]))
    m = g0_ref[...].astype(F32) * _dot(yp.astype(BF16), wbp_ref[...])
    half = wbg0_ref.shape[0]
    gla_proj = _dot(yg[:, :half], wbg0_ref[...]) + _dot(yg[:, half:], wbg1_ref[...])
    m = m + g1_ref[...].astype(F32) * gla_proj
    m = m + g2_ref[...].astype(F32) * _dot(y.astype(BF16), wbs_ref[...])
    o_ref[...] = x_ref[...] + _dot(m.astype(BF16), wo_ref[...])


def _merge(x, o_gla, ycat, z, r_col0, ssm_col0, gate_col0, seq, pool_w, pool_scale, gla_gain,
           d_skip, w_glu, wb, w_out, layer, *, tm=256):
    m, d = x.shape
    wp, wg, ws = pool_scale.shape[2], o_gla.shape[1], 2 * ycat.shape[2]
    br = BRANCH_ROWS
    assert wp == br and ws == br and wg == 2 * br and seq % tm == 0 and tm % POOL_HALO == 0
    gb = gate_col0 // d
    hb = tm // POOL_HALO
    one = pl.Buffered(1)
    wspec = lambda shape, r: pl.BlockSpec((None,) + shape, lambda i: (layer, r, 0), pipeline_mode=one)
    return pl.pallas_call(
        functools.partial(_merge_body, tiles_per_seq=seq // tm),
        grid=(m // tm,),
        in_specs=[
            pl.BlockSpec((tm, d), lambda i: (i, 0)),
            pl.BlockSpec((tm, wp), lambda i: (i, 0)),
            pl.BlockSpec((POOL_HALO, wp), lambda i: (jnp.maximum(i * hb - 1, 0), 0)),
            pl.BlockSpec((tm, wg), lambda i: (i, 0)),
            pl.BlockSpec((tm, br), lambda i: (i, r_col0 // br)),
            pl.BlockSpec((tm, br), lambda i: (i, r_col0 // br + 1)),
            pl.BlockSpec((ycat.shape[0], tm // SSM_CHUNK, ws // 2), lambda i: (0, i, 0)),
            pl.BlockSpec((tm, ws), lambda i: (i, ssm_col0 // ws)),
            pl.BlockSpec((tm, d), lambda i: (i, gb)),
            pl.BlockSpec((tm, d), lambda i: (i, gb + 1)),
            pl.BlockSpec((tm, d), lambda i: (i, gb + 2)),
            pl.BlockSpec((None, len(POOL_WINDOWS), POOL_GROUP, POOL_GROUP), lambda i: (layer, 0, 0, 0),
                         pipeline_mode=one),
            wspec((1, wp), 0),
            wspec((1, wg), 0),
            wspec((1, ws), 0),
            wspec((ws, ws), 0),
            wspec((br, d), 0),
            wspec((br, d), 1),
            wspec((br, d), 2),
            wspec((br, d), 3),
            wspec((d, d), 0),
        ],
        out_specs=pl.BlockSpec((tm, d), lambda i: (i, 0)),
        out_shape=jax.ShapeDtypeStruct((m, d), F32),
        scratch_shapes=[pltpu.VMEM((ws // LANES, tm, LANES), F32)],
        compiler_params=_cparams(("parallel",)),
        name="merge",
    )(x, z, z, o_gla, z, z, ycat, z, z, z, z, pool_w, pool_scale, gla_gain, d_skip, w_glu,
      wb, wb, wb, wb, w_out)


def kernel(x, ffn1_norm, ffn1_w_gate, ffn1_w_up, ffn1_w_down, mix_norm, w_in, pool_w, pool_scale, gla_w_gate2, gla_gate_bias, gla_norm, ssm_a_re, ssm_a_im, ssm_log_dt, ssm_b_re, ssm_b_im, ssm_c_re, ssm_c_im, ssm_d, ssm_w_glu, w_branch, w_out, ffn2_norm, ffn2_w_gate, ffn2_w_up, ffn2_w_down, final_norm):
    batch, seq, d = x.shape
    depth = w_in.shape[0]
    m = batch * seq
    pool_width = pool_scale.shape[1]
    qk_width = GLA_HEADS * GLA_DK
    v_width = GLA_HEADS * GLA_DV
    ssm_width = ssm_d.shape[1]
    n_groups = ssm_width // SSM_GROUP
    q0 = pool_width
    k0 = q0 + qk_width
    v0 = k0 + qk_width
    r0 = v0 + v_width
    glr0 = r0 + v_width
    ssm_src0 = glr0 + GLA_RANK
    ssm0 = glr0
    gate0 = ssm0 + ssm_width
    chunks_per_seq = seq // SSM_CHUNK

    bf = lambda a: a.astype(BF16)
    row3 = lambda a: a.reshape(depth, 1, a.shape[-1])
    w_in_t = jnp.swapaxes(w_in, 1, 2)
    ffn1_w = (ffn1_w_gate, ffn1_w_up, ffn1_w_down)
    ffn2_w = (ffn2_w_gate, ffn2_w_up, ffn2_w_down)
    f1 = [bf(w[0]) for w in ffn1_w]
    wbr, wo, wglu, pw = bf(w_branch), bf(w_out), bf(ssm_w_glu), bf(pool_w)
    n1, nm, n2 = row3(ffn1_norm), row3(mix_norm), row3(ffn2_norm)
    pscale, dskip = row3(pool_scale), row3(ssm_d)
    w2 = gla_w_gate2.reshape(depth, GLA_RANK, GLA_HEADS, GLA_DK).transpose(0, 2, 1, 3)
    w2 = bf(jnp.pad(w2, ((0, 0), (0, 0), (0, LANES - GLA_RANK), (0, 0))))
    gbias = gla_gate_bias.reshape(depth, GLA_HEADS, 1, GLA_DK)
    ggain = row3(gla_norm)
    ssm_ops = jax.vmap(functools.partial(_ssm_operators, chunks_per_seq=chunks_per_seq))(
        ssm_a_re, ssm_a_im, ssm_log_dt, ssm_b_re, ssm_b_im, ssm_c_re, ssm_c_im)

    xf = x.reshape(m, d)
    for l in range(depth):
        xf, *side = _ffn(xf, n1, l, *f1, cast=(l,) + ffn2_w, repack=(l, w_in_t, glr0, GLA_RANK))
        f2, (w_main, w_glr) = side[:3], side[3:]
        z, glr, xcat = _inproj(xf, nm, w_main, w_glr, l, ssm0, ssm_width, gate0)
        o_gla = _gla(z, glr, w2, gbias, l, batch, seq, (q0, k0, v0))
        ycat = _ssm_core(xcat, *ssm_ops, l, chunks_per_seq)
        xf = _merge(xf, o_gla, ycat, z, r0, ssm0, gate0, seq, pw, pscale, ggain, dskip, wglu, wbr, wo, l)
        last = l == depth - 1
        xf, *f1 = _ffn(xf, n2, l, *f2, cast=None if last else (l + 1,) + ffn1_w,
                       final_g=final_norm if last else None)
    return xf.reshape(batch, seq, d)
```

```python
import functools

import jax
import jax.numpy as jnp
import numpy as np
from jax import lax
from jax.experimental import pallas as pl
from jax.experimental.pallas import tpu as pltpu

F32 = jnp.float32
BF16 = jnp.bfloat16

EPS = 1e-6
POOL_WINDOWS = (2, 4, 8, 16)
POOL_GROUP = 128
POOL_HALO = 16
GLA_HEADS = 4
GLA_DK = 128
GLA_DV = 256
GLA_RANK = 16
GLA_TAU = 16.0
GLA_CHUNK = 64
SSM_GROUP = 16
SSM_STATE = 64
SSM_CHUNK = 16
LANES = 128
FFN_DOWN_CHUNK = 512
FFN_ROW_CHUNK = 1024
FFN_EPILOGUE_ROWS = 256
REPACK_ROWS = 64
BRANCH_ROWS = 512

VMEM_LIMIT = 60 * 1024 * 1024


def _cparams(sem):
    return pltpu.CompilerParams(dimension_semantics=sem, vmem_limit_bytes=VMEM_LIMIT)


def _rms(xf, g):
    return xf * lax.rsqrt(jnp.mean(xf * xf, axis=-1, keepdims=True) + EPS) * g


def _dot(a, b):
    return jnp.dot(a, b, preferred_element_type=F32)


def _dot_nt(a, b_t):
    return lax.dot_general(a, b_t, (((1,), (1,)), ((), ())), preferred_element_type=F32)


def _ffn_body(x_ref, g_ref, wg_ref, wu_ref, wd_ref, *rest, final, n_cast, repack):
    rest = list(rest)
    cast_in = [rest.pop(0) for _ in range(n_cast)]
    rp_in = [rest.pop(0) for _ in range(2 if repack else 0)]
    fg_ref = rest.pop(0) if final else None
    o_ref = rest.pop(0)
    cast_out = [rest.pop(0) for _ in range(n_cast)]
    rp_out = [rest.pop(0) for _ in range(2 if repack else 0)]
    xn_ref, = rest
    j = pl.program_id(1)

    for src, dst in zip(cast_in, cast_out):
        dst[...] = src[...].astype(BF16)
    if repack:
        (rows_ref, drop_ref), (w_ref, wl_ref) = rp_in, rp_out
        w_ref[...] = rows_ref[0].astype(BF16)
        pad = jnp.zeros((wl_ref.shape[0] - drop_ref.shape[1], drop_ref.shape[2]), F32)
        wl_ref[...] = jnp.concatenate([drop_ref[0], pad], axis=0).astype(BF16)

    row_chunks = [slice(r0, r0 + FFN_ROW_CHUNK) for r0 in range(0, o_ref.shape[0], FFN_ROW_CHUNK)]

    @pl.when(j == 0)
    def _():
        for rs in row_chunks:
            xn_ref[rs, :] = _rms(x_ref[rs, :], g_ref[...]).astype(BF16)
            o_ref[rs, :] = jnp.zeros((FFN_ROW_CHUNK, o_ref.shape[1]), F32)

    for rs in row_chunks:
        xn = xn_ref[rs, :]
        a = _dot(xn, wg_ref[...])
        b = _dot(xn, wu_ref[...])
        h = (jax.nn.silu(a) * b).astype(BF16)
        for c0 in range(0, o_ref.shape[1], FFN_DOWN_CHUNK):
            cs = slice(c0, c0 + FFN_DOWN_CHUNK)
            o_ref[rs, cs] += _dot(h, wd_ref[:, cs])

    @pl.when(j == pl.num_programs(1) - 1)
    def _():
        def epilogue(r, carry):
            rs = pl.ds(pl.multiple_of(r * FFN_EPILOGUE_ROWS, FFN_EPILOGUE_ROWS), FFN_EPILOGUE_ROWS)
            y = x_ref[rs, :] + 0.5 * o_ref[rs, :]
            if final:
                y = _rms(y, fg_ref[...])
            o_ref[rs, :] = y
            return carry
        lax.fori_loop(0, o_ref.shape[0] // FFN_EPILOGUE_ROWS, epilogue, 0)


def _ffn(x, g, layer, wg, wu, wd, cast=None, repack=None, final_g=None, *, tm=1024, tf=512):
    m, d = x.shape
    f = wg.shape[1]
    ni, nj = m // tm, f // tf
    final = final_g is not None
    in_specs = [
        pl.BlockSpec((tm, d), lambda i, j: (i, 0)),
        pl.BlockSpec((None, 1, d), lambda i, j: (layer, 0, 0)),
        pl.BlockSpec((d, tf), lambda i, j: (0, j)),
        pl.BlockSpec((d, tf), lambda i, j: (0, j)),
        pl.BlockSpec((tf, d), lambda i, j: (j, 0)),
    ]
    args = [x, g, wg, wu, wd]
    out_specs = [pl.BlockSpec((tm, d), lambda i, j: (i, 0))]
    out_shape = [jax.ShapeDtypeStruct((m, d), F32)]
    n_cast = 0
    if cast is not None:
        cl, cg, cu, cd = cast
        n_cast = 3
        rows_up = d // ni
        rows_dn = f // (ni * nj)
        assert d % ni == 0 and f % (ni * nj) == 0 and rows_up % 16 == 0 and rows_dn % 16 == 0
        for w in (cg, cu):
            in_specs.append(pl.BlockSpec((None, rows_up, tf), lambda i, j: (cl, i, j)))
            out_specs.append(pl.BlockSpec((rows_up, tf), lambda i, j: (i, j)))
            out_shape.append(jax.ShapeDtypeStruct((d, f), BF16))
            args.append(w)
        in_specs.append(pl.BlockSpec((None, rows_dn, d), lambda i, j: (cl, i * nj + j, 0)))
        out_specs.append(pl.BlockSpec((rows_dn, d), lambda i, j: (i * nj + j, 0)))
        out_shape.append(jax.ShapeDtypeStruct((f, d), BF16))
        args.append(cd)
    if repack is not None:
        rl, w_in_t, drop0, drop_width = repack
        n = w_in_t.shape[1] - drop_width
        rb = REPACK_ROWS
        nblk = n // rb
        assert n % rb == 0 and nblk <= ni * nj and drop0 % rb == 0 and drop_width % 8 == 0
        blk = lambda i, j: jnp.minimum(i * nj + j, nblk - 1)
        src_row = lambda b: (b * (rb // 8) + jnp.where(b >= drop0 // rb, drop_width // 8, 0)) * 8
        in_specs.append(pl.BlockSpec((pl.Element(1), pl.Element(rb), pl.Element(d)),
                                     lambda i, j: (rl, src_row(blk(i, j)), 0)))
        in_specs.append(pl.BlockSpec((pl.Element(1), pl.Element(drop_width), pl.Element(d)),
                                     lambda i, j: (rl, drop0, 0)))
        out_specs.append(pl.BlockSpec((rb, d), lambda i, j: (blk(i, j), 0)))
        out_specs.append(pl.BlockSpec((LANES, d), lambda i, j: (0, 0)))
        out_shape.append(jax.ShapeDtypeStruct((n, d), BF16))
        out_shape.append(jax.ShapeDtypeStruct((LANES, d), BF16))
        args += [w_in_t, w_in_t]
    if final:
        in_specs.append(pl.BlockSpec((1, d), lambda i, j: (0, 0)))
        args.append(final_g.reshape(1, d))
    return pl.pallas_call(
        functools.partial(_ffn_body, final=final, n_cast=n_cast, repack=repack is not None),
        grid=(ni, nj),
        in_specs=in_specs,
        out_specs=out_specs,
        out_shape=out_shape,
        scratch_shapes=[pltpu.VMEM((tm, d), BF16)],
        compiler_params=_cparams(("arbitrary", "arbitrary")),
        name="ffn_final" if final else "ffn",
    )(*args)


def _inproj_body(x_ref, g_ref, w_ref, wl_ref, z_ref, glr_ref, xc_ref, un_ref, ph_ref, *,
                 gate_block0, ssm_block, ssm_off):
    j = pl.program_id(1)

    @pl.when(j == 0)
    def _():
        un = _rms(x_ref[...], g_ref[...]).astype(BF16)
        un_ref[...] = un
        glr_ref[...] = _dot_nt(un, wl_ref[...])

    acc = _dot_nt(un_ref[...], w_ref[...])
    z_ref[...] = jnp.where(j >= gate_block0, jax.nn.sigmoid(acc), acc).astype(BF16)

    @pl.when(j == ssm_block)
    def _():
        ncol = ph_ref.shape[0]
        for c in range(ncol):
            ph_ref[c] = acc[:, ssm_off + c * LANES:ssm_off + (c + 1) * LANES]
        rows = ph_ref.shape[1] // SSM_CHUNK
        for s in range(SSM_CHUNK):
            for c in range(ncol):
                piece = ph_ref[c, pl.ds(s, rows, stride=SSM_CHUNK), :].astype(BF16)
                xc_ref[c, :, s * LANES:(s + 1) * LANES] = piece


def _inproj(x, g, w_main, w_glr, layer, ssm_col0, ssm_width, gate_col0, *, tm=1024, tn=1024):
    m, d = x.shape
    n = w_main.shape[0]
    body = functools.partial(_inproj_body, gate_block0=gate_col0 // tn,
                             ssm_block=ssm_col0 // tn, ssm_off=ssm_col0 % tn)
    return pl.pallas_call(
        body,
        grid=(m // tm, n // tn),
        in_specs=[
            pl.BlockSpec((tm, d), lambda i, j: (i, 0)),
            pl.BlockSpec((None, 1, d), lambda i, j: (layer, 0, 0)),
            pl.BlockSpec((tn, d), lambda i, j: (j, 0)),
            pl.BlockSpec((LANES, d), lambda i, j: (0, 0)),
        ],
        out_specs=[
            pl.BlockSpec((tm, tn), lambda i, j: (i, j)),
            pl.BlockSpec((tm, LANES), lambda i, j: (i, 0)),
            pl.BlockSpec((ssm_width // LANES, tm // SSM_CHUNK, SSM_CHUNK * LANES), lambda i, j: (0, i, 0)),
        ],
        out_shape=[
            jax.ShapeDtypeStruct((m, n), BF16),
            jax.ShapeDtypeStruct((m, LANES), F32),
            jax.ShapeDtypeStruct((ssm_width // LANES, m // SSM_CHUNK, SSM_CHUNK * LANES), BF16),
        ],
        scratch_shapes=[pltpu.VMEM((tm, d), BF16), pltpu.VMEM((ssm_width // LANES, tm, LANES), F32)],
        compiler_params=_cparams(("parallel", "arbitrary")),
        name="inproj",
    )(x, g, w_main, w_glr)


def _pool_mix(x, halo, t, w_ref, scale):
    ts = x.shape[0]
    halo = jnp.where(t > 0, halo, 0.0)
    e = jnp.concatenate([halo, x], axis=0)
    s2 = e + pltpu.roll(e, 1, 0)
    s4 = s2 + pltpu.roll(s2, 2, 0)
    s8 = s4 + pltpu.roll(s4, 4, 0)
    s16 = s8 + pltpu.roll(s8, 8, 0)
    pos = (t * ts + 1 + lax.broadcasted_iota(jnp.int32, (ts, 1), 0)).astype(F32)
    outs = []
    for gi, (w, s) in enumerate(zip(POOL_WINDOWS, (s2, s4, s8, s16))):
        cols = slice(gi * POOL_GROUP, (gi + 1) * POOL_GROUP)
        mean = s[POOL_HALO:, cols] / jnp.minimum(pos, float(w))
        outs.append(_dot((mean - x[:, cols]).astype(BF16), w_ref[gi]))
    return jnp.concatenate(outs, axis=1) * scale


def _gla_head(q, k, v, glr, w2, bias, st, ts):
    c = GLA_CHUNK
    nc = ts // c
    logits = _dot(glr, w2) + bias
    log_a = (jnp.minimum(logits, 0.0) - jnp.log1p(jnp.exp(-jnp.abs(logits)))) * (1.0 / GLA_TAU)
    rowc = lax.broadcasted_iota(jnp.int32, (ts, 1), 0) % c
    bc = log_a
    step = 1
    while step < c:
        bc = bc + jnp.where(rowc >= step, pltpu.roll(bc, step, 0), 0.0)
        step *= 2
    bc3 = bc.reshape(nc, c, GLA_DK)
    b_last = bc3[:, c - 1:c, :]
    b_mid = bc3[:, c // 2 - 1:c // 2, :]
    q3 = q.astype(F32).reshape(nc, c, GLA_DK) * (GLA_DK ** -0.5)
    k3 = k.astype(F32).reshape(nc, c, GLA_DK)
    v3 = v.reshape(nc, c, GLA_DV)
    qe = (q3 * jnp.exp(bc3 - b_mid)).astype(BF16)
    ke = (k3 * jnp.exp(b_mid - bc3)).astype(BF16)
    scores = jnp.einsum('nid,njd->nij', qe, ke, preferred_element_type=F32)
    causal = (lax.broadcasted_iota(jnp.int32, (c, c), 0)
              >= lax.broadcasted_iota(jnp.int32, (c, c), 1))
    scores = jnp.where(causal[None], scores, 0.0).astype(BF16)
    o_intra = jnp.einsum('nij,njv->niv', scores, v3, preferred_element_type=F32)
    qd = (q3 * jnp.exp(bc3)).astype(BF16)
    kd = (k3 * jnp.exp(b_last - bc3)).astype(BF16)
    dec_t = jnp.transpose(jnp.exp(b_last.reshape(nc, GLA_DK)))
    outs = []
    for n in range(nc):
        outs.append(o_intra[n] + _dot(qd[n], st.astype(BF16)))
        kv = lax.dot_general(kd[n], v3[n], (((0,), (0,)), ((), ())), preferred_element_type=F32)
        st = dec_t[:, n:n + 1] * st + kv
    return jnp.concatenate(outs, axis=0).astype(BF16), st


def _gla_body(q_ref, k_ref, v0_ref, v1_ref, glr_ref, w2_ref, b_ref, o_ref, st_ref, *, ts):
    @pl.when(pl.program_id(1) == 0)
    def _():
        st_ref[...] = jnp.zeros_like(st_ref)

    v_refs = (v0_ref, v1_ref)
    heads_per_block = v0_ref.shape[2] // GLA_DV
    for s in range(q_ref.shape[0]):
        glr = glr_ref[s].astype(BF16)
        for h in range(GLA_HEADS):
            v_ref = v_refs[h // heads_per_block]
            vs = slice((h % heads_per_block) * GLA_DV, (h % heads_per_block + 1) * GLA_DV)
            ks = slice(h * GLA_DK, (h + 1) * GLA_DK)
            o, st = _gla_head(q_ref[s, :, ks], k_ref[s, :, ks], v_ref[s, :, vs], glr,
                              w2_ref[h], b_ref[h], st_ref[s, h], ts)
            st_ref[s, h] = st
            o_ref[s, :, h * GLA_DV:(h + 1) * GLA_DV] = o


def _gla(z, glr, w2, bias, layer, batch, seq, cols, *, ts=512, nseq=1):
    h = GLA_HEADS
    nt = seq // ts
    q0, k0, v0 = cols
    wb = h * GLA_DK
    z3 = z.reshape(batch, seq, z.shape[1])
    zspec = lambda c0: pl.BlockSpec((nseq, ts, wb), lambda b, t: (b, t, c0 // wb))
    wspec = lambda last2: pl.BlockSpec((None, h) + last2, lambda b, t: (layer, 0, 0, 0))
    out = pl.pallas_call(
        functools.partial(_gla_body, ts=ts),
        grid=(batch // nseq, nt),
        in_specs=[
            zspec(q0), zspec(k0), zspec(v0), zspec(v0 + wb),
            pl.BlockSpec((nseq, ts, LANES), lambda b, t: (b, t, 0)),
            wspec((LANES, GLA_DK)), wspec((1, GLA_DK)),
        ],
        out_specs=pl.BlockSpec((nseq, ts, h * GLA_DV), lambda b, t: (b, t, 0)),
        out_shape=jax.ShapeDtypeStruct((batch, seq, h * GLA_DV), BF16),
        scratch_shapes=[pltpu.VMEM((nseq, h, GLA_DK, GLA_DV), F32)],
        compiler_params=_cparams(("parallel", "arbitrary")),
        name="gla",
    )(z3, z3, z3, z3, glr.reshape(batch, seq, LANES), w2, bias)
    return out.reshape(batch * seq, h * GLA_DV)


SSM_COL_GROUPS = LANES // SSM_GROUP
SSM_BLOCK_GROUPS = 2
SSM_TPAIR = 2


def _ssm_body(x_ref, kc_ref, pc_ref, qc_ref, ar_ref, ai_ref, o_ref, hp_ref, kst_ref, v_ref, *,
              chunks_per_seq, n_state_blocks):
    u = pl.program_id(0)
    ncol, rows = x_ref.shape[0], x_ref.shape[1]
    blocks_per_col = n_state_blocks // ncol
    npairs = SSM_CHUNK // SSM_TPAIR
    ns = 2 * SSM_STATE
    bw = SSM_BLOCK_GROUPS * ns
    lane_group = lax.broadcasted_iota(jnp.int32, (1, LANES), 1) // SSM_GROUP

    def p_tile(n):
        pc = pc_ref[n // blocks_per_col]
        row_group = (lax.broadcasted_iota(jnp.int32, (pc.shape[0], 1), 0) // SSM_GROUP) % SSM_COL_GROUPS
        g0 = (n % blocks_per_col) * SSM_BLOCK_GROUPS
        return jnp.concatenate([jnp.where(row_group == g0 + gg, pc, 0.0)
                                for gg in range(SSM_BLOCK_GROUPS)], axis=1).astype(BF16)

    @pl.when(u == 0)
    def _():
        row_group = lax.broadcasted_iota(jnp.int32, (LANES, 1), 0) // SSM_GROUP

        def kbd(c, lag):
            kc = kc_ref[c, lag]
            tiled = jnp.broadcast_to(kc[None], (SSM_COL_GROUPS,) + kc.shape).reshape(LANES, LANES)
            return jnp.where(row_group == lane_group, tiled, 0.0).astype(BF16)

        for c in range(ncol):
            for r in range(SSM_CHUNK):
                left = kbd(c, SSM_CHUNK - 2 - r) if r <= SSM_CHUNK - 2 else jnp.zeros((LANES, LANES), BF16)
                kst_ref[c, r * LANES:(r + 1) * LANES, :] = jnp.concatenate(
                    [left, kbd(c, SSM_CHUNK - 1 - r)], axis=1)
        v_ref[...] = _dot(x_ref[0], p_tile(0))

    @pl.when(u < n_state_blocks)
    def _():
        rowc = lax.broadcasted_iota(jnp.int32, (rows, 1), 0) % chunks_per_seq
        hst = v_ref[...]
        nxt = jnp.minimum(u + 1, n_state_blocks - 1)
        v_next = _dot(x_ref[nxt // blocks_per_col], p_tile(nxt))
        k, si = 1, 0
        while k < chunks_per_seq:
            hs = jnp.where(rowc >= k, pltpu.roll(hst, k, 0), 0.0)
            sw = jnp.concatenate([pltpu.roll(hs[:, b0:b0 + ns], SSM_STATE, 1)
                                  for b0 in range(0, bw, ns)], axis=1)
            hst = hst + ar_ref[u, si:si + 1, :] * hs + ai_ref[u, si:si + 1, :] * sw
            k *= 2
            si += 1
        hp_ref[u] = jnp.where(rowc >= 1, pltpu.roll(hst, 1, 0), 0.0).astype(BF16)
        v_ref[...] = v_next

    @pl.when(u >= n_state_blocks)
    def _():
        ob = u - n_state_blocks
        c = ob // npairs
        v = ob % npairs
        grp = lax.broadcasted_iota(jnp.int32, (SSM_COL_GROUPS, 1, 1), 0)
        q_tile = jnp.concatenate(
            [jnp.where(grp == lane_group[None], qc_ref[c, SSM_TPAIR * v + tt][None], 0.0)
             .reshape(SSM_COL_GROUPS * ns, LANES) for tt in range(SSM_TPAIR)], axis=1).astype(BF16)
        yq = _dot(hp_ref[c * blocks_per_col], q_tile[0:bw, :])
        for kk in range(1, blocks_per_col):
            yq = yq + _dot(hp_ref[c * blocks_per_col + kk], q_tile[kk * bw:(kk + 1) * bw, :])
        for vs in range(npairs):
            kext = (vs + 1) * SSM_TPAIR * LANES

            @pl.when(v == vs)
            def _():
                start = (npairs - 1 - vs) * SSM_TPAIR * LANES
                intra = _dot(x_ref[c, :, 0:kext], kst_ref[c, start:start + kext, :])
                o_ref[...] = (yq + intra).astype(BF16)


def _ssm_core(xcat, kc, pc, qc, ar, ai, layer, chunks_per_seq):
    ncol, rows, xw = xcat.shape
    nb = ar.shape[1]
    bw = ar.shape[3]
    ow = SSM_TPAIR * LANES
    nout = ncol * SSM_CHUNK // SSM_TPAIR
    one = pl.Buffered(1)
    whole = lambda a: pl.BlockSpec((None,) + a.shape[1:], lambda u: (layer,) + (0,) * (a.ndim - 1),
                                   pipeline_mode=one)
    return pl.pallas_call(
        functools.partial(_ssm_body, chunks_per_seq=chunks_per_seq, n_state_blocks=nb),
        grid=(nb + nout,),
        in_specs=[
            pl.BlockSpec((ncol, rows, xw), lambda u: (0, 0, 0), pipeline_mode=one),
            whole(kc), whole(pc), whole(qc), whole(ar), whole(ai),
        ],
        out_specs=pl.BlockSpec((None, rows, ow), lambda u: (jnp.maximum(u - nb, 0), 0, 0)),
        out_shape=jax.ShapeDtypeStruct((nout, rows, ow), BF16),
        scratch_shapes=[
            pltpu.VMEM((nb, rows, bw), BF16),
            pltpu.VMEM((ncol, SSM_CHUNK * LANES, ow), BF16),
            pltpu.VMEM((rows, bw), F32),
        ],
        compiler_params=_cparams(("arbitrary",)),
        name="ssm_core",
    )(xcat, kc, pc, qc, ar, ai)


def _ssm_operators(a_re, a_im, log_dt, b_re, b_im, c_re, c_im, chunks_per_seq):
    t = SSM_CHUNK
    g, p = a_re.shape
    hg = b_re.shape[-1]
    cgc = SSM_COL_GROUPS
    ncol = g // cgc
    bgc = SSM_BLOCK_GROUPS
    dt = jnp.exp(log_dt)[:, None]
    lam_re, lam_im = dt * a_re, dt * a_im

    def powers(n):
        n = jnp.asarray(n, F32)[:, None, None]
        mag = jnp.exp(n * lam_re)
        return mag * jnp.cos(n * lam_im), mag * jnp.sin(n * lam_im)

    ab_re, ab_im = powers([1.0])
    ab_re, ab_im = ab_re[0], ab_im[0]
    den = a_re * a_re + a_im * a_im
    f_re = ((ab_re - 1.0) * a_re + ab_im * a_im) / den
    f_im = (ab_im * a_re - (ab_re - 1.0) * a_im) / den
    bb_re = f_re[..., None] * b_re - f_im[..., None] * b_im
    bb_im = f_re[..., None] * b_im + f_im[..., None] * b_re
    pw_re, pw_im = powers(np.arange(t + 1))
    cp_re = c_re[None] * pw_re[:, :, None, :] - c_im[None] * pw_im[:, :, None, :]
    cp_im = c_re[None] * pw_im[:, :, None, :] + c_im[None] * pw_re[:, :, None, :]
    ktau = (jnp.einsum('tghp,gpk->tkgh', cp_re[:t], bb_re)
            - jnp.einsum('tghp,gpk->tkgh', cp_im[:t], bb_im))
    kc = ktau.reshape(t, hg, ncol, cgc * hg).transpose(2, 0, 1, 3)
    bt_re, bt_im = bb_re.transpose(0, 2, 1), bb_im.transpose(0, 2, 1)
    rp_re = pw_re[t - 1 - np.arange(t)][:, :, None, :]
    rp_im = pw_im[t - 1 - np.arange(t)][:, :, None, :]
    pin = jnp.concatenate([rp_re * bt_re[None] - rp_im * bt_im[None],
                           rp_re * bt_im[None] + rp_im * bt_re[None]], axis=3)
    pc = pin.reshape(t, ncol, cgc, hg, 2 * p).transpose(1, 0, 2, 3, 4).reshape(ncol, t * cgc * hg, 2 * p)
    ct_re, ct_im = c_re.transpose(2, 0, 1)[None], c_im.transpose(2, 0, 1)[None]
    pt_re = pw_re[1:].transpose(0, 2, 1)[..., None]
    pt_im = pw_im[1:].transpose(0, 2, 1)[..., None]
    qg = jnp.concatenate([ct_re * pt_re - ct_im * pt_im,
                          -(ct_re * pt_im + ct_im * pt_re)], axis=1)
    qc = qg.reshape(t, 2 * p, ncol, cgc * hg).transpose(2, 0, 1, 3)
    nblk = g // bgc
    nsteps = int(np.log2(chunks_per_seq))
    sr, si = powers(t * 2.0 ** np.arange(nsteps))
    blk = lambda a: a.reshape(nsteps, nblk, bgc * 2 * p).transpose(1, 0, 2)
    ar = blk(jnp.concatenate([sr, sr], axis=2))
    ai = blk(jnp.concatenate([-si, si], axis=2))
    return kc, pc, qc, ar, ai


def _merge_body(x_ref, pin_ref, halo_ref, og_ref, r0_ref, r1_ref, yc_ref, sin_ref, g0_ref, g1_ref, g2_ref,
                pw_ref, ps_ref, gain_ref, d_ref, wglu_ref, wbp_ref, wbg0_ref, wbg1_ref, wbs_ref, wo_ref,
                o_ref, ys_ref, *, tiles_per_seq):
    t = pl.program_id(0) % tiles_per_seq
    yp = _pool_mix(pin_ref[...].astype(F32), halo_ref[...].astype(F32), t, pw_ref, ps_ref[...])
    r_refs = (r0_ref, r1_ref)
    heads_per_block = r0_ref.shape[1] // GLA_DV
    yg_heads = []
    for h in range(GLA_HEADS):
        hs = slice(h * GLA_DV, (h + 1) * GLA_DV)
        rs = slice((h % heads_per_block) * GLA_DV, (h % heads_per_block + 1) * GLA_DV)
        o = og_ref[:, hs].astype(F32)
        o = o * lax.rsqrt(jnp.mean(o * o, axis=-1, keepdims=True) + EPS) * gain_ref[:, hs]
        yg_heads.append((o * jax.nn.silu(r_refs[h // heads_per_block][:, rs].astype(F32))).astype(BF16))
    yg = jnp.concatenate(yg_heads, axis=1)
    rows = yc_ref.shape[1]
    npairs = SSM_CHUNK // SSM_TPAIR
    for ob in range(yc_ref.shape[0]):
        c, v = divmod(ob, npairs)
        plane = yc_ref[ob].astype(F32)
        for tt in range(SSM_TPAIR):
            ys_ref[c, pl.ds(SSM_TPAIR * v + tt, rows, stride=SSM_CHUNK), :] = plane[:, tt * LANES:(tt + 1) * LANES]
    ys = jnp.concatenate([ys_ref[c] for c in range(ys_ref.shape[0])], axis=1)
    y = ys + d_ref[...] * sin_ref[...].astype(F32)
    y = jax.nn.gelu(y)
    y = y * jax.nn.sigmoid(_dot(y.astype(BF16), wglu_ref[...]))
    m = g0_ref[...].astype(F32) * _dot(yp.astype(BF16), wbp_ref[...])
    half = wbg0_ref.shape[0]
    gla_proj = _dot(yg[:, :half], wbg0_ref[...]) + _dot(yg[:, half:], wbg1_ref[...])
    m = m + g1_ref[...].astype(F32) * gla_proj
    m = m + g2_ref[...].astype(F32) * _dot(y.astype(BF16), wbs_ref[...])
    o_ref[...] = x_ref[...] + _dot(m.astype(BF16), wo_ref[...])


def _merge(x, o_gla, ycat, z, r_col0, ssm_col0, gate_col0, seq, pool_w, pool_scale, gla_gain,
           d_skip, w_glu, wb, w_out, layer, *, tm=256):
    m, d = x.shape
    wp, wg, ws = pool_scale.shape[2], o_gla.shape[1], d_skip.shape[2]
    br = BRANCH_ROWS
    assert wp == br and ws == br and wg == 2 * br and seq % tm == 0 and tm % POOL_HALO == 0
    gb = gate_col0 // d
    hb = tm // POOL_HALO
    one = pl.Buffered(1)
    wspec = lambda shape, r: pl.BlockSpec((None,) + shape, lambda i: (layer, r, 0), pipeline_mode=one)
    return pl.pallas_call(
        functools.partial(_merge_body, tiles_per_seq=seq // tm),
        grid=(m // tm,),
        in_specs=[
            pl.BlockSpec((tm, d), lambda i: (i, 0)),
            pl.BlockSpec((tm, wp), lambda i: (i, 0)),
            pl.BlockSpec((POOL_HALO, wp), lambda i: (jnp.maximum(i * hb - 1, 0), 0)),
            pl.BlockSpec((tm, wg), lambda i: (i, 0)),
            pl.BlockSpec((tm, br), lambda i: (i, r_col0 // br)),
            pl.BlockSpec((tm, br), lambda i: (i, r_col0 // br + 1)),
            pl.BlockSpec((ycat.shape[0], tm // SSM_CHUNK, ycat.shape[2]), lambda i: (0, i, 0)),
            pl.BlockSpec((tm, ws), lambda i: (i, ssm_col0 // ws)),
            pl.BlockSpec((tm, d), lambda i: (i, gb)),
            pl.BlockSpec((tm, d), lambda i: (i, gb + 1)),
            pl.BlockSpec((tm, d), lambda i: (i, gb + 2)),
            pl.BlockSpec((None, len(POOL_WINDOWS), POOL_GROUP, POOL_GROUP), lambda i: (layer, 0, 0, 0),
                         pipeline_mode=one),
            wspec((1, wp), 0),
            wspec((1, wg), 0),
            wspec((1, ws), 0),
            wspec((ws, ws), 0),
            wspec((br, d), 0),
            wspec((br, d), 1),
            wspec((br, d), 2),
            wspec((br, d), 3),
            wspec((d, d), 0),
        ],
        out_specs=pl.BlockSpec((tm, d), lambda i: (i, 0)),
        out_shape=jax.ShapeDtypeStruct((m, d), F32),
        scratch_shapes=[pltpu.VMEM((ws // LANES, tm, LANES), F32)],
        compiler_params=_cparams(("parallel",)),
        name="merge",
    )(x, z, z, o_gla, z, z, ycat, z, z, z, z, pool_w, pool_scale, gla_gain, d_skip, w_glu,
      wb, wb, wb, wb, w_out)


def kernel(x, ffn1_norm, ffn1_w_gate, ffn1_w_up, ffn1_w_down, mix_norm, w_in, pool_w, pool_scale, gla_w_gate2, gla_gate_bias, gla_norm, ssm_a_re, ssm_a_im, ssm_log_dt, ssm_b_re, ssm_b_im, ssm_c_re, ssm_c_im, ssm_d, ssm_w_glu, w_branch, w_out, ffn2_norm, ffn2_w_gate, ffn2_w_up, ffn2_w_down, final_norm):
    batch, seq, d = x.shape
    depth = w_in.shape[0]
    m = batch * seq
    pool_width = pool_scale.shape[1]
    qk_width = GLA_HEADS * GLA_DK
    v_width = GLA_HEADS * GLA_DV
    ssm_width = ssm_d.shape[1]
    n_groups = ssm_width // SSM_GROUP
    q0 = pool_width
    k0 = q0 + qk_width
    v0 = k0 + qk_width
    r0 = v0 + v_width
    glr0 = r0 + v_width
    ssm_src0 = glr0 + GLA_RANK
    ssm0 = glr0
    gate0 = ssm0 + ssm_width
    chunks_per_seq = seq // SSM_CHUNK

    bf = lambda a: a.astype(BF16)
    row3 = lambda a: a.reshape(depth, 1, a.shape[-1])
    w_in_t = jnp.swapaxes(w_in, 1, 2)
    ffn1_w = (ffn1_w_gate, ffn1_w_up, ffn1_w_down)
    ffn2_w = (ffn2_w_gate, ffn2_w_up, ffn2_w_down)
    f1 = [bf(w[0]) for w in ffn1_w]
    wbr, wo, wglu, pw = bf(w_branch), bf(w_out), bf(ssm_w_glu), bf(pool_w)
    n1, nm, n2 = row3(ffn1_norm), row3(mix_norm), row3(ffn2_norm)
    pscale, dskip = row3(pool_scale), row3(ssm_d)
    w2 = gla_w_gate2.reshape(depth, GLA_RANK, GLA_HEADS, GLA_DK).transpose(0, 2, 1, 3)
    w2 = bf(jnp.pad(w2, ((0, 0), (0, 0), (0, LANES - GLA_RANK), (0, 0))))
    gbias = gla_gate_bias.reshape(depth, GLA_HEADS, 1, GLA_DK)
    ggain = row3(gla_norm)
    ssm_ops = jax.vmap(functools.partial(_ssm_operators, chunks_per_seq=chunks_per_seq))(
        ssm_a_re, ssm_a_im, ssm_log_dt, ssm_b_re, ssm_b_im, ssm_c_re, ssm_c_im)

    xf = x.reshape(m, d)
    for l in range(depth):
        xf, *side = _ffn(xf, n1, l, *f1, cast=(l,) + ffn2_w, repack=(l, w_in_t, glr0, GLA_RANK))
        f2, (w_main, w_glr) = side[:3], side[3:]
        z, glr, xcat = _inproj(xf, nm, w_main, w_glr, l, ssm0, ssm_width, gate0)
        o_gla = _gla(z, glr, w2, gbias, l, batch, seq, (q0, k0, v0))
        ycat = _ssm_core(xcat, *ssm_ops, l, chunks_per_seq)
        xf = _merge(xf, o_gla, ycat, z, r0, ssm0, gate0, seq, pw, pscale, ggain, dskip, wglu, wbr, wo, l)
        last = l == depth - 1
        xf, *f1 = _ffn(xf, n2, l, *f2, cast=None if last else (l + 1,) + ffn1_w,
                       final_g=final_norm if last else None)
    return xf.reshape(batch, seq, d)
```

```python
import functools

import jax
import jax.numpy as jnp
import numpy as np
from jax import lax
from jax.experimental import pallas as pl
from jax.experimental.pallas import tpu as pltpu

F32 = jnp.float32
BF16 = jnp.bfloat16

EPS = 1e-6
POOL_WINDOWS = (2, 4, 8, 16)
POOL_GROUP = 128
POOL_HALO = 16
GLA_HEADS = 4
GLA_DK = 128
GLA_DV = 256
GLA_RANK = 16
GLA_TAU = 16.0
GLA_CHUNK = 64
SSM_GROUP = 16
SSM_STATE = 64
SSM_CHUNK = 16
LANES = 128
FFN_DOWN_CHUNK = 512
FFN_ROW_CHUNK = 1024
FFN_EPILOGUE_ROWS = 256
REPACK_ROWS = 64
BRANCH_ROWS = 512

VMEM_LIMIT = 60 * 1024 * 1024


def _cparams(sem):
    return pltpu.CompilerParams(dimension_semantics=sem, vmem_limit_bytes=VMEM_LIMIT)


def _rms(xf, g):
    return xf * lax.rsqrt(jnp.mean(xf * xf, axis=-1, keepdims=True) + EPS) * g


def _dot(a, b):
    return jnp.dot(a, b, preferred_element_type=F32)


def _sigmoid(a):
    return 0.5 * jnp.tanh(0.5 * a) + 0.5


def _dot_nt(a, b_t):
    return lax.dot_general(a, b_t, (((1,), (1,)), ((), ())), preferred_element_type=F32)


def _ffn_body(x_ref, g_ref, wg_ref, wu_ref, wd_ref, *rest, final, n_cast, repack):
    rest = list(rest)
    cast_in = [rest.pop(0) for _ in range(n_cast)]
    rp_in = [rest.pop(0) for _ in range(2 if repack else 0)]
    fg_ref = rest.pop(0) if final else None
    o_ref = rest.pop(0)
    cast_out = [rest.pop(0) for _ in range(n_cast)]
    rp_out = [rest.pop(0) for _ in range(2 if repack else 0)]
    xn_ref, = rest
    j = pl.program_id(1)

    for src, dst in zip(cast_in, cast_out):
        dst[...] = src[...].astype(BF16)
    if repack:
        (rows_ref, drop_ref), (w_ref, wl_ref) = rp_in, rp_out
        w_ref[...] = rows_ref[0].astype(BF16)
        pad = jnp.zeros((wl_ref.shape[0] - drop_ref.shape[1], drop_ref.shape[2]), F32)
        wl_ref[...] = jnp.concatenate([drop_ref[0], pad], axis=0).astype(BF16)

    row_chunks = [slice(r0, r0 + FFN_ROW_CHUNK) for r0 in range(0, o_ref.shape[0], FFN_ROW_CHUNK)]

    @pl.when(j == 0)
    def _():
        for rs in row_chunks:
            xn_ref[rs, :] = _rms(x_ref[rs, :], g_ref[...]).astype(BF16)
            o_ref[rs, :] = jnp.zeros((FFN_ROW_CHUNK, o_ref.shape[1]), F32)

    for rs in row_chunks:
        xn = xn_ref[rs, :]
        a = _dot(xn, wg_ref[...])
        b = _dot(xn, wu_ref[...])
        h = (a * _sigmoid(a) * b).astype(BF16)
        for c0 in range(0, o_ref.shape[1], FFN_DOWN_CHUNK):
            cs = slice(c0, c0 + FFN_DOWN_CHUNK)
            o_ref[rs, cs] += _dot(h, wd_ref[:, cs])

    @pl.when(j == pl.num_programs(1) - 1)
    def _():
        def epilogue(r, carry):
            rs = pl.ds(pl.multiple_of(r * FFN_EPILOGUE_ROWS, FFN_EPILOGUE_ROWS), FFN_EPILOGUE_ROWS)
            y = x_ref[rs, :] + 0.5 * o_ref[rs, :]
            if final:
                y = _rms(y, fg_ref[...])
            o_ref[rs, :] = y
            return carry
        lax.fori_loop(0, o_ref.shape[0] // FFN_EPILOGUE_ROWS, epilogue, 0)


def _ffn(x, g, layer, wg, wu, wd, cast=None, repack=None, final_g=None, *, tm=1024, tf=512):
    m, d = x.shape
    f = wg.shape[1]
    ni, nj = m // tm, f // tf
    final = final_g is not None
    in_specs = [
        pl.BlockSpec((tm, d), lambda i, j: (i, 0)),
        pl.BlockSpec((None, 1, d), lambda i, j: (layer, 0, 0)),
        pl.BlockSpec((d, tf), lambda i, j: (0, j)),
        pl.BlockSpec((d, tf), lambda i, j: (0, j)),
        pl.BlockSpec((tf, d), lambda i, j: (j, 0)),
    ]
    args = [x, g, wg, wu, wd]
    out_specs = [pl.BlockSpec((tm, d), lambda i, j: (i, 0))]
    out_shape = [jax.ShapeDtypeStruct((m, d), F32)]
    n_cast = 0
    if cast is not None:
        cl, cg, cu, cd = cast
        n_cast = 3
        rows_up = d // ni
        rows_dn = f // (ni * nj)
        assert d % ni == 0 and f % (ni * nj) == 0 and rows_up % 16 == 0 and rows_dn % 16 == 0
        for w in (cg, cu):
            in_specs.append(pl.BlockSpec((None, rows_up, tf), lambda i, j: (cl, i, j)))
            out_specs.append(pl.BlockSpec((rows_up, tf), lambda i, j: (i, j)))
            out_shape.append(jax.ShapeDtypeStruct((d, f), BF16))
            args.append(w)
        in_specs.append(pl.BlockSpec((None, rows_dn, d), lambda i, j: (cl, i * nj + j, 0)))
        out_specs.append(pl.BlockSpec((rows_dn, d), lambda i, j: (i * nj + j, 0)))
        out_shape.append(jax.ShapeDtypeStruct((f, d), BF16))
        args.append(cd)
    if repack is not None:
        rl, w_in_t, drop0, drop_width = repack
        n = w_in_t.shape[1] - drop_width
        rb = REPACK_ROWS
        nblk = n // rb
        assert n % rb == 0 and nblk <= ni * nj and drop0 % rb == 0 and drop_width % 8 == 0
        blk = lambda i, j: jnp.minimum(i * nj + j, nblk - 1)
        src_row = lambda b: (b * (rb // 8) + jnp.where(b >= drop0 // rb, drop_width // 8, 0)) * 8
        in_specs.append(pl.BlockSpec((pl.Element(1), pl.Element(rb), pl.Element(d)),
                                     lambda i, j: (rl, src_row(blk(i, j)), 0)))
        in_specs.append(pl.BlockSpec((pl.Element(1), pl.Element(drop_width), pl.Element(d)),
                                     lambda i, j: (rl, drop0, 0)))
        out_specs.append(pl.BlockSpec((rb, d), lambda i, j: (blk(i, j), 0)))
        out_specs.append(pl.BlockSpec((LANES, d), lambda i, j: (0, 0)))
        out_shape.append(jax.ShapeDtypeStruct((n, d), BF16))
        out_shape.append(jax.ShapeDtypeStruct((LANES, d), BF16))
        args += [w_in_t, w_in_t]
    if final:
        in_specs.append(pl.BlockSpec((1, d), lambda i, j: (0, 0)))
        args.append(final_g.reshape(1, d))
    return pl.pallas_call(
        functools.partial(_ffn_body, final=final, n_cast=n_cast, repack=repack is not None),
        grid=(ni, nj),
        in_specs=in_specs,
        out_specs=out_specs,
        out_shape=out_shape,
        scratch_shapes=[pltpu.VMEM((tm, d), BF16)],
        compiler_params=_cparams(("arbitrary", "arbitrary")),
        name="ffn_final" if final else "ffn",
    )(*args)


def _inproj_body(x_ref, g_ref, w_ref, wl_ref, z_ref, glr_ref, xc_ref, un_ref, ph_ref, *,
                 gate_block0, ssm_block, ssm_off):
    j = pl.program_id(1)

    @pl.when(j == 0)
    def _():
        un = _rms(x_ref[...], g_ref[...]).astype(BF16)
        un_ref[...] = un
        glr_ref[...] = _dot_nt(un, wl_ref[...])

    acc = _dot_nt(un_ref[...], w_ref[...])
    z_ref[...] = jnp.where(j >= gate_block0, _sigmoid(acc), acc).astype(BF16)

    @pl.when(j == ssm_block)
    def _():
        ncol = ph_ref.shape[0]
        for c in range(ncol):
            ph_ref[c] = acc[:, ssm_off + c * LANES:ssm_off + (c + 1) * LANES]
        rows = ph_ref.shape[1] // SSM_CHUNK
        for s in range(SSM_CHUNK):
            for c in range(ncol):
                piece = ph_ref[c, pl.ds(s, rows, stride=SSM_CHUNK), :].astype(BF16)
                xc_ref[c, :, s * LANES:(s + 1) * LANES] = piece


def _inproj(x, g, w_main, w_glr, layer, ssm_col0, ssm_width, gate_col0, *, tm=1024, tn=1024):
    m, d = x.shape
    n = w_main.shape[0]
    body = functools.partial(_inproj_body, gate_block0=gate_col0 // tn,
                             ssm_block=ssm_col0 // tn, ssm_off=ssm_col0 % tn)
    return pl.pallas_call(
        body,
        grid=(m // tm, n // tn),
        in_specs=[
            pl.BlockSpec((tm, d), lambda i, j: (i, 0)),
            pl.BlockSpec((None, 1, d), lambda i, j: (layer, 0, 0)),
            pl.BlockSpec((tn, d), lambda i, j: (j, 0)),
            pl.BlockSpec((LANES, d), lambda i, j: (0, 0)),
        ],
        out_specs=[
            pl.BlockSpec((tm, tn), lambda i, j: (i, j)),
            pl.BlockSpec((tm, LANES), lambda i, j: (i, 0)),
            pl.BlockSpec((ssm_width // LANES, tm // SSM_CHUNK, SSM_CHUNK * LANES), lambda i, j: (0, i, 0)),
        ],
        out_shape=[
            jax.ShapeDtypeStruct((m, n), BF16),
            jax.ShapeDtypeStruct((m, LANES), F32),
            jax.ShapeDtypeStruct((ssm_width // LANES, m // SSM_CHUNK, SSM_CHUNK * LANES), BF16),
        ],
        scratch_shapes=[pltpu.VMEM((tm, d), BF16), pltpu.VMEM((ssm_width // LANES, tm, LANES), F32)],
        compiler_params=_cparams(("parallel", "arbitrary")),
        name="inproj",
    )(x, g, w_main, w_glr)


def _pool_mix(x, halo, t, w_ref, scale):
    ts = x.shape[0]
    halo = jnp.where(t > 0, halo, 0.0)
    e = jnp.concatenate([halo, x], axis=0)
    s2 = e + pltpu.roll(e, 1, 0)
    s4 = s2 + pltpu.roll(s2, 2, 0)
    s8 = s4 + pltpu.roll(s4, 4, 0)
    s16 = s8 + pltpu.roll(s8, 8, 0)
    pos = (t * ts + 1 + lax.broadcasted_iota(jnp.int32, (ts, 1), 0)).astype(F32)
    outs = []
    for gi, (w, s) in enumerate(zip(POOL_WINDOWS, (s2, s4, s8, s16))):
        cols = slice(gi * POOL_GROUP, (gi + 1) * POOL_GROUP)
        mean = s[POOL_HALO:, cols] / jnp.minimum(pos, float(w))
        outs.append(_dot((mean - x[:, cols]).astype(BF16), w_ref[gi]))
    return jnp.concatenate(outs, axis=1) * scale


def _gla_head(q, k, v, glr, w2, bias, st, ts):
    c = GLA_CHUNK
    nc = ts // c
    logits = _dot(glr, w2) + bias
    log_a = (jnp.minimum(logits, 0.0) - jnp.log1p(jnp.exp(-jnp.abs(logits)))) * (1.0 / GLA_TAU)
    rowc = lax.broadcasted_iota(jnp.int32, (ts, 1), 0) % c
    bc = log_a
    step = 1
    while step < c:
        bc = bc + jnp.where(rowc >= step, pltpu.roll(bc, step, 0), 0.0)
        step *= 2
    bc3 = bc.reshape(nc, c, GLA_DK)
    b_last = bc3[:, c - 1:c, :]
    b_mid = bc3[:, c // 2 - 1:c // 2, :]
    q3 = q.astype(F32).reshape(nc, c, GLA_DK) * (GLA_DK ** -0.5)
    k3 = k.astype(F32).reshape(nc, c, GLA_DK)
    v3 = v.reshape(nc, c, GLA_DV)
    qe = (q3 * jnp.exp(bc3 - b_mid)).astype(BF16)
    ke = (k3 * jnp.exp(b_mid - bc3)).astype(BF16)
    scores = jnp.einsum('nid,njd->nij', qe, ke, preferred_element_type=F32)
    causal = (lax.broadcasted_iota(jnp.int32, (c, c), 0)
              >= lax.broadcasted_iota(jnp.int32, (c, c), 1))
    scores = jnp.where(causal[None], scores, 0.0).astype(BF16)
    o_intra = jnp.einsum('nij,njv->niv', scores, v3, preferred_element_type=F32)
    qd = (q3 * jnp.exp(bc3)).astype(BF16)
    kd = (k3 * jnp.exp(b_last - bc3)).astype(BF16)
    dec_t = jnp.transpose(jnp.exp(b_last.reshape(nc, GLA_DK)))
    outs = []
    for n in range(nc):
        outs.append(o_intra[n] + _dot(qd[n], st.astype(BF16)))
        kv = lax.dot_general(kd[n], v3[n], (((0,), (0,)), ((), ())), preferred_element_type=F32)
        st = dec_t[:, n:n + 1] * st + kv
    return jnp.concatenate(outs, axis=0).astype(BF16), st


def _gla_body(q_ref, k_ref, v0_ref, v1_ref, glr_ref, w2_ref, b_ref, o_ref, st_ref, *, ts):
    @pl.when(pl.program_id(1) == 0)
    def _():
        st_ref[...] = jnp.zeros_like(st_ref)

    v_refs = (v0_ref, v1_ref)
    heads_per_block = v0_ref.shape[2] // GLA_DV
    for s in range(q_ref.shape[0]):
        glr = glr_ref[s].astype(BF16)
        for h in range(GLA_HEADS):
            v_ref = v_refs[h // heads_per_block]
            vs = slice((h % heads_per_block) * GLA_DV, (h % heads_per_block + 1) * GLA_DV)
            ks = slice(h * GLA_DK, (h + 1) * GLA_DK)
            o, st = _gla_head(q_ref[s, :, ks], k_ref[s, :, ks], v_ref[s, :, vs], glr,
                              w2_ref[h], b_ref[h], st_ref[s, h], ts)
            st_ref[s, h] = st
            o_ref[s, :, h * GLA_DV:(h + 1) * GLA_DV] = o


def _gla(z, glr, w2, bias, layer, batch, seq, cols, *, ts=512, nseq=1):
    h = GLA_HEADS
    nt = seq // ts
    q0, k0, v0 = cols
    wb = h * GLA_DK
    z3 = z.reshape(batch, seq, z.shape[1])
    zspec = lambda c0: pl.BlockSpec((nseq, ts, wb), lambda b, t: (b, t, c0 // wb))
    wspec = lambda last2: pl.BlockSpec((None, h) + last2, lambda b, t: (layer, 0, 0, 0))
    out = pl.pallas_call(
        functools.partial(_gla_body, ts=ts),
        grid=(batch // nseq, nt),
        in_specs=[
            zspec(q0), zspec(k0), zspec(v0), zspec(v0 + wb),
            pl.BlockSpec((nseq, ts, LANES), lambda b, t: (b, t, 0)),
            wspec((LANES, GLA_DK)), wspec((1, GLA_DK)),
        ],
        out_specs=pl.BlockSpec((nseq, ts, h * GLA_DV), lambda b, t: (b, t, 0)),
        out_shape=jax.ShapeDtypeStruct((batch, seq, h * GLA_DV), BF16),
        scratch_shapes=[pltpu.VMEM((nseq, h, GLA_DK, GLA_DV), F32)],
        compiler_params=_cparams(("parallel", "arbitrary")),
        name="gla",
    )(z3, z3, z3, z3, glr.reshape(batch, seq, LANES), w2, bias)
    return out.reshape(batch * seq, h * GLA_DV)


SSM_COL_GROUPS = LANES // SSM_GROUP
SSM_BLOCK_GROUPS = 2
SSM_TPAIR = 2


def _ssm_body(x_ref, kc_ref, pc_ref, qc_ref, ar_ref, ai_ref, o_ref, hp_ref, kst_ref, v_ref, *,
              chunks_per_seq, n_state_blocks):
    u = pl.program_id(0)
    ncol, rows = x_ref.shape[0], x_ref.shape[1]
    blocks_per_col = n_state_blocks // ncol
    npairs = SSM_CHUNK // SSM_TPAIR
    ns = 2 * SSM_STATE
    bw = SSM_BLOCK_GROUPS * ns
    lane_group = lax.broadcasted_iota(jnp.int32, (1, LANES), 1) // SSM_GROUP

    def p_tile(n):
        pc = pc_ref[n // blocks_per_col]
        row_group = (lax.broadcasted_iota(jnp.int32, (pc.shape[0], 1), 0) // SSM_GROUP) % SSM_COL_GROUPS
        g0 = (n % blocks_per_col) * SSM_BLOCK_GROUPS
        return jnp.concatenate([jnp.where(row_group == g0 + gg, pc, 0.0)
                                for gg in range(SSM_BLOCK_GROUPS)], axis=1).astype(BF16)

    @pl.when(u == 0)
    def _():
        row_group = lax.broadcasted_iota(jnp.int32, (LANES, 1), 0) // SSM_GROUP

        def kbd(c, lag):
            kc = kc_ref[c, lag]
            tiled = jnp.broadcast_to(kc[None], (SSM_COL_GROUPS,) + kc.shape).reshape(LANES, LANES)
            return jnp.where(row_group == lane_group, tiled, 0.0).astype(BF16)

        for c in range(ncol):
            for r in range(SSM_CHUNK):
                left = kbd(c, SSM_CHUNK - 2 - r) if r <= SSM_CHUNK - 2 else jnp.zeros((LANES, LANES), BF16)
                kst_ref[c, r * LANES:(r + 1) * LANES, :] = jnp.concatenate(
                    [left, kbd(c, SSM_CHUNK - 1 - r)], axis=1)
        v_ref[...] = _dot(x_ref[0], p_tile(0))

    @pl.when(u < n_state_blocks)
    def _():
        rowc = lax.broadcasted_iota(jnp.int32, (rows, 1), 0) % chunks_per_seq
        hst = v_ref[...]
        nxt = jnp.minimum(u + 1, n_state_blocks - 1)
        v_next = _dot(x_ref[nxt // blocks_per_col], p_tile(nxt))
        k, si = 1, 0
        while k < chunks_per_seq:
            hs = jnp.where(rowc >= k, pltpu.roll(hst, k, 0), 0.0)
            sw = jnp.concatenate([pltpu.roll(hs[:, b0:b0 + ns], SSM_STATE, 1)
                                  for b0 in range(0, bw, ns)], axis=1)
            hst = hst + ar_ref[u, si:si + 1, :] * hs + ai_ref[u, si:si + 1, :] * sw
            k *= 2
            si += 1
        hp_ref[u] = jnp.where(rowc >= 1, pltpu.roll(hst, 1, 0), 0.0).astype(BF16)
        v_ref[...] = v_next

    @pl.when(u >= n_state_blocks)
    def _():
        ob = u - n_state_blocks
        c = ob // npairs
        v = ob % npairs
        grp = lax.broadcasted_iota(jnp.int32, (SSM_COL_GROUPS, 1, 1), 0)
        q_tile = jnp.concatenate(
            [jnp.where(grp == lane_group[None], qc_ref[c, SSM_TPAIR * v + tt][None], 0.0)
             .reshape(SSM_COL_GROUPS * ns, LANES) for tt in range(SSM_TPAIR)], axis=1).astype(BF16)
        yq = _dot(hp_ref[c * blocks_per_col], q_tile[0:bw, :])
        for kk in range(1, blocks_per_col):
            yq = yq + _dot(hp_ref[c * blocks_per_col + kk], q_tile[kk * bw:(kk + 1) * bw, :])
        for vs in range(npairs):
            kext = (vs + 1) * SSM_TPAIR * LANES

            @pl.when(v == vs)
            def _():
                start = (npairs - 1 - vs) * SSM_TPAIR * LANES
                intra = _dot(x_ref[c, :, 0:kext], kst_ref[c, start:start + kext, :])
                o_ref[...] = (yq + intra).astype(BF16)


def _ssm_core(xcat, kc, pc, qc, ar, ai, layer, chunks_per_seq):
    ncol, rows, xw = xcat.shape
    nb = ar.shape[1]
    bw = ar.shape[3]
    ow = SSM_TPAIR * LANES
    nout = ncol * SSM_CHUNK // SSM_TPAIR
    one = pl.Buffered(1)
    whole = lambda a: pl.BlockSpec((None,) + a.shape[1:], lambda u: (layer,) + (0,) * (a.ndim - 1),
                                   pipeline_mode=one)
    return pl.pallas_call(
        functools.partial(_ssm_body, chunks_per_seq=chunks_per_seq, n_state_blocks=nb),
        grid=(nb + nout,),
        in_specs=[
            pl.BlockSpec((ncol, rows, xw), lambda u: (0, 0, 0), pipeline_mode=one),
            whole(kc), whole(pc), whole(qc), whole(ar), whole(ai),
        ],
        out_specs=pl.BlockSpec((None, rows, ow), lambda u: (jnp.maximum(u - nb, 0), 0, 0)),
        out_shape=jax.ShapeDtypeStruct((nout, rows, ow), BF16),
        scratch_shapes=[
            pltpu.VMEM((nb, rows, bw), BF16),
            pltpu.VMEM((ncol, SSM_CHUNK * LANES, ow), BF16),
            pltpu.VMEM((rows, bw), F32),
        ],
        compiler_params=_cparams(("arbitrary",)),
        name="ssm_core",
    )(xcat, kc, pc, qc, ar, ai)


def _ssm_operators(a_re, a_im, log_dt, b_re, b_im, c_re, c_im, chunks_per_seq):
    t = SSM_CHUNK
    g, p = a_re.shape
    hg = b_re.shape[-1]
    cgc = SSM_COL_GROUPS
    ncol = g // cgc
    bgc = SSM_BLOCK_GROUPS
    dt = jnp.exp(log_dt)[:, None]
    lam_re, lam_im = dt * a_re, dt * a_im

    def powers(n):
        n = jnp.asarray(n, F32)[:, None, None]
        mag = jnp.exp(n * lam_re)
        return mag * jnp.cos(n * lam_im), mag * jnp.sin(n * lam_im)

    ab_re, ab_im = powers([1.0])
    ab_re, ab_im = ab_re[0], ab_im[0]
    den = a_re * a_re + a_im * a_im
    f_re = ((ab_re - 1.0) * a_re + ab_im * a_im) / den
    f_im = (ab_im * a_re - (ab_re - 1.0) * a_im) / den
    bb_re = f_re[..., None] * b_re - f_im[..., None] * b_im
    bb_im = f_re[..., None] * b_im + f_im[..., None] * b_re
    pw_re, pw_im = powers(np.arange(t + 1))
    cp_re = c_re[None] * pw_re[:, :, None, :] - c_im[None] * pw_im[:, :, None, :]
    cp_im = c_re[None] * pw_im[:, :, None, :] + c_im[None] * pw_re[:, :, None, :]
    ktau = (jnp.einsum('tghp,gpk->tkgh', cp_re[:t], bb_re)
            - jnp.einsum('tghp,gpk->tkgh', cp_im[:t], bb_im))
    kc = ktau.reshape(t, hg, ncol, cgc * hg).transpose(2, 0, 1, 3)
    bt_re, bt_im = bb_re.transpose(0, 2, 1), bb_im.transpose(0, 2, 1)
    rp_re = pw_re[t - 1 - np.arange(t)][:, :, None, :]
    rp_im = pw_im[t - 1 - np.arange(t)][:, :, None, :]
    pin = jnp.concatenate([rp_re * bt_re[None] - rp_im * bt_im[None],
                           rp_re * bt_im[None] + rp_im * bt_re[None]], axis=3)
    pc = pin.reshape(t, ncol, cgc, hg, 2 * p).transpose(1, 0, 2, 3, 4).reshape(ncol, t * cgc * hg, 2 * p)
    ct_re, ct_im = c_re.transpose(2, 0, 1)[None], c_im.transpose(2, 0, 1)[None]
    pt_re = pw_re[1:].transpose(0, 2, 1)[..., None]
    pt_im = pw_im[1:].transpose(0, 2, 1)[..., None]
    qg = jnp.concatenate([ct_re * pt_re - ct_im * pt_im,
                          -(ct_re * pt_im + ct_im * pt_re)], axis=1)
    qc = qg.reshape(t, 2 * p, ncol, cgc * hg).transpose(2, 0, 1, 3)
    nblk = g // bgc
    nsteps = int(np.log2(chunks_per_seq))
    sr, si = powers(t * 2.0 ** np.arange(nsteps))
    blk = lambda a: a.reshape(nsteps, nblk, bgc * 2 * p).transpose(1, 0, 2)
    ar = blk(jnp.concatenate([sr, sr], axis=2))
    ai = blk(jnp.concatenate([-si, si], axis=2))
    return kc, pc, qc, ar, ai


def _merge_body(x_ref, pin_ref, halo_ref, og_ref, r0_ref, r1_ref, yc_ref, sin_ref, g0_ref, g1_ref, g2_ref,
                pw_ref, ps_ref, gain_ref, d_ref, wglu_ref, wbp_ref, wbg0_ref, wbg1_ref, wbs_ref, wo_ref,
                o_ref, ys_ref, *, tiles_per_seq):
    t = pl.program_id(0) % tiles_per_seq
    yp = _pool_mix(pin_ref[...].astype(F32), halo_ref[...].astype(F32), t, pw_ref, ps_ref[...])
    r_refs = (r0_ref, r1_ref)
    heads_per_block = r0_ref.shape[1] // GLA_DV
    yg_heads = []
    for h in range(GLA_HEADS):
        hs = slice(h * GLA_DV, (h + 1) * GLA_DV)
        rs = slice((h % heads_per_block) * GLA_DV, (h % heads_per_block + 1) * GLA_DV)
        o = og_ref[:, hs].astype(F32)
        o = o * lax.rsqrt(jnp.mean(o * o, axis=-1, keepdims=True) + EPS) * gain_ref[:, hs]
        r = r_refs[h // heads_per_block][:, rs].astype(F32)
        yg_heads.append((o * (r * _sigmoid(r))).astype(BF16))
    yg = jnp.concatenate(yg_heads, axis=1)
    rows = yc_ref.shape[1]
    npairs = SSM_CHUNK // SSM_TPAIR
    for ob in range(yc_ref.shape[0]):
        c, v = divmod(ob, npairs)
        plane = yc_ref[ob].astype(F32)
        for tt in range(SSM_TPAIR):
            ys_ref[c, pl.ds(SSM_TPAIR * v + tt, rows, stride=SSM_CHUNK), :] = plane[:, tt * LANES:(tt + 1) * LANES]
    ys = jnp.concatenate([ys_ref[c] for c in range(ys_ref.shape[0])], axis=1)
    y = ys + d_ref[...] * sin_ref[...].astype(F32)
    y = jax.nn.gelu(y)
    y = y * _sigmoid(_dot(y.astype(BF16), wglu_ref[...]))
    m = g0_ref[...].astype(F32) * _dot(yp.astype(BF16), wbp_ref[...])
    half = wbg0_ref.shape[0]
    gla_proj = _dot(yg[:, :half], wbg0_ref[...]) + _dot(yg[:, half:], wbg1_ref[...])
    m = m + g1_ref[...].astype(F32) * gla_proj
    m = m + g2_ref[...].astype(F32) * _dot(y.astype(BF16), wbs_ref[...])
    o_ref[...] = x_ref[...] + _dot(m.astype(BF16), wo_ref[...])


def _merge(x, o_gla, ycat, z, r_col0, ssm_col0, gate_col0, seq, pool_w, pool_scale, gla_gain,
           d_skip, w_glu, wb, w_out, layer, *, tm=256):
    m, d = x.shape
    wp, wg, ws = pool_scale.shape[2], o_gla.shape[1], d_skip.shape[2]
    br = BRANCH_ROWS
    assert wp == br and ws == br and wg == 2 * br and seq % tm == 0 and tm % POOL_HALO == 0
    gb = gate_col0 // d
    hb = tm // POOL_HALO
    one = pl.Buffered(1)
    wspec = lambda shape, r: pl.BlockSpec((None,) + shape, lambda i: (layer, r, 0), pipeline_mode=one)
    return pl.pallas_call(
        functools.partial(_merge_body, tiles_per_seq=seq // tm),
        grid=(m // tm,),
        in_specs=[
            pl.BlockSpec((tm, d), lambda i: (i, 0)),
            pl.BlockSpec((tm, wp), lambda i: (i, 0)),
            pl.BlockSpec((POOL_HALO, wp), lambda i: (jnp.maximum(i * hb - 1, 0), 0)),
            pl.BlockSpec((tm, wg), lambda i: (i, 0)),
            pl.BlockSpec((tm, br), lambda i: (i, r_col0 // br)),
            pl.BlockSpec((tm, br), lambda i: (i, r_col0 // br + 1)),
            pl.BlockSpec((ycat.shape[0], tm // SSM_CHUNK, ycat.shape[2]), lambda i: (0, i, 0)),
            pl.BlockSpec((tm, ws), lambda i: (i, ssm_col0 // ws)),
            pl.BlockSpec((tm, d), lambda i: (i, gb)),
            pl.BlockSpec((tm, d), lambda i: (i, gb + 1)),
            pl.BlockSpec((tm, d), lambda i: (i, gb + 2)),
            pl.BlockSpec((None, len(POOL_WINDOWS), POOL_GROUP, POOL_GROUP), lambda i: (layer, 0, 0, 0),
                         pipeline_mode=one),
            wspec((1, wp), 0),
            wspec((1, wg), 0),
            wspec((1, ws), 0),
            wspec((ws, ws), 0),
            wspec((br, d), 0),
            wspec((br, d), 1),
            wspec((br, d), 2),
            wspec((br, d), 3),
            wspec((d, d), 0),
        ],
        out_specs=pl.BlockSpec((tm, d), lambda i: (i, 0)),
        out_shape=jax.ShapeDtypeStruct((m, d), F32),
        scratch_shapes=[pltpu.VMEM((ws // LANES, tm, LANES), F32)],
        compiler_params=_cparams(("parallel",)),
        name="merge",
    )(x, z, z, o_gla, z, z, ycat, z, z, z, z, pool_w, pool_scale, gla_gain, d_skip, w_glu,
      wb, wb, wb, wb, w_out)


def kernel(x, ffn1_norm, ffn1_w_gate, ffn1_w_up, ffn1_w_down, mix_norm, w_in, pool_w, pool_scale, gla_w_gate2, gla_gate_bias, gla_norm, ssm_a_re, ssm_a_im, ssm_log_dt, ssm_b_re, ssm_b_im, ssm_c_re, ssm_c_im, ssm_d, ssm_w_glu, w_branch, w_out, ffn2_norm, ffn2_w_gate, ffn2_w_up, ffn2_w_down, final_norm):
    batch, seq, d = x.shape
    depth = w_in.shape[0]
    m = batch * seq
    pool_width = pool_scale.shape[1]
    qk_width = GLA_HEADS * GLA_DK
    v_width = GLA_HEADS * GLA_DV
    ssm_width = ssm_d.shape[1]
    n_groups = ssm_width // SSM_GROUP
    q0 = pool_width
    k0 = q0 + qk_width
    v0 = k0 + qk_width
    r0 = v0 + v_width
    glr0 = r0 + v_width
    ssm_src0 = glr0 + GLA_RANK
    ssm0 = glr0
    gate0 = ssm0 + ssm_width
    chunks_per_seq = seq // SSM_CHUNK

    bf = lambda a: a.astype(BF16)
    row3 = lambda a: a.reshape(depth, 1, a.shape[-1])
    w_in_t = jnp.swapaxes(w_in, 1, 2)
    ffn1_w = (ffn1_w_gate, ffn1_w_up, ffn1_w_down)
    ffn2_w = (ffn2_w_gate, ffn2_w_up, ffn2_w_down)
    f1 = [bf(w[0]) for w in ffn1_w]
    wbr, wo, wglu, pw = bf(w_branch), bf(w_out), bf(ssm_w_glu), bf(pool_w)
    n1, nm, n2 = row3(ffn1_norm), row3(mix_norm), row3(ffn2_norm)
    pscale, dskip = row3(pool_scale), row3(ssm_d)
    w2 = gla_w_gate2.reshape(depth, GLA_RANK, GLA_HEADS, GLA_DK).transpose(0, 2, 1, 3)
    w2 = bf(jnp.pad(w2, ((0, 0), (0, 0), (0, LANES - GLA_RANK), (0, 0))))
    gbias = gla_gate_bias.reshape(depth, GLA_HEADS, 1, GLA_DK)
    ggain = row3(gla_norm)
    ssm_ops = jax.vmap(functools.partial(_ssm_operators, chunks_per_seq=chunks_per_seq))(
        ssm_a_re, ssm_a_im, ssm_log_dt, ssm_b_re, ssm_b_im, ssm_c_re, ssm_c_im)

    xf = x.reshape(m, d)
    for l in range(depth):
        xf, *side = _ffn(xf, n1, l, *f1, cast=(l,) + ffn2_w, repack=(l, w_in_t, glr0, GLA_RANK))
        f2, (w_main, w_glr) = side[:3], side[3:]
        z, glr, xcat = _inproj(xf, nm, w_main, w_glr, l, ssm0, ssm_width, gate0)
        o_gla = _gla(z, glr, w2, gbias, l, batch, seq, (q0, k0, v0))
        ycat = _ssm_core(xcat, *ssm_ops, l, chunks_per_seq)
        xf = _merge(xf, o_gla, ycat, z, r0, ssm0, gate0, seq, pw, pscale, ggain, dskip, wglu, wbr, wo, l)
        last = l == depth - 1
        xf, *f1 = _ffn(xf, n2, l, *f2, cast=None if last else (l + 1,) + ffn1_w,
                       final_g=final_norm if last else None)
    return xf.reshape(batch, seq, d)
```

```python
import functools

import jax
import jax.numpy as jnp
import numpy as np
from jax import lax
from jax.experimental import pallas as pl
from jax.experimental.pallas import tpu as pltpu

F32 = jnp.float32
BF16 = jnp.bfloat16

EPS = 1e-6
POOL_WINDOWS = (2, 4, 8, 16)
POOL_GROUP = 128
POOL_HALO = 16
GLA_HEADS = 4
GLA_DK = 128
GLA_DV = 256
GLA_RANK = 16
GLA_TAU = 16.0
GLA_CHUNK = 64
SSM_GROUP = 16
SSM_STATE = 64
SSM_CHUNK = 16
LANES = 128
FFN_DOWN_CHUNK = 512
FFN_ROW_CHUNK = 1024
FFN_EPILOGUE_ROWS = 256
REPACK_ROWS = 64
BRANCH_ROWS = 512

VMEM_LIMIT = 60 * 1024 * 1024


def _cparams(sem):
    return pltpu.CompilerParams(dimension_semantics=sem, vmem_limit_bytes=VMEM_LIMIT)


def _rms(xf, g):
    return xf * lax.rsqrt(jnp.mean(xf * xf, axis=-1, keepdims=True) + EPS) * g


def _dot(a, b):
    return jnp.dot(a, b, preferred_element_type=F32)


def _sigmoid(a):
    return 0.5 * jnp.tanh(0.5 * a) + 0.5


def _dot_nt(a, b_t):
    return lax.dot_general(a, b_t, (((1,), (1,)), ((), ())), preferred_element_type=F32)


def _ffn_body(x_ref, g_ref, wg_ref, wu_ref, wd_ref, *rest, final, n_cast, repack):
    rest = list(rest)
    cast_in = [rest.pop(0) for _ in range(n_cast)]
    rp_in = [rest.pop(0) for _ in range(2 if repack else 0)]
    fg_ref = rest.pop(0) if final else None
    o_ref = rest.pop(0)
    cast_out = [rest.pop(0) for _ in range(n_cast)]
    rp_out = [rest.pop(0) for _ in range(2 if repack else 0)]
    xn_ref, = rest
    j = pl.program_id(1)

    for src, dst in zip(cast_in, cast_out):
        dst[...] = src[...].astype(BF16)
    if repack:
        (rows_ref, drop_ref), (w_ref, wl_ref) = rp_in, rp_out
        w_ref[...] = rows_ref[0].astype(BF16)
        pad = jnp.zeros((wl_ref.shape[0] - drop_ref.shape[1], drop_ref.shape[2]), F32)
        wl_ref[...] = jnp.concatenate([drop_ref[0], pad], axis=0).astype(BF16)

    row_chunks = [slice(r0, r0 + FFN_ROW_CHUNK) for r0 in range(0, o_ref.shape[0], FFN_ROW_CHUNK)]

    @pl.when(j == 0)
    def _():
        for rs in row_chunks:
            xn_ref[rs, :] = _rms(x_ref[rs, :], g_ref[...]).astype(BF16)
            o_ref[rs, :] = jnp.zeros((FFN_ROW_CHUNK, o_ref.shape[1]), F32)

    for rs in row_chunks:
        xn = xn_ref[rs, :]
        a = _dot(xn, wg_ref[...])
        b = _dot(xn, wu_ref[...])
        h = (a * _sigmoid(a) * b).astype(BF16)
        for c0 in range(0, o_ref.shape[1], FFN_DOWN_CHUNK):
            cs = slice(c0, c0 + FFN_DOWN_CHUNK)
            o_ref[rs, cs] += _dot(h, wd_ref[:, cs])

    @pl.when(j == pl.num_programs(1) - 1)
    def _():
        def epilogue(r, carry):
            rs = pl.ds(pl.multiple_of(r * FFN_EPILOGUE_ROWS, FFN_EPILOGUE_ROWS), FFN_EPILOGUE_ROWS)
            y = x_ref[rs, :] + 0.5 * o_ref[rs, :]
            if final:
                y = _rms(y, fg_ref[...])
            o_ref[rs, :] = y
            return carry
        lax.fori_loop(0, o_ref.shape[0] // FFN_EPILOGUE_ROWS, epilogue, 0)


def _ffn(x, g, layer, wg, wu, wd, cast=None, repack=None, final_g=None, *, tm=1024, tf=512):
    m, d = x.shape
    f = wg.shape[1]
    ni, nj = m // tm, f // tf
    final = final_g is not None
    in_specs = [
        pl.BlockSpec((tm, d), lambda i, j: (i, 0)),
        pl.BlockSpec((None, 1, d), lambda i, j: (layer, 0, 0)),
        pl.BlockSpec((d, tf), lambda i, j: (0, j)),
        pl.BlockSpec((d, tf), lambda i, j: (0, j)),
        pl.BlockSpec((tf, d), lambda i, j: (j, 0)),
    ]
    args = [x, g, wg, wu, wd]
    out_specs = [pl.BlockSpec((tm, d), lambda i, j: (i, 0))]
    out_shape = [jax.ShapeDtypeStruct((m, d), F32)]
    n_cast = 0
    if cast is not None:
        cl, cg, cu, cd = cast
        n_cast = 3
        rows_up = d // ni
        rows_dn = f // (ni * nj)
        assert d % ni == 0 and f % (ni * nj) == 0 and rows_up % 16 == 0 and rows_dn % 16 == 0
        for w in (cg, cu):
            in_specs.append(pl.BlockSpec((None, rows_up, tf), lambda i, j: (cl, i, j)))
            out_specs.append(pl.BlockSpec((rows_up, tf), lambda i, j: (i, j)))
            out_shape.append(jax.ShapeDtypeStruct((d, f), BF16))
            args.append(w)
        in_specs.append(pl.BlockSpec((None, rows_dn, d), lambda i, j: (cl, i * nj + j, 0)))
        out_specs.append(pl.BlockSpec((rows_dn, d), lambda i, j: (i * nj + j, 0)))
        out_shape.append(jax.ShapeDtypeStruct((f, d), BF16))
        args.append(cd)
    if repack is not None:
        rl, w_in_t, drop0, drop_width = repack
        n = w_in_t.shape[1] - drop_width
        rb = REPACK_ROWS
        nblk = n // rb
        assert n % rb == 0 and nblk <= ni * nj and drop0 % rb == 0 and drop_width % 8 == 0
        blk = lambda i, j: jnp.minimum(i * nj + j, nblk - 1)
        src_row = lambda b: (b * (rb // 8) + jnp.where(b >= drop0 // rb, drop_width // 8, 0)) * 8
        in_specs.append(pl.BlockSpec((pl.Element(1), pl.Element(rb), pl.Element(d)),
                                     lambda i, j: (rl, src_row(blk(i, j)), 0)))
        in_specs.append(pl.BlockSpec((pl.Element(1), pl.Element(drop_width), pl.Element(d)),
                                     lambda i, j: (rl, drop0, 0)))
        out_specs.append(pl.BlockSpec((rb, d), lambda i, j: (blk(i, j), 0)))
        out_specs.append(pl.BlockSpec((LANES, d), lambda i, j: (0, 0)))
        out_shape.append(jax.ShapeDtypeStruct((n, d), BF16))
        out_shape.append(jax.ShapeDtypeStruct((LANES, d), BF16))
        args += [w_in_t, w_in_t]
    if final:
        in_specs.append(pl.BlockSpec((1, d), lambda i, j: (0, 0)))
        args.append(final_g.reshape(1, d))
    return pl.pallas_call(
        functools.partial(_ffn_body, final=final, n_cast=n_cast, repack=repack is not None),
        grid=(ni, nj),
        in_specs=in_specs,
        out_specs=out_specs,
        out_shape=out_shape,
        scratch_shapes=[pltpu.VMEM((tm, d), BF16)],
        compiler_params=_cparams(("arbitrary", "arbitrary")),
        name="ffn_final" if final else "ffn",
    )(*args)


def _inproj_body(x_ref, g_ref, w_ref, wl_ref, z_ref, glr_ref, xc_ref, un_ref, ph_ref, *,
                 gate_block0, ssm_block, ssm_off):
    j = pl.program_id(1)

    @pl.when(j == 0)
    def _():
        un = _rms(x_ref[...], g_ref[...]).astype(BF16)
        un_ref[...] = un
        glr_ref[...] = _dot_nt(un, wl_ref[...])

    acc = _dot_nt(un_ref[...], w_ref[...])
    z_ref[...] = jnp.where(j >= gate_block0, _sigmoid(acc), acc).astype(BF16)

    @pl.when(j == ssm_block)
    def _():
        ncol = ph_ref.shape[0]
        for c in range(ncol):
            ph_ref[c] = acc[:, ssm_off + c * LANES:ssm_off + (c + 1) * LANES]
        rows = ph_ref.shape[1] // SSM_CHUNK
        for s in range(SSM_CHUNK):
            for c in range(ncol):
                piece = ph_ref[c, pl.ds(s, rows, stride=SSM_CHUNK), :].astype(BF16)
                xc_ref[c, :, s * LANES:(s + 1) * LANES] = piece


def _inproj(x, g, w_main, w_glr, layer, ssm_col0, ssm_width, gate_col0, *, tm=1024, tn=1024):
    m, d = x.shape
    n = w_main.shape[0]
    body = functools.partial(_inproj_body, gate_block0=gate_col0 // tn,
                             ssm_block=ssm_col0 // tn, ssm_off=ssm_col0 % tn)
    return pl.pallas_call(
        body,
        grid=(m // tm, n // tn),
        in_specs=[
            pl.BlockSpec((tm, d), lambda i, j: (i, 0)),
            pl.BlockSpec((None, 1, d), lambda i, j: (layer, 0, 0)),
            pl.BlockSpec((tn, d), lambda i, j: (j, 0)),
            pl.BlockSpec((LANES, d), lambda i, j: (0, 0)),
        ],
        out_specs=[
            pl.BlockSpec((tm, tn), lambda i, j: (i, j)),
            pl.BlockSpec((tm, LANES), lambda i, j: (i, 0)),
            pl.BlockSpec((ssm_width // LANES, tm // SSM_CHUNK, SSM_CHUNK * LANES), lambda i, j: (0, i, 0)),
        ],
        out_shape=[
            jax.ShapeDtypeStruct((m, n), BF16),
            jax.ShapeDtypeStruct((m, LANES), F32),
            jax.ShapeDtypeStruct((ssm_width // LANES, m // SSM_CHUNK, SSM_CHUNK * LANES), BF16),
        ],
        scratch_shapes=[pltpu.VMEM((tm, d), BF16), pltpu.VMEM((ssm_width // LANES, tm, LANES), F32)],
        compiler_params=_cparams(("parallel", "arbitrary")),
        name="inproj",
    )(x, g, w_main, w_glr)


def _pool_mix(x, halo, t, w_ref, scale):
    ts = x.shape[0]
    halo = jnp.where(t > 0, halo, 0.0)
    e = jnp.concatenate([halo, x], axis=0)
    s2 = e + pltpu.roll(e, 1, 0)
    s4 = s2 + pltpu.roll(s2, 2, 0)
    s8 = s4 + pltpu.roll(s4, 4, 0)
    s16 = s8 + pltpu.roll(s8, 8, 0)
    pos = (t * ts + 1 + lax.broadcasted_iota(jnp.int32, (ts, 1), 0)).astype(F32)
    outs = []
    for gi, (w, s) in enumerate(zip(POOL_WINDOWS, (s2, s4, s8, s16))):
        cols = slice(gi * POOL_GROUP, (gi + 1) * POOL_GROUP)
        mean = s[POOL_HALO:, cols] / jnp.minimum(pos, float(w))
        outs.append(_dot((mean - x[:, cols]).astype(BF16), w_ref[gi]))
    return jnp.concatenate(outs, axis=1) * scale


def _gla_streams(streams, ts):
    c = GLA_CHUNK
    nc = ts // c
    rowc = lax.broadcasted_iota(jnp.int32, (ts, 1), 0) % c
    causal = (lax.broadcasted_iota(jnp.int32, (c, c), 0)
              >= lax.broadcasted_iota(jnp.int32, (c, c), 1))
    qs, ks, vs, glrs, w2s, biases, sts = (list(a) for a in zip(*streams))
    logits = [_dot(glr, w2) + bias for glr, w2, bias in zip(glrs, w2s, biases)]
    bcs = [(jnp.minimum(l, 0.0) - jnp.log1p(jnp.exp(-jnp.abs(l)))) * (1.0 / GLA_TAU) for l in logits]
    step = 1
    while step < c:
        bcs = [bc + jnp.where(rowc >= step, pltpu.roll(bc, step, 0), 0.0) for bc in bcs]
        step *= 2
    bc3 = [bc.reshape(nc, c, GLA_DK) for bc in bcs]
    b_last = [b[:, c - 1:c, :] for b in bc3]
    b_mid = [b[:, c // 2 - 1:c // 2, :] for b in bc3]
    q3 = [q.astype(F32).reshape(nc, c, GLA_DK) * (GLA_DK ** -0.5) for q in qs]
    k3 = [k.astype(F32).reshape(nc, c, GLA_DK) for k in ks]
    v3 = [v.reshape(nc, c, GLA_DV) for v in vs]
    qe = [(q * jnp.exp(b - m)).astype(BF16) for q, b, m in zip(q3, bc3, b_mid)]
    ke = [(k * jnp.exp(m - b)).astype(BF16) for k, b, m in zip(k3, bc3, b_mid)]
    scores = [jnp.einsum('nid,njd->nij', a, b, preferred_element_type=F32) for a, b in zip(qe, ke)]
    scores = [jnp.where(causal[None], s, 0.0).astype(BF16) for s in scores]
    o_intra = [jnp.einsum('nij,njv->niv', s, v, preferred_element_type=F32) for s, v in zip(scores, v3)]
    qd = [(q * jnp.exp(b)).astype(BF16) for q, b in zip(q3, bc3)]
    kd = [(k * jnp.exp(bl - b)).astype(BF16) for k, b, bl in zip(k3, bc3, b_last)]
    dec_t = [jnp.transpose(jnp.exp(bl.reshape(nc, GLA_DK))) for bl in b_last]
    outs = [[] for _ in streams]
    for n in range(nc):
        for i in range(len(streams)):
            outs[i].append(o_intra[i][n] + _dot(qd[i][n], sts[i].astype(BF16)))
            kv = lax.dot_general(kd[i][n], v3[i][n], (((0,), (0,)), ((), ())), preferred_element_type=F32)
            sts[i] = dec_t[i][:, n:n + 1] * sts[i] + kv
    return [jnp.concatenate(o, axis=0).astype(BF16) for o in outs], sts


def _gla_body(q_ref, k_ref, v0_ref, v1_ref, glr_ref, w2_ref, b_ref, o_ref, st_ref, *, ts):
    @pl.when(pl.program_id(1) == 0)
    def _():
        st_ref[...] = jnp.zeros_like(st_ref)

    v_refs = (v0_ref, v1_ref)
    heads_per_block = v0_ref.shape[2] // GLA_DV
    streams, where = [], []
    for s in range(q_ref.shape[0]):
        glr = glr_ref[s].astype(BF16)
        for h in range(GLA_HEADS):
            v_ref = v_refs[h // heads_per_block]
            vs = slice((h % heads_per_block) * GLA_DV, (h % heads_per_block + 1) * GLA_DV)
            ks = slice(h * GLA_DK, (h + 1) * GLA_DK)
            streams.append((q_ref[s, :, ks], k_ref[s, :, ks], v_ref[s, :, vs], glr,
                            w2_ref[h], b_ref[h], st_ref[s, h]))
            where.append((s, h))
    outs, sts = _gla_streams(streams, ts)
    for (s, h), o, st in zip(where, outs, sts):
        st_ref[s, h] = st
        o_ref[s, :, h * GLA_DV:(h + 1) * GLA_DV] = o


def _gla(z, glr, w2, bias, layer, batch, seq, cols, *, ts=512, nseq=2):
    h = GLA_HEADS
    nt = seq // ts
    q0, k0, v0 = cols
    wb = h * GLA_DK
    z3 = z.reshape(batch, seq, z.shape[1])
    zspec = lambda c0: pl.BlockSpec((nseq, ts, wb), lambda b, t: (b, t, c0 // wb))
    wspec = lambda last2: pl.BlockSpec((None, h) + last2, lambda b, t: (layer, 0, 0, 0))
    out = pl.pallas_call(
        functools.partial(_gla_body, ts=ts),
        grid=(batch // nseq, nt),
        in_specs=[
            zspec(q0), zspec(k0), zspec(v0), zspec(v0 + wb),
            pl.BlockSpec((nseq, ts, LANES), lambda b, t: (b, t, 0)),
            wspec((LANES, GLA_DK)), wspec((1, GLA_DK)),
        ],
        out_specs=pl.BlockSpec((nseq, ts, h * GLA_DV), lambda b, t: (b, t, 0)),
        out_shape=jax.ShapeDtypeStruct((batch, seq, h * GLA_DV), BF16),
        scratch_shapes=[pltpu.VMEM((nseq, h, GLA_DK, GLA_DV), F32)],
        compiler_params=_cparams(("parallel", "arbitrary")),
        name="gla",
    )(z3, z3, z3, z3, glr.reshape(batch, seq, LANES), w2, bias)
    return out.reshape(batch * seq, h * GLA_DV)


SSM_COL_GROUPS = LANES // SSM_GROUP
SSM_BLOCK_GROUPS = 2
SSM_TPAIR = 2


def _ssm_body(x_ref, kc_ref, pc_ref, qc_ref, ar_ref, ai_ref, o_ref, hp_ref, kst_ref, v_ref, *,
              chunks_per_seq, n_state_blocks):
    u = pl.program_id(0)
    ncol, rows = x_ref.shape[0], x_ref.shape[1]
    blocks_per_col = n_state_blocks // ncol
    npairs = SSM_CHUNK // SSM_TPAIR
    ns = 2 * SSM_STATE
    bw = SSM_BLOCK_GROUPS * ns
    lane_group = lax.broadcasted_iota(jnp.int32, (1, LANES), 1) // SSM_GROUP

    def p_tile(n):
        pc = pc_ref[n // blocks_per_col]
        row_group = (lax.broadcasted_iota(jnp.int32, (pc.shape[0], 1), 0) // SSM_GROUP) % SSM_COL_GROUPS
        g0 = (n % blocks_per_col) * SSM_BLOCK_GROUPS
        return jnp.concatenate([jnp.where(row_group == g0 + gg, pc, 0.0)
                                for gg in range(SSM_BLOCK_GROUPS)], axis=1).astype(BF16)

    @pl.when(u == 0)
    def _():
        row_group = lax.broadcasted_iota(jnp.int32, (LANES, 1), 0) // SSM_GROUP

        def kbd(c, lag):
            kc = kc_ref[c, lag]
            tiled = jnp.broadcast_to(kc[None], (SSM_COL_GROUPS,) + kc.shape).reshape(LANES, LANES)
            return jnp.where(row_group == lane_group, tiled, 0.0).astype(BF16)

        for c in range(ncol):
            for r in range(SSM_CHUNK):
                left = kbd(c, SSM_CHUNK - 2 - r) if r <= SSM_CHUNK - 2 else jnp.zeros((LANES, LANES), BF16)
                kst_ref[c, r * LANES:(r + 1) * LANES, :] = jnp.concatenate(
                    [left, kbd(c, SSM_CHUNK - 1 - r)], axis=1)
        v_ref[...] = _dot(x_ref[0], p_tile(0))

    @pl.when(u < n_state_blocks)
    def _():
        rowc = lax.broadcasted_iota(jnp.int32, (rows, 1), 0) % chunks_per_seq
        hst = v_ref[...]
        nxt = jnp.minimum(u + 1, n_state_blocks - 1)
        v_next = _dot(x_ref[nxt // blocks_per_col], p_tile(nxt))
        k, si = 1, 0
        while k < chunks_per_seq:
            hs = jnp.where(rowc >= k, pltpu.roll(hst, k, 0), 0.0)
            sw = jnp.concatenate([pltpu.roll(hs[:, b0:b0 + ns], SSM_STATE, 1)
                                  for b0 in range(0, bw, ns)], axis=1)
            hst = hst + ar_ref[u, si:si + 1, :] * hs + ai_ref[u, si:si + 1, :] * sw
            k *= 2
            si += 1
        hp_ref[u] = jnp.where(rowc >= 1, pltpu.roll(hst, 1, 0), 0.0).astype(BF16)
        v_ref[...] = v_next

    @pl.when(u >= n_state_blocks)
    def _():
        ob = u - n_state_blocks
        c = ob // npairs
        v = ob % npairs
        grp = lax.broadcasted_iota(jnp.int32, (SSM_COL_GROUPS, 1, 1), 0)
        q_tile = jnp.concatenate(
            [jnp.where(grp == lane_group[None], qc_ref[c, SSM_TPAIR * v + tt][None], 0.0)
             .reshape(SSM_COL_GROUPS * ns, LANES) for tt in range(SSM_TPAIR)], axis=1).astype(BF16)
        yq = _dot(hp_ref[c * blocks_per_col], q_tile[0:bw, :])
        for kk in range(1, blocks_per_col):
            yq = yq + _dot(hp_ref[c * blocks_per_col + kk], q_tile[kk * bw:(kk + 1) * bw, :])
        for vs in range(npairs):
            kext = (vs + 1) * SSM_TPAIR * LANES

            @pl.when(v == vs)
            def _():
                start = (npairs - 1 - vs) * SSM_TPAIR * LANES
                intra = _dot(x_ref[c, :, 0:kext], kst_ref[c, start:start + kext, :])
                o_ref[...] = (yq + intra).astype(BF16)


def _ssm_core(xcat, kc, pc, qc, ar, ai, layer, chunks_per_seq):
    ncol, rows, xw = xcat.shape
    nb = ar.shape[1]
    bw = ar.shape[3]
    ow = SSM_TPAIR * LANES
    nout = ncol * SSM_CHUNK // SSM_TPAIR
    one = pl.Buffered(1)
    whole = lambda a: pl.BlockSpec((None,) + a.shape[1:], lambda u: (layer,) + (0,) * (a.ndim - 1),
                                   pipeline_mode=one)
    return pl.pallas_call(
        functools.partial(_ssm_body, chunks_per_seq=chunks_per_seq, n_state_blocks=nb),
        grid=(nb + nout,),
        in_specs=[
            pl.BlockSpec((ncol, rows, xw), lambda u: (0, 0, 0), pipeline_mode=one),
            whole(kc), whole(pc), whole(qc), whole(ar), whole(ai),
        ],
        out_specs=pl.BlockSpec((None, rows, ow), lambda u: (jnp.maximum(u - nb, 0), 0, 0)),
        out_shape=jax.ShapeDtypeStruct((nout, rows, ow), BF16),
        scratch_shapes=[
            pltpu.VMEM((nb, rows, bw), BF16),
            pltpu.VMEM((ncol, SSM_CHUNK * LANES, ow), BF16),
            pltpu.VMEM((rows, bw), F32),
        ],
        compiler_params=_cparams(("arbitrary",)),
        name="ssm_core",
    )(xcat, kc, pc, qc, ar, ai)


def _ssm_operators(a_re, a_im, log_dt, b_re, b_im, c_re, c_im, chunks_per_seq):
    t = SSM_CHUNK
    g, p = a_re.shape
    hg = b_re.shape[-1]
    cgc = SSM_COL_GROUPS
    ncol = g // cgc
    bgc = SSM_BLOCK_GROUPS
    dt = jnp.exp(log_dt)[:, None]
    lam_re, lam_im = dt * a_re, dt * a_im

    def powers(n):
        n = jnp.asarray(n, F32)[:, None, None]
        mag = jnp.exp(n * lam_re)
        return mag * jnp.cos(n * lam_im), mag * jnp.sin(n * lam_im)

    ab_re, ab_im = powers([1.0])
    ab_re, ab_im = ab_re[0], ab_im[0]
    den = a_re * a_re + a_im * a_im
    f_re = ((ab_re - 1.0) * a_re + ab_im * a_im) / den
    f_im = (ab_im * a_re - (ab_re - 1.0) * a_im) / den
    bb_re = f_re[..., None] * b_re - f_im[..., None] * b_im
    bb_im = f_re[..., None] * b_im + f_im[..., None] * b_re
    pw_re, pw_im = powers(np.arange(t + 1))
    cp_re = c_re[None] * pw_re[:, :, None, :] - c_im[None] * pw_im[:, :, None, :]
    cp_im = c_re[None] * pw_im[:, :, None, :] + c_im[None] * pw_re[:, :, None, :]
    ktau = (jnp.einsum('tghp,gpk->tkgh', cp_re[:t], bb_re)
            - jnp.einsum('tghp,gpk->tkgh', cp_im[:t], bb_im))
    kc = ktau.reshape(t, hg, ncol, cgc * hg).transpose(2, 0, 1, 3)
    bt_re, bt_im = bb_re.transpose(0, 2, 1), bb_im.transpose(0, 2, 1)
    rp_re = pw_re[t - 1 - np.arange(t)][:, :, None, :]
    rp_im = pw_im[t - 1 - np.arange(t)][:, :, None, :]
    pin = jnp.concatenate([rp_re * bt_re[None] - rp_im * bt_im[None],
                           rp_re * bt_im[None] + rp_im * bt_re[None]], axis=3)
    pc = pin.reshape(t, ncol, cgc, hg, 2 * p).transpose(1, 0, 2, 3, 4).reshape(ncol, t * cgc * hg, 2 * p)
    ct_re, ct_im = c_re.transpose(2, 0, 1)[None], c_im.transpose(2, 0, 1)[None]
    pt_re = pw_re[1:].transpose(0, 2, 1)[..., None]
    pt_im = pw_im[1:].transpose(0, 2, 1)[..., None]
    qg = jnp.concatenate([ct_re * pt_re - ct_im * pt_im,
                          -(ct_re * pt_im + ct_im * pt_re)], axis=1)
    qc = qg.reshape(t, 2 * p, ncol, cgc * hg).transpose(2, 0, 1, 3)
    nblk = g // bgc
    nsteps = int(np.log2(chunks_per_seq))
    sr, si = powers(t * 2.0 ** np.arange(nsteps))
    blk = lambda a: a.reshape(nsteps, nblk, bgc * 2 * p).transpose(1, 0, 2)
    ar = blk(jnp.concatenate([sr, sr], axis=2))
    ai = blk(jnp.concatenate([-si, si], axis=2))
    return kc, pc, qc, ar, ai


def _merge_body(x_ref, pin_ref, halo_ref, og_ref, r0_ref, r1_ref, yc_ref, sin_ref, g0_ref, g1_ref, g2_ref,
                pw_ref, ps_ref, gain_ref, d_ref, wglu_ref, wbp_ref, wbg0_ref, wbg1_ref, wbs_ref, wo_ref,
                o_ref, ys_ref, *, tiles_per_seq):
    rows = yc_ref.shape[1]
    npairs = SSM_CHUNK // SSM_TPAIR
    for ob in range(yc_ref.shape[0]):
        c, v = divmod(ob, npairs)
        plane = yc_ref[ob].astype(F32)
        for tt in range(SSM_TPAIR):
            ys_ref[c, pl.ds(SSM_TPAIR * v + tt, rows, stride=SSM_CHUNK), :] = plane[:, tt * LANES:(tt + 1) * LANES]
    t = pl.program_id(0) % tiles_per_seq
    yp = _pool_mix(pin_ref[...].astype(F32), halo_ref[...].astype(F32), t, pw_ref, ps_ref[...])
    m = g0_ref[...].astype(F32) * _dot(yp.astype(BF16), wbp_ref[...])
    r_refs = (r0_ref, r1_ref)
    heads_per_block = r0_ref.shape[1] // GLA_DV
    yg_heads = []
    for h in range(GLA_HEADS):
        hs = slice(h * GLA_DV, (h + 1) * GLA_DV)
        rs = slice((h % heads_per_block) * GLA_DV, (h % heads_per_block + 1) * GLA_DV)
        o = og_ref[:, hs].astype(F32)
        o = o * lax.rsqrt(jnp.mean(o * o, axis=-1, keepdims=True) + EPS) * gain_ref[:, hs]
        r = r_refs[h // heads_per_block][:, rs].astype(F32)
        yg_heads.append((o * (r * _sigmoid(r))).astype(BF16))
    yg = jnp.concatenate(yg_heads, axis=1)
    half = wbg0_ref.shape[0]
    gla_proj = _dot(yg[:, :half], wbg0_ref[...]) + _dot(yg[:, half:], wbg1_ref[...])
    m = m + g1_ref[...].astype(F32) * gla_proj
    ys = jnp.concatenate([ys_ref[c] for c in range(ys_ref.shape[0])], axis=1)
    y = ys + d_ref[...] * sin_ref[...].astype(F32)
    y = jax.nn.gelu(y)
    y = y * _sigmoid(_dot(y.astype(BF16), wglu_ref[...]))
    m = m + g2_ref[...].astype(F32) * _dot(y.astype(BF16), wbs_ref[...])
    o_ref[...] = x_ref[...] + _dot(m.astype(BF16), wo_ref[...])


def _merge(x, o_gla, ycat, z, r_col0, ssm_col0, gate_col0, seq, pool_w, pool_scale, gla_gain,
           d_skip, w_glu, wb, w_out, layer, *, tm=256):
    m, d = x.shape
    wp, wg, ws = pool_scale.shape[2], o_gla.shape[1], d_skip.shape[2]
    br = BRANCH_ROWS
    assert wp == br and ws == br and wg == 2 * br and seq % tm == 0 and tm % POOL_HALO == 0
    gb = gate_col0 // d
    hb = tm // POOL_HALO
    one = pl.Buffered(1)
    wspec = lambda shape, r: pl.BlockSpec((None,) + shape, lambda i: (layer, r, 0), pipeline_mode=one)
    return pl.pallas_call(
        functools.partial(_merge_body, tiles_per_seq=seq // tm),
        grid=(m // tm,),
        in_specs=[
            pl.BlockSpec((tm, d), lambda i: (i, 0)),
            pl.BlockSpec((tm, wp), lambda i: (i, 0)),
            pl.BlockSpec((POOL_HALO, wp), lambda i: (jnp.maximum(i * hb - 1, 0), 0)),
            pl.BlockSpec((tm, wg), lambda i: (i, 0)),
            pl.BlockSpec((tm, br), lambda i: (i, r_col0 // br)),
            pl.BlockSpec((tm, br), lambda i: (i, r_col0 // br + 1)),
            pl.BlockSpec((ycat.shape[0], tm // SSM_CHUNK, ycat.shape[2]), lambda i: (0, i, 0)),
            pl.BlockSpec((tm, ws), lambda i: (i, ssm_col0 // ws)),
            pl.BlockSpec((tm, d), lambda i: (i, gb)),
            pl.BlockSpec((tm, d), lambda i: (i, gb + 1)),
            pl.BlockSpec((tm, d), lambda i: (i, gb + 2)),
            pl.BlockSpec((None, len(POOL_WINDOWS), POOL_GROUP, POOL_GROUP), lambda i: (layer, 0, 0, 0),
                         pipeline_mode=one),
            wspec((1, wp), 0),
            wspec((1, wg), 0),
            wspec((1, ws), 0),
            wspec((ws, ws), 0),
            wspec((br, d), 0),
            wspec((br, d), 1),
            wspec((br, d), 2),
            wspec((br, d), 3),
            wspec((d, d), 0),
        ],
        out_specs=pl.BlockSpec((tm, d), lambda i: (i, 0)),
        out_shape=jax.ShapeDtypeStruct((m, d), F32),
        scratch_shapes=[pltpu.VMEM((ws // LANES, tm, LANES), F32)],
        compiler_params=_cparams(("parallel",)),
        name="merge",
    )(x, z, z, o_gla, z, z, ycat, z, z, z, z, pool_w, pool_scale, gla_gain, d_skip, w_glu,
      wb, wb, wb, wb, w_out)


def kernel(x, ffn1_norm, ffn1_w_gate, ffn1_w_up, ffn1_w_down, mix_norm, w_in, pool_w, pool_scale, gla_w_gate2, gla_gate_bias, gla_norm, ssm_a_re, ssm_a_im, ssm_log_dt, ssm_b_re, ssm_b_im, ssm_c_re, ssm_c_im, ssm_d, ssm_w_glu, w_branch, w_out, ffn2_norm, ffn2_w_gate, ffn2_w_up, ffn2_w_down, final_norm):
    batch, seq, d = x.shape
    depth = w_in.shape[0]
    m = batch * seq
    pool_width = pool_scale.shape[1]
    qk_width = GLA_HEADS * GLA_DK
    v_width = GLA_HEADS * GLA_DV
    ssm_width = ssm_d.shape[1]
    n_groups = ssm_width // SSM_GROUP
    q0 = pool_width
    k0 = q0 + qk_width
    v0 = k0 + qk_width
    r0 = v0 + v_width
    glr0 = r0 + v_width
    ssm_src0 = glr0 + GLA_RANK
    ssm0 = glr0
    gate0 = ssm0 + ssm_width
    chunks_per_seq = seq // SSM_CHUNK

    bf = lambda a: a.astype(BF16)
    row3 = lambda a: a.reshape(depth, 1, a.shape[-1])
    w_in_t = jnp.swapaxes(w_in, 1, 2)
    ffn1_w = (ffn1_w_gate, ffn1_w_up, ffn1_w_down)
    ffn2_w = (ffn2_w_gate, ffn2_w_up, ffn2_w_down)
    f1 = [bf(w[0]) for w in ffn1_w]
    wbr, wo, wglu, pw = bf(w_branch), bf(w_out), bf(ssm_w_glu), bf(pool_w)
    n1, nm, n2 = row3(ffn1_norm), row3(mix_norm), row3(ffn2_norm)
    pscale, dskip = row3(pool_scale), row3(ssm_d)
    w2 = gla_w_gate2.reshape(depth, GLA_RANK, GLA_HEADS, GLA_DK).transpose(0, 2, 1, 3)
    w2 = bf(jnp.pad(w2, ((0, 0), (0, 0), (0, LANES - GLA_RANK), (0, 0))))
    gbias = gla_gate_bias.reshape(depth, GLA_HEADS, 1, GLA_DK)
    ggain = row3(gla_norm)
    ssm_ops = jax.vmap(functools.partial(_ssm_operators, chunks_per_seq=chunks_per_seq))(
        ssm_a_re, ssm_a_im, ssm_log_dt, ssm_b_re, ssm_b_im, ssm_c_re, ssm_c_im)

    xf = x.reshape(m, d)
    for l in range(depth):
        xf, *side = _ffn(xf, n1, l, *f1, cast=(l,) + ffn2_w, repack=(l, w_in_t, glr0, GLA_RANK))
        f2, (w_main, w_glr) = side[:3], side[3:]
        z, glr, xcat = _inproj(xf, nm, w_main, w_glr, l, ssm0, ssm_width, gate0)
        o_gla = _gla(z, glr, w2, gbias, l, batch, seq, (q0, k0, v0))
        ycat = _ssm_core(xcat, *ssm_ops, l, chunks_per_seq)
        xf = _merge(xf, o_gla, ycat, z, r0, ssm0, gate0, seq, pw, pscale, ggain, dskip, wglu, wbr, wo, l)
        last = l == depth - 1
        xf, *f1 = _ffn(xf, n2, l, *f2, cast=None if last else (l + 1,) + ffn1_w,
                       final_g=final_norm if last else None)
    return xf.reshape(batch, seq, d)
```

```python
import functools

import jax
import jax.numpy as jnp
import numpy as np
from jax import lax
from jax.experimental import pallas as pl
from jax.experimental.pallas import tpu as pltpu

F32 = jnp.float32
BF16 = jnp.bfloat16

EPS = 1e-6
POOL_WINDOWS = (2, 4, 8, 16)
POOL_GROUP = 128
POOL_HALO = 16
GLA_HEADS = 4
GLA_DK = 128
GLA_DV = 256
GLA_RANK = 16
GLA_TAU = 16.0
GLA_CHUNK = 64
SSM_GROUP = 16
SSM_STATE = 64
SSM_CHUNK = 16
LANES = 128
FFN_DOWN_CHUNK = 512
FFN_ROW_CHUNK = 1024
FFN_FIRST_ROWS = 256
FFN_EPILOGUE_ROWS = 256
REPACK_ROWS = 64
INPROJ_FIRST_ROWS = 256
BRANCH_ROWS = 512

VMEM_LIMIT = 60 * 1024 * 1024


def _cparams(sem):
    return pltpu.CompilerParams(dimension_semantics=sem, vmem_limit_bytes=VMEM_LIMIT)


def _rms(xf, g):
    return xf * lax.rsqrt(jnp.mean(xf * xf, axis=-1, keepdims=True) + EPS) * g


def _dot(a, b):
    return jnp.dot(a, b, preferred_element_type=F32)


def _sigmoid(a):
    return 0.5 * jnp.tanh(0.5 * a) + 0.5


def _dot_nt(a, b_t):
    return lax.dot_general(a, b_t, (((1,), (1,)), ((), ())), preferred_element_type=F32)


def _ffn_body(x_ref, g_ref, wg_ref, wu_ref, wd_ref, *rest, final, n_cast, repack):
    rest = list(rest)
    cast_in = [rest.pop(0) for _ in range(n_cast)]
    rp_in = [rest.pop(0) for _ in range(2 if repack else 0)]
    fg_ref = rest.pop(0) if final else None
    o_ref = rest.pop(0)
    cast_out = [rest.pop(0) for _ in range(n_cast)]
    rp_out = [rest.pop(0) for _ in range(2 if repack else 0)]
    xn_ref, = rest
    j = pl.program_id(1)

    for src, dst in zip(cast_in, cast_out):
        dst[...] = src[...].astype(BF16)
    if repack:
        (rows_ref, drop_ref), (w_ref, wl_ref) = rp_in, rp_out
        w_ref[...] = rows_ref[0].astype(BF16)
        pad = jnp.zeros((wl_ref.shape[0] - drop_ref.shape[1], drop_ref.shape[2]), F32)
        wl_ref[...] = jnp.concatenate([drop_ref[0], pad], axis=0).astype(BF16)

    def swiglu_step(rs, xn, first):
        a = _dot(xn, wg_ref[...])
        b = _dot(xn, wu_ref[...])
        h = (a * _sigmoid(a) * b).astype(BF16)
        for c0 in range(0, o_ref.shape[1], FFN_DOWN_CHUNK):
            cs = slice(c0, c0 + FFN_DOWN_CHUNK)
            if first:
                o_ref[rs, cs] = _dot(h, wd_ref[:, cs])
            else:
                o_ref[rs, cs] += _dot(h, wd_ref[:, cs])

    @pl.when(j == 0)
    def _():
        for r0 in range(0, o_ref.shape[0], FFN_FIRST_ROWS):
            rs = slice(r0, r0 + FFN_FIRST_ROWS)
            xn = _rms(x_ref[rs, :], g_ref[...]).astype(BF16)
            xn_ref[rs, :] = xn
            swiglu_step(rs, xn, True)

    @pl.when(j > 0)
    def _():
        for r0 in range(0, o_ref.shape[0], FFN_ROW_CHUNK):
            rs = slice(r0, r0 + FFN_ROW_CHUNK)
            swiglu_step(rs, xn_ref[rs, :], False)

    @pl.when(j == pl.num_programs(1) - 1)
    def _():
        def epilogue(r, carry):
            rs = pl.ds(pl.multiple_of(r * FFN_EPILOGUE_ROWS, FFN_EPILOGUE_ROWS), FFN_EPILOGUE_ROWS)
            y = x_ref[rs, :] + 0.5 * o_ref[rs, :]
            if final:
                y = _rms(y, fg_ref[...])
            o_ref[rs, :] = y
            return carry
        lax.fori_loop(0, o_ref.shape[0] // FFN_EPILOGUE_ROWS, epilogue, 0)


def _ffn(x, g, layer, wg, wu, wd, cast=None, repack=None, final_g=None, *, tm=1024, tf=512):
    m, d = x.shape
    f = wg.shape[1]
    ni, nj = m // tm, f // tf
    final = final_g is not None
    in_specs = [
        pl.BlockSpec((tm, d), lambda i, j: (i, 0)),
        pl.BlockSpec((None, 1, d), lambda i, j: (layer, 0, 0)),
        pl.BlockSpec((d, tf), lambda i, j: (0, j)),
        pl.BlockSpec((d, tf), lambda i, j: (0, j)),
        pl.BlockSpec((tf, d), lambda i, j: (j, 0)),
    ]
    args = [x, g, wg, wu, wd]
    out_specs = [pl.BlockSpec((tm, d), lambda i, j: (i, 0))]
    out_shape = [jax.ShapeDtypeStruct((m, d), F32)]
    n_cast = 0
    if cast is not None:
        cl, cg, cu, cd = cast
        n_cast = 3
        rows_up = d // ni
        rows_dn = f // (ni * nj)
        assert d % ni == 0 and f % (ni * nj) == 0 and rows_up % 16 == 0 and rows_dn % 16 == 0
        for w in (cg, cu):
            in_specs.append(pl.BlockSpec((None, rows_up, tf), lambda i, j: (cl, i, j)))
            out_specs.append(pl.BlockSpec((rows_up, tf), lambda i, j: (i, j)))
            out_shape.append(jax.ShapeDtypeStruct((d, f), BF16))
            args.append(w)
        in_specs.append(pl.BlockSpec((None, rows_dn, d), lambda i, j: (cl, i * nj + j, 0)))
        out_specs.append(pl.BlockSpec((rows_dn, d), lambda i, j: (i * nj + j, 0)))
        out_shape.append(jax.ShapeDtypeStruct((f, d), BF16))
        args.append(cd)
    if repack is not None:
        rl, w_in_t, drop0, drop_width = repack
        n = w_in_t.shape[1] - drop_width
        rb = REPACK_ROWS
        nblk = n // rb
        assert n % rb == 0 and nblk <= ni * nj and drop0 % rb == 0 and drop_width % 8 == 0
        blk = lambda i, j: jnp.minimum(i * nj + j, nblk - 1)
        src_row = lambda b: (b * (rb // 8) + jnp.where(b >= drop0 // rb, drop_width // 8, 0)) * 8
        in_specs.append(pl.BlockSpec((pl.Element(1), pl.Element(rb), pl.Element(d)),
                                     lambda i, j: (rl, src_row(blk(i, j)), 0)))
        in_specs.append(pl.BlockSpec((pl.Element(1), pl.Element(drop_width), pl.Element(d)),
                                     lambda i, j: (rl, drop0, 0)))
        out_specs.append(pl.BlockSpec((rb, d), lambda i, j: (blk(i, j), 0)))
        out_specs.append(pl.BlockSpec((LANES, d), lambda i, j: (0, 0)))
        out_shape.append(jax.ShapeDtypeStruct((n, d), BF16))
        out_shape.append(jax.ShapeDtypeStruct((LANES, d), BF16))
        args += [w_in_t, w_in_t]
    if final:
        in_specs.append(pl.BlockSpec((1, d), lambda i, j: (0, 0)))
        args.append(final_g.reshape(1, d))
    return pl.pallas_call(
        functools.partial(_ffn_body, final=final, n_cast=n_cast, repack=repack is not None),
        grid=(ni, nj),
        in_specs=in_specs,
        out_specs=out_specs,
        out_shape=out_shape,
        scratch_shapes=[pltpu.VMEM((tm, d), BF16)],
        compiler_params=_cparams(("arbitrary", "arbitrary")),
        name="ffn_final" if final else "ffn",
    )(*args)


def _inproj_body(x_ref, g_ref, w_ref, wl_ref, z_ref, glr_ref, xc_ref, un_ref, ph_ref, *,
                 gate_block0, ssm_block, ssm_off):
    j = pl.program_id(1)

    @pl.when(j == 0)
    def _():
        for r0 in range(0, z_ref.shape[0], INPROJ_FIRST_ROWS):
            rs = slice(r0, r0 + INPROJ_FIRST_ROWS)
            un = _rms(x_ref[rs, :], g_ref[...]).astype(BF16)
            un_ref[rs, :] = un
            glr_ref[rs, :] = _dot_nt(un, wl_ref[...])
            z_ref[rs, :] = _dot_nt(un, w_ref[...]).astype(BF16)

    @pl.when(j > 0)
    def _():
        acc = _dot_nt(un_ref[...], w_ref[...])
        z_ref[...] = jnp.where(j >= gate_block0, _sigmoid(acc), acc).astype(BF16)

        @pl.when(j == ssm_block)
        def _():
            ncol = ph_ref.shape[0]
            for c in range(ncol):
                ph_ref[c] = acc[:, ssm_off + c * LANES:ssm_off + (c + 1) * LANES]
            rows = ph_ref.shape[1] // SSM_CHUNK
            for s in range(SSM_CHUNK):
                for c in range(ncol):
                    piece = ph_ref[c, pl.ds(s, rows, stride=SSM_CHUNK), :].astype(BF16)
                    xc_ref[c, :, s * LANES:(s + 1) * LANES] = piece


def _inproj(x, g, w_main, w_glr, layer, ssm_col0, ssm_width, gate_col0, *, tm=1024, tn=1024):
    m, d = x.shape
    n = w_main.shape[0]
    assert gate_col0 >= tn and ssm_col0 >= tn
    body = functools.partial(_inproj_body, gate_block0=gate_col0 // tn,
                             ssm_block=ssm_col0 // tn, ssm_off=ssm_col0 % tn)
    return pl.pallas_call(
        body,
        grid=(m // tm, n // tn),
        in_specs=[
            pl.BlockSpec((tm, d), lambda i, j: (i, 0)),
            pl.BlockSpec((None, 1, d), lambda i, j: (layer, 0, 0)),
            pl.BlockSpec((tn, d), lambda i, j: (j, 0)),
            pl.BlockSpec((LANES, d), lambda i, j: (0, 0)),
        ],
        out_specs=[
            pl.BlockSpec((tm, tn), lambda i, j: (i, j)),
            pl.BlockSpec((tm, LANES), lambda i, j: (i, 0)),
            pl.BlockSpec((ssm_width // LANES, tm // SSM_CHUNK, SSM_CHUNK * LANES), lambda i, j: (0, i, 0)),
        ],
        out_shape=[
            jax.ShapeDtypeStruct((m, n), BF16),
            jax.ShapeDtypeStruct((m, LANES), F32),
            jax.ShapeDtypeStruct((ssm_width // LANES, m // SSM_CHUNK, SSM_CHUNK * LANES), BF16),
        ],
        scratch_shapes=[pltpu.VMEM((tm, d), BF16), pltpu.VMEM((ssm_width // LANES, tm, LANES), F32)],
        compiler_params=_cparams(("parallel", "arbitrary")),
        name="inproj",
    )(x, g, w_main, w_glr)


def _pool_mix(x, halo, t, w_ref, scale):
    ts = x.shape[0]
    halo = jnp.where(t > 0, halo, 0.0)
    e = jnp.concatenate([halo, x], axis=0)
    s2 = e + pltpu.roll(e, 1, 0)
    s4 = s2 + pltpu.roll(s2, 2, 0)
    s8 = s4 + pltpu.roll(s4, 4, 0)
    s16 = s8 + pltpu.roll(s8, 8, 0)
    pos = (t * ts + 1 + lax.broadcasted_iota(jnp.int32, (ts, 1), 0)).astype(F32)
    outs = []
    for gi, (w, s) in enumerate(zip(POOL_WINDOWS, (s2, s4, s8, s16))):
        cols = slice(gi * POOL_GROUP, (gi + 1) * POOL_GROUP)
        mean = s[POOL_HALO:, cols] / jnp.minimum(pos, float(w))
        outs.append(_dot((mean - x[:, cols]).astype(BF16), w_ref[gi]))
    return jnp.concatenate(outs, axis=1) * scale


def _gla_streams(streams, ts):
    c = GLA_CHUNK
    nc = ts // c
    rowc = lax.broadcasted_iota(jnp.int32, (ts, 1), 0) % c
    causal = (lax.broadcasted_iota(jnp.int32, (c, c), 0)
              >= lax.broadcasted_iota(jnp.int32, (c, c), 1))
    qs, ks, vs, glrs, w2s, biases, sts = (list(a) for a in zip(*streams))
    logits = [_dot(glr, w2) + bias for glr, w2, bias in zip(glrs, w2s, biases)]
    bcs = [(jnp.minimum(l, 0.0) - jnp.log1p(jnp.exp(-jnp.abs(l)))) * (1.0 / GLA_TAU) for l in logits]
    step = 1
    while step < c:
        bcs = [bc + jnp.where(rowc >= step, pltpu.roll(bc, step, 0), 0.0) for bc in bcs]
        step *= 2
    bc3 = [bc.reshape(nc, c, GLA_DK) for bc in bcs]
    b_last = [b[:, c - 1:c, :] for b in bc3]
    b_mid = [b[:, c // 2 - 1:c // 2, :] for b in bc3]
    q3 = [q.astype(F32).reshape(nc, c, GLA_DK) * (GLA_DK ** -0.5) for q in qs]
    k3 = [k.astype(F32).reshape(nc, c, GLA_DK) for k in ks]
    v3 = [v.reshape(nc, c, GLA_DV) for v in vs]
    qe = [(q * jnp.exp(b - m)).astype(BF16) for q, b, m in zip(q3, bc3, b_mid)]
    ke = [(k * jnp.exp(m - b)).astype(BF16) for k, b, m in zip(k3, bc3, b_mid)]
    scores = [jnp.einsum('nid,njd->nij', a, b, preferred_element_type=F32) for a, b in zip(qe, ke)]
    scores = [jnp.where(causal[None], s, 0.0).astype(BF16) for s in scores]
    o_intra = [jnp.einsum('nij,njv->niv', s, v, preferred_element_type=F32) for s, v in zip(scores, v3)]
    qd = [(q * jnp.exp(b)).astype(BF16) for q, b in zip(q3, bc3)]
    kd = [(k * jnp.exp(bl - b)).astype(BF16) for k, b, bl in zip(k3, bc3, b_last)]
    dec_t = [jnp.transpose(jnp.exp(bl.reshape(nc, GLA_DK))) for bl in b_last]
    outs = [[] for _ in streams]
    for n in range(nc):
        for i in range(len(streams)):
            outs[i].append(o_intra[i][n] + _dot(qd[i][n], sts[i].astype(BF16)))
            kv = lax.dot_general(kd[i][n], v3[i][n], (((0,), (0,)), ((), ())), preferred_element_type=F32)
            sts[i] = dec_t[i][:, n:n + 1] * sts[i] + kv
    return [jnp.concatenate(o, axis=0).astype(BF16) for o in outs], sts


def _gla_body(q_ref, k_ref, v0_ref, v1_ref, glr_ref, w2_ref, b_ref, o_ref, st_ref, *, ts):
    @pl.when(pl.program_id(1) == 0)
    def _():
        st_ref[...] = jnp.zeros_like(st_ref)

    v_refs = (v0_ref, v1_ref)
    heads_per_block = v0_ref.shape[2] // GLA_DV
    streams, where = [], []
    for s in range(q_ref.shape[0]):
        glr = glr_ref[s].astype(BF16)
        for h in range(GLA_HEADS):
            v_ref = v_refs[h // heads_per_block]
            vs = slice((h % heads_per_block) * GLA_DV, (h % heads_per_block + 1) * GLA_DV)
            ks = slice(h * GLA_DK, (h + 1) * GLA_DK)
            streams.append((q_ref[s, :, ks], k_ref[s, :, ks], v_ref[s, :, vs], glr,
                            w2_ref[h], b_ref[h], st_ref[s, h]))
            where.append((s, h))
    outs, sts = _gla_streams(streams, ts)
    for (s, h), o, st in zip(where, outs, sts):
        st_ref[s, h] = st
        o_ref[s, :, h * GLA_DV:(h + 1) * GLA_DV] = o


def _gla(z, glr, w2, bias, layer, batch, seq, cols, *, ts=512, nseq=2):
    h = GLA_HEADS
    nt = seq // ts
    q0, k0, v0 = cols
    wb = h * GLA_DK
    z3 = z.reshape(batch, seq, z.shape[1])
    zspec = lambda c0: pl.BlockSpec((nseq, ts, wb), lambda b, t: (b, t, c0 // wb))
    wspec = lambda last2: pl.BlockSpec((None, h) + last2, lambda b, t: (layer, 0, 0, 0))
    out = pl.pallas_call(
        functools.partial(_gla_body, ts=ts),
        grid=(batch // nseq, nt),
        in_specs=[
            zspec(q0), zspec(k0), zspec(v0), zspec(v0 + wb),
            pl.BlockSpec((nseq, ts, LANES), lambda b, t: (b, t, 0)),
            wspec((LANES, GLA_DK)), wspec((1, GLA_DK)),
        ],
        out_specs=pl.BlockSpec((nseq, ts, h * GLA_DV), lambda b, t: (b, t, 0)),
        out_shape=jax.ShapeDtypeStruct((batch, seq, h * GLA_DV), BF16),
        scratch_shapes=[pltpu.VMEM((nseq, h, GLA_DK, GLA_DV), F32)],
        compiler_params=_cparams(("parallel", "arbitrary")),
        name="gla",
    )(z3, z3, z3, z3, glr.reshape(batch, seq, LANES), w2, bias)
    return out.reshape(batch * seq, h * GLA_DV)


SSM_COL_GROUPS = LANES // SSM_GROUP
SSM_BLOCK_GROUPS = 2
SSM_TPAIR = 2


def _ssm_body(x_ref, kc_ref, pc_ref, qc_ref, ar_ref, ai_ref, o_ref, hp_ref, kst_ref, v_ref, *,
              chunks_per_seq, n_state_blocks):
    u = pl.program_id(0)
    ncol, rows = x_ref.shape[0], x_ref.shape[1]
    blocks_per_col = n_state_blocks // ncol
    npairs = SSM_CHUNK // SSM_TPAIR
    ns = 2 * SSM_STATE
    bw = SSM_BLOCK_GROUPS * ns
    lane_group = lax.broadcasted_iota(jnp.int32, (1, LANES), 1) // SSM_GROUP

    def p_tile(n):
        pc = pc_ref[n // blocks_per_col]
        row_group = (lax.broadcasted_iota(jnp.int32, (pc.shape[0], 1), 0) // SSM_GROUP) % SSM_COL_GROUPS
        g0 = (n % blocks_per_col) * SSM_BLOCK_GROUPS
        return jnp.concatenate([jnp.where(row_group == g0 + gg, pc, 0.0)
                                for gg in range(SSM_BLOCK_GROUPS)], axis=1).astype(BF16)

    @pl.when(u == 0)
    def _():
        row_group = lax.broadcasted_iota(jnp.int32, (LANES, 1), 0) // SSM_GROUP

        def kbd(c, lag):
            kc = kc_ref[c, lag]
            tiled = jnp.broadcast_to(kc[None], (SSM_COL_GROUPS,) + kc.shape).reshape(LANES, LANES)
            return jnp.where(row_group == lane_group, tiled, 0.0).astype(BF16)

        for c in range(ncol):
            for r in range(SSM_CHUNK):
                left = kbd(c, SSM_CHUNK - 2 - r) if r <= SSM_CHUNK - 2 else jnp.zeros((LANES, LANES), BF16)
                kst_ref[c, r * LANES:(r + 1) * LANES, :] = jnp.concatenate(
                    [left, kbd(c, SSM_CHUNK - 1 - r)], axis=1)
        v_ref[...] = _dot(x_ref[0], p_tile(0))

    @pl.when(u < n_state_blocks)
    def _():
        rowc = lax.broadcasted_iota(jnp.int32, (rows, 1), 0) % chunks_per_seq
        hst = v_ref[...]
        nxt = jnp.minimum(u + 1, n_state_blocks - 1)
        v_next = _dot(x_ref[nxt // blocks_per_col], p_tile(nxt))
        k, si = 1, 0
        while k < chunks_per_seq:
            hs = jnp.where(rowc >= k, pltpu.roll(hst, k, 0), 0.0)
            sw = jnp.concatenate([pltpu.roll(hs[:, b0:b0 + ns], SSM_STATE, 1)
                                  for b0 in range(0, bw, ns)], axis=1)
            hst = hst + ar_ref[u, si:si + 1, :] * hs + ai_ref[u, si:si + 1, :] * sw
            k *= 2
            si += 1
        hp_ref[u] = jnp.where(rowc >= 1, pltpu.roll(hst, 1, 0), 0.0).astype(BF16)
        v_ref[...] = v_next

    @pl.when(u >= n_state_blocks)
    def _():
        ob = u - n_state_blocks
        c = ob // npairs
        v = ob % npairs
        grp = lax.broadcasted_iota(jnp.int32, (SSM_COL_GROUPS, 1, 1), 0)
        q_tile = jnp.concatenate(
            [jnp.where(grp == lane_group[None], qc_ref[c, SSM_TPAIR * v + tt][None], 0.0)
             .reshape(SSM_COL_GROUPS * ns, LANES) for tt in range(SSM_TPAIR)], axis=1).astype(BF16)
        yq = _dot(hp_ref[c * blocks_per_col], q_tile[0:bw, :])
        for kk in range(1, blocks_per_col):
            yq = yq + _dot(hp_ref[c * blocks_per_col + kk], q_tile[kk * bw:(kk + 1) * bw, :])
        for vs in range(npairs):
            kext = (vs + 1) * SSM_TPAIR * LANES

            @pl.when(v == vs)
            def _():
                start = (npairs - 1 - vs) * SSM_TPAIR * LANES
                intra = _dot(x_ref[c, :, 0:kext], kst_ref[c, start:start + kext, :])
                o_ref[...] = (yq + intra).astype(BF16)


def _ssm_core(xcat, kc, pc, qc, ar, ai, layer, chunks_per_seq):
    ncol, rows, xw = xcat.shape
    nb = ar.shape[1]
    bw = ar.shape[3]
    ow = SSM_TPAIR * LANES
    nout = ncol * SSM_CHUNK // SSM_TPAIR
    one = pl.Buffered(1)
    whole = lambda a: pl.BlockSpec((None,) + a.shape[1:], lambda u: (layer,) + (0,) * (a.ndim - 1),
                                   pipeline_mode=one)
    return pl.pallas_call(
        functools.partial(_ssm_body, chunks_per_seq=chunks_per_seq, n_state_blocks=nb),
        grid=(nb + nout,),
        in_specs=[
            pl.BlockSpec((ncol, rows, xw), lambda u: (0, 0, 0), pipeline_mode=one),
            whole(kc), whole(pc), whole(qc), whole(ar), whole(ai),
        ],
        out_specs=pl.BlockSpec((None, rows, ow), lambda u: (jnp.maximum(u - nb, 0), 0, 0)),
        out_shape=jax.ShapeDtypeStruct((nout, rows, ow), BF16),
        scratch_shapes=[
            pltpu.VMEM((nb, rows, bw), BF16),
            pltpu.VMEM((ncol, SSM_CHUNK * LANES, ow), BF16),
            pltpu.VMEM((rows, bw), F32),
        ],
        compiler_params=_cparams(("arbitrary",)),
        name="ssm_core",
    )(xcat, kc, pc, qc, ar, ai)


def _ssm_operators(a_re, a_im, log_dt, b_re, b_im, c_re, c_im, chunks_per_seq):
    t = SSM_CHUNK
    g, p = a_re.shape
    hg = b_re.shape[-1]
    cgc = SSM_COL_GROUPS
    ncol = g // cgc
    bgc = SSM_BLOCK_GROUPS
    dt = jnp.exp(log_dt)[:, None]
    lam_re, lam_im = dt * a_re, dt * a_im

    def powers(n):
        n = jnp.asarray(n, F32)[:, None, None]
        mag = jnp.exp(n * lam_re)
        return mag * jnp.cos(n * lam_im), mag * jnp.sin(n * lam_im)

    ab_re, ab_im = powers([1.0])
    ab_re, ab_im = ab_re[0], ab_im[0]
    den = a_re * a_re + a_im * a_im
    f_re = ((ab_re - 1.0) * a_re + ab_im * a_im) / den
    f_im = (ab_im * a_re - (ab_re - 1.0) * a_im) / den
    bb_re = f_re[..., None] * b_re - f_im[..., None] * b_im
    bb_im = f_re[..., None] * b_im + f_im[..., None] * b_re
    pw_re, pw_im = powers(np.arange(t + 1))
    cp_re = c_re[None] * pw_re[:, :, None, :] - c_im[None] * pw_im[:, :, None, :]
    cp_im = c_re[None] * pw_im[:, :, None, :] + c_im[None] * pw_re[:, :, None, :]
    ktau = (jnp.einsum('tghp,gpk->tkgh', cp_re[:t], bb_re)
            - jnp.einsum('tghp,gpk->tkgh', cp_im[:t], bb_im))
    kc = ktau.reshape(t, hg, ncol, cgc * hg).transpose(2, 0, 1, 3)
    bt_re, bt_im = bb_re.transpose(0, 2, 1), bb_im.transpose(0, 2, 1)
    rp_re = pw_re[t - 1 - np.arange(t)][:, :, None, :]
    rp_im = pw_im[t - 1 - np.arange(t)][:, :, None, :]
    pin = jnp.concatenate([rp_re * bt_re[None] - rp_im * bt_im[None],
                           rp_re * bt_im[None] + rp_im * bt_re[None]], axis=3)
    pc = pin.reshape(t, ncol, cgc, hg, 2 * p).transpose(1, 0, 2, 3, 4).reshape(ncol, t * cgc * hg, 2 * p)
    ct_re, ct_im = c_re.transpose(2, 0, 1)[None], c_im.transpose(2, 0, 1)[None]
    pt_re = pw_re[1:].transpose(0, 2, 1)[..., None]
    pt_im = pw_im[1:].transpose(0, 2, 1)[..., None]
    qg = jnp.concatenate([ct_re * pt_re - ct_im * pt_im,
                          -(ct_re * pt_im + ct_im * pt_re)], axis=1)
    qc = qg.reshape(t, 2 * p, ncol, cgc * hg).transpose(2, 0, 1, 3)
    nblk = g // bgc
    nsteps = int(np.log2(chunks_per_seq))
    sr, si = powers(t * 2.0 ** np.arange(nsteps))
    blk = lambda a: a.reshape(nsteps, nblk, bgc * 2 * p).transpose(1, 0, 2)
    ar = blk(jnp.concatenate([sr, sr], axis=2))
    ai = blk(jnp.concatenate([-si, si], axis=2))
    return kc, pc, qc, ar, ai


def _merge_body(x_ref, pin_ref, halo_ref, og_ref, r0_ref, r1_ref, yc_ref, sin_ref, g0_ref, g1_ref, g2_ref,
                pw_ref, ps_ref, gain_ref, d_ref, wglu_ref, wbp_ref, wbg0_ref, wbg1_ref, wbs_ref, wo_ref,
                o_ref, ys_ref, *, tiles_per_seq):
    rows = yc_ref.shape[1]
    npairs = SSM_CHUNK // SSM_TPAIR
    for ob in range(yc_ref.shape[0]):
        c, v = divmod(ob, npairs)
        plane = yc_ref[ob].astype(F32)
        for tt in range(SSM_TPAIR):
            ys_ref[c, pl.ds(SSM_TPAIR * v + tt, rows, stride=SSM_CHUNK), :] = plane[:, tt * LANES:(tt + 1) * LANES]
    t = pl.program_id(0) % tiles_per_seq
    yp = _pool_mix(pin_ref[...].astype(F32), halo_ref[...].astype(F32), t, pw_ref, ps_ref[...])
    m = g0_ref[...].astype(F32) * _dot(yp.astype(BF16), wbp_ref[...])
    r_refs = (r0_ref, r1_ref)
    heads_per_block = r0_ref.shape[1] // GLA_DV
    yg_heads = []
    for h in range(GLA_HEADS):
        hs = slice(h * GLA_DV, (h + 1) * GLA_DV)
        rs = slice((h % heads_per_block) * GLA_DV, (h % heads_per_block + 1) * GLA_DV)
        o = og_ref[:, hs].astype(F32)
        o = o * lax.rsqrt(jnp.mean(o * o, axis=-1, keepdims=True) + EPS) * gain_ref[:, hs]
        r = r_refs[h // heads_per_block][:, rs].astype(F32)
        yg_heads.append((o * (r * _sigmoid(r))).astype(BF16))
    yg = jnp.concatenate(yg_heads, axis=1)
    half = wbg0_ref.shape[0]
    gla_proj = _dot(yg[:, :half], wbg0_ref[...]) + _dot(yg[:, half:], wbg1_ref[...])
    m = m + g1_ref[...].astype(F32) * gla_proj
    ys = jnp.concatenate([ys_ref[c] for c in range(ys_ref.shape[0])], axis=1)
    y = ys + d_ref[...] * sin_ref[...].astype(F32)
    y = jax.nn.gelu(y)
    y = y * _sigmoid(_dot(y.astype(BF16), wglu_ref[...]))
    m = m + g2_ref[...].astype(F32) * _dot(y.astype(BF16), wbs_ref[...])
    o_ref[...] = x_ref[...] + _dot(m.astype(BF16), wo_ref[...])


def _merge(x, o_gla, ycat, z, r_col0, ssm_col0, gate_col0, seq, pool_w, pool_scale, gla_gain,
           d_skip, w_glu, wb, w_out, layer, *, tm=256):
    m, d = x.shape
    wp, wg, ws = pool_scale.shape[2], o_gla.shape[1], d_skip.shape[2]
    br = BRANCH_ROWS
    assert wp == br and ws == br and wg == 2 * br and seq % tm == 0 and tm % POOL_HALO == 0
    gb = gate_col0 // d
    hb = tm // POOL_HALO
    one = pl.Buffered(1)
    wspec = lambda shape, r: pl.BlockSpec((None,) + shape, lambda i: (layer, r, 0), pipeline_mode=one)
    return pl.pallas_call(
        functools.partial(_merge_body, tiles_per_seq=seq // tm),
        grid=(m // tm,),
        in_specs=[
            pl.BlockSpec((tm, d), lambda i: (i, 0)),
            pl.BlockSpec((tm, wp), lambda i: (i, 0)),
            pl.BlockSpec((POOL_HALO, wp), lambda i: (jnp.maximum(i * hb - 1, 0), 0)),
            pl.BlockSpec((tm, wg), lambda i: (i, 0)),
            pl.BlockSpec((tm, br), lambda i: (i, r_col0 // br)),
            pl.BlockSpec((tm, br), lambda i: (i, r_col0 // br + 1)),
            pl.BlockSpec((ycat.shape[0], tm // SSM_CHUNK, ycat.shape[2]), lambda i: (0, i, 0)),
            pl.BlockSpec((tm, ws), lambda i: (i, ssm_col0 // ws)),
            pl.BlockSpec((tm, d), lambda i: (i, gb)),
            pl.BlockSpec((tm, d), lambda i: (i, gb + 1)),
            pl.BlockSpec((tm, d), lambda i: (i, gb + 2)),
            pl.BlockSpec((None, len(POOL_WINDOWS), POOL_GROUP, POOL_GROUP), lambda i: (layer, 0, 0, 0),
                         pipeline_mode=one),
            wspec((1, wp), 0),
            wspec((1, wg), 0),
            wspec((1, ws), 0),
            wspec((ws, ws), 0),
            wspec((br, d), 0),
            wspec((br, d), 1),
            wspec((br, d), 2),
            wspec((br, d), 3),
            wspec((d, d), 0),
        ],
        out_specs=pl.BlockSpec((tm, d), lambda i: (i, 0)),
        out_shape=jax.ShapeDtypeStruct((m, d), F32),
        scratch_shapes=[pltpu.VMEM((ws // LANES, tm, LANES), F32)],
        compiler_params=_cparams(("parallel",)),
        name="merge",
    )(x, z, z, o_gla, z, z, ycat, z, z, z, z, pool_w, pool_scale, gla_gain, d_skip, w_glu,
      wb, wb, wb, wb, w_out)


def kernel(x, ffn1_norm, ffn1_w_gate, ffn1_w_up, ffn1_w_down, mix_norm, w_in, pool_w, pool_scale, gla_w_gate2, gla_gate_bias, gla_norm, ssm_a_re, ssm_a_im, ssm_log_dt, ssm_b_re, ssm_b_im, ssm_c_re, ssm_c_im, ssm_d, ssm_w_glu, w_branch, w_out, ffn2_norm, ffn2_w_gate, ffn2_w_up, ffn2_w_down, final_norm):
    batch, seq, d = x.shape
    depth = w_in.shape[0]
    m = batch * seq
    pool_width = pool_scale.shape[1]
    qk_width = GLA_HEADS * GLA_DK
    v_width = GLA_HEADS * GLA_DV
    ssm_width = ssm_d.shape[1]
    n_groups = ssm_width // SSM_GROUP
    q0 = pool_width
    k0 = q0 + qk_width
    v0 = k0 + qk_width
    r0 = v0 + v_width
    glr0 = r0 + v_width
    ssm_src0 = glr0 + GLA_RANK
    ssm0 = glr0
    gate0 = ssm0 + ssm_width
    chunks_per_seq = seq // SSM_CHUNK

    bf = lambda a: a.astype(BF16)
    row3 = lambda a: a.reshape(depth, 1, a.shape[-1])
    w_in_t = jnp.swapaxes(w_in, 1, 2)
    ffn1_w = (ffn1_w_gate, ffn1_w_up, ffn1_w_down)
    ffn2_w = (ffn2_w_gate, ffn2_w_up, ffn2_w_down)
    f1 = [bf(w[0]) for w in ffn1_w]
    wbr, wo, wglu, pw = bf(w_branch), bf(w_out), bf(ssm_w_glu), bf(pool_w)
    n1, nm, n2 = row3(ffn1_norm), row3(mix_norm), row3(ffn2_norm)
    pscale, dskip = row3(pool_scale), row3(ssm_d)
    w2 = gla_w_gate2.reshape(depth, GLA_RANK, GLA_HEADS, GLA_DK).transpose(0, 2, 1, 3)
    w2 = bf(jnp.pad(w2, ((0, 0), (0, 0), (0, LANES - GLA_RANK), (0, 0))))
    gbias = gla_gate_bias.reshape(depth, GLA_HEADS, 1, GLA_DK)
    ggain = row3(gla_norm)
    ssm_ops = jax.vmap(functools.partial(_ssm_operators, chunks_per_seq=chunks_per_seq))(
        ssm_a_re, ssm_a_im, ssm_log_dt, ssm_b_re, ssm_b_im, ssm_c_re, ssm_c_im)

    xf = x.reshape(m, d)
    for l in range(depth):
        xf, *side = _ffn(xf, n1, l, *f1, cast=(l,) + ffn2_w, repack=(l, w_in_t, glr0, GLA_RANK))
        f2, (w_main, w_glr) = side[:3], side[3:]
        z, glr, xcat = _inproj(xf, nm, w_main, w_glr, l, ssm0, ssm_width, gate0)
        o_gla = _gla(z, glr, w2, gbias, l, batch, seq, (q0, k0, v0))
        ycat = _ssm_core(xcat, *ssm_ops, l, chunks_per_seq)
        xf = _merge(xf, o_gla, ycat, z, r0, ssm0, gate0, seq, pw, pscale, ggain, dskip, wglu, wbr, wo, l)
        last = l == depth - 1
        xf, *f1 = _ffn(xf, n2, l, *f2, cast=None if last else (l + 1,) + ffn1_w,
                       final_g=final_norm if last else None)
    return xf.reshape(batch, seq, d)
```

```python
import functools

import jax
import jax.numpy as jnp
import numpy as np
from jax import lax
from jax.experimental import pallas as pl
from jax.experimental.pallas import tpu as pltpu

F32 = jnp.float32
BF16 = jnp.bfloat16

EPS = 1e-6
POOL_WINDOWS = (2, 4, 8, 16)
POOL_GROUP = 128
POOL_HALO = 16
GLA_HEADS = 4
GLA_DK = 128
GLA_DV = 256
GLA_RANK = 16
GLA_TAU = 16.0
GLA_CHUNK = 64
SSM_GROUP = 16
SSM_STATE = 64
SSM_CHUNK = 16
LANES = 128
FFN_DOWN_CHUNK = 512
FFN_FIRST_ROWS = 256
FFN_EPILOGUE_ROWS = 256
REPACK_ROWS = 64
INPROJ_FIRST_ROWS = 256
BRANCH_ROWS = 512

VMEM_LIMIT = 60 * 1024 * 1024


def _cparams(sem):
    return pltpu.CompilerParams(dimension_semantics=sem, vmem_limit_bytes=VMEM_LIMIT)


def _rms(xf, g):
    return xf * lax.rsqrt(jnp.mean(xf * xf, axis=-1, keepdims=True) + EPS) * g


def _dot(a, b):
    return jnp.dot(a, b, preferred_element_type=F32)


def _sigmoid(a):
    return 0.5 * jnp.tanh(0.5 * a) + 0.5


def _dot_nt(a, b_t):
    return lax.dot_general(a, b_t, (((1,), (1,)), ((), ())), preferred_element_type=F32)


def _ffn_body(x_ref, g_ref, wg_ref, wu_ref, wd_ref, *rest, final, n_cast, repack):
    rest = list(rest)
    cast_in = [rest.pop(0) for _ in range(n_cast)]
    rp_in = [rest.pop(0) for _ in range(2 if repack else 0)]
    fg_ref = rest.pop(0) if final else None
    o_ref = rest.pop(0)
    cast_out = [rest.pop(0) for _ in range(n_cast)]
    rp_out = [rest.pop(0) for _ in range(2 if repack else 0)]
    xn_ref, = rest
    j = pl.program_id(1)

    for src, dst in zip(cast_in, cast_out):
        dst[...] = src[...].astype(BF16)
    if repack:
        (rows_ref, drop_ref), (w_ref, wl_ref) = rp_in, rp_out
        w_ref[...] = rows_ref[0].astype(BF16)
        pad = jnp.zeros((wl_ref.shape[0] - drop_ref.shape[1], drop_ref.shape[2]), F32)
        wl_ref[...] = jnp.concatenate([drop_ref[0], pad], axis=0).astype(BF16)

    def swiglu_step(rs, xn, first):
        a = _dot(xn, wg_ref[...])
        b = _dot(xn, wu_ref[...])
        h = (a * _sigmoid(a) * b).astype(BF16)
        for c0 in range(0, o_ref.shape[1], FFN_DOWN_CHUNK):
            cs = slice(c0, c0 + FFN_DOWN_CHUNK)
            if first:
                o_ref[rs, cs] = _dot(h, wd_ref[:, cs])
            else:
                o_ref[rs, cs] += _dot(h, wd_ref[:, cs])

    @pl.when(j == 0)
    def _():
        for r0 in range(0, o_ref.shape[0], FFN_FIRST_ROWS):
            rs = slice(r0, r0 + FFN_FIRST_ROWS)
            xn = _rms(x_ref[rs, :], g_ref[...]).astype(BF16)
            xn_ref[rs, :] = xn
            swiglu_step(rs, xn, True)

    @pl.when(j > 0)
    def _():
        swiglu_step(slice(None), xn_ref[...], False)

    @pl.when(j == pl.num_programs(1) - 1)
    def _():
        def epilogue(r, carry):
            rs = pl.ds(pl.multiple_of(r * FFN_EPILOGUE_ROWS, FFN_EPILOGUE_ROWS), FFN_EPILOGUE_ROWS)
            y = x_ref[rs, :] + 0.5 * o_ref[rs, :]
            if final:
                y = _rms(y, fg_ref[...])
            o_ref[rs, :] = y
            return carry
        lax.fori_loop(0, o_ref.shape[0] // FFN_EPILOGUE_ROWS, epilogue, 0)


def _ffn(x, g, layer, wg, wu, wd, cast=None, repack=None, final_g=None, *, tm=1024, tf=512):
    m, d = x.shape
    f = wg.shape[1]
    ni, nj = m // tm, f // tf
    final = final_g is not None
    in_specs = [
        pl.BlockSpec((tm, d), lambda i, j: (i, 0)),
        pl.BlockSpec((None, 1, d), lambda i, j: (layer, 0, 0)),
        pl.BlockSpec((d, tf), lambda i, j: (0, j)),
        pl.BlockSpec((d, tf), lambda i, j: (0, j)),
        pl.BlockSpec((tf, d), lambda i, j: (j, 0)),
    ]
    args = [x, g, wg, wu, wd]
    out_specs = [pl.BlockSpec((tm, d), lambda i, j: (i, 0))]
    out_shape = [jax.ShapeDtypeStruct((m, d), F32)]
    n_cast = 0
    if cast is not None:
        cl, cg, cu, cd = cast
        n_cast = 3
        rows_up = d // ni
        rows_dn = f // (ni * nj)
        assert d % ni == 0 and f % (ni * nj) == 0 and rows_up % 16 == 0 and rows_dn % 16 == 0
        for w in (cg, cu):
            in_specs.append(pl.BlockSpec((None, rows_up, tf), lambda i, j: (cl, i, j)))
            out_specs.append(pl.BlockSpec((rows_up, tf), lambda i, j: (i, j)))
            out_shape.append(jax.ShapeDtypeStruct((d, f), BF16))
            args.append(w)
        in_specs.append(pl.BlockSpec((None, rows_dn, d), lambda i, j: (cl, i * nj + j, 0)))
        out_specs.append(pl.BlockSpec((rows_dn, d), lambda i, j: (i * nj + j, 0)))
        out_shape.append(jax.ShapeDtypeStruct((f, d), BF16))
        args.append(cd)
    if repack is not None:
        rl, w_in_t, drop0, drop_width = repack
        n = w_in_t.shape[1] - drop_width
        rb = REPACK_ROWS
        nblk = n // rb
        assert n % rb == 0 and nblk <= ni * nj and drop0 % rb == 0 and drop_width % 8 == 0
        blk = lambda i, j: jnp.minimum(i * nj + j, nblk - 1)
        src_row = lambda b: (b * (rb // 8) + jnp.where(b >= drop0 // rb, drop_width // 8, 0)) * 8
        in_specs.append(pl.BlockSpec((pl.Element(1), pl.Element(rb), pl.Element(d)),
                                     lambda i, j: (rl, src_row(blk(i, j)), 0)))
        in_specs.append(pl.BlockSpec((pl.Element(1), pl.Element(drop_width), pl.Element(d)),
                                     lambda i, j: (rl, drop0, 0)))
        out_specs.append(pl.BlockSpec((rb, d), lambda i, j: (blk(i, j), 0)))
        out_specs.append(pl.BlockSpec((LANES, d), lambda i, j: (0, 0)))
        out_shape.append(jax.ShapeDtypeStruct((n, d), BF16))
        out_shape.append(jax.ShapeDtypeStruct((LANES, d), BF16))
        args += [w_in_t, w_in_t]
    if final:
        in_specs.append(pl.BlockSpec((1, d), lambda i, j: (0, 0)))
        args.append(final_g.reshape(1, d))
    return pl.pallas_call(
        functools.partial(_ffn_body, final=final, n_cast=n_cast, repack=repack is not None),
        grid=(ni, nj),
        in_specs=in_specs,
        out_specs=out_specs,
        out_shape=out_shape,
        scratch_shapes=[pltpu.VMEM((tm, d), BF16)],
        compiler_params=_cparams(("arbitrary", "arbitrary")),
        name="ffn_final" if final else "ffn",
    )(*args)


def _inproj_body(x_ref, g_ref, w_ref, wl_ref, z_ref, glr_ref, xc_ref, un_ref, ph_ref, *,
                 gate_block0, ssm_block, ssm_off):
    j = pl.program_id(1)

    @pl.when(j == 0)
    def _():
        for r0 in range(0, z_ref.shape[0], INPROJ_FIRST_ROWS):
            rs = slice(r0, r0 + INPROJ_FIRST_ROWS)
            un = _rms(x_ref[rs, :], g_ref[...]).astype(BF16)
            un_ref[rs, :] = un
            glr_ref[rs, :] = _dot_nt(un, wl_ref[...])
            z_ref[rs, :] = _dot_nt(un, w_ref[...]).astype(BF16)

    @pl.when(j > 0)
    def _():
        acc = _dot_nt(un_ref[...], w_ref[...])
        z_ref[...] = jnp.where(j >= gate_block0, _sigmoid(acc), acc).astype(BF16)

        @pl.when(j == ssm_block)
        def _():
            ncol = ph_ref.shape[0]
            for c in range(ncol):
                ph_ref[c] = acc[:, ssm_off + c * LANES:ssm_off + (c + 1) * LANES]
            rows = ph_ref.shape[1] // SSM_CHUNK
            for s in range(SSM_CHUNK):
                for c in range(ncol):
                    piece = ph_ref[c, pl.ds(s, rows, stride=SSM_CHUNK), :].astype(BF16)
                    xc_ref[c, :, s * LANES:(s + 1) * LANES] = piece


def _inproj(x, g, w_main, w_glr, layer, ssm_col0, ssm_width, gate_col0, *, tm=1024, tn=1024):
    m, d = x.shape
    n = w_main.shape[0]
    assert gate_col0 >= tn and ssm_col0 >= tn
    body = functools.partial(_inproj_body, gate_block0=gate_col0 // tn,
                             ssm_block=ssm_col0 // tn, ssm_off=ssm_col0 % tn)
    return pl.pallas_call(
        body,
        grid=(m // tm, n // tn),
        in_specs=[
            pl.BlockSpec((tm, d), lambda i, j: (i, 0)),
            pl.BlockSpec((None, 1, d), lambda i, j: (layer, 0, 0)),
            pl.BlockSpec((tn, d), lambda i, j: (j, 0)),
            pl.BlockSpec((LANES, d), lambda i, j: (0, 0)),
        ],
        out_specs=[
            pl.BlockSpec((tm, tn), lambda i, j: (i, j)),
            pl.BlockSpec((tm, LANES), lambda i, j: (i, 0)),
            pl.BlockSpec((ssm_width // LANES, tm // SSM_CHUNK, SSM_CHUNK * LANES), lambda i, j: (0, i, 0)),
        ],
        out_shape=[
            jax.ShapeDtypeStruct((m, n), BF16),
            jax.ShapeDtypeStruct((m, LANES), F32),
            jax.ShapeDtypeStruct((ssm_width // LANES, m // SSM_CHUNK, SSM_CHUNK * LANES), BF16),
        ],
        scratch_shapes=[pltpu.VMEM((tm, d), BF16), pltpu.VMEM((ssm_width // LANES, tm, LANES), F32)],
        compiler_params=_cparams(("parallel", "arbitrary")),
        name="inproj",
    )(x, g, w_main, w_glr)


def _pool_mix(x, halo, t, w_ref, scale):
    ts = x.shape[0]
    halo = jnp.where(t > 0, halo, 0.0)
    e = jnp.concatenate([halo, x], axis=0)
    s2 = e + pltpu.roll(e, 1, 0)
    s4 = s2 + pltpu.roll(s2, 2, 0)
    s8 = s4 + pltpu.roll(s4, 4, 0)
    s16 = s8 + pltpu.roll(s8, 8, 0)
    pos = (t * ts + 1 + lax.broadcasted_iota(jnp.int32, (ts, 1), 0)).astype(F32)
    outs = []
    for gi, (w, s) in enumerate(zip(POOL_WINDOWS, (s2, s4, s8, s16))):
        cols = slice(gi * POOL_GROUP, (gi + 1) * POOL_GROUP)
        mean = s[POOL_HALO:, cols] / jnp.minimum(pos, float(w))
        outs.append(_dot((mean - x[:, cols]).astype(BF16), w_ref[gi]))
    return jnp.concatenate(outs, axis=1) * scale


def _gla_streams(streams, ts):
    c = GLA_CHUNK
    nc = ts // c
    rowc = lax.broadcasted_iota(jnp.int32, (ts, 1), 0) % c
    causal = (lax.broadcasted_iota(jnp.int32, (c, c), 0)
              >= lax.broadcasted_iota(jnp.int32, (c, c), 1))
    qs, ks, vs, glrs, w2s, biases, sts = (list(a) for a in zip(*streams))
    logits = [_dot(glr, w2) + bias for glr, w2, bias in zip(glrs, w2s, biases)]
    bcs = [(jnp.minimum(l, 0.0) - jnp.log1p(jnp.exp(-jnp.abs(l)))) * (1.0 / GLA_TAU) for l in logits]
    step = 1
    while step < c:
        bcs = [bc + jnp.where(rowc >= step, pltpu.roll(bc, step, 0), 0.0) for bc in bcs]
        step *= 2
    bc3 = [bc.reshape(nc, c, GLA_DK) for bc in bcs]
    b_last = [b[:, c - 1:c, :] for b in bc3]
    b_mid = [b[:, c // 2 - 1:c // 2, :] for b in bc3]
    q3 = [q.astype(F32).reshape(nc, c, GLA_DK) * (GLA_DK ** -0.5) for q in qs]
    k3 = [k.astype(F32).reshape(nc, c, GLA_DK) for k in ks]
    v3 = [v.reshape(nc, c, GLA_DV) for v in vs]
    qe = [(q * jnp.exp(b - m)).astype(BF16) for q, b, m in zip(q3, bc3, b_mid)]
    ke = [(k * jnp.exp(m - b)).astype(BF16) for k, b, m in zip(k3, bc3, b_mid)]
    scores = [jnp.einsum('nid,njd->nij', a, b, preferred_element_type=F32) for a, b in zip(qe, ke)]
    scores = [jnp.where(causal[None], s, 0.0).astype(BF16) for s in scores]
    o_intra = [jnp.einsum('nij,njv->niv', s, v, preferred_element_type=F32) for s, v in zip(scores, v3)]
    qd = [(q * jnp.exp(b)).astype(BF16) for q, b in zip(q3, bc3)]
    kd = [(k * jnp.exp(bl - b)).astype(BF16) for k, b, bl in zip(k3, bc3, b_last)]
    dec_t = [jnp.transpose(jnp.exp(bl.reshape(nc, GLA_DK))) for bl in b_last]
    outs = [[] for _ in streams]
    for n in range(nc):
        for i in range(len(streams)):
            outs[i].append(o_intra[i][n] + _dot(qd[i][n], sts[i].astype(BF16)))
            kv = lax.dot_general(kd[i][n], v3[i][n], (((0,), (0,)), ((), ())), preferred_element_type=F32)
            sts[i] = dec_t[i][:, n:n + 1] * sts[i] + kv
    return [jnp.concatenate(o, axis=0).astype(BF16) for o in outs], sts


def _gla_body(q_ref, k_ref, v0_ref, v1_ref, glr_ref, w2_ref, b_ref, o_ref, st_ref, *, ts):
    @pl.when(pl.program_id(1) == 0)
    def _():
        st_ref[...] = jnp.zeros_like(st_ref)

    v_refs = (v0_ref, v1_ref)
    heads_per_block = v0_ref.shape[2] // GLA_DV
    streams, where = [], []
    for s in range(q_ref.shape[0]):
        glr = glr_ref[s].astype(BF16)
        for h in range(GLA_HEADS):
            v_ref = v_refs[h // heads_per_block]
            vs = slice((h % heads_per_block) * GLA_DV, (h % heads_per_block + 1) * GLA_DV)
            ks = slice(h * GLA_DK, (h + 1) * GLA_DK)
            streams.append((q_ref[s, :, ks], k_ref[s, :, ks], v_ref[s, :, vs], glr,
                            w2_ref[h], b_ref[h], st_ref[s, h]))
            where.append((s, h))
    outs, sts = _gla_streams(streams, ts)
    for (s, h), o, st in zip(where, outs, sts):
        st_ref[s, h] = st
        o_ref[s, :, h * GLA_DV:(h + 1) * GLA_DV] = o


def _gla(z, glr, w2, bias, layer, batch, seq, cols, *, ts=512, nseq=2):
    h = GLA_HEADS
    nt = seq // ts
    q0, k0, v0 = cols
    wb = h * GLA_DK
    z3 = z.reshape(batch, seq, z.shape[1])
    zspec = lambda c0: pl.BlockSpec((nseq, ts, wb), lambda b, t: (b, t, c0 // wb))
    wspec = lambda last2: pl.BlockSpec((None, h) + last2, lambda b, t: (layer, 0, 0, 0))
    out = pl.pallas_call(
        functools.partial(_gla_body, ts=ts),
        grid=(batch // nseq, nt),
        in_specs=[
            zspec(q0), zspec(k0), zspec(v0), zspec(v0 + wb),
            pl.BlockSpec((nseq, ts, LANES), lambda b, t: (b, t, 0)),
            wspec((LANES, GLA_DK)), wspec((1, GLA_DK)),
        ],
        out_specs=pl.BlockSpec((nseq, ts, h * GLA_DV), lambda b, t: (b, t, 0)),
        out_shape=jax.ShapeDtypeStruct((batch, seq, h * GLA_DV), BF16),
        scratch_shapes=[pltpu.VMEM((nseq, h, GLA_DK, GLA_DV), F32)],
        compiler_params=_cparams(("parallel", "arbitrary")),
        name="gla",
    )(z3, z3, z3, z3, glr.reshape(batch, seq, LANES), w2, bias)
    return out.reshape(batch * seq, h * GLA_DV)


SSM_COL_GROUPS = LANES // SSM_GROUP
SSM_BLOCK_GROUPS = 2
SSM_TPAIR = 2


def _ssm_body(x_ref, kc_ref, pc_ref, qc_ref, ar_ref, ai_ref, o_ref, hp_ref, kst_ref, v_ref, *,
              chunks_per_seq, n_state_blocks):
    u = pl.program_id(0)
    ncol, rows = x_ref.shape[0], x_ref.shape[1]
    blocks_per_col = n_state_blocks // ncol
    npairs = SSM_CHUNK // SSM_TPAIR
    ns = 2 * SSM_STATE
    bw = SSM_BLOCK_GROUPS * ns
    lane_group = lax.broadcasted_iota(jnp.int32, (1, LANES), 1) // SSM_GROUP

    def p_tile(n):
        pc = pc_ref[n // blocks_per_col]
        row_group = (lax.broadcasted_iota(jnp.int32, (pc.shape[0], 1), 0) // SSM_GROUP) % SSM_COL_GROUPS
        g0 = (n % blocks_per_col) * SSM_BLOCK_GROUPS
        return jnp.concatenate([jnp.where(row_group == g0 + gg, pc, 0.0)
                                for gg in range(SSM_BLOCK_GROUPS)], axis=1).astype(BF16)

    @pl.when(u == 0)
    def _():
        row_group = lax.broadcasted_iota(jnp.int32, (LANES, 1), 0) // SSM_GROUP

        def kbd(c, lag):
            kc = kc_ref[c, lag]
            tiled = jnp.broadcast_to(kc[None], (SSM_COL_GROUPS,) + kc.shape).reshape(LANES, LANES)
            return jnp.where(row_group == lane_group, tiled, 0.0).astype(BF16)

        for c in range(ncol):
            for r in range(SSM_CHUNK):
                left = kbd(c, SSM_CHUNK - 2 - r) if r <= SSM_CHUNK - 2 else jnp.zeros((LANES, LANES), BF16)
                kst_ref[c, r * LANES:(r + 1) * LANES, :] = jnp.concatenate(
                    [left, kbd(c, SSM_CHUNK - 1 - r)], axis=1)
        v_ref[...] = _dot(x_ref[0], p_tile(0))

    @pl.when(u < n_state_blocks)
    def _():
        rowc = lax.broadcasted_iota(jnp.int32, (rows, 1), 0) % chunks_per_seq
        hst = v_ref[...]
        nxt = jnp.minimum(u + 1, n_state_blocks - 1)
        v_next = _dot(x_ref[nxt // blocks_per_col], p_tile(nxt))
        k, si = 1, 0
        while k < chunks_per_seq:
            hs = jnp.where(rowc >= k, pltpu.roll(hst, k, 0), 0.0)
            sw = jnp.concatenate([pltpu.roll(hs[:, b0:b0 + ns], SSM_STATE, 1)
                                  for b0 in range(0, bw, ns)], axis=1)
            hst = hst + ar_ref[u, si:si + 1, :] * hs + ai_ref[u, si:si + 1, :] * sw
            k *= 2
            si += 1
        hp_ref[u] = jnp.where(rowc >= 1, pltpu.roll(hst, 1, 0), 0.0).astype(BF16)
        v_ref[...] = v_next

    @pl.when(u >= n_state_blocks)
    def _():
        ob = u - n_state_blocks
        c = ob // npairs
        v = ob % npairs
        grp = lax.broadcasted_iota(jnp.int32, (SSM_COL_GROUPS, 1, 1), 0)
        q_tile = jnp.concatenate(
            [jnp.where(grp == lane_group[None], qc_ref[c, SSM_TPAIR * v + tt][None], 0.0)
             .reshape(SSM_COL_GROUPS * ns, LANES) for tt in range(SSM_TPAIR)], axis=1).astype(BF16)
        yq = _dot(hp_ref[c * blocks_per_col], q_tile[0:bw, :])
        for kk in range(1, blocks_per_col):
            yq = yq + _dot(hp_ref[c * blocks_per_col + kk], q_tile[kk * bw:(kk + 1) * bw, :])
        for vs in range(npairs):
            kext = (vs + 1) * SSM_TPAIR * LANES

            @pl.when(v == vs)
            def _():
                start = (npairs - 1 - vs) * SSM_TPAIR * LANES
                intra = _dot(x_ref[c, :, 0:kext], kst_ref[c, start:start + kext, :])
                o_ref[...] = (yq + intra).astype(BF16)


def _ssm_core(xcat, kc, pc, qc, ar, ai, layer, chunks_per_seq):
    ncol, rows, xw = xcat.shape
    nb = ar.shape[1]
    bw = ar.shape[3]
    ow = SSM_TPAIR * LANES
    nout = ncol * SSM_CHUNK // SSM_TPAIR
    one = pl.Buffered(1)
    whole = lambda a: pl.BlockSpec((None,) + a.shape[1:], lambda u: (layer,) + (0,) * (a.ndim - 1),
                                   pipeline_mode=one)
    return pl.pallas_call(
        functools.partial(_ssm_body, chunks_per_seq=chunks_per_seq, n_state_blocks=nb),
        grid=(nb + nout,),
        in_specs=[
            pl.BlockSpec((ncol, rows, xw), lambda u: (0, 0, 0), pipeline_mode=one),
            whole(kc), whole(pc), whole(qc), whole(ar), whole(ai),
        ],
        out_specs=pl.BlockSpec((None, rows, ow), lambda u: (jnp.maximum(u - nb, 0), 0, 0)),
        out_shape=jax.ShapeDtypeStruct((nout, rows, ow), BF16),
        scratch_shapes=[
            pltpu.VMEM((nb, rows, bw), BF16),
            pltpu.VMEM((ncol, SSM_CHUNK * LANES, ow), BF16),
            pltpu.VMEM((rows, bw), F32),
        ],
        compiler_params=_cparams(("arbitrary",)),
        name="ssm_core",
    )(xcat, kc, pc, qc, ar, ai)


def _ssm_operators(a_re, a_im, log_dt, b_re, b_im, c_re, c_im, chunks_per_seq):
    t = SSM_CHUNK
    g, p = a_re.shape
    hg = b_re.shape[-1]
    cgc = SSM_COL_GROUPS
    ncol = g // cgc
    bgc = SSM_BLOCK_GROUPS
    dt = jnp.exp(log_dt)[:, None]
    lam_re, lam_im = dt * a_re, dt * a_im

    def powers(n):
        n = jnp.asarray(n, F32)[:, None, None]
        mag = jnp.exp(n * lam_re)
        return mag * jnp.cos(n * lam_im), mag * jnp.sin(n * lam_im)

    ab_re, ab_im = powers([1.0])
    ab_re, ab_im = ab_re[0], ab_im[0]
    den = a_re * a_re + a_im * a_im
    f_re = ((ab_re - 1.0) * a_re + ab_im * a_im) / den
    f_im = (ab_im * a_re - (ab_re - 1.0) * a_im) / den
    bb_re = f_re[..., None] * b_re - f_im[..., None] * b_im
    bb_im = f_re[..., None] * b_im + f_im[..., None] * b_re
    pw_re, pw_im = powers(np.arange(t + 1))
    cp_re = c_re[None] * pw_re[:, :, None, :] - c_im[None] * pw_im[:, :, None, :]
    cp_im = c_re[None] * pw_im[:, :, None, :] + c_im[None] * pw_re[:, :, None, :]
    ktau = (jnp.einsum('tghp,gpk->tkgh', cp_re[:t], bb_re)
            - jnp.einsum('tghp,gpk->tkgh', cp_im[:t], bb_im))
    kc = ktau.reshape(t, hg, ncol, cgc * hg).transpose(2, 0, 1, 3)
    bt_re, bt_im = bb_re.transpose(0, 2, 1), bb_im.transpose(0, 2, 1)
    rp_re = pw_re[t - 1 - np.arange(t)][:, :, None, :]
    rp_im = pw_im[t - 1 - np.arange(t)][:, :, None, :]
    pin = jnp.concatenate([rp_re * bt_re[None] - rp_im * bt_im[None],
                           rp_re * bt_im[None] + rp_im * bt_re[None]], axis=3)
    pc = pin.reshape(t, ncol, cgc, hg, 2 * p).transpose(1, 0, 2, 3, 4).reshape(ncol, t * cgc * hg, 2 * p)
    ct_re, ct_im = c_re.transpose(2, 0, 1)[None], c_im.transpose(2, 0, 1)[None]
    pt_re = pw_re[1:].transpose(0, 2, 1)[..., None]
    pt_im = pw_im[1:].transpose(0, 2, 1)[..., None]
    qg = jnp.concatenate([ct_re * pt_re - ct_im * pt_im,
                          -(ct_re * pt_im + ct_im * pt_re)], axis=1)
    qc = qg.reshape(t, 2 * p, ncol, cgc * hg).transpose(2, 0, 1, 3)
    nblk = g // bgc
    nsteps = int(np.log2(chunks_per_seq))
    sr, si = powers(t * 2.0 ** np.arange(nsteps))
    blk = lambda a: a.reshape(nsteps, nblk, bgc * 2 * p).transpose(1, 0, 2)
    ar = blk(jnp.concatenate([sr, sr], axis=2))
    ai = blk(jnp.concatenate([-si, si], axis=2))
    return kc, pc, qc, ar, ai


def _merge_body(x_ref, pin_ref, halo_ref, og_ref, r0_ref, r1_ref, yc_ref, sin_ref, g0_ref, g1_ref, g2_ref,
                pw_ref, ps_ref, gain_ref, d_ref, wglu_ref, wbp_ref, wbg0_ref, wbg1_ref, wbs_ref, wo_ref,
                o_ref, ys_ref, *, tiles_per_seq):
    rows = yc_ref.shape[1]
    npairs = SSM_CHUNK // SSM_TPAIR
    for ob in range(yc_ref.shape[0]):
        c, v = divmod(ob, npairs)
        plane = yc_ref[ob].astype(F32)
        for tt in range(SSM_TPAIR):
            ys_ref[c, pl.ds(SSM_TPAIR * v + tt, rows, stride=SSM_CHUNK), :] = plane[:, tt * LANES:(tt + 1) * LANES]
    t = pl.program_id(0) % tiles_per_seq
    yp = _pool_mix(pin_ref[...].astype(F32), halo_ref[...].astype(F32), t, pw_ref, ps_ref[...])
    m = g0_ref[...].astype(F32) * _dot(yp.astype(BF16), wbp_ref[...])
    r_refs = (r0_ref, r1_ref)
    heads_per_block = r0_ref.shape[1] // GLA_DV
    yg_heads = []
    for h in range(GLA_HEADS):
        hs = slice(h * GLA_DV, (h + 1) * GLA_DV)
        rs = slice((h % heads_per_block) * GLA_DV, (h % heads_per_block + 1) * GLA_DV)
        o = og_ref[:, hs].astype(F32)
        o = o * lax.rsqrt(jnp.mean(o * o, axis=-1, keepdims=True) + EPS) * gain_ref[:, hs]
        r = r_refs[h // heads_per_block][:, rs].astype(F32)
        yg_heads.append((o * (r * _sigmoid(r))).astype(BF16))
    yg = jnp.concatenate(yg_heads, axis=1)
    half = wbg0_ref.shape[0]
    gla_proj = _dot(yg[:, :half], wbg0_ref[...]) + _dot(yg[:, half:], wbg1_ref[...])
    m = m + g1_ref[...].astype(F32) * gla_proj
    ys = jnp.concatenate([ys_ref[c] for c in range(ys_ref.shape[0])], axis=1)
    y = ys + d_ref[...] * sin_ref[...].astype(F32)
    y = jax.nn.gelu(y)
    y = y * _sigmoid(_dot(y.astype(BF16), wglu_ref[...]))
    m = m + g2_ref[...].astype(F32) * _dot(y.astype(BF16), wbs_ref[...])
    o_ref[...] = x_ref[...] + _dot(m.astype(BF16), wo_ref[...])


def _merge(x, o_gla, ycat, z, r_col0, ssm_col0, gate_col0, seq, pool_w, pool_scale, gla_gain,
           d_skip, w_glu, wb, w_out, layer, *, tm=256):
    m, d = x.shape
    wp, wg, ws = pool_scale.shape[2], o_gla.shape[1], d_skip.shape[2]
    br = BRANCH_ROWS
    assert wp == br and ws == br and wg == 2 * br and seq % tm == 0 and tm % POOL_HALO == 0
    gb = gate_col0 // d
    hb = tm // POOL_HALO
    one = pl.Buffered(1)
    wspec = lambda shape, r: pl.BlockSpec((None,) + shape, lambda i: (layer, r, 0), pipeline_mode=one)
    return pl.pallas_call(
        functools.partial(_merge_body, tiles_per_seq=seq // tm),
        grid=(m // tm,),
        in_specs=[
            pl.BlockSpec((tm, d), lambda i: (i, 0)),
            pl.BlockSpec((tm, wp), lambda i: (i, 0)),
            pl.BlockSpec((POOL_HALO, wp), lambda i: (jnp.maximum(i * hb - 1, 0), 0)),
            pl.BlockSpec((tm, wg), lambda i: (i, 0)),
            pl.BlockSpec((tm, br), lambda i: (i, r_col0 // br)),
            pl.BlockSpec((tm, br), lambda i: (i, r_col0 // br + 1)),
            pl.BlockSpec((ycat.shape[0], tm // SSM_CHUNK, ycat.shape[2]), lambda i: (0, i, 0)),
            pl.BlockSpec((tm, ws), lambda i: (i, ssm_col0 // ws)),
            pl.BlockSpec((tm, d), lambda i: (i, gb)),
            pl.BlockSpec((tm, d), lambda i: (i, gb + 1)),
            pl.BlockSpec((tm, d), lambda i: (i, gb + 2)),
            pl.BlockSpec((None, len(POOL_WINDOWS), POOL_GROUP, POOL_GROUP), lambda i: (layer, 0, 0, 0),
                         pipeline_mode=one),
            wspec((1, wp), 0),
            wspec((1, wg), 0),
            wspec((1, ws), 0),
            wspec((ws, ws), 0),
            wspec((br, d), 0),
            wspec((br, d), 1),
            wspec((br, d), 2),
            wspec((br, d), 3),
            wspec((d, d), 0),
        ],
        out_specs=pl.BlockSpec((tm, d), lambda i: (i, 0)),
        out_shape=jax.ShapeDtypeStruct((m, d), F32),
        scratch_shapes=[pltpu.VMEM((ws // LANES, tm, LANES), F32)],
        compiler_params=_cparams(("parallel",)),
        name="merge",
    )(x, z, z, o_gla, z, z, ycat, z, z, z, z, pool_w, pool_scale, gla_gain, d_skip, w_glu,
      wb, wb, wb, wb, w_out)


def kernel(x, ffn1_norm, ffn1_w_gate, ffn1_w_up, ffn1_w_down, mix_norm, w_in, pool_w, pool_scale, gla_w_gate2, gla_gate_bias, gla_norm, ssm_a_re, ssm_a_im, ssm_log_dt, ssm_b_re, ssm_b_im, ssm_c_re, ssm_c_im, ssm_d, ssm_w_glu, w_branch, w_out, ffn2_norm, ffn2_w_gate, ffn2_w_up, ffn2_w_down, final_norm):
    batch, seq, d = x.shape
    depth = w_in.shape[0]
    m = batch * seq
    pool_width = pool_scale.shape[1]
    qk_width = GLA_HEADS * GLA_DK
    v_width = GLA_HEADS * GLA_DV
    ssm_width = ssm_d.shape[1]
    n_groups = ssm_width // SSM_GROUP
    q0 = pool_width
    k0 = q0 + qk_width
    v0 = k0 + qk_width
    r0 = v0 + v_width
    glr0 = r0 + v_width
    ssm_src0 = glr0 + GLA_RANK
    ssm0 = glr0
    gate0 = ssm0 + ssm_width
    chunks_per_seq = seq // SSM_CHUNK

    bf = lambda a: a.astype(BF16)
    row3 = lambda a: a.reshape(depth, 1, a.shape[-1])
    w_in_t = jnp.swapaxes(w_in, 1, 2)
    ffn1_w = (ffn1_w_gate, ffn1_w_up, ffn1_w_down)
    ffn2_w = (ffn2_w_gate, ffn2_w_up, ffn2_w_down)
    f1 = [bf(w[0]) for w in ffn1_w]
    wbr, wo, wglu, pw = bf(w_branch), bf(w_out), bf(ssm_w_glu), bf(pool_w)
    n1, nm, n2 = row3(ffn1_norm), row3(mix_norm), row3(ffn2_norm)
    pscale, dskip = row3(pool_scale), row3(ssm_d)
    w2 = gla_w_gate2.reshape(depth, GLA_RANK, GLA_HEADS, GLA_DK).transpose(0, 2, 1, 3)
    w2 = bf(jnp.pad(w2, ((0, 0), (0, 0), (0, LANES - GLA_RANK), (0, 0))))
    gbias = gla_gate_bias.reshape(depth, GLA_HEADS, 1, GLA_DK)
    ggain = row3(gla_norm)
    ssm_ops = jax.vmap(functools.partial(_ssm_operators, chunks_per_seq=chunks_per_seq))(
        ssm_a_re, ssm_a_im, ssm_log_dt, ssm_b_re, ssm_b_im, ssm_c_re, ssm_c_im)

    xf = x.reshape(m, d)
    for l in range(depth):
        xf, *side = _ffn(xf, n1, l, *f1, cast=(l,) + ffn2_w, repack=(l, w_in_t, glr0, GLA_RANK))
        f2, (w_main, w_glr) = side[:3], side[3:]
        z, glr, xcat = _inproj(xf, nm, w_main, w_glr, l, ssm0, ssm_width, gate0)
        o_gla = _gla(z, glr, w2, gbias, l, batch, seq, (q0, k0, v0))
        ycat = _ssm_core(xcat, *ssm_ops, l, chunks_per_seq)
        xf = _merge(xf, o_gla, ycat, z, r0, ssm0, gate0, seq, pw, pscale, ggain, dskip, wglu, wbr, wo, l)
        last = l == depth - 1
        xf, *f1 = _ffn(xf, n2, l, *f2, cast=None if last else (l + 1,) + ffn1_w,
                       final_g=final_norm if last else None)
    return xf.reshape(batch, seq, d)
```

```python
import functools

import jax
import jax.numpy as jnp
import numpy as np
from jax import lax
from jax.experimental import pallas as pl
from jax.experimental.pallas import tpu as pltpu

F32 = jnp.float32
BF16 = jnp.bfloat16

EPS = 1e-6
POOL_WINDOWS = (2, 4, 8, 16)
POOL_GROUP = 128
POOL_HALO = 16
GLA_HEADS = 4
GLA_DK = 128
GLA_DV = 256
GLA_RANK = 16
GLA_TAU = 16.0
GLA_CHUNK = 64
SSM_GROUP = 16
SSM_STATE = 64
SSM_CHUNK = 16
LANES = 128
FFN_DOWN_CHUNK = 512
FFN_ROW_CHUNK = 1024
FFN_FIRST_ROWS = 256
FFN_EPILOGUE_ROWS = 256
REPACK_ROWS = 64
INPROJ_FIRST_ROWS = 256
BRANCH_ROWS = 512

VMEM_LIMIT = 60 * 1024 * 1024


def _cparams(sem):
    return pltpu.CompilerParams(dimension_semantics=sem, vmem_limit_bytes=VMEM_LIMIT)


def _rms(xf, g):
    return xf * lax.rsqrt(jnp.mean(xf * xf, axis=-1, keepdims=True) + EPS) * g


def _dot(a, b):
    return jnp.dot(a, b, preferred_element_type=F32)


def _sigmoid(a):
    return 0.5 * jnp.tanh(0.5 * a) + 0.5


def _dot_nt(a, b_t):
    return lax.dot_general(a, b_t, (((1,), (1,)), ((), ())), preferred_element_type=F32)


def _ffn_body(x_ref, g_ref, wg_ref, wu_ref, wd_ref, *rest, final, n_cast, repack):
    rest = list(rest)
    cast_in = [rest.pop(0) for _ in range(n_cast)]
    rp_in = [rest.pop(0) for _ in range(2 if repack else 0)]
    fg_ref = rest.pop(0) if final else None
    o_ref = rest.pop(0)
    cast_out = [rest.pop(0) for _ in range(n_cast)]
    rp_out = [rest.pop(0) for _ in range(2 if repack else 0)]
    xn_ref, = rest
    j = pl.program_id(1)

    for src, dst in zip(cast_in, cast_out):
        dst[...] = src[...].astype(BF16)
    if repack:
        (rows_ref, drop_ref), (w_ref, wl_ref) = rp_in, rp_out
        w_ref[...] = rows_ref[0].astype(BF16)
        pad = jnp.zeros((wl_ref.shape[0] - drop_ref.shape[1], drop_ref.shape[2]), F32)
        wl_ref[...] = jnp.concatenate([drop_ref[0], pad], axis=0).astype(BF16)

    def swiglu_step(rs, xn, first):
        a = _dot(xn, wg_ref[...])
        b = _dot(xn, wu_ref[...])
        h = (a * _sigmoid(a) * b).astype(BF16)
        for c0 in range(0, o_ref.shape[1], FFN_DOWN_CHUNK):
            cs = slice(c0, c0 + FFN_DOWN_CHUNK)
            if first:
                o_ref[rs, cs] = _dot(h, wd_ref[:, cs])
            else:
                o_ref[rs, cs] += _dot(h, wd_ref[:, cs])

    @pl.when(j == 0)
    def _():
        for r0 in range(0, o_ref.shape[0], FFN_FIRST_ROWS):
            rs = slice(r0, r0 + FFN_FIRST_ROWS)
            xn = _rms(x_ref[rs, :], g_ref[...]).astype(BF16)
            xn_ref[rs, :] = xn
            swiglu_step(rs, xn, True)

    @pl.when(j > 0)
    def _():
        for r0 in range(0, o_ref.shape[0], FFN_ROW_CHUNK):
            rs = slice(r0, r0 + FFN_ROW_CHUNK)
            swiglu_step(rs, xn_ref[rs, :], False)

    @pl.when(j == pl.num_programs(1) - 1)
    def _():
        def epilogue(r, carry):
            rs = pl.ds(pl.multiple_of(r * FFN_EPILOGUE_ROWS, FFN_EPILOGUE_ROWS), FFN_EPILOGUE_ROWS)
            y = x_ref[rs, :] + 0.5 * o_ref[rs, :]
            if final:
                y = _rms(y, fg_ref[...])
            o_ref[rs, :] = y
            return carry
        lax.fori_loop(0, o_ref.shape[0] // FFN_EPILOGUE_ROWS, epilogue, 0)


def _ffn(x, g, layer, wg, wu, wd, cast=None, repack=None, final_g=None, *, tm=1024, tf=512):
    m, d = x.shape
    f = wg.shape[1]
    ni, nj = m // tm, f // tf
    final = final_g is not None
    in_specs = [
        pl.BlockSpec((tm, d), lambda i, j: (i, 0)),
        pl.BlockSpec((None, 1, d), lambda i, j: (layer, 0, 0)),
        pl.BlockSpec((d, tf), lambda i, j: (0, j)),
        pl.BlockSpec((d, tf), lambda i, j: (0, j)),
        pl.BlockSpec((tf, d), lambda i, j: (j, 0)),
    ]
    args = [x, g, wg, wu, wd]
    out_specs = [pl.BlockSpec((tm, d), lambda i, j: (i, 0))]
    out_shape = [jax.ShapeDtypeStruct((m, d), F32)]
    n_cast = 0
    if cast is not None:
        cl, cg, cu, cd = cast
        n_cast = 3
        rows_up = d // ni
        rows_dn = f // (ni * nj)
        assert d % ni == 0 and f % (ni * nj) == 0 and rows_up % 16 == 0 and rows_dn % 16 == 0
        for w in (cg, cu):
            in_specs.append(pl.BlockSpec((None, rows_up, tf), lambda i, j: (cl, i, j)))
            out_specs.append(pl.BlockSpec((rows_up, tf), lambda i, j: (i, j)))
            out_shape.append(jax.ShapeDtypeStruct((d, f), BF16))
            args.append(w)
        in_specs.append(pl.BlockSpec((None, rows_dn, d), lambda i, j: (cl, i * nj + j, 0)))
        out_specs.append(pl.BlockSpec((rows_dn, d), lambda i, j: (i * nj + j, 0)))
        out_shape.append(jax.ShapeDtypeStruct((f, d), BF16))
        args.append(cd)
    if repack is not None:
        rl, w_in_t, drop0, drop_width = repack
        n = w_in_t.shape[1] - drop_width
        rb = REPACK_ROWS
        nblk = n // rb
        assert n % rb == 0 and nblk <= ni * nj and drop0 % rb == 0 and drop_width % 8 == 0
        blk = lambda i, j: jnp.minimum(i * nj + j, nblk - 1)
        src_row = lambda b: (b * (rb // 8) + jnp.where(b >= drop0 // rb, drop_width // 8, 0)) * 8
        in_specs.append(pl.BlockSpec((pl.Element(1), pl.Element(rb), pl.Element(d)),
                                     lambda i, j: (rl, src_row(blk(i, j)), 0)))
        in_specs.append(pl.BlockSpec((pl.Element(1), pl.Element(drop_width), pl.Element(d)),
                                     lambda i, j: (rl, drop0, 0)))
        out_specs.append(pl.BlockSpec((rb, d), lambda i, j: (blk(i, j), 0)))
        out_specs.append(pl.BlockSpec((LANES, d), lambda i, j: (0, 0)))
        out_shape.append(jax.ShapeDtypeStruct((n, d), BF16))
        out_shape.append(jax.ShapeDtypeStruct((LANES, d), BF16))
        args += [w_in_t, w_in_t]
    if final:
        in_specs.append(pl.BlockSpec((1, d), lambda i, j: (0, 0)))
        args.append(final_g.reshape(1, d))
    return pl.pallas_call(
        functools.partial(_ffn_body, final=final, n_cast=n_cast, repack=repack is not None),
        grid=(ni, nj),
        in_specs=in_specs,
        out_specs=out_specs,
        out_shape=out_shape,
        scratch_shapes=[pltpu.VMEM((tm, d), BF16)],
        compiler_params=_cparams(("arbitrary", "arbitrary")),
        name="ffn_final" if final else "ffn",
    )(*args)


def _inproj_body(x_ref, g_ref, w_ref, wl_ref, z_ref, glr_ref, xc_ref, un_ref, ph_ref, *,
                 gate_block0, ssm_block, ssm_off):
    j = pl.program_id(1)

    @pl.when(j == 0)
    def _():
        for r0 in range(0, z_ref.shape[0], INPROJ_FIRST_ROWS):
            rs = slice(r0, r0 + INPROJ_FIRST_ROWS)
            un = _rms(x_ref[rs, :], g_ref[...]).astype(BF16)
            un_ref[rs, :] = un
            glr_ref[rs, :] = _dot_nt(un, wl_ref[...])
            z_ref[rs, :] = _dot_nt(un, w_ref[...]).astype(BF16)

    @pl.when(j >= gate_block0)
    def _():
        z_ref[...] = _sigmoid(_dot_nt(un_ref[...], w_ref[...])).astype(BF16)

    @pl.when((j > 0) & (j < gate_block0))
    def _():
        acc = _dot_nt(un_ref[...], w_ref[...])
        z_ref[...] = acc.astype(BF16)

        @pl.when(j == ssm_block)
        def _():
            ncol = ph_ref.shape[0]
            for c in range(ncol):
                ph_ref[c] = acc[:, ssm_off + c * LANES:ssm_off + (c + 1) * LANES]
            rows = ph_ref.shape[1] // SSM_CHUNK
            for s in range(SSM_CHUNK):
                for c in range(ncol):
                    piece = ph_ref[c, pl.ds(s, rows, stride=SSM_CHUNK), :].astype(BF16)
                    xc_ref[c, :, s * LANES:(s + 1) * LANES] = piece


def _inproj(x, g, w_main, w_glr, layer, ssm_col0, ssm_width, gate_col0, *, tm=1024, tn=1024):
    m, d = x.shape
    n = w_main.shape[0]
    assert gate_col0 >= tn and tn <= ssm_col0 < gate_col0
    body = functools.partial(_inproj_body, gate_block0=gate_col0 // tn,
                             ssm_block=ssm_col0 // tn, ssm_off=ssm_col0 % tn)
    return pl.pallas_call(
        body,
        grid=(m // tm, n // tn),
        in_specs=[
            pl.BlockSpec((tm, d), lambda i, j: (i, 0)),
            pl.BlockSpec((None, 1, d), lambda i, j: (layer, 0, 0)),
            pl.BlockSpec((tn, d), lambda i, j: (j, 0)),
            pl.BlockSpec((LANES, d), lambda i, j: (0, 0)),
        ],
        out_specs=[
            pl.BlockSpec((tm, tn), lambda i, j: (i, j)),
            pl.BlockSpec((tm, LANES), lambda i, j: (i, 0)),
            pl.BlockSpec((ssm_width // LANES, tm // SSM_CHUNK, SSM_CHUNK * LANES), lambda i, j: (0, i, 0)),
        ],
        out_shape=[
            jax.ShapeDtypeStruct((m, n), BF16),
            jax.ShapeDtypeStruct((m, LANES), F32),
            jax.ShapeDtypeStruct((ssm_width // LANES, m // SSM_CHUNK, SSM_CHUNK * LANES), BF16),
        ],
        scratch_shapes=[pltpu.VMEM((tm, d), BF16), pltpu.VMEM((ssm_width // LANES, tm, LANES), F32)],
        compiler_params=_cparams(("parallel", "arbitrary")),
        name="inproj",
    )(x, g, w_main, w_glr)


def _pool_mix(x, halo, t, w_ref, scale):
    ts = x.shape[0]
    halo = jnp.where(t > 0, halo, 0.0)
    e = jnp.concatenate([halo, x], axis=0)
    s2 = e + pltpu.roll(e, 1, 0)
    s4 = s2 + pltpu.roll(s2, 2, 0)
    s8 = s4 + pltpu.roll(s4, 4, 0)
    s16 = s8 + pltpu.roll(s8, 8, 0)
    pos = (t * ts + 1 + lax.broadcasted_iota(jnp.int32, (ts, 1), 0)).astype(F32)
    outs = []
    for gi, (w, s) in enumerate(zip(POOL_WINDOWS, (s2, s4, s8, s16))):
        cols = slice(gi * POOL_GROUP, (gi + 1) * POOL_GROUP)
        mean = s[POOL_HALO:, cols] / jnp.minimum(pos, float(w))
        outs.append(_dot((mean - x[:, cols]).astype(BF16), w_ref[gi]))
    return jnp.concatenate(outs, axis=1) * scale


def _gla_streams(streams, ts):
    c = GLA_CHUNK
    nc = ts // c
    rowc = lax.broadcasted_iota(jnp.int32, (ts, 1), 0) % c
    causal = (lax.broadcasted_iota(jnp.int32, (c, c), 0)
              >= lax.broadcasted_iota(jnp.int32, (c, c), 1))
    qs, ks, vs, glrs, w2s, biases, sts = (list(a) for a in zip(*streams))
    logits = [_dot(glr, w2) + bias for glr, w2, bias in zip(glrs, w2s, biases)]
    bcs = [(jnp.minimum(l, 0.0) - jnp.log1p(jnp.exp(-jnp.abs(l)))) * (1.0 / GLA_TAU) for l in logits]
    step = 1
    while step < c:
        bcs = [bc + jnp.where(rowc >= step, pltpu.roll(bc, step, 0), 0.0) for bc in bcs]
        step *= 2
    bc3 = [bc.reshape(nc, c, GLA_DK) for bc in bcs]
    b_last = [b[:, c - 1:c, :] for b in bc3]
    b_mid = [b[:, c // 2 - 1:c // 2, :] for b in bc3]
    q3 = [q.astype(F32).reshape(nc, c, GLA_DK) * (GLA_DK ** -0.5) for q in qs]
    k3 = [k.astype(F32).reshape(nc, c, GLA_DK) for k in ks]
    v3 = [v.reshape(nc, c, GLA_DV) for v in vs]
    qe = [(q * jnp.exp(b - m)).astype(BF16) for q, b, m in zip(q3, bc3, b_mid)]
    ke = [(k * jnp.exp(m - b)).astype(BF16) for k, b, m in zip(k3, bc3, b_mid)]
    scores = [jnp.einsum('nid,njd->nij', a, b, preferred_element_type=F32) for a, b in zip(qe, ke)]
    scores = [jnp.where(causal[None], s, 0.0).astype(BF16) for s in scores]
    o_intra = [jnp.einsum('nij,njv->niv', s, v, preferred_element_type=F32) for s, v in zip(scores, v3)]
    qd = [(q * jnp.exp(b)).astype(BF16) for q, b in zip(q3, bc3)]
    kd = [(k * jnp.exp(bl - b)).astype(BF16) for k, b, bl in zip(k3, bc3, b_last)]
    dec_t = [jnp.transpose(jnp.exp(bl.reshape(nc, GLA_DK))) for bl in b_last]
    outs = [[] for _ in streams]
    for n in range(nc):
        for i in range(len(streams)):
            outs[i].append(o_intra[i][n] + _dot(qd[i][n], sts[i].astype(BF16)))
            kv = lax.dot_general(kd[i][n], v3[i][n], (((0,), (0,)), ((), ())), preferred_element_type=F32)
            sts[i] = dec_t[i][:, n:n + 1] * sts[i] + kv
    return [jnp.concatenate(o, axis=0).astype(BF16) for o in outs], sts


def _gla_body(q_ref, k_ref, v0_ref, v1_ref, glr_ref, w2_ref, b_ref, o_ref, st_ref, *, ts):
    @pl.when(pl.program_id(1) == 0)
    def _():
        st_ref[...] = jnp.zeros_like(st_ref)

    v_refs = (v0_ref, v1_ref)
    heads_per_block = v0_ref.shape[2] // GLA_DV
    streams, where = [], []
    for s in range(q_ref.shape[0]):
        glr = glr_ref[s].astype(BF16)
        for h in range(GLA_HEADS):
            v_ref = v_refs[h // heads_per_block]
            vs = slice((h % heads_per_block) * GLA_DV, (h % heads_per_block + 1) * GLA_DV)
            ks = slice(h * GLA_DK, (h + 1) * GLA_DK)
            streams.append((q_ref[s, :, ks], k_ref[s, :, ks], v_ref[s, :, vs], glr,
                            w2_ref[h], b_ref[h], st_ref[s, h]))
            where.append((s, h))
    outs, sts = _gla_streams(streams, ts)
    for (s, h), o, st in zip(where, outs, sts):
        st_ref[s, h] = st
        o_ref[s, :, h * GLA_DV:(h + 1) * GLA_DV] = o


def _gla(z, glr, w2, bias, layer, batch, seq, cols, *, ts=512, nseq=2):
    h = GLA_HEADS
    nt = seq // ts
    q0, k0, v0 = cols
    wb = h * GLA_DK
    z3 = z.reshape(batch, seq, z.shape[1])
    zspec = lambda c0: pl.BlockSpec((nseq, ts, wb), lambda b, t: (b, t, c0 // wb))
    wspec = lambda last2: pl.BlockSpec((None, h) + last2, lambda b, t: (layer, 0, 0, 0))
    out = pl.pallas_call(
        functools.partial(_gla_body, ts=ts),
        grid=(batch // nseq, nt),
        in_specs=[
            zspec(q0), zspec(k0), zspec(v0), zspec(v0 + wb),
            pl.BlockSpec((nseq, ts, LANES), lambda b, t: (b, t, 0)),
            wspec((LANES, GLA_DK)), wspec((1, GLA_DK)),
        ],
        out_specs=pl.BlockSpec((nseq, ts, h * GLA_DV), lambda b, t: (b, t, 0)),
        out_shape=jax.ShapeDtypeStruct((batch, seq, h * GLA_DV), BF16),
        scratch_shapes=[pltpu.VMEM((nseq, h, GLA_DK, GLA_DV), F32)],
        compiler_params=_cparams(("parallel", "arbitrary")),
        name="gla",
    )(z3, z3, z3, z3, glr.reshape(batch, seq, LANES), w2, bias)
    return out.reshape(batch * seq, h * GLA_DV)


SSM_COL_GROUPS = LANES // SSM_GROUP
SSM_BLOCK_GROUPS = 2
SSM_TPAIR = 2


def _ssm_body(x_ref, kc_ref, pc_ref, qc_ref, ar_ref, ai_ref, o_ref, hp_ref, kst_ref, v_ref, *,
              chunks_per_seq, n_state_blocks):
    u = pl.program_id(0)
    ncol, rows = x_ref.shape[0], x_ref.shape[1]
    blocks_per_col = n_state_blocks // ncol
    npairs = SSM_CHUNK // SSM_TPAIR
    ns = 2 * SSM_STATE
    bw = SSM_BLOCK_GROUPS * ns
    lane_group = lax.broadcasted_iota(jnp.int32, (1, LANES), 1) // SSM_GROUP

    def p_tile(n):
        pc = pc_ref[n // blocks_per_col]
        row_group = (lax.broadcasted_iota(jnp.int32, (pc.shape[0], 1), 0) // SSM_GROUP) % SSM_COL_GROUPS
        g0 = (n % blocks_per_col) * SSM_BLOCK_GROUPS
        return jnp.concatenate([jnp.where(row_group == g0 + gg, pc, 0.0)
                                for gg in range(SSM_BLOCK_GROUPS)], axis=1).astype(BF16)

    @pl.when(u == 0)
    def _():
        row_group = lax.broadcasted_iota(jnp.int32, (LANES, 1), 0) // SSM_GROUP

        def kbd(c, lag):
            kc = kc_ref[c, lag]
            tiled = jnp.broadcast_to(kc[None], (SSM_COL_GROUPS,) + kc.shape).reshape(LANES, LANES)
            return jnp.where(row_group == lane_group, tiled, 0.0).astype(BF16)

        for c in range(ncol):
            for r in range(SSM_CHUNK):
                left = kbd(c, SSM_CHUNK - 2 - r) if r <= SSM_CHUNK - 2 else jnp.zeros((LANES, LANES), BF16)
                kst_ref[c, r * LANES:(r + 1) * LANES, :] = jnp.concatenate(
                    [left, kbd(c, SSM_CHUNK - 1 - r)], axis=1)
        v_ref[...] = _dot(x_ref[0], p_tile(0))

    @pl.when(u < n_state_blocks)
    def _():
        rowc = lax.broadcasted_iota(jnp.int32, (rows, 1), 0) % chunks_per_seq
        hst = v_ref[...]
        nxt = jnp.minimum(u + 1, n_state_blocks - 1)
        v_next = _dot(x_ref[nxt // blocks_per_col], p_tile(nxt))
        k, si = 1, 0
        while k < chunks_per_seq:
            hs = jnp.where(rowc >= k, pltpu.roll(hst, k, 0), 0.0)
            sw = jnp.concatenate([pltpu.roll(hs[:, b0:b0 + ns], SSM_STATE, 1)
                                  for b0 in range(0, bw, ns)], axis=1)
            hst = hst + ar_ref[u, si:si + 1, :] * hs + ai_ref[u, si:si + 1, :] * sw
            k *= 2
            si += 1
        hp_ref[u] = jnp.where(rowc >= 1, pltpu.roll(hst, 1, 0), 0.0).astype(BF16)
        v_ref[...] = v_next

    @pl.when(u >= n_state_blocks)
    def _():
        ob = u - n_state_blocks
        c = ob // npairs
        v = ob % npairs
        grp = lax.broadcasted_iota(jnp.int32, (SSM_COL_GROUPS, 1, 1), 0)
        q_tile = jnp.concatenate(
            [jnp.where(grp == lane_group[None], qc_ref[c, SSM_TPAIR * v + tt][None], 0.0)
             .reshape(SSM_COL_GROUPS * ns, LANES) for tt in range(SSM_TPAIR)], axis=1).astype(BF16)
        yq = _dot(hp_ref[c * blocks_per_col], q_tile[0:bw, :])
        for kk in range(1, blocks_per_col):
            yq = yq + _dot(hp_ref[c * blocks_per_col + kk], q_tile[kk * bw:(kk + 1) * bw, :])
        for vs in range(npairs):
            kext = (vs + 1) * SSM_TPAIR * LANES

            @pl.when(v == vs)
            def _():
                start = (npairs - 1 - vs) * SSM_TPAIR * LANES
                intra = _dot(x_ref[c, :, 0:kext], kst_ref[c, start:start + kext, :])
                o_ref[...] = (yq + intra).astype(BF16)


def _ssm_core(xcat, kc, pc, qc, ar, ai, layer, chunks_per_seq):
    ncol, rows, xw = xcat.shape
    nb = ar.shape[1]
    bw = ar.shape[3]
    ow = SSM_TPAIR * LANES
    nout = ncol * SSM_CHUNK // SSM_TPAIR
    one = pl.Buffered(1)
    whole = lambda a: pl.BlockSpec((None,) + a.shape[1:], lambda u: (layer,) + (0,) * (a.ndim - 1),
                                   pipeline_mode=one)
    return pl.pallas_call(
        functools.partial(_ssm_body, chunks_per_seq=chunks_per_seq, n_state_blocks=nb),
        grid=(nb + nout,),
        in_specs=[
            pl.BlockSpec((ncol, rows, xw), lambda u: (0, 0, 0), pipeline_mode=one),
            whole(kc), whole(pc), whole(qc), whole(ar), whole(ai),
        ],
        out_specs=pl.BlockSpec((None, rows, ow), lambda u: (jnp.maximum(u - nb, 0), 0, 0)),
        out_shape=jax.ShapeDtypeStruct((nout, rows, ow), BF16),
        scratch_shapes=[
            pltpu.VMEM((nb, rows, bw), BF16),
            pltpu.VMEM((ncol, SSM_CHUNK * LANES, ow), BF16),
            pltpu.VMEM((rows, bw), F32),
        ],
        compiler_params=_cparams(("arbitrary",)),
        name="ssm_core",
    )(xcat, kc, pc, qc, ar, ai)


def _ssm_operators(a_re, a_im, log_dt, b_re, b_im, c_re, c_im, chunks_per_seq):
    t = SSM_CHUNK
    g, p = a_re.shape
    hg = b_re.shape[-1]
    cgc = SSM_COL_GROUPS
    ncol = g // cgc
    bgc = SSM_BLOCK_GROUPS
    dt = jnp.exp(log_dt)[:, None]
    lam_re, lam_im = dt * a_re, dt * a_im

    def powers(n):
        n = jnp.asarray(n, F32)[:, None, None]
        mag = jnp.exp(n * lam_re)
        return mag * jnp.cos(n * lam_im), mag * jnp.sin(n * lam_im)

    ab_re, ab_im = powers([1.0])
    ab_re, ab_im = ab_re[0], ab_im[0]
    den = a_re * a_re + a_im * a_im
    f_re = ((ab_re - 1.0) * a_re + ab_im * a_im) / den
    f_im = (ab_im * a_re - (ab_re - 1.0) * a_im) / den
    bb_re = f_re[..., None] * b_re - f_im[..., None] * b_im
    bb_im = f_re[..., None] * b_im + f_im[..., None] * b_re
    pw_re, pw_im = powers(np.arange(t + 1))
    cp_re = c_re[None] * pw_re[:, :, None, :] - c_im[None] * pw_im[:, :, None, :]
    cp_im = c_re[None] * pw_im[:, :, None, :] + c_im[None] * pw_re[:, :, None, :]
    ktau = (jnp.einsum('tghp,gpk->tkgh', cp_re[:t], bb_re)
            - jnp.einsum('tghp,gpk->tkgh', cp_im[:t], bb_im))
    kc = ktau.reshape(t, hg, ncol, cgc * hg).transpose(2, 0, 1, 3)
    bt_re, bt_im = bb_re.transpose(0, 2, 1), bb_im.transpose(0, 2, 1)
    rp_re = pw_re[t - 1 - np.arange(t)][:, :, None, :]
    rp_im = pw_im[t - 1 - np.arange(t)][:, :, None, :]
    pin = jnp.concatenate([rp_re * bt_re[None] - rp_im * bt_im[None],
                           rp_re * bt_im[None] + rp_im * bt_re[None]], axis=3)
    pc = pin.reshape(t, ncol, cgc, hg, 2 * p).transpose(1, 0, 2, 3, 4).reshape(ncol, t * cgc * hg, 2 * p)
    ct_re, ct_im = c_re.transpose(2, 0, 1)[None], c_im.transpose(2, 0, 1)[None]
    pt_re = pw_re[1:].transpose(0, 2, 1)[..., None]
    pt_im = pw_im[1:].transpose(0, 2, 1)[..., None]
    qg = jnp.concatenate([ct_re * pt_re - ct_im * pt_im,
                          -(ct_re * pt_im + ct_im * pt_re)], axis=1)
    qc = qg.reshape(t, 2 * p, ncol, cgc * hg).transpose(2, 0, 1, 3)
    nblk = g // bgc
    nsteps = int(np.log2(chunks_per_seq))
    sr, si = powers(t * 2.0 ** np.arange(nsteps))
    blk = lambda a: a.reshape(nsteps, nblk, bgc * 2 * p).transpose(1, 0, 2)
    ar = blk(jnp.concatenate([sr, sr], axis=2))
    ai = blk(jnp.concatenate([-si, si], axis=2))
    return kc, pc, qc, ar, ai


def _merge_body(x_ref, pin_ref, halo_ref, og_ref, r0_ref, r1_ref, yc_ref, sin_ref, g0_ref, g1_ref, g2_ref,
                pw_ref, ps_ref, gain_ref, d_ref, wglu_ref, wbp_ref, wbg0_ref, wbg1_ref, wbs_ref, wo_ref,
                o_ref, ys_ref, *, tiles_per_seq):
    rows = yc_ref.shape[1]
    npairs = SSM_CHUNK // SSM_TPAIR
    for ob in range(yc_ref.shape[0]):
        c, v = divmod(ob, npairs)
        plane = yc_ref[ob].astype(F32)
        for tt in range(SSM_TPAIR):
            ys_ref[c, pl.ds(SSM_TPAIR * v + tt, rows, stride=SSM_CHUNK), :] = plane[:, tt * LANES:(tt + 1) * LANES]
    t = pl.program_id(0) % tiles_per_seq
    yp = _pool_mix(pin_ref[...].astype(F32), halo_ref[...].astype(F32), t, pw_ref, ps_ref[...])
    m = g0_ref[...].astype(F32) * _dot(yp.astype(BF16), wbp_ref[...])
    r_refs = (r0_ref, r1_ref)
    heads_per_block = r0_ref.shape[1] // GLA_DV
    yg_heads = []
    for h in range(GLA_HEADS):
        hs = slice(h * GLA_DV, (h + 1) * GLA_DV)
        rs = slice((h % heads_per_block) * GLA_DV, (h % heads_per_block + 1) * GLA_DV)
        o = og_ref[:, hs].astype(F32)
        o = o * lax.rsqrt(jnp.mean(o * o, axis=-1, keepdims=True) + EPS) * gain_ref[:, hs]
        r = r_refs[h // heads_per_block][:, rs].astype(F32)
        yg_heads.append((o * (r * _sigmoid(r))).astype(BF16))
    yg = jnp.concatenate(yg_heads, axis=1)
    half = wbg0_ref.shape[0]
    gla_proj = _dot(yg[:, :half], wbg0_ref[...]) + _dot(yg[:, half:], wbg1_ref[...])
    m = m + g1_ref[...].astype(F32) * gla_proj
    ys = jnp.concatenate([ys_ref[c] for c in range(ys_ref.shape[0])], axis=1)
    y = ys + d_ref[...] * sin_ref[...].astype(F32)
    y = jax.nn.gelu(y)
    y = y * _sigmoid(_dot(y.astype(BF16), wglu_ref[...]))
    m = m + g2_ref[...].astype(F32) * _dot(y.astype(BF16), wbs_ref[...])
    o_ref[...] = x_ref[...] + _dot(m.astype(BF16), wo_ref[...])


def _merge(x, o_gla, ycat, z, r_col0, ssm_col0, gate_col0, seq, pool_w, pool_scale, gla_gain,
           d_skip, w_glu, wb, w_out, layer, *, tm=256):
    m, d = x.shape
    wp, wg, ws = pool_scale.shape[2], o_gla.shape[1], d_skip.shape[2]
    br = BRANCH_ROWS
    assert wp == br and ws == br and wg == 2 * br and seq % tm == 0 and tm % POOL_HALO == 0
    gb = gate_col0 // d
    hb = tm // POOL_HALO
    one = pl.Buffered(1)
    wspec = lambda shape, r: pl.BlockSpec((None,) + shape, lambda i: (layer, r, 0), pipeline_mode=one)
    return pl.pallas_call(
        functools.partial(_merge_body, tiles_per_seq=seq // tm),
        grid=(m // tm,),
        in_specs=[
            pl.BlockSpec((tm, d), lambda i: (i, 0)),
            pl.BlockSpec((tm, wp), lambda i: (i, 0)),
            pl.BlockSpec((POOL_HALO, wp), lambda i: (jnp.maximum(i * hb - 1, 0), 0)),
            pl.BlockSpec((tm, wg), lambda i: (i, 0)),
            pl.BlockSpec((tm, br), lambda i: (i, r_col0 // br)),
            pl.BlockSpec((tm, br), lambda i: (i, r_col0 // br + 1)),
            pl.BlockSpec((ycat.shape[0], tm // SSM_CHUNK, ycat.shape[2]), lambda i: (0, i, 0)),
            pl.BlockSpec((tm, ws), lambda i: (i, ssm_col0 // ws)),
            pl.BlockSpec((tm, d), lambda i: (i, gb)),
            pl.BlockSpec((tm, d), lambda i: (i, gb + 1)),
            pl.BlockSpec((tm, d), lambda i: (i, gb + 2)),
            pl.BlockSpec((None, len(POOL_WINDOWS), POOL_GROUP, POOL_GROUP), lambda i: (layer, 0, 0, 0),
                         pipeline_mode=one),
            wspec((1, wp), 0),
            wspec((1, wg), 0),
            wspec((1, ws), 0),
            wspec((ws, ws), 0),
            wspec((br, d), 0),
            wspec((br, d), 1),
            wspec((br, d), 2),
            wspec((br, d), 3),
            wspec((d, d), 0),
        ],
        out_specs=pl.BlockSpec((tm, d), lambda i: (i, 0)),
        out_shape=jax.ShapeDtypeStruct((m, d), F32),
        scratch_shapes=[pltpu.VMEM((ws // LANES, tm, LANES), F32)],
        compiler_params=_cparams(("parallel",)),
        name="merge",
    )(x, z, z, o_gla, z, z, ycat, z, z, z, z, pool_w, pool_scale, gla_gain, d_skip, w_glu,
      wb, wb, wb, wb, w_out)


def kernel(x, ffn1_norm, ffn1_w_gate, ffn1_w_up, ffn1_w_down, mix_norm, w_in, pool_w, pool_scale, gla_w_gate2, gla_gate_bias, gla_norm, ssm_a_re, ssm_a_im, ssm_log_dt, ssm_b_re, ssm_b_im, ssm_c_re, ssm_c_im, ssm_d, ssm_w_glu, w_branch, w_out, ffn2_norm, ffn2_w_gate, ffn2_w_up, ffn2_w_down, final_norm):
    batch, seq, d = x.shape
    depth = w_in.shape[0]
    m = batch * seq
    pool_width = pool_scale.shape[1]
    qk_width = GLA_HEADS * GLA_DK
    v_width = GLA_HEADS * GLA_DV
    ssm_width = ssm_d.shape[1]
    n_groups = ssm_width // SSM_GROUP
    q0 = pool_width
    k0 = q0 + qk_width
    v0 = k0 + qk_width
    r0 = v0 + v_width
    glr0 = r0 + v_width
    ssm_src0 = glr0 + GLA_RANK
    ssm0 = glr0
    gate0 = ssm0 + ssm_width
    chunks_per_seq = seq // SSM_CHUNK

    bf = lambda a: a.astype(BF16)
    row3 = lambda a: a.reshape(depth, 1, a.shape[-1])
    w_in_t = jnp.swapaxes(w_in, 1, 2)
    ffn1_w = (ffn1_w_gate, ffn1_w_up, ffn1_w_down)
    ffn2_w = (ffn2_w_gate, ffn2_w_up, ffn2_w_down)
    f1 = [bf(w[0]) for w in ffn1_w]
    wbr, wo, wglu, pw = bf(w_branch), bf(w_out), bf(ssm_w_glu), bf(pool_w)
    n1, nm, n2 = row3(ffn1_norm), row3(mix_norm), row3(ffn2_norm)
    pscale, dskip = row3(pool_scale), row3(ssm_d)
    w2 = gla_w_gate2.reshape(depth, GLA_RANK, GLA_HEADS, GLA_DK).transpose(0, 2, 1, 3)
    w2 = bf(jnp.pad(w2, ((0, 0), (0, 0), (0, LANES - GLA_RANK), (0, 0))))
    gbias = gla_gate_bias.reshape(depth, GLA_HEADS, 1, GLA_DK)
    ggain = row3(gla_norm)
    ssm_ops = jax.vmap(functools.partial(_ssm_operators, chunks_per_seq=chunks_per_seq))(
        ssm_a_re, ssm_a_im, ssm_log_dt, ssm_b_re, ssm_b_im, ssm_c_re, ssm_c_im)

    xf = x.reshape(m, d)
    for l in range(depth):
        xf, *side = _ffn(xf, n1, l, *f1, cast=(l,) + ffn2_w, repack=(l, w_in_t, glr0, GLA_RANK))
        f2, (w_main, w_glr) = side[:3], side[3:]
        z, glr, xcat = _inproj(xf, nm, w_main, w_glr, l, ssm0, ssm_width, gate0)
        o_gla = _gla(z, glr, w2, gbias, l, batch, seq, (q0, k0, v0))
        ycat = _ssm_core(xcat, *ssm_ops, l, chunks_per_seq)
        xf = _merge(xf, o_gla, ycat, z, r0, ssm0, gate0, seq, pw, pscale, ggain, dskip, wglu, wbr, wo, l)
        last = l == depth - 1
        xf, *f1 = _ffn(xf, n2, l, *f2, cast=None if last else (l + 1,) + ffn1_w,
                       final_g=final_norm if last else None)
    return xf.reshape(batch, seq, d)
```

```python
import functools

import jax
import jax.numpy as jnp
import numpy as np
from jax import lax
from jax.experimental import pallas as pl
from jax.experimental.pallas import tpu as pltpu

F32 = jnp.float32
BF16 = jnp.bfloat16

EPS = 1e-6
POOL_WINDOWS = (2, 4, 8, 16)
POOL_GROUP = 128
POOL_HALO = 16
GLA_HEADS = 4
GLA_DK = 128
GLA_DV = 256
GLA_RANK = 16
GLA_TAU = 16.0
LOG2_E = 1.4426950408889634
GLA_CHUNK = 64
SSM_GROUP = 16
SSM_STATE = 64
SSM_CHUNK = 16
LANES = 128
FFN_DOWN_CHUNK = 512
FFN_ROW_CHUNK = 1024
FFN_FIRST_ROWS = 256
FFN_EPILOGUE_ROWS = 256
REPACK_ROWS = 64
INPROJ_FIRST_ROWS = 256
BRANCH_ROWS = 512

VMEM_LIMIT = 60 * 1024 * 1024


def _cparams(sem):
    return pltpu.CompilerParams(dimension_semantics=sem, vmem_limit_bytes=VMEM_LIMIT)


def _rms(xf, g):
    return xf * lax.rsqrt(jnp.mean(xf * xf, axis=-1, keepdims=True) + EPS) * g


def _dot(a, b):
    return jnp.dot(a, b, preferred_element_type=F32)


def _sigmoid(a):
    return 0.5 * jnp.tanh(0.5 * a) + 0.5


def _dot_nt(a, b_t):
    return lax.dot_general(a, b_t, (((1,), (1,)), ((), ())), preferred_element_type=F32)


def _ffn_body(x_ref, g_ref, wg_ref, wu_ref, wd_ref, *rest, final, n_cast, repack):
    rest = list(rest)
    cast_in = [rest.pop(0) for _ in range(n_cast)]
    rp_in = [rest.pop(0) for _ in range(2 if repack else 0)]
    fg_ref = rest.pop(0) if final else None
    o_ref = rest.pop(0)
    cast_out = [rest.pop(0) for _ in range(n_cast)]
    rp_out = [rest.pop(0) for _ in range(2 if repack else 0)]
    xn_ref, = rest
    j = pl.program_id(1)

    for src, dst in zip(cast_in, cast_out):
        dst[...] = src[...].astype(BF16)
    if repack:
        (rows_ref, drop_ref), (w_ref, wl_ref) = rp_in, rp_out
        w_ref[...] = rows_ref[0].astype(BF16)
        pad = jnp.zeros((wl_ref.shape[0] - drop_ref.shape[1], drop_ref.shape[2]), F32)
        wl_ref[...] = jnp.concatenate([drop_ref[0], pad], axis=0).astype(BF16)

    def swiglu_step(rs, xn, first):
        a = _dot(xn, wg_ref[...])
        b = _dot(xn, wu_ref[...])
        h = (a * _sigmoid(a) * b).astype(BF16)
        for c0 in range(0, o_ref.shape[1], FFN_DOWN_CHUNK):
            cs = slice(c0, c0 + FFN_DOWN_CHUNK)
            if first:
                o_ref[rs, cs] = _dot(h, wd_ref[:, cs])
            else:
                o_ref[rs, cs] += _dot(h, wd_ref[:, cs])

    @pl.when(j == 0)
    def _():
        for r0 in range(0, o_ref.shape[0], FFN_FIRST_ROWS):
            rs = slice(r0, r0 + FFN_FIRST_ROWS)
            xn = _rms(x_ref[rs, :], g_ref[...]).astype(BF16)
            xn_ref[rs, :] = xn
            swiglu_step(rs, xn, True)

    @pl.when(j > 0)
    def _():
        for r0 in range(0, o_ref.shape[0], FFN_ROW_CHUNK):
            rs = slice(r0, r0 + FFN_ROW_CHUNK)
            swiglu_step(rs, xn_ref[rs, :], False)

    @pl.when(j == pl.num_programs(1) - 1)
    def _():
        def epilogue(r, carry):
            rs = pl.ds(pl.multiple_of(r * FFN_EPILOGUE_ROWS, FFN_EPILOGUE_ROWS), FFN_EPILOGUE_ROWS)
            y = x_ref[rs, :] + 0.5 * o_ref[rs, :]
            if final:
                y = _rms(y, fg_ref[...])
            o_ref[rs, :] = y
            return carry
        lax.fori_loop(0, o_ref.shape[0] // FFN_EPILOGUE_ROWS, epilogue, 0)


def _ffn(x, g, layer, wg, wu, wd, cast=None, repack=None, final_g=None, *, tm=1024, tf=512):
    m, d = x.shape
    f = wg.shape[1]
    ni, nj = m // tm, f // tf
    final = final_g is not None
    in_specs = [
        pl.BlockSpec((tm, d), lambda i, j: (i, 0)),
        pl.BlockSpec((None, 1, d), lambda i, j: (layer, 0, 0)),
        pl.BlockSpec((d, tf), lambda i, j: (0, j)),
        pl.BlockSpec((d, tf), lambda i, j: (0, j)),
        pl.BlockSpec((tf, d), lambda i, j: (j, 0)),
    ]
    args = [x, g, wg, wu, wd]
    out_specs = [pl.BlockSpec((tm, d), lambda i, j: (i, 0))]
    out_shape = [jax.ShapeDtypeStruct((m, d), F32)]
    n_cast = 0
    if cast is not None:
        cl, cg, cu, cd = cast
        n_cast = 3
        rows_up = d // ni
        rows_dn = f // (ni * nj)
        assert d % ni == 0 and f % (ni * nj) == 0 and rows_up % 16 == 0 and rows_dn % 16 == 0
        for w in (cg, cu):
            in_specs.append(pl.BlockSpec((None, rows_up, tf), lambda i, j: (cl, i, j)))
            out_specs.append(pl.BlockSpec((rows_up, tf), lambda i, j: (i, j)))
            out_shape.append(jax.ShapeDtypeStruct((d, f), BF16))
            args.append(w)
        in_specs.append(pl.BlockSpec((None, rows_dn, d), lambda i, j: (cl, i * nj + j, 0)))
        out_specs.append(pl.BlockSpec((rows_dn, d), lambda i, j: (i * nj + j, 0)))
        out_shape.append(jax.ShapeDtypeStruct((f, d), BF16))
        args.append(cd)
    if repack is not None:
        rl, w_in_t, drop0, drop_width = repack
        n = w_in_t.shape[1] - drop_width
        rb = REPACK_ROWS
        nblk = n // rb
        assert n % rb == 0 and nblk <= ni * nj and drop0 % rb == 0 and drop_width % 8 == 0
        blk = lambda i, j: jnp.minimum(i * nj + j, nblk - 1)
        src_row = lambda b: (b * (rb // 8) + jnp.where(b >= drop0 // rb, drop_width // 8, 0)) * 8
        in_specs.append(pl.BlockSpec((pl.Element(1), pl.Element(rb), pl.Element(d)),
                                     lambda i, j: (rl, src_row(blk(i, j)), 0)))
        in_specs.append(pl.BlockSpec((pl.Element(1), pl.Element(drop_width), pl.Element(d)),
                                     lambda i, j: (rl, drop0, 0)))
        out_specs.append(pl.BlockSpec((rb, d), lambda i, j: (blk(i, j), 0)))
        out_specs.append(pl.BlockSpec((LANES, d), lambda i, j: (0, 0)))
        out_shape.append(jax.ShapeDtypeStruct((n, d), BF16))
        out_shape.append(jax.ShapeDtypeStruct((LANES, d), BF16))
        args += [w_in_t, w_in_t]
    if final:
        in_specs.append(pl.BlockSpec((1, d), lambda i, j: (0, 0)))
        args.append(final_g.reshape(1, d))
    return pl.pallas_call(
        functools.partial(_ffn_body, final=final, n_cast=n_cast, repack=repack is not None),
        grid=(ni, nj),
        in_specs=in_specs,
        out_specs=out_specs,
        out_shape=out_shape,
        scratch_shapes=[pltpu.VMEM((tm, d), BF16)],
        compiler_params=_cparams(("arbitrary", "arbitrary")),
        name="ffn_final" if final else "ffn",
    )(*args)


def _inproj_body(x_ref, g_ref, w_ref, wl_ref, z_ref, glr_ref, xc_ref, un_ref, ph_ref, *,
                 gate_block0, ssm_block, ssm_off):
    j = pl.program_id(1)

    @pl.when(j == 0)
    def _():
        for r0 in range(0, z_ref.shape[0], INPROJ_FIRST_ROWS):
            rs = slice(r0, r0 + INPROJ_FIRST_ROWS)
            un = _rms(x_ref[rs, :], g_ref[...]).astype(BF16)
            un_ref[rs, :] = un
            glr_ref[rs, :] = _dot_nt(un, wl_ref[...])
            z_ref[rs, :] = _dot_nt(un, w_ref[...]).astype(BF16)

    @pl.when(j >= gate_block0)
    def _():
        z_ref[...] = _sigmoid(_dot_nt(un_ref[...], w_ref[...])).astype(BF16)

    @pl.when((j > 0) & (j < gate_block0))
    def _():
        acc = _dot_nt(un_ref[...], w_ref[...])
        z_ref[...] = acc.astype(BF16)

        @pl.when(j == ssm_block)
        def _():
            ncol = ph_ref.shape[0]
            for c in range(ncol):
                ph_ref[c] = acc[:, ssm_off + c * LANES:ssm_off + (c + 1) * LANES]
            rows = ph_ref.shape[1] // SSM_CHUNK
            for s in range(SSM_CHUNK):
                for c in range(ncol):
                    piece = ph_ref[c, pl.ds(s, rows, stride=SSM_CHUNK), :].astype(BF16)
                    xc_ref[c, :, s * LANES:(s + 1) * LANES] = piece


def _inproj(x, g, w_main, w_glr, layer, ssm_col0, ssm_width, gate_col0, *, tm=1024, tn=1024):
    m, d = x.shape
    n = w_main.shape[0]
    assert gate_col0 >= tn and tn <= ssm_col0 < gate_col0
    body = functools.partial(_inproj_body, gate_block0=gate_col0 // tn,
                             ssm_block=ssm_col0 // tn, ssm_off=ssm_col0 % tn)
    return pl.pallas_call(
        body,
        grid=(m // tm, n // tn),
        in_specs=[
            pl.BlockSpec((tm, d), lambda i, j: (i, 0)),
            pl.BlockSpec((None, 1, d), lambda i, j: (layer, 0, 0)),
            pl.BlockSpec((tn, d), lambda i, j: (j, 0)),
            pl.BlockSpec((LANES, d), lambda i, j: (0, 0)),
        ],
        out_specs=[
            pl.BlockSpec((tm, tn), lambda i, j: (i, j)),
            pl.BlockSpec((tm, LANES), lambda i, j: (i, 0)),
            pl.BlockSpec((ssm_width // LANES, tm // SSM_CHUNK, SSM_CHUNK * LANES), lambda i, j: (0, i, 0)),
        ],
        out_shape=[
            jax.ShapeDtypeStruct((m, n), BF16),
            jax.ShapeDtypeStruct((m, LANES), F32),
            jax.ShapeDtypeStruct((ssm_width // LANES, m // SSM_CHUNK, SSM_CHUNK * LANES), BF16),
        ],
        scratch_shapes=[pltpu.VMEM((tm, d), BF16), pltpu.VMEM((ssm_width // LANES, tm, LANES), F32)],
        compiler_params=_cparams(("parallel", "arbitrary")),
        name="inproj",
    )(x, g, w_main, w_glr)


def _pool_mix(x, halo, t, w_ref, scale):
    ts = x.shape[0]
    halo = jnp.where(t > 0, halo, 0.0)
    e = jnp.concatenate([halo, x], axis=0)
    s2 = e + pltpu.roll(e, 1, 0)
    s4 = s2 + pltpu.roll(s2, 2, 0)
    s8 = s4 + pltpu.roll(s4, 4, 0)
    s16 = s8 + pltpu.roll(s8, 8, 0)
    pos = (t * ts + 1 + lax.broadcasted_iota(jnp.int32, (ts, 1), 0)).astype(F32)
    outs = []
    for gi, (w, s) in enumerate(zip(POOL_WINDOWS, (s2, s4, s8, s16))):
        cols = slice(gi * POOL_GROUP, (gi + 1) * POOL_GROUP)
        mean = s[POOL_HALO:, cols] / jnp.minimum(pos, float(w))
        outs.append(_dot((mean - x[:, cols]).astype(BF16), w_ref[gi]))
    return jnp.concatenate(outs, axis=1) * scale


def _gla_streams(streams, ts):
    c = GLA_CHUNK
    nc = ts // c
    rowc = lax.broadcasted_iota(jnp.int32, (ts, 1), 0) % c
    causal = (lax.broadcasted_iota(jnp.int32, (c, c), 0)
              >= lax.broadcasted_iota(jnp.int32, (c, c), 1))
    qs, ks, vs, glrs, w2s, biases, sts = (list(a) for a in zip(*streams))
    logits = [_dot(glr, w2) + bias for glr, w2, bias in zip(glrs, w2s, biases)]
    bcs = [(jnp.minimum(l, 0.0) - jnp.log1p(jnp.exp(-jnp.abs(l)))) * (LOG2_E / GLA_TAU) for l in logits]
    step = 1
    while step < c:
        bcs = [bc + jnp.where(rowc >= step, pltpu.roll(bc, step, 0), 0.0) for bc in bcs]
        step *= 2
    bc3 = [bc.reshape(nc, c, GLA_DK) for bc in bcs]
    b_last = [b[:, c - 1:c, :] for b in bc3]
    b_mid = [b[:, c // 2 - 1:c // 2, :] for b in bc3]
    q3 = [q.astype(F32).reshape(nc, c, GLA_DK) * (GLA_DK ** -0.5) for q in qs]
    k3 = [k.astype(F32).reshape(nc, c, GLA_DK) for k in ks]
    v3 = [v.reshape(nc, c, GLA_DV) for v in vs]
    qe = [(q * jnp.exp2(b - m)).astype(BF16) for q, b, m in zip(q3, bc3, b_mid)]
    ke = [(k * jnp.exp2(m - b)).astype(BF16) for k, b, m in zip(k3, bc3, b_mid)]
    scores = [jnp.einsum('nid,njd->nij', a, b, preferred_element_type=F32) for a, b in zip(qe, ke)]
    scores = [jnp.where(causal[None], s, 0.0).astype(BF16) for s in scores]
    o_intra = [jnp.einsum('nij,njv->niv', s, v, preferred_element_type=F32) for s, v in zip(scores, v3)]
    qd = [(q * jnp.exp2(b)).astype(BF16) for q, b in zip(q3, bc3)]
    kd = [(k * jnp.exp2(bl - b)).astype(BF16) for k, b, bl in zip(k3, bc3, b_last)]
    dec_t = [jnp.transpose(jnp.exp2(bl.reshape(nc, GLA_DK))) for bl in b_last]
    outs = [[] for _ in streams]
    for n in range(nc):
        for i in range(len(streams)):
            outs[i].append(o_intra[i][n] + _dot(qd[i][n], sts[i].astype(BF16)))
            kv = lax.dot_general(kd[i][n], v3[i][n], (((0,), (0,)), ((), ())), preferred_element_type=F32)
            sts[i] = dec_t[i][:, n:n + 1] * sts[i] + kv
    return [jnp.concatenate(o, axis=0).astype(BF16) for o in outs], sts


def _gla_body(q_ref, k_ref, v0_ref, v1_ref, glr_ref, w2_ref, b_ref, o_ref, st_ref, *, ts):
    @pl.when(pl.program_id(1) == 0)
    def _():
        st_ref[...] = jnp.zeros_like(st_ref)

    v_refs = (v0_ref, v1_ref)
    heads_per_block = v0_ref.shape[2] // GLA_DV
    streams, where = [], []
    for s in range(q_ref.shape[0]):
        glr = glr_ref[s].astype(BF16)
        for h in range(GLA_HEADS):
            v_ref = v_refs[h // heads_per_block]
            vs = slice((h % heads_per_block) * GLA_DV, (h % heads_per_block + 1) * GLA_DV)
            ks = slice(h * GLA_DK, (h + 1) * GLA_DK)
            streams.append((q_ref[s, :, ks], k_ref[s, :, ks], v_ref[s, :, vs], glr,
                            w2_ref[h], b_ref[h], st_ref[s, h]))
            where.append((s, h))
    outs, sts = _gla_streams(streams, ts)
    for (s, h), o, st in zip(where, outs, sts):
        st_ref[s, h] = st
        o_ref[s, :, h * GLA_DV:(h + 1) * GLA_DV] = o


def _gla(z, glr, w2, bias, layer, batch, seq, cols, *, ts=512, nseq=2):
    h = GLA_HEADS
    nt = seq // ts
    q0, k0, v0 = cols
    wb = h * GLA_DK
    z3 = z.reshape(batch, seq, z.shape[1])
    zspec = lambda c0: pl.BlockSpec((nseq, ts, wb), lambda b, t: (b, t, c0 // wb))
    wspec = lambda last2: pl.BlockSpec((None, h) + last2, lambda b, t: (layer, 0, 0, 0))
    out = pl.pallas_call(
        functools.partial(_gla_body, ts=ts),
        grid=(batch // nseq, nt),
        in_specs=[
            zspec(q0), zspec(k0), zspec(v0), zspec(v0 + wb),
            pl.BlockSpec((nseq, ts, LANES), lambda b, t: (b, t, 0)),
            wspec((LANES, GLA_DK)), wspec((1, GLA_DK)),
        ],
        out_specs=pl.BlockSpec((nseq, ts, h * GLA_DV), lambda b, t: (b, t, 0)),
        out_shape=jax.ShapeDtypeStruct((batch, seq, h * GLA_DV), BF16),
        scratch_shapes=[pltpu.VMEM((nseq, h, GLA_DK, GLA_DV), F32)],
        compiler_params=_cparams(("parallel", "arbitrary")),
        name="gla",
    )(z3, z3, z3, z3, glr.reshape(batch, seq, LANES), w2, bias)
    return out.reshape(batch * seq, h * GLA_DV)


SSM_COL_GROUPS = LANES // SSM_GROUP
SSM_BLOCK_GROUPS = 2
SSM_TPAIR = 2


def _ssm_body(x_ref, kc_ref, pc_ref, qc_ref, ar_ref, ai_ref, o_ref, hp_ref, kst_ref, v_ref, *,
              chunks_per_seq, n_state_blocks):
    u = pl.program_id(0)
    ncol, rows = x_ref.shape[0], x_ref.shape[1]
    blocks_per_col = n_state_blocks // ncol
    npairs = SSM_CHUNK // SSM_TPAIR
    ns = 2 * SSM_STATE
    bw = SSM_BLOCK_GROUPS * ns
    lane_group = lax.broadcasted_iota(jnp.int32, (1, LANES), 1) // SSM_GROUP

    def p_tile(n):
        pc = pc_ref[n // blocks_per_col]
        row_group = (lax.broadcasted_iota(jnp.int32, (pc.shape[0], 1), 0) // SSM_GROUP) % SSM_COL_GROUPS
        g0 = (n % blocks_per_col) * SSM_BLOCK_GROUPS
        return jnp.concatenate([jnp.where(row_group == g0 + gg, pc, 0.0)
                                for gg in range(SSM_BLOCK_GROUPS)], axis=1).astype(BF16)

    @pl.when(u == 0)
    def _():
        row_group = lax.broadcasted_iota(jnp.int32, (LANES, 1), 0) // SSM_GROUP

        def kbd(c, lag):
            kc = kc_ref[c, lag]
            tiled = jnp.broadcast_to(kc[None], (SSM_COL_GROUPS,) + kc.shape).reshape(LANES, LANES)
            return jnp.where(row_group == lane_group, tiled, 0.0).astype(BF16)

        for c in range(ncol):
            for r in range(SSM_CHUNK):
                left = kbd(c, SSM_CHUNK - 2 - r) if r <= SSM_CHUNK - 2 else jnp.zeros((LANES, LANES), BF16)
                kst_ref[c, r * LANES:(r + 1) * LANES, :] = jnp.concatenate(
                    [left, kbd(c, SSM_CHUNK - 1 - r)], axis=1)
        v_ref[...] = _dot(x_ref[0], p_tile(0))

    @pl.when(u < n_state_blocks)
    def _():
        rowc = lax.broadcasted_iota(jnp.int32, (rows, 1), 0) % chunks_per_seq
        hst = v_ref[...]
        nxt = jnp.minimum(u + 1, n_state_blocks - 1)
        v_next = _dot(x_ref[nxt // blocks_per_col], p_tile(nxt))
        k, si = 1, 0
        while k < chunks_per_seq:
            hs = jnp.where(rowc >= k, pltpu.roll(hst, k, 0), 0.0)
            sw = jnp.concatenate([pltpu.roll(hs[:, b0:b0 + ns], SSM_STATE, 1)
                                  for b0 in range(0, bw, ns)], axis=1)
            hst = hst + ar_ref[u, si:si + 1, :] * hs + ai_ref[u, si:si + 1, :] * sw
            k *= 2
            si += 1
        hp_ref[u] = jnp.where(rowc >= 1, pltpu.roll(hst, 1, 0), 0.0).astype(BF16)
        v_ref[...] = v_next

    @pl.when(u >= n_state_blocks)
    def _():
        ob = u - n_state_blocks
        c = ob // npairs
        v = ob % npairs
        grp = lax.broadcasted_iota(jnp.int32, (SSM_COL_GROUPS, 1, 1), 0)
        q_tile = jnp.concatenate(
            [jnp.where(grp == lane_group[None], qc_ref[c, SSM_TPAIR * v + tt][None], 0.0)
             .reshape(SSM_COL_GROUPS * ns, LANES) for tt in range(SSM_TPAIR)], axis=1).astype(BF16)
        yq = _dot(hp_ref[c * blocks_per_col], q_tile[0:bw, :])
        for kk in range(1, blocks_per_col):
            yq = yq + _dot(hp_ref[c * blocks_per_col + kk], q_tile[kk * bw:(kk + 1) * bw, :])
        for vs in range(npairs):
            kext = (vs + 1) * SSM_TPAIR * LANES

            @pl.when(v == vs)
            def _():
                start = (npairs - 1 - vs) * SSM_TPAIR * LANES
                intra = _dot(x_ref[c, :, 0:kext], kst_ref[c, start:start + kext, :])
                o_ref[...] = (yq + intra).astype(BF16)


def _ssm_core(xcat, kc, pc, qc, ar, ai, layer, chunks_per_seq):
    ncol, rows, xw = xcat.shape
    nb = ar.shape[1]
    bw = ar.shape[3]
    ow = SSM_TPAIR * LANES
    nout = ncol * SSM_CHUNK // SSM_TPAIR
    one = pl.Buffered(1)
    whole = lambda a: pl.BlockSpec((None,) + a.shape[1:], lambda u: (layer,) + (0,) * (a.ndim - 1),
                                   pipeline_mode=one)
    return pl.pallas_call(
        functools.partial(_ssm_body, chunks_per_seq=chunks_per_seq, n_state_blocks=nb),
        grid=(nb + nout,),
        in_specs=[
            pl.BlockSpec((ncol, rows, xw), lambda u: (0, 0, 0), pipeline_mode=one),
            whole(kc), whole(pc), whole(qc), whole(ar), whole(ai),
        ],
        out_specs=pl.BlockSpec((None, rows, ow), lambda u: (jnp.maximum(u - nb, 0), 0, 0)),
        out_shape=jax.ShapeDtypeStruct((nout, rows, ow), BF16),
        scratch_shapes=[
            pltpu.VMEM((nb, rows, bw), BF16),
            pltpu.VMEM((ncol, SSM_CHUNK * LANES, ow), BF16),
            pltpu.VMEM((rows, bw), F32),
        ],
        compiler_params=_cparams(("arbitrary",)),
        name="ssm_core",
    )(xcat, kc, pc, qc, ar, ai)


def _ssm_operators(a_re, a_im, log_dt, b_re, b_im, c_re, c_im, chunks_per_seq):
    t = SSM_CHUNK
    g, p = a_re.shape
    hg = b_re.shape[-1]
    cgc = SSM_COL_GROUPS
    ncol = g // cgc
    bgc = SSM_BLOCK_GROUPS
    dt = jnp.exp(log_dt)[:, None]
    lam_re, lam_im = dt * a_re, dt * a_im

    def powers(n):
        n = jnp.asarray(n, F32)[:, None, None]
        mag = jnp.exp(n * lam_re)
        return mag * jnp.cos(n * lam_im), mag * jnp.sin(n * lam_im)

    ab_re, ab_im = powers([1.0])
    ab_re, ab_im = ab_re[0], ab_im[0]
    den = a_re * a_re + a_im * a_im
    f_re = ((ab_re - 1.0) * a_re + ab_im * a_im) / den
    f_im = (ab_im * a_re - (ab_re - 1.0) * a_im) / den
    bb_re = f_re[..., None] * b_re - f_im[..., None] * b_im
    bb_im = f_re[..., None] * b_im + f_im[..., None] * b_re
    pw_re, pw_im = powers(np.arange(t + 1))
    cp_re = c_re[None] * pw_re[:, :, None, :] - c_im[None] * pw_im[:, :, None, :]
    cp_im = c_re[None] * pw_im[:, :, None, :] + c_im[None] * pw_re[:, :, None, :]
    ktau = (jnp.einsum('tghp,gpk->tkgh', cp_re[:t], bb_re)
            - jnp.einsum('tghp,gpk->tkgh', cp_im[:t], bb_im))
    kc = ktau.reshape(t, hg, ncol, cgc * hg).transpose(2, 0, 1, 3)
    bt_re, bt_im = bb_re.transpose(0, 2, 1), bb_im.transpose(0, 2, 1)
    rp_re = pw_re[t - 1 - np.arange(t)][:, :, None, :]
    rp_im = pw_im[t - 1 - np.arange(t)][:, :, None, :]
    pin = jnp.concatenate([rp_re * bt_re[None] - rp_im * bt_im[None],
                           rp_re * bt_im[None] + rp_im * bt_re[None]], axis=3)
    pc = pin.reshape(t, ncol, cgc, hg, 2 * p).transpose(1, 0, 2, 3, 4).reshape(ncol, t * cgc * hg, 2 * p)
    ct_re, ct_im = c_re.transpose(2, 0, 1)[None], c_im.transpose(2, 0, 1)[None]
    pt_re = pw_re[1:].transpose(0, 2, 1)[..., None]
    pt_im = pw_im[1:].transpose(0, 2, 1)[..., None]
    qg = jnp.concatenate([ct_re * pt_re - ct_im * pt_im,
                          -(ct_re * pt_im + ct_im * pt_re)], axis=1)
    qc = qg.reshape(t, 2 * p, ncol, cgc * hg).transpose(2, 0, 1, 3)
    nblk = g // bgc
    nsteps = int(np.log2(chunks_per_seq))
    sr, si = powers(t * 2.0 ** np.arange(nsteps))
    blk = lambda a: a.reshape(nsteps, nblk, bgc * 2 * p).transpose(1, 0, 2)
    ar = blk(jnp.concatenate([sr, sr], axis=2))
    ai = blk(jnp.concatenate([-si, si], axis=2))
    return kc, pc, qc, ar, ai


def _merge_body(x_ref, pin_ref, halo_ref, og_ref, r0_ref, r1_ref, yc_ref, sin_ref, g0_ref, g1_ref, g2_ref,
                pw_ref, ps_ref, gain_ref, d_ref, wglu_ref, wbp_ref, wbg0_ref, wbg1_ref, wbs_ref, wo_ref,
                o_ref, ys_ref, *, tiles_per_seq):
    rows = yc_ref.shape[1]
    npairs = SSM_CHUNK // SSM_TPAIR
    for ob in range(yc_ref.shape[0]):
        c, v = divmod(ob, npairs)
        plane = yc_ref[ob].astype(F32)
        for tt in range(SSM_TPAIR):
            ys_ref[c, pl.ds(SSM_TPAIR * v + tt, rows, stride=SSM_CHUNK), :] = plane[:, tt * LANES:(tt + 1) * LANES]
    t = pl.program_id(0) % tiles_per_seq
    yp = _pool_mix(pin_ref[...].astype(F32), halo_ref[...].astype(F32), t, pw_ref, ps_ref[...])
    m = g0_ref[...].astype(F32) * _dot(yp.astype(BF16), wbp_ref[...])
    r_refs = (r0_ref, r1_ref)
    heads_per_block = r0_ref.shape[1] // GLA_DV
    yg_heads = []
    for h in range(GLA_HEADS):
        hs = slice(h * GLA_DV, (h + 1) * GLA_DV)
        rs = slice((h % heads_per_block) * GLA_DV, (h % heads_per_block + 1) * GLA_DV)
        o = og_ref[:, hs].astype(F32)
        o = o * lax.rsqrt(jnp.mean(o * o, axis=-1, keepdims=True) + EPS) * gain_ref[:, hs]
        r = r_refs[h // heads_per_block][:, rs].astype(F32)
        yg_heads.append((o * (r * _sigmoid(r))).astype(BF16))
    yg = jnp.concatenate(yg_heads, axis=1)
    half = wbg0_ref.shape[0]
    gla_proj = _dot(yg[:, :half], wbg0_ref[...]) + _dot(yg[:, half:], wbg1_ref[...])
    m = m + g1_ref[...].astype(F32) * gla_proj
    ys = jnp.concatenate([ys_ref[c] for c in range(ys_ref.shape[0])], axis=1)
    y = ys + d_ref[...] * sin_ref[...].astype(F32)
    y = jax.nn.gelu(y)
    y = y * _sigmoid(_dot(y.astype(BF16), wglu_ref[...]))
    m = m + g2_ref[...].astype(F32) * _dot(y.astype(BF16), wbs_ref[...])
    o_ref[...] = x_ref[...] + _dot(m.astype(BF16), wo_ref[...])


def _merge(x, o_gla, ycat, z, r_col0, ssm_col0, gate_col0, seq, pool_w, pool_scale, gla_gain,
           d_skip, w_glu, wb, w_out, layer, *, tm=256):
    m, d = x.shape
    wp, wg, ws = pool_scale.shape[2], o_gla.shape[1], d_skip.shape[2]
    br = BRANCH_ROWS
    assert wp == br and ws == br and wg == 2 * br and seq % tm == 0 and tm % POOL_HALO == 0
    gb = gate_col0 // d
    hb = tm // POOL_HALO
    one = pl.Buffered(1)
    wspec = lambda shape, r: pl.BlockSpec((None,) + shape, lambda i: (layer, r, 0), pipeline_mode=one)
    return pl.pallas_call(
        functools.partial(_merge_body, tiles_per_seq=seq // tm),
        grid=(m // tm,),
        in_specs=[
            pl.BlockSpec((tm, d), lambda i: (i, 0)),
            pl.BlockSpec((tm, wp), lambda i: (i, 0)),
            pl.BlockSpec((POOL_HALO, wp), lambda i: (jnp.maximum(i * hb - 1, 0), 0)),
            pl.BlockSpec((tm, wg), lambda i: (i, 0)),
            pl.BlockSpec((tm, br), lambda i: (i, r_col0 // br)),
            pl.BlockSpec((tm, br), lambda i: (i, r_col0 // br + 1)),
            pl.BlockSpec((ycat.shape[0], tm // SSM_CHUNK, ycat.shape[2]), lambda i: (0, i, 0)),
            pl.BlockSpec((tm, ws), lambda i: (i, ssm_col0 // ws)),
            pl.BlockSpec((tm, d), lambda i: (i, gb)),
            pl.BlockSpec((tm, d), lambda i: (i, gb + 1)),
            pl.BlockSpec((tm, d), lambda i: (i, gb + 2)),
            pl.BlockSpec((None, len(POOL_WINDOWS), POOL_GROUP, POOL_GROUP), lambda i: (layer, 0, 0, 0),
                         pipeline_mode=one),
            wspec((1, wp), 0),
            wspec((1, wg), 0),
            wspec((1, ws), 0),
            wspec((ws, ws), 0),
            wspec((br, d), 0),
            wspec((br, d), 1),
            wspec((br, d), 2),
            wspec((br, d), 3),
            wspec((d, d), 0),
        ],
        out_specs=pl.BlockSpec((tm, d), lambda i: (i, 0)),
        out_shape=jax.ShapeDtypeStruct((m, d), F32),
        scratch_shapes=[pltpu.VMEM((ws // LANES, tm, LANES), F32)],
        compiler_params=_cparams(("parallel",)),
        name="merge",
    )(x, z, z, o_gla, z, z, ycat, z, z, z, z, pool_w, pool_scale, gla_gain, d_skip, w_glu,
      wb, wb, wb, wb, w_out)


def kernel(x, ffn1_norm, ffn1_w_gate, ffn1_w_up, ffn1_w_down, mix_norm, w_in, pool_w, pool_scale, gla_w_gate2, gla_gate_bias, gla_norm, ssm_a_re, ssm_a_im, ssm_log_dt, ssm_b_re, ssm_b_im, ssm_c_re, ssm_c_im, ssm_d, ssm_w_glu, w_branch, w_out, ffn2_norm, ffn2_w_gate, ffn2_w_up, ffn2_w_down, final_norm):
    batch, seq, d = x.shape
    depth = w_in.shape[0]
    m = batch * seq
    pool_width = pool_scale.shape[1]
    qk_width = GLA_HEADS * GLA_DK
    v_width = GLA_HEADS * GLA_DV
    ssm_width = ssm_d.shape[1]
    n_groups = ssm_width // SSM_GROUP
    q0 = pool_width
    k0 = q0 + qk_width
    v0 = k0 + qk_width
    r0 = v0 + v_width
    glr0 = r0 + v_width
    ssm_src0 = glr0 + GLA_RANK
    ssm0 = glr0
    gate0 = ssm0 + ssm_width
    chunks_per_seq = seq // SSM_CHUNK

    bf = lambda a: a.astype(BF16)
    row3 = lambda a: a.reshape(depth, 1, a.shape[-1])
    w_in_t = jnp.swapaxes(w_in, 1, 2)
    ffn1_w = (ffn1_w_gate, ffn1_w_up, ffn1_w_down)
    ffn2_w = (ffn2_w_gate, ffn2_w_up, ffn2_w_down)
    f1 = [bf(w[0]) for w in ffn1_w]
    wbr, wo, wglu, pw = bf(w_branch), bf(w_out), bf(ssm_w_glu), bf(pool_w)
    n1, nm, n2 = row3(ffn1_norm), row3(mix_norm), row3(ffn2_norm)
    pscale, dskip = row3(pool_scale), row3(ssm_d)
    w2 = gla_w_gate2.reshape(depth, GLA_RANK, GLA_HEADS, GLA_DK).transpose(0, 2, 1, 3)
    w2 = bf(jnp.pad(w2, ((0, 0), (0, 0), (0, LANES - GLA_RANK), (0, 0))))
    gbias = gla_gate_bias.reshape(depth, GLA_HEADS, 1, GLA_DK)
    ggain = row3(gla_norm)
    ssm_ops = jax.vmap(functools.partial(_ssm_operators, chunks_per_seq=chunks_per_seq))(
        ssm_a_re, ssm_a_im, ssm_log_dt, ssm_b_re, ssm_b_im, ssm_c_re, ssm_c_im)

    xf = x.reshape(m, d)
    for l in range(depth):
        xf, *side = _ffn(xf, n1, l, *f1, cast=(l,) + ffn2_w, repack=(l, w_in_t, glr0, GLA_RANK))
        f2, (w_main, w_glr) = side[:3], side[3:]
        z, glr, xcat = _inproj(xf, nm, w_main, w_glr, l, ssm0, ssm_width, gate0)
        o_gla = _gla(z, glr, w2, gbias, l, batch, seq, (q0, k0, v0))
        ycat = _ssm_core(xcat, *ssm_ops, l, chunks_per_seq)
        xf = _merge(xf, o_gla, ycat, z, r0, ssm0, gate0, seq, pw, pscale, ggain, dskip, wglu, wbr, wo, l)
        last = l == depth - 1
        xf, *f1 = _ffn(xf, n2, l, *f2, cast=None if last else (l + 1,) + ffn1_w,
                       final_g=final_norm if last else None)
    return xf.reshape(batch, seq, d)
```

```python
import functools

import jax
import jax.numpy as jnp
import numpy as np
from jax import lax
from jax.experimental import pallas as pl
from jax.experimental.pallas import tpu as pltpu

F32 = jnp.float32
BF16 = jnp.bfloat16

EPS = 1e-6
POOL_WINDOWS = (2, 4, 8, 16)
POOL_GROUP = 128
POOL_HALO = 16
GLA_HEADS = 4
GLA_DK = 128
GLA_DV = 256
GLA_RANK = 16
GLA_TAU = 16.0
LOG2_E = 1.4426950408889634
GLA_CHUNK = 64
SSM_GROUP = 16
SSM_STATE = 64
SSM_CHUNK = 16
LANES = 128
FFN_DOWN_CHUNK = 512
FFN_FIRST_ROWS = 256
REPACK_ROWS = 64
INPROJ_FIRST_ROWS = 256
BRANCH_ROWS = 512

VMEM_LIMIT = 60 * 1024 * 1024


def _cparams(sem):
    return pltpu.CompilerParams(dimension_semantics=sem, vmem_limit_bytes=VMEM_LIMIT)


def _rms(xf, g):
    return xf * lax.rsqrt(jnp.mean(xf * xf, axis=-1, keepdims=True) + EPS) * g


def _dot(a, b):
    return jnp.dot(a, b, preferred_element_type=F32)


def _sigmoid(a):
    return 0.5 * jnp.tanh(0.5 * a) + 0.5


def _dot_nt(a, b_t):
    return lax.dot_general(a, b_t, (((1,), (1,)), ((), ())), preferred_element_type=F32)


def _ffn_body(x_ref, g_ref, wg_ref, wu_ref, wd_ref, *rest, final, n_cast, repack):
    rest = list(rest)
    cast_in = [rest.pop(0) for _ in range(n_cast)]
    rp_in = [rest.pop(0) for _ in range(2 if repack else 0)]
    fg_ref = rest.pop(0) if final else None
    o_ref = rest.pop(0)
    cast_out = [rest.pop(0) for _ in range(n_cast)]
    rp_out = [rest.pop(0) for _ in range(2 if repack else 0)]
    xn_ref, = rest
    j = pl.program_id(1)

    for src, dst in zip(cast_in, cast_out):
        dst[...] = src[...].astype(BF16)
    if repack:
        (rows_ref, drop_ref), (w_ref, wl_ref) = rp_in, rp_out
        w_ref[...] = rows_ref[0].astype(BF16)
        pad = jnp.zeros((wl_ref.shape[0] - drop_ref.shape[1], drop_ref.shape[2]), F32)
        wl_ref[...] = jnp.concatenate([drop_ref[0], pad], axis=0).astype(BF16)

    def swiglu_step(rs, xn, first):
        a = _dot(xn, wg_ref[...])
        b = _dot(xn, wu_ref[...])
        h = (a * _sigmoid(a) * b).astype(BF16)
        for c0 in range(0, o_ref.shape[1], FFN_DOWN_CHUNK):
            cs = slice(c0, c0 + FFN_DOWN_CHUNK)
            if first:
                o_ref[rs, cs] = _dot(h, wd_ref[:, cs])
            else:
                o_ref[rs, cs] += _dot(h, wd_ref[:, cs])

    @pl.when(j == 0)
    def _():
        for r0 in range(0, o_ref.shape[0], FFN_FIRST_ROWS):
            rs = slice(r0, r0 + FFN_FIRST_ROWS)
            xn = _rms(x_ref[rs, :], g_ref[...]).astype(BF16)
            xn_ref[rs, :] = xn
            swiglu_step(rs, xn, True)

    last = pl.num_programs(1) - 1

    @pl.when((j > 0) & (j < last))
    def _():
        swiglu_step(slice(None), xn_ref[...], False)

    @pl.when(j == last)
    def _():
        for r0 in range(0, o_ref.shape[0], FFN_FIRST_ROWS):
            rs = slice(r0, r0 + FFN_FIRST_ROWS)
            swiglu_step(rs, xn_ref[rs, :], False)
            y = x_ref[rs, :] + 0.5 * o_ref[rs, :]
            if final:
                y = _rms(y, fg_ref[...])
            o_ref[rs, :] = y


def _ffn(x, g, layer, wg, wu, wd, cast=None, repack=None, final_g=None, *, tm=1024, tf=512):
    m, d = x.shape
    f = wg.shape[1]
    ni, nj = m // tm, f // tf
    final = final_g is not None
    in_specs = [
        pl.BlockSpec((tm, d), lambda i, j: (i, 0)),
        pl.BlockSpec((None, 1, d), lambda i, j: (layer, 0, 0)),
        pl.BlockSpec((d, tf), lambda i, j: (0, j)),
        pl.BlockSpec((d, tf), lambda i, j: (0, j)),
        pl.BlockSpec((tf, d), lambda i, j: (j, 0)),
    ]
    args = [x, g, wg, wu, wd]
    out_specs = [pl.BlockSpec((tm, d), lambda i, j: (i, 0))]
    out_shape = [jax.ShapeDtypeStruct((m, d), F32)]
    n_cast = 0
    if cast is not None:
        cl, cg, cu, cd = cast
        n_cast = 3
        rows_up = d // ni
        rows_dn = f // (ni * nj)
        assert d % ni == 0 and f % (ni * nj) == 0 and rows_up % 16 == 0 and rows_dn % 16 == 0
        for w in (cg, cu):
            in_specs.append(pl.BlockSpec((None, rows_up, tf), lambda i, j: (cl, i, j)))
            out_specs.append(pl.BlockSpec((rows_up, tf), lambda i, j: (i, j)))
            out_shape.append(jax.ShapeDtypeStruct((d, f), BF16))
            args.append(w)
        in_specs.append(pl.BlockSpec((None, rows_dn, d), lambda i, j: (cl, i * nj + j, 0)))
        out_specs.append(pl.BlockSpec((rows_dn, d), lambda i, j: (i * nj + j, 0)))
        out_shape.append(jax.ShapeDtypeStruct((f, d), BF16))
        args.append(cd)
    if repack is not None:
        rl, w_in_t, drop0, drop_width = repack
        n = w_in_t.shape[1] - drop_width
        rb = REPACK_ROWS
        nblk = n // rb
        assert n % rb == 0 and nblk <= ni * nj and drop0 % rb == 0 and drop_width % 8 == 0
        blk = lambda i, j: jnp.minimum(i * nj + j, nblk - 1)
        src_row = lambda b: (b * (rb // 8) + jnp.where(b >= drop0 // rb, drop_width // 8, 0)) * 8
        in_specs.append(pl.BlockSpec((pl.Element(1), pl.Element(rb), pl.Element(d)),
                                     lambda i, j: (rl, src_row(blk(i, j)), 0)))
        in_specs.append(pl.BlockSpec((pl.Element(1), pl.Element(drop_width), pl.Element(d)),
                                     lambda i, j: (rl, drop0, 0)))
        out_specs.append(pl.BlockSpec((rb, d), lambda i, j: (blk(i, j), 0)))
        out_specs.append(pl.BlockSpec((LANES, d), lambda i, j: (0, 0)))
        out_shape.append(jax.ShapeDtypeStruct((n, d), BF16))
        out_shape.append(jax.ShapeDtypeStruct((LANES, d), BF16))
        args += [w_in_t, w_in_t]
    if final:
        in_specs.append(pl.BlockSpec((1, d), lambda i, j: (0, 0)))
        args.append(final_g.reshape(1, d))
    return pl.pallas_call(
        functools.partial(_ffn_body, final=final, n_cast=n_cast, repack=repack is not None),
        grid=(ni, nj),
        in_specs=in_specs,
        out_specs=out_specs,
        out_shape=out_shape,
        scratch_shapes=[pltpu.VMEM((tm, d), BF16)],
        compiler_params=_cparams(("arbitrary", "arbitrary")),
        name="ffn_final" if final else "ffn",
    )(*args)


def _inproj_body(x_ref, g_ref, w_ref, wl_ref, z_ref, glr_ref, xc_ref, un_ref, ph_ref, *,
                 gate_block0, ssm_block, ssm_off):
    j = pl.program_id(1)

    @pl.when(j == 0)
    def _():
        for r0 in range(0, z_ref.shape[0], INPROJ_FIRST_ROWS):
            rs = slice(r0, r0 + INPROJ_FIRST_ROWS)
            un = _rms(x_ref[rs, :], g_ref[...]).astype(BF16)
            un_ref[rs, :] = un
            glr_ref[rs, :] = _dot_nt(un, wl_ref[...])
            z_ref[rs, :] = _dot_nt(un, w_ref[...]).astype(BF16)

    @pl.when(j >= gate_block0)
    def _():
        z_ref[...] = _sigmoid(_dot_nt(un_ref[...], w_ref[...])).astype(BF16)

    @pl.when((j > 0) & (j < gate_block0))
    def _():
        acc = _dot_nt(un_ref[...], w_ref[...])
        z_ref[...] = acc.astype(BF16)

        @pl.when(j == ssm_block)
        def _():
            ncol = ph_ref.shape[0]
            for c in range(ncol):
                ph_ref[c] = acc[:, ssm_off + c * LANES:ssm_off + (c + 1) * LANES]
            rows = ph_ref.shape[1] // SSM_CHUNK
            for s in range(SSM_CHUNK):
                for c in range(ncol):
                    piece = ph_ref[c, pl.ds(s, rows, stride=SSM_CHUNK), :].astype(BF16)
                    xc_ref[c, :, s * LANES:(s + 1) * LANES] = piece


def _inproj(x, g, w_main, w_glr, layer, ssm_col0, ssm_width, gate_col0, *, tm=1024, tn=1024):
    m, d = x.shape
    n = w_main.shape[0]
    assert gate_col0 >= tn and tn <= ssm_col0 < gate_col0
    body = functools.partial(_inproj_body, gate_block0=gate_col0 // tn,
                             ssm_block=ssm_col0 // tn, ssm_off=ssm_col0 % tn)
    return pl.pallas_call(
        body,
        grid=(m // tm, n // tn),
        in_specs=[
            pl.BlockSpec((tm, d), lambda i, j: (i, 0)),
            pl.BlockSpec((None, 1, d), lambda i, j: (layer, 0, 0)),
            pl.BlockSpec((tn, d), lambda i, j: (j, 0)),
            pl.BlockSpec((LANES, d), lambda i, j: (0, 0)),
        ],
        out_specs=[
            pl.BlockSpec((tm, tn), lambda i, j: (i, j)),
            pl.BlockSpec((tm, LANES), lambda i, j: (i, 0)),
            pl.BlockSpec((ssm_width // LANES, tm // SSM_CHUNK, SSM_CHUNK * LANES), lambda i, j: (0, i, 0)),
        ],
        out_shape=[
            jax.ShapeDtypeStruct((m, n), BF16),
            jax.ShapeDtypeStruct((m, LANES), F32),
            jax.ShapeDtypeStruct((ssm_width // LANES, m // SSM_CHUNK, SSM_CHUNK * LANES), BF16),
        ],
        scratch_shapes=[pltpu.VMEM((tm, d), BF16), pltpu.VMEM((ssm_width // LANES, tm, LANES), F32)],
        compiler_params=_cparams(("parallel", "arbitrary")),
        name="inproj",
    )(x, g, w_main, w_glr)


def _pool_mix(x, halo, t, w_ref, scale):
    ts = x.shape[0]
    halo = jnp.where(t > 0, halo, 0.0)
    e = jnp.concatenate([halo, x], axis=0)
    s2 = e + pltpu.roll(e, 1, 0)
    s4 = s2 + pltpu.roll(s2, 2, 0)
    s8 = s4 + pltpu.roll(s4, 4, 0)
    s16 = s8 + pltpu.roll(s8, 8, 0)
    pos = (t * ts + 1 + lax.broadcasted_iota(jnp.int32, (ts, 1), 0)).astype(F32)
    outs = []
    for gi, (w, s) in enumerate(zip(POOL_WINDOWS, (s2, s4, s8, s16))):
        cols = slice(gi * POOL_GROUP, (gi + 1) * POOL_GROUP)
        mean = s[POOL_HALO:, cols] / jnp.minimum(pos, float(w))
        outs.append(_dot((mean - x[:, cols]).astype(BF16), w_ref[gi]))
    return jnp.concatenate(outs, axis=1) * scale


def _gla_streams(streams, ts):
    c = GLA_CHUNK
    nc = ts // c
    rowc = lax.broadcasted_iota(jnp.int32, (ts, 1), 0) % c
    causal = (lax.broadcasted_iota(jnp.int32, (c, c), 0)
              >= lax.broadcasted_iota(jnp.int32, (c, c), 1))
    qs, ks, vs, glrs, w2s, biases, sts = (list(a) for a in zip(*streams))
    logits = [_dot(glr, w2) + bias for glr, w2, bias in zip(glrs, w2s, biases)]
    bcs = [(jnp.minimum(l, 0.0) - jnp.log1p(jnp.exp(-jnp.abs(l)))) * (LOG2_E / GLA_TAU) for l in logits]
    step = 1
    while step < c:
        bcs = [bc + jnp.where(rowc >= step, pltpu.roll(bc, step, 0), 0.0) for bc in bcs]
        step *= 2
    bc3 = [bc.reshape(nc, c, GLA_DK) for bc in bcs]
    b_last = [b[:, c - 1:c, :] for b in bc3]
    b_mid = [b[:, c // 2 - 1:c // 2, :] for b in bc3]
    q3 = [q.astype(F32).reshape(nc, c, GLA_DK) * (GLA_DK ** -0.5) for q in qs]
    k3 = [k.astype(F32).reshape(nc, c, GLA_DK) for k in ks]
    v3 = [v.reshape(nc, c, GLA_DV) for v in vs]
    qe = [(q * jnp.exp2(b - m)).astype(BF16) for q, b, m in zip(q3, bc3, b_mid)]
    ke = [(k * jnp.exp2(m - b)).astype(BF16) for k, b, m in zip(k3, bc3, b_mid)]
    scores = [jnp.einsum('nid,njd->nij', a, b, preferred_element_type=F32) for a, b in zip(qe, ke)]
    scores = [jnp.where(causal[None], s, 0.0).astype(BF16) for s in scores]
    o_intra = [jnp.einsum('nij,njv->niv', s, v, preferred_element_type=F32) for s, v in zip(scores, v3)]
    qd = [(q * jnp.exp2(b)).astype(BF16) for q, b in zip(q3, bc3)]
    kd = [(k * jnp.exp2(bl - b)).astype(BF16) for k, b, bl in zip(k3, bc3, b_last)]
    dec_t = [jnp.transpose(jnp.exp2(bl.reshape(nc, GLA_DK))) for bl in b_last]
    outs = [[] for _ in streams]
    for n in range(nc):
        for i in range(len(streams)):
            outs[i].append(o_intra[i][n] + _dot(qd[i][n], sts[i].astype(BF16)))
            kv = lax.dot_general(kd[i][n], v3[i][n], (((0,), (0,)), ((), ())), preferred_element_type=F32)
            sts[i] = dec_t[i][:, n:n + 1] * sts[i] + kv
    return [jnp.concatenate(o, axis=0).astype(BF16) for o in outs], sts


def _gla_body(q_ref, k_ref, v0_ref, v1_ref, glr_ref, w2_ref, b_ref, o_ref, st_ref, *, ts):
    @pl.when(pl.program_id(1) == 0)
    def _():
        st_ref[...] = jnp.zeros_like(st_ref)

    v_refs = (v0_ref, v1_ref)
    heads_per_block = v0_ref.shape[2] // GLA_DV
    streams, where = [], []
    for s in range(q_ref.shape[0]):
        glr = glr_ref[s].astype(BF16)
        for h in range(GLA_HEADS):
            v_ref = v_refs[h // heads_per_block]
            vs = slice((h % heads_per_block) * GLA_DV, (h % heads_per_block + 1) * GLA_DV)
            ks = slice(h * GLA_DK, (h + 1) * GLA_DK)
            streams.append((q_ref[s, :, ks], k_ref[s, :, ks], v_ref[s, :, vs], glr,
                            w2_ref[h], b_ref[h], st_ref[s, h]))
            where.append((s, h))
    outs, sts = _gla_streams(streams, ts)
    for (s, h), o, st in zip(where, outs, sts):
        st_ref[s, h] = st
        o_ref[s, :, h * GLA_DV:(h + 1) * GLA_DV] = o


def _gla(z, glr, w2, bias, layer, batch, seq, cols, *, ts=512, nseq=2):
    h = GLA_HEADS
    nt = seq // ts
    q0, k0, v0 = cols
    wb = h * GLA_DK
    z3 = z.reshape(batch, seq, z.shape[1])
    zspec = lambda c0: pl.BlockSpec((nseq, ts, wb), lambda b, t: (b, t, c0 // wb))
    wspec = lambda last2: pl.BlockSpec((None, h) + last2, lambda b, t: (layer, 0, 0, 0))
    out = pl.pallas_call(
        functools.partial(_gla_body, ts=ts),
        grid=(batch // nseq, nt),
        in_specs=[
            zspec(q0), zspec(k0), zspec(v0), zspec(v0 + wb),
            pl.BlockSpec((nseq, ts, LANES), lambda b, t: (b, t, 0)),
            wspec((LANES, GLA_DK)), wspec((1, GLA_DK)),
        ],
        out_specs=pl.BlockSpec((nseq, ts, h * GLA_DV), lambda b, t: (b, t, 0)),
        out_shape=jax.ShapeDtypeStruct((batch, seq, h * GLA_DV), BF16),
        scratch_shapes=[pltpu.VMEM((nseq, h, GLA_DK, GLA_DV), F32)],
        compiler_params=_cparams(("parallel", "arbitrary")),
        name="gla",
    )(z3, z3, z3, z3, glr.reshape(batch, seq, LANES), w2, bias)
    return out.reshape(batch * seq, h * GLA_DV)


SSM_COL_GROUPS = LANES // SSM_GROUP
SSM_BLOCK_GROUPS = 2
SSM_TPAIR = 2


def _ssm_body(x_ref, kc_ref, pc_ref, qc_ref, ar_ref, ai_ref, o_ref, hp_ref, kst_ref, v_ref, *,
              chunks_per_seq, n_state_blocks):
    u = pl.program_id(0)
    ncol, rows = x_ref.shape[0], x_ref.shape[1]
    blocks_per_col = n_state_blocks // ncol
    npairs = SSM_CHUNK // SSM_TPAIR
    ns = 2 * SSM_STATE
    bw = SSM_BLOCK_GROUPS * ns
    lane_group = lax.broadcasted_iota(jnp.int32, (1, LANES), 1) // SSM_GROUP

    def p_tile(n):
        pc = pc_ref[n // blocks_per_col]
        row_group = (lax.broadcasted_iota(jnp.int32, (pc.shape[0], 1), 0) // SSM_GROUP) % SSM_COL_GROUPS
        g0 = (n % blocks_per_col) * SSM_BLOCK_GROUPS
        return jnp.concatenate([jnp.where(row_group == g0 + gg, pc, 0.0)
                                for gg in range(SSM_BLOCK_GROUPS)], axis=1).astype(BF16)

    @pl.when(u == 0)
    def _():
        row_group = lax.broadcasted_iota(jnp.int32, (LANES, 1), 0) // SSM_GROUP

        def kbd(c, lag):
            kc = kc_ref[c, lag]
            tiled = jnp.broadcast_to(kc[None], (SSM_COL_GROUPS,) + kc.shape).reshape(LANES, LANES)
            return jnp.where(row_group == lane_group, tiled, 0.0).astype(BF16)

        for c in range(ncol):
            for r in range(SSM_CHUNK):
                left = kbd(c, SSM_CHUNK - 2 - r) if r <= SSM_CHUNK - 2 else jnp.zeros((LANES, LANES), BF16)
                kst_ref[c, r * LANES:(r + 1) * LANES, :] = jnp.concatenate(
                    [left, kbd(c, SSM_CHUNK - 1 - r)], axis=1)
        v_ref[...] = _dot(x_ref[0], p_tile(0))

    @pl.when(u < n_state_blocks)
    def _():
        rowc = lax.broadcasted_iota(jnp.int32, (rows, 1), 0) % chunks_per_seq
        hst = v_ref[...]
        nxt = jnp.minimum(u + 1, n_state_blocks - 1)
        v_next = _dot(x_ref[nxt // blocks_per_col], p_tile(nxt))
        k, si = 1, 0
        while k < chunks_per_seq:
            hs = jnp.where(rowc >= k, pltpu.roll(hst, k, 0), 0.0)
            sw = jnp.concatenate([pltpu.roll(hs[:, b0:b0 + ns], SSM_STATE, 1)
                                  for b0 in range(0, bw, ns)], axis=1)
            hst = hst + ar_ref[u, si:si + 1, :] * hs + ai_ref[u, si:si + 1, :] * sw
            k *= 2
            si += 1
        hp_ref[u] = jnp.where(rowc >= 1, pltpu.roll(hst, 1, 0), 0.0).astype(BF16)
        v_ref[...] = v_next

    @pl.when(u >= n_state_blocks)
    def _():
        ob = u - n_state_blocks
        c = ob // npairs
        v = ob % npairs
        grp = lax.broadcasted_iota(jnp.int32, (SSM_COL_GROUPS, 1, 1), 0)
        q_tile = jnp.concatenate(
            [jnp.where(grp == lane_group[None], qc_ref[c, SSM_TPAIR * v + tt][None], 0.0)
             .reshape(SSM_COL_GROUPS * ns, LANES) for tt in range(SSM_TPAIR)], axis=1).astype(BF16)
        yq = _dot(hp_ref[c * blocks_per_col], q_tile[0:bw, :])
        for kk in range(1, blocks_per_col):
            yq = yq + _dot(hp_ref[c * blocks_per_col + kk], q_tile[kk * bw:(kk + 1) * bw, :])
        for vs in range(npairs):
            kext = (vs + 1) * SSM_TPAIR * LANES

            @pl.when(v == vs)
            def _():
                start = (npairs - 1 - vs) * SSM_TPAIR * LANES
                intra = _dot(x_ref[c, :, 0:kext], kst_ref[c, start:start + kext, :])
                o_ref[...] = (yq + intra).astype(BF16)


def _ssm_core(xcat, kc, pc, qc, ar, ai, layer, chunks_per_seq):
    ncol, rows, xw = xcat.shape
    nb = ar.shape[1]
    bw = ar.shape[3]
    ow = SSM_TPAIR * LANES
    nout = ncol * SSM_CHUNK // SSM_TPAIR
    one = pl.Buffered(1)
    whole = lambda a: pl.BlockSpec((None,) + a.shape[1:], lambda u: (layer,) + (0,) * (a.ndim - 1),
                                   pipeline_mode=one)
    return pl.pallas_call(
        functools.partial(_ssm_body, chunks_per_seq=chunks_per_seq, n_state_blocks=nb),
        grid=(nb + nout,),
        in_specs=[
            pl.BlockSpec((ncol, rows, xw), lambda u: (0, 0, 0), pipeline_mode=one),
            whole(kc), whole(pc), whole(qc), whole(ar), whole(ai),
        ],
        out_specs=pl.BlockSpec((None, rows, ow), lambda u: (jnp.maximum(u - nb, 0), 0, 0)),
        out_shape=jax.ShapeDtypeStruct((nout, rows, ow), BF16),
        scratch_shapes=[
            pltpu.VMEM((nb, rows, bw), BF16),
            pltpu.VMEM((ncol, SSM_CHUNK * LANES, ow), BF16),
            pltpu.VMEM((rows, bw), F32),
        ],
        compiler_params=_cparams(("arbitrary",)),
        name="ssm_core",
    )(xcat, kc, pc, qc, ar, ai)


def _ssm_operators(a_re, a_im, log_dt, b_re, b_im, c_re, c_im, chunks_per_seq):
    t = SSM_CHUNK
    g, p = a_re.shape
    hg = b_re.shape[-1]
    cgc = SSM_COL_GROUPS
    ncol = g // cgc
    bgc = SSM_BLOCK_GROUPS
    dt = jnp.exp(log_dt)[:, None]
    lam_re, lam_im = dt * a_re, dt * a_im

    def powers(n):
        n = jnp.asarray(n, F32)[:, None, None]
        mag = jnp.exp(n * lam_re)
        return mag * jnp.cos(n * lam_im), mag * jnp.sin(n * lam_im)

    ab_re, ab_im = powers([1.0])
    ab_re, ab_im = ab_re[0], ab_im[0]
    den = a_re * a_re + a_im * a_im
    f_re = ((ab_re - 1.0) * a_re + ab_im * a_im) / den
    f_im = (ab_im * a_re - (ab_re - 1.0) * a_im) / den
    bb_re = f_re[..., None] * b_re - f_im[..., None] * b_im
    bb_im = f_re[..., None] * b_im + f_im[..., None] * b_re
    pw_re, pw_im = powers(np.arange(t + 1))
    cp_re = c_re[None] * pw_re[:, :, None, :] - c_im[None] * pw_im[:, :, None, :]
    cp_im = c_re[None] * pw_im[:, :, None, :] + c_im[None] * pw_re[:, :, None, :]
    ktau = (jnp.einsum('tghp,gpk->tkgh', cp_re[:t], bb_re)
            - jnp.einsum('tghp,gpk->tkgh', cp_im[:t], bb_im))
    kc = ktau.reshape(t, hg, ncol, cgc * hg).transpose(2, 0, 1, 3)
    bt_re, bt_im = bb_re.transpose(0, 2, 1), bb_im.transpose(0, 2, 1)
    rp_re = pw_re[t - 1 - np.arange(t)][:, :, None, :]
    rp_im = pw_im[t - 1 - np.arange(t)][:, :, None, :]
    pin = jnp.concatenate([rp_re * bt_re[None] - rp_im * bt_im[None],
                           rp_re * bt_im[None] + rp_im * bt_re[None]], axis=3)
    pc = pin.reshape(t, ncol, cgc, hg, 2 * p).transpose(1, 0, 2, 3, 4).reshape(ncol, t * cgc * hg, 2 * p)
    ct_re, ct_im = c_re.transpose(2, 0, 1)[None], c_im.transpose(2, 0, 1)[None]
    pt_re = pw_re[1:].transpose(0, 2, 1)[..., None]
    pt_im = pw_im[1:].transpose(0, 2, 1)[..., None]
    qg = jnp.concatenate([ct_re * pt_re - ct_im * pt_im,
                          -(ct_re * pt_im + ct_im * pt_re)], axis=1)
    qc = qg.reshape(t, 2 * p, ncol, cgc * hg).transpose(2, 0, 1, 3)
    nblk = g // bgc
    nsteps = int(np.log2(chunks_per_seq))
    sr, si = powers(t * 2.0 ** np.arange(nsteps))
    blk = lambda a: a.reshape(nsteps, nblk, bgc * 2 * p).transpose(1, 0, 2)
    ar = blk(jnp.concatenate([sr, sr], axis=2))
    ai = blk(jnp.concatenate([-si, si], axis=2))
    return kc, pc, qc, ar, ai


def _merge_body(x_ref, pin_ref, halo_ref, og_ref, r0_ref, r1_ref, yc_ref, sin_ref, g0_ref, g1_ref, g2_ref,
                pw_ref, ps_ref, gain_ref, d_ref, wglu_ref, wbp_ref, wbg0_ref, wbg1_ref, wbs_ref, wo_ref,
                o_ref, ys_ref, *, tiles_per_seq):
    rows = yc_ref.shape[1]
    npairs = SSM_CHUNK // SSM_TPAIR
    for ob in range(yc_ref.shape[0]):
        c, v = divmod(ob, npairs)
        plane = yc_ref[ob].astype(F32)
        for tt in range(SSM_TPAIR):
            ys_ref[c, pl.ds(SSM_TPAIR * v + tt, rows, stride=SSM_CHUNK), :] = plane[:, tt * LANES:(tt + 1) * LANES]
    t = pl.program_id(0) % tiles_per_seq
    yp = _pool_mix(pin_ref[...].astype(F32), halo_ref[...].astype(F32), t, pw_ref, ps_ref[...])
    m = g0_ref[...].astype(F32) * _dot(yp.astype(BF16), wbp_ref[...])
    r_refs = (r0_ref, r1_ref)
    heads_per_block = r0_ref.shape[1] // GLA_DV
    yg_heads = []
    for h in range(GLA_HEADS):
        hs = slice(h * GLA_DV, (h + 1) * GLA_DV)
        rs = slice((h % heads_per_block) * GLA_DV, (h % heads_per_block + 1) * GLA_DV)
        o = og_ref[:, hs].astype(F32)
        o = o * lax.rsqrt(jnp.mean(o * o, axis=-1, keepdims=True) + EPS) * gain_ref[:, hs]
        r = r_refs[h // heads_per_block][:, rs].astype(F32)
        yg_heads.append((o * (r * _sigmoid(r))).astype(BF16))
    yg = jnp.concatenate(yg_heads, axis=1)
    half = wbg0_ref.shape[0]
    gla_proj = _dot(yg[:, :half], wbg0_ref[...]) + _dot(yg[:, half:], wbg1_ref[...])
    m = m + g1_ref[...].astype(F32) * gla_proj
    ys = jnp.concatenate([ys_ref[c] for c in range(ys_ref.shape[0])], axis=1)
    y = ys + d_ref[...] * sin_ref[...].astype(F32)
    y = jax.nn.gelu(y)
    y = y * _sigmoid(_dot(y.astype(BF16), wglu_ref[...]))
    m = m + g2_ref[...].astype(F32) * _dot(y.astype(BF16), wbs_ref[...])
    o_ref[...] = x_ref[...] + _dot(m.astype(BF16), wo_ref[...])


def _merge(x, o_gla, ycat, z, r_col0, ssm_col0, gate_col0, seq, pool_w, pool_scale, gla_gain,
           d_skip, w_glu, wb, w_out, layer, *, tm=256):
    m, d = x.shape
    wp, wg, ws = pool_scale.shape[2], o_gla.shape[1], d_skip.shape[2]
    br = BRANCH_ROWS
    assert wp == br and ws == br and wg == 2 * br and seq % tm == 0 and tm % POOL_HALO == 0
    gb = gate_col0 // d
    hb = tm // POOL_HALO
    one = pl.Buffered(1)
    wspec = lambda shape, r: pl.BlockSpec((None,) + shape, lambda i: (layer, r, 0), pipeline_mode=one)
    return pl.pallas_call(
        functools.partial(_merge_body, tiles_per_seq=seq // tm),
        grid=(m // tm,),
        in_specs=[
            pl.BlockSpec((tm, d), lambda i: (i, 0)),
            pl.BlockSpec((tm, wp), lambda i: (i, 0)),
            pl.BlockSpec((POOL_HALO, wp), lambda i: (jnp.maximum(i * hb - 1, 0), 0)),
            pl.BlockSpec((tm, wg), lambda i: (i, 0)),
            pl.BlockSpec((tm, br), lambda i: (i, r_col0 // br)),
            pl.BlockSpec((tm, br), lambda i: (i, r_col0 // br + 1)),
            pl.BlockSpec((ycat.shape[0], tm // SSM_CHUNK, ycat.shape[2]), lambda i: (0, i, 0)),
            pl.BlockSpec((tm, ws), lambda i: (i, ssm_col0 // ws)),
            pl.BlockSpec((tm, d), lambda i: (i, gb)),
            pl.BlockSpec((tm, d), lambda i: (i, gb + 1)),
            pl.BlockSpec((tm, d), lambda i: (i, gb + 2)),
            pl.BlockSpec((None, len(POOL_WINDOWS), POOL_GROUP, POOL_GROUP), lambda i: (layer, 0, 0, 0),
                         pipeline_mode=one),
            wspec((1, wp), 0),
            wspec((1, wg), 0),
            wspec((1, ws), 0),
            wspec((ws, ws), 0),
            wspec((br, d), 0),
            wspec((br, d), 1),
            wspec((br, d), 2),
            wspec((br, d), 3),
            wspec((d, d), 0),
        ],
        out_specs=pl.BlockSpec((tm, d), lambda i: (i, 0)),
        out_shape=jax.ShapeDtypeStruct((m, d), F32),
        scratch_shapes=[pltpu.VMEM((ws // LANES, tm, LANES), F32)],
        compiler_params=_cparams(("parallel",)),
        name="merge",
    )(x, z, z, o_gla, z, z, ycat, z, z, z, z, pool_w, pool_scale, gla_gain, d_skip, w_glu,
      wb, wb, wb, wb, w_out)


def kernel(x, ffn1_norm, ffn1_w_gate, ffn1_w_up, ffn1_w_down, mix_norm, w_in, pool_w, pool_scale, gla_w_gate2, gla_gate_bias, gla_norm, ssm_a_re, ssm_a_im, ssm_log_dt, ssm_b_re, ssm_b_im, ssm_c_re, ssm_c_im, ssm_d, ssm_w_glu, w_branch, w_out, ffn2_norm, ffn2_w_gate, ffn2_w_up, ffn2_w_down, final_norm):
    batch, seq, d = x.shape
    depth = w_in.shape[0]
    m = batch * seq
    pool_width = pool_scale.shape[1]
    qk_width = GLA_HEADS * GLA_DK
    v_width = GLA_HEADS * GLA_DV
    ssm_width = ssm_d.shape[1]
    n_groups = ssm_width // SSM_GROUP
    q0 = pool_width
    k0 = q0 + qk_width
    v0 = k0 + qk_width
    r0 = v0 + v_width
    glr0 = r0 + v_width
    ssm_src0 = glr0 + GLA_RANK
    ssm0 = glr0
    gate0 = ssm0 + ssm_width
    chunks_per_seq = seq // SSM_CHUNK

    bf = lambda a: a.astype(BF16)
    row3 = lambda a: a.reshape(depth, 1, a.shape[-1])
    w_in_t = jnp.swapaxes(w_in, 1, 2)
    ffn1_w = (ffn1_w_gate, ffn1_w_up, ffn1_w_down)
    ffn2_w = (ffn2_w_gate, ffn2_w_up, ffn2_w_down)
    f1 = [bf(w[0]) for w in ffn1_w]
    wbr, wo, wglu, pw = bf(w_branch), bf(w_out), bf(ssm_w_glu), bf(pool_w)
    n1, nm, n2 = row3(ffn1_norm), row3(mix_norm), row3(ffn2_norm)
    pscale, dskip = row3(pool_scale), row3(ssm_d)
    w2 = gla_w_gate2.reshape(depth, GLA_RANK, GLA_HEADS, GLA_DK).transpose(0, 2, 1, 3)
    w2 = bf(jnp.pad(w2, ((0, 0), (0, 0), (0, LANES - GLA_RANK), (0, 0))))
    gbias = gla_gate_bias.reshape(depth, GLA_HEADS, 1, GLA_DK)
    ggain = row3(gla_norm)
    ssm_ops = jax.vmap(functools.partial(_ssm_operators, chunks_per_seq=chunks_per_seq))(
        ssm_a_re, ssm_a_im, ssm_log_dt, ssm_b_re, ssm_b_im, ssm_c_re, ssm_c_im)

    xf = x.reshape(m, d)
    for l in range(depth):
        xf, *side = _ffn(xf, n1, l, *f1, cast=(l,) + ffn2_w, repack=(l, w_in_t, glr0, GLA_RANK))
        f2, (w_main, w_glr) = side[:3], side[3:]
        z, glr, xcat = _inproj(xf, nm, w_main, w_glr, l, ssm0, ssm_width, gate0)
        o_gla = _gla(z, glr, w2, gbias, l, batch, seq, (q0, k0, v0))
        ycat = _ssm_core(xcat, *ssm_ops, l, chunks_per_seq)
        xf = _merge(xf, o_gla, ycat, z, r0, ssm0, gate0, seq, pw, pscale, ggain, dskip, wglu, wbr, wo, l)
        last = l == depth - 1
        xf, *f1 = _ffn(xf, n2, l, *f2, cast=None if last else (l + 1,) + ffn1_w,
                       final_g=final_norm if last else None)
    return xf.reshape(batch, seq, d)
```

```python
import functools

import jax
import jax.numpy as jnp
import numpy as np
from jax import lax
from jax.experimental import pallas as pl
from jax.experimental.pallas import tpu as pltpu

F32 = jnp.float32
BF16 = jnp.bfloat16

EPS = 1e-6
POOL_WINDOWS = (2, 4, 8, 16)
POOL_GROUP = 128
POOL_HALO = 16
GLA_HEADS = 4
GLA_DK = 128
GLA_DV = 256
GLA_RANK = 16
GLA_TAU = 16.0
LOG2_E = 1.4426950408889634
GLA_CHUNK = 64
SSM_GROUP = 16
SSM_STATE = 64
SSM_CHUNK = 16
LANES = 128
FFN_DOWN_CHUNK = 512
FFN_FIRST_ROWS = 256
REPACK_ROWS = 64
INPROJ_FIRST_ROWS = 256
BRANCH_ROWS = 512

VMEM_LIMIT = 60 * 1024 * 1024


def _cparams(sem):
    return pltpu.CompilerParams(dimension_semantics=sem, vmem_limit_bytes=VMEM_LIMIT)


def _rms(xf, g):
    return xf * lax.rsqrt(jnp.mean(xf * xf, axis=-1, keepdims=True) + EPS) * g


def _dot(a, b):
    return jnp.dot(a, b, preferred_element_type=F32)


def _sigmoid(a):
    return 0.5 * jnp.tanh(0.5 * a) + 0.5


def _dot_nt(a, b_t):
    return lax.dot_general(a, b_t, (((1,), (1,)), ((), ())), preferred_element_type=F32)


def _ffn_body(x_ref, g_ref, wg_ref, wu_ref, wd_ref, *rest, final, n_cast, repack):
    rest = list(rest)
    cast_in = [rest.pop(0) for _ in range(n_cast)]
    rp_in = [rest.pop(0) for _ in range(2 if repack else 0)]
    fg_ref = rest.pop(0) if final else None
    o_ref = rest.pop(0)
    cast_out = [rest.pop(0) for _ in range(n_cast)]
    rp_out = [rest.pop(0) for _ in range(2 if repack else 0)]
    xn_ref, = rest
    j = pl.program_id(1)

    for src, dst in zip(cast_in, cast_out):
        dst[...] = src[...].astype(BF16)
    if repack:
        (rows_ref, drop_ref), (w_ref, wl_ref) = rp_in, rp_out
        w_ref[...] = rows_ref[0].astype(BF16)
        pad = jnp.zeros((wl_ref.shape[0] - drop_ref.shape[1], drop_ref.shape[2]), F32)
        wl_ref[...] = jnp.concatenate([drop_ref[0], pad], axis=0).astype(BF16)

    def swiglu_step(rs, xn, first):
        a = _dot(xn, wg_ref[...])
        b = _dot(xn, wu_ref[...])
        h = (a * _sigmoid(a) * b).astype(BF16)
        for c0 in range(0, o_ref.shape[1], FFN_DOWN_CHUNK):
            cs = slice(c0, c0 + FFN_DOWN_CHUNK)
            if first:
                o_ref[rs, cs] = _dot(h, wd_ref[:, cs])
            else:
                o_ref[rs, cs] += _dot(h, wd_ref[:, cs])

    @pl.when(j == 0)
    def _():
        for r0 in range(0, o_ref.shape[0], FFN_FIRST_ROWS):
            rs = slice(r0, r0 + FFN_FIRST_ROWS)
            xn = _rms(x_ref[rs, :], g_ref[...]).astype(BF16)
            xn_ref[rs, :] = xn
            swiglu_step(rs, xn, True)

    last = pl.num_programs(1) - 1

    @pl.when((j > 0) & (j < last))
    def _():
        swiglu_step(slice(None), xn_ref[...], False)

    @pl.when(j == last)
    def _():
        for r0 in range(0, o_ref.shape[0], FFN_FIRST_ROWS):
            rs = slice(r0, r0 + FFN_FIRST_ROWS)
            swiglu_step(rs, xn_ref[rs, :], False)
            y = x_ref[rs, :] + 0.5 * o_ref[rs, :]
            if final:
                y = _rms(y, fg_ref[...])
            o_ref[rs, :] = y


def _ffn(x, g, layer, wg, wu, wd, cast=None, repack=None, final_g=None, *, tm=1024, tf=512):
    m, d = x.shape
    f = wg.shape[1]
    ni, nj = m // tm, f // tf
    final = final_g is not None
    in_specs = [
        pl.BlockSpec((tm, d), lambda i, j: (i, 0)),
        pl.BlockSpec((None, 1, d), lambda i, j: (layer, 0, 0)),
        pl.BlockSpec((d, tf), lambda i, j: (0, j)),
        pl.BlockSpec((d, tf), lambda i, j: (0, j)),
        pl.BlockSpec((tf, d), lambda i, j: (j, 0)),
    ]
    args = [x, g, wg, wu, wd]
    out_specs = [pl.BlockSpec((tm, d), lambda i, j: (i, 0))]
    out_shape = [jax.ShapeDtypeStruct((m, d), F32)]
    n_cast = 0
    if cast is not None:
        cl, cg, cu, cd = cast
        n_cast = 3
        rows_up = d // ni
        rows_dn = f // (ni * nj)
        assert d % ni == 0 and f % (ni * nj) == 0 and rows_up % 16 == 0 and rows_dn % 16 == 0
        for w in (cg, cu):
            in_specs.append(pl.BlockSpec((None, rows_up, tf), lambda i, j: (cl, i, j)))
            out_specs.append(pl.BlockSpec((rows_up, tf), lambda i, j: (i, j)))
            out_shape.append(jax.ShapeDtypeStruct((d, f), BF16))
            args.append(w)
        in_specs.append(pl.BlockSpec((None, rows_dn, d), lambda i, j: (cl, i * nj + j, 0)))
        out_specs.append(pl.BlockSpec((rows_dn, d), lambda i, j: (i * nj + j, 0)))
        out_shape.append(jax.ShapeDtypeStruct((f, d), BF16))
        args.append(cd)
    if repack is not None:
        rl, w_in_t, drop0, drop_width = repack
        n = w_in_t.shape[1] - drop_width
        rb = REPACK_ROWS
        nblk = n // rb
        assert n % rb == 0 and nblk <= ni * nj and drop0 % rb == 0 and drop_width % 8 == 0
        blk = lambda i, j: jnp.minimum(i * nj + j, nblk - 1)
        src_row = lambda b: (b * (rb // 8) + jnp.where(b >= drop0 // rb, drop_width // 8, 0)) * 8
        in_specs.append(pl.BlockSpec((pl.Element(1), pl.Element(rb), pl.Element(d)),
                                     lambda i, j: (rl, src_row(blk(i, j)), 0)))
        in_specs.append(pl.BlockSpec((pl.Element(1), pl.Element(drop_width), pl.Element(d)),
                                     lambda i, j: (rl, drop0, 0)))
        out_specs.append(pl.BlockSpec((rb, d), lambda i, j: (blk(i, j), 0)))
        out_specs.append(pl.BlockSpec((LANES, d), lambda i, j: (0, 0)))
        out_shape.append(jax.ShapeDtypeStruct((n, d), BF16))
        out_shape.append(jax.ShapeDtypeStruct((LANES, d), BF16))
        args += [w_in_t, w_in_t]
    if final:
        in_specs.append(pl.BlockSpec((1, d), lambda i, j: (0, 0)))
        args.append(final_g.reshape(1, d))
    return pl.pallas_call(
        functools.partial(_ffn_body, final=final, n_cast=n_cast, repack=repack is not None),
        grid=(ni, nj),
        in_specs=in_specs,
        out_specs=out_specs,
        out_shape=out_shape,
        scratch_shapes=[pltpu.VMEM((tm, d), BF16)],
        compiler_params=_cparams(("arbitrary", "arbitrary")),
        name="ffn_final" if final else "ffn",
    )(*args)


def _inproj_body(x_ref, g_ref, w_ref, wl_ref, z_ref, glr_ref, xc_ref, un_ref, ph_ref, *,
                 gate_block0, ssm_block, ssm_off):
    j = pl.program_id(1)

    @pl.when(j == 0)
    def _():
        for r0 in range(0, z_ref.shape[0], INPROJ_FIRST_ROWS):
            rs = slice(r0, r0 + INPROJ_FIRST_ROWS)
            un = _rms(x_ref[rs, :], g_ref[...]).astype(BF16)
            un_ref[rs, :] = un
            glr_ref[rs, :] = _dot_nt(un, wl_ref[...])
            z_ref[rs, :] = _dot_nt(un, w_ref[...]).astype(BF16)

    @pl.when(j >= gate_block0)
    def _():
        z_ref[...] = _sigmoid(_dot_nt(un_ref[...], w_ref[...])).astype(BF16)

    @pl.when((j > 0) & (j < gate_block0))
    def _():
        acc = _dot_nt(un_ref[...], w_ref[...])
        z_ref[...] = acc.astype(BF16)

        @pl.when(j == ssm_block)
        def _():
            ncol = ph_ref.shape[0]
            for c in range(ncol):
                ph_ref[c] = acc[:, ssm_off + c * LANES:ssm_off + (c + 1) * LANES]
            rows = ph_ref.shape[1] // SSM_CHUNK
            for s in range(SSM_CHUNK):
                for c in range(ncol):
                    piece = ph_ref[c, pl.ds(s, rows, stride=SSM_CHUNK), :].astype(BF16)
                    xc_ref[c, :, s * LANES:(s + 1) * LANES] = piece


def _inproj(x, g, w_main, w_glr, layer, ssm_col0, ssm_width, gate_col0, *, tm=1024, tn=1024):
    m, d = x.shape
    n = w_main.shape[0]
    assert gate_col0 >= tn and tn <= ssm_col0 < gate_col0
    body = functools.partial(_inproj_body, gate_block0=gate_col0 // tn,
                             ssm_block=ssm_col0 // tn, ssm_off=ssm_col0 % tn)
    return pl.pallas_call(
        body,
        grid=(m // tm, n // tn),
        in_specs=[
            pl.BlockSpec((tm, d), lambda i, j: (i, 0)),
            pl.BlockSpec((None, 1, d), lambda i, j: (layer, 0, 0)),
            pl.BlockSpec((tn, d), lambda i, j: (j, 0)),
            pl.BlockSpec((LANES, d), lambda i, j: (0, 0)),
        ],
        out_specs=[
            pl.BlockSpec((tm, tn), lambda i, j: (i, j)),
            pl.BlockSpec((tm, LANES), lambda i, j: (i, 0)),
            pl.BlockSpec((ssm_width // LANES, tm // SSM_CHUNK, SSM_CHUNK * LANES), lambda i, j: (0, i, 0)),
        ],
        out_shape=[
            jax.ShapeDtypeStruct((m, n), BF16),
            jax.ShapeDtypeStruct((m, LANES), F32),
            jax.ShapeDtypeStruct((ssm_width // LANES, m // SSM_CHUNK, SSM_CHUNK * LANES), BF16),
        ],
        scratch_shapes=[pltpu.VMEM((tm, d), BF16), pltpu.VMEM((ssm_width // LANES, tm, LANES), F32)],
        compiler_params=_cparams(("parallel", "arbitrary")),
        name="inproj",
    )(x, g, w_main, w_glr)


def _pool_mix(x, halo, t, w_ref, scale):
    ts = x.shape[0]
    halo = jnp.where(t > 0, halo, 0.0)
    e = jnp.concatenate([halo, x], axis=0)
    s2 = e + pltpu.roll(e, 1, 0)
    s4 = s2 + pltpu.roll(s2, 2, 0)
    s8 = s4 + pltpu.roll(s4, 4, 0)
    s16 = s8 + pltpu.roll(s8, 8, 0)
    pos = (t * ts + 1 + lax.broadcasted_iota(jnp.int32, (ts, 1), 0)).astype(F32)
    outs = []
    for gi, (w, s) in enumerate(zip(POOL_WINDOWS, (s2, s4, s8, s16))):
        cols = slice(gi * POOL_GROUP, (gi + 1) * POOL_GROUP)
        mean = s[POOL_HALO:, cols] / jnp.minimum(pos, float(w))
        outs.append(_dot((mean - x[:, cols]).astype(BF16), w_ref[gi]))
    return jnp.concatenate(outs, axis=1) * scale


def _gla_streams(streams, ts):
    c = GLA_CHUNK
    nc = ts // c
    rowc = lax.broadcasted_iota(jnp.int32, (ts, 1), 0) % c
    causal = (lax.broadcasted_iota(jnp.int32, (c, c), 0)
              >= lax.broadcasted_iota(jnp.int32, (c, c), 1))
    qs, ks, vs, glrs, w2s, biases, sts = (list(a) for a in zip(*streams))
    logits = [_dot(glr, w2) + bias for glr, w2, bias in zip(glrs, w2s, biases)]
    bcs = [(jnp.minimum(l, 0.0) - jnp.log1p(jnp.exp(-jnp.abs(l)))) * (LOG2_E / GLA_TAU) for l in logits]
    step = 1
    while step < c:
        bcs = [bc + jnp.where(rowc >= step, pltpu.roll(bc, step, 0), 0.0) for bc in bcs]
        step *= 2
    bc3 = [bc.reshape(nc, c, GLA_DK) for bc in bcs]
    b_last = [b[:, c - 1:c, :] for b in bc3]
    b_mid = [b[:, c // 2 - 1:c // 2, :] for b in bc3]
    q3 = [q.astype(F32).reshape(nc, c, GLA_DK) * (GLA_DK ** -0.5) for q in qs]
    k3 = [k.astype(F32).reshape(nc, c, GLA_DK) for k in ks]
    v3 = [v.reshape(nc, c, GLA_DV) for v in vs]
    qe = [(q * jnp.exp2(b - m)).astype(BF16) for q, b, m in zip(q3, bc3, b_mid)]
    ke = [(k * jnp.exp2(m - b)).astype(BF16) for k, b, m in zip(k3, bc3, b_mid)]
    scores = [jnp.einsum('nid,njd->nij', a, b, preferred_element_type=F32) for a, b in zip(qe, ke)]
    scores = [jnp.where(causal[None], s, 0.0).astype(BF16) for s in scores]
    o_intra = [jnp.einsum('nij,njv->niv', s, v, preferred_element_type=F32) for s, v in zip(scores, v3)]
    qd = [(q * jnp.exp2(b)).astype(BF16) for q, b in zip(q3, bc3)]
    kd = [(k * jnp.exp2(bl - b)).astype(BF16) for k, b, bl in zip(k3, bc3, b_last)]
    dec_t = [jnp.transpose(jnp.exp2(bl.reshape(nc, GLA_DK))) for bl in b_last]
    outs = [[] for _ in streams]
    for n in range(nc):
        for i in range(len(streams)):
            outs[i].append(o_intra[i][n] + _dot(qd[i][n], sts[i].astype(BF16)))
            kv = lax.dot_general(kd[i][n], v3[i][n], (((0,), (0,)), ((), ())), preferred_element_type=F32)
            sts[i] = dec_t[i][:, n:n + 1] * sts[i] + kv
    return [jnp.concatenate(o, axis=0).astype(BF16) for o in outs], sts


def _gla_body(q_ref, k_ref, v0_ref, v1_ref, glr_ref, w2_ref, b_ref, o_ref, st_ref, *, ts):
    @pl.when(pl.program_id(1) == 0)
    def _():
        st_ref[...] = jnp.zeros_like(st_ref)

    v_refs = (v0_ref, v1_ref)
    heads_per_block = v0_ref.shape[2] // GLA_DV
    streams, where = [], []
    for s in range(q_ref.shape[0]):
        glr = glr_ref[s].astype(BF16)
        for h in range(GLA_HEADS):
            v_ref = v_refs[h // heads_per_block]
            vs = slice((h % heads_per_block) * GLA_DV, (h % heads_per_block + 1) * GLA_DV)
            ks = slice(h * GLA_DK, (h + 1) * GLA_DK)
            streams.append((q_ref[s, :, ks], k_ref[s, :, ks], v_ref[s, :, vs], glr,
                            w2_ref[h], b_ref[h], st_ref[s, h]))
            where.append((s, h))
    outs, sts = _gla_streams(streams, ts)
    for (s, h), o, st in zip(where, outs, sts):
        st_ref[s, h] = st
        o_ref[s, :, h * GLA_DV:(h + 1) * GLA_DV] = o


def _gla(z, glr, w2, bias, layer, batch, seq, cols, *, ts=512, nseq=2):
    h = GLA_HEADS
    nt = seq // ts
    q0, k0, v0 = cols
    wb = h * GLA_DK
    z3 = z.reshape(batch, seq, z.shape[1])
    zspec = lambda c0: pl.BlockSpec((nseq, ts, wb), lambda b, t: (b, t, c0 // wb))
    wspec = lambda last2: pl.BlockSpec((None, h) + last2, lambda b, t: (layer, 0, 0, 0))
    out = pl.pallas_call(
        functools.partial(_gla_body, ts=ts),
        grid=(batch // nseq, nt),
        in_specs=[
            zspec(q0), zspec(k0), zspec(v0), zspec(v0 + wb),
            pl.BlockSpec((nseq, ts, LANES), lambda b, t: (b, t, 0)),
            wspec((LANES, GLA_DK)), wspec((1, GLA_DK)),
        ],
        out_specs=pl.BlockSpec((nseq, ts, h * GLA_DV), lambda b, t: (b, t, 0)),
        out_shape=jax.ShapeDtypeStruct((batch, seq, h * GLA_DV), BF16),
        scratch_shapes=[pltpu.VMEM((nseq, h, GLA_DK, GLA_DV), F32)],
        compiler_params=_cparams(("parallel", "arbitrary")),
        name="gla",
    )(z3, z3, z3, z3, glr.reshape(batch, seq, LANES), w2, bias)
    return out.reshape(batch * seq, h * GLA_DV)


SSM_COL_GROUPS = LANES // SSM_GROUP
SSM_BLOCK_GROUPS = 2
SSM_TPAIR = 2


def _ssm_body(x_ref, kc_ref, pc_ref, qc_ref, ar_ref, ai_ref, o_ref, hp_ref, kst_ref, v_ref, *,
              chunks_per_seq, n_state_blocks):
    u = pl.program_id(0)
    ncol, rows = x_ref.shape[0], x_ref.shape[1]
    blocks_per_col = n_state_blocks // ncol
    npairs = SSM_CHUNK // SSM_TPAIR
    ns = 2 * SSM_STATE
    bw = SSM_BLOCK_GROUPS * ns
    lane_group = lax.broadcasted_iota(jnp.int32, (1, LANES), 1) // SSM_GROUP

    def p_tile(n):
        pc = pc_ref[n // blocks_per_col]
        row_group = (lax.broadcasted_iota(jnp.int32, (pc.shape[0], 1), 0) // SSM_GROUP) % SSM_COL_GROUPS
        g0 = (n % blocks_per_col) * SSM_BLOCK_GROUPS
        return jnp.concatenate([jnp.where(row_group == g0 + gg, pc, 0.0)
                                for gg in range(SSM_BLOCK_GROUPS)], axis=1).astype(BF16)

    @pl.when(u == 0)
    def _():
        row_group = lax.broadcasted_iota(jnp.int32, (LANES, 1), 0) // SSM_GROUP

        def kbd(c, lag):
            kc = kc_ref[c, lag]
            tiled = jnp.broadcast_to(kc[None], (SSM_COL_GROUPS,) + kc.shape).reshape(LANES, LANES)
            return jnp.where(row_group == lane_group, tiled, 0.0).astype(BF16)

        for c in range(ncol):
            for r in range(SSM_CHUNK):
                left = kbd(c, SSM_CHUNK - 2 - r) if r <= SSM_CHUNK - 2 else jnp.zeros((LANES, LANES), BF16)
                kst_ref[c, r * LANES:(r + 1) * LANES, :] = jnp.concatenate(
                    [left, kbd(c, SSM_CHUNK - 1 - r)], axis=1)
        v_ref[...] = _dot(x_ref[0], p_tile(0))

    @pl.when(u < n_state_blocks)
    def _():
        rowc = lax.broadcasted_iota(jnp.int32, (rows, 1), 0) % chunks_per_seq
        hst = v_ref[...]
        nxt = jnp.minimum(u + 1, n_state_blocks - 1)
        v_next = _dot(x_ref[nxt // blocks_per_col], p_tile(nxt))
        k, si = 1, 0
        while k < chunks_per_seq:
            hs = jnp.where(rowc >= k, pltpu.roll(hst, k, 0), 0.0)
            sw = jnp.concatenate([pltpu.roll(hs[:, b0:b0 + ns], SSM_STATE, 1)
                                  for b0 in range(0, bw, ns)], axis=1)
            hst = hst + ar_ref[u, si:si + 1, :] * hs + ai_ref[u, si:si + 1, :] * sw
            k *= 2
            si += 1
        hp_ref[u] = jnp.where(rowc >= 1, pltpu.roll(hst, 1, 0), 0.0).astype(BF16)
        v_ref[...] = v_next

    @pl.when(u >= n_state_blocks)
    def _():
        ob = u - n_state_blocks
        c = ob // npairs
        v = ob % npairs
        grp = lax.broadcasted_iota(jnp.int32, (SSM_COL_GROUPS, 1, 1), 0)
        q_tile = jnp.concatenate(
            [jnp.where(grp == lane_group[None], qc_ref[c, SSM_TPAIR * v + tt][None], 0.0)
             .reshape(SSM_COL_GROUPS * ns, LANES) for tt in range(SSM_TPAIR)], axis=1).astype(BF16)
        yq = _dot(hp_ref[c * blocks_per_col], q_tile[0:bw, :])
        for kk in range(1, blocks_per_col):
            yq = yq + _dot(hp_ref[c * blocks_per_col + kk], q_tile[kk * bw:(kk + 1) * bw, :])
        for vs in range(npairs):
            kext = (vs + 1) * SSM_TPAIR * LANES

            @pl.when(v == vs)
            def _():
                start = (npairs - 1 - vs) * SSM_TPAIR * LANES
                intra = _dot(x_ref[c, :, 0:kext], kst_ref[c, start:start + kext, :])
                o_ref[...] = (yq + intra).astype(BF16)


def _ssm_core(xcat, kc, pc, qc, ar, ai, layer, chunks_per_seq):
    ncol, rows, xw = xcat.shape
    nb = ar.shape[1]
    bw = ar.shape[3]
    ow = SSM_TPAIR * LANES
    nout = ncol * SSM_CHUNK // SSM_TPAIR
    one = pl.Buffered(1)
    whole = lambda a: pl.BlockSpec((None,) + a.shape[1:], lambda u: (layer,) + (0,) * (a.ndim - 1),
                                   pipeline_mode=one)
    return pl.pallas_call(
        functools.partial(_ssm_body, chunks_per_seq=chunks_per_seq, n_state_blocks=nb),
        grid=(nb + nout,),
        in_specs=[
            pl.BlockSpec((ncol, rows, xw), lambda u: (0, 0, 0), pipeline_mode=one),
            whole(kc), whole(pc), whole(qc), whole(ar), whole(ai),
        ],
        out_specs=pl.BlockSpec((None, rows, ow), lambda u: (jnp.maximum(u - nb, 0), 0, 0)),
        out_shape=jax.ShapeDtypeStruct((nout, rows, ow), BF16),
        scratch_shapes=[
            pltpu.VMEM((nb, rows, bw), BF16),
            pltpu.VMEM((ncol, SSM_CHUNK * LANES, ow), BF16),
            pltpu.VMEM((rows, bw), F32),
        ],
        compiler_params=_cparams(("arbitrary",)),
        name="ssm_core",
    )(xcat, kc, pc, qc, ar, ai)


def _ssm_operators(a_re, a_im, log_dt, b_re, b_im, c_re, c_im, chunks_per_seq):
    t = SSM_CHUNK
    g, p = a_re.shape
    hg = b_re.shape[-1]
    cgc = SSM_COL_GROUPS
    ncol = g // cgc
    bgc = SSM_BLOCK_GROUPS
    dt = jnp.exp(log_dt)[:, None]
    lam_re, lam_im = dt * a_re, dt * a_im

    def powers(n):
        n = jnp.asarray(n, F32)[:, None, None]
        mag = jnp.exp(n * lam_re)
        return mag * jnp.cos(n * lam_im), mag * jnp.sin(n * lam_im)

    ab_re, ab_im = powers([1.0])
    ab_re, ab_im = ab_re[0], ab_im[0]
    den = a_re * a_re + a_im * a_im
    f_re = ((ab_re - 1.0) * a_re + ab_im * a_im) / den
    f_im = (ab_im * a_re - (ab_re - 1.0) * a_im) / den
    bb_re = f_re[..., None] * b_re - f_im[..., None] * b_im
    bb_im = f_re[..., None] * b_im + f_im[..., None] * b_re
    pw_re, pw_im = powers(np.arange(t + 1))
    cp_re = c_re[None] * pw_re[:, :, None, :] - c_im[None] * pw_im[:, :, None, :]
    cp_im = c_re[None] * pw_im[:, :, None, :] + c_im[None] * pw_re[:, :, None, :]
    ktau = (jnp.einsum('tghp,gpk->tkgh', cp_re[:t], bb_re)
            - jnp.einsum('tghp,gpk->tkgh', cp_im[:t], bb_im))
    kc = ktau.reshape(t, hg, ncol, cgc * hg).transpose(2, 0, 1, 3)
    bt_re, bt_im = bb_re.transpose(0, 2, 1), bb_im.transpose(0, 2, 1)
    rp_re = pw_re[t - 1 - np.arange(t)][:, :, None, :]
    rp_im = pw_im[t - 1 - np.arange(t)][:, :, None, :]
    pin = jnp.concatenate([rp_re * bt_re[None] - rp_im * bt_im[None],
                           rp_re * bt_im[None] + rp_im * bt_re[None]], axis=3)
    pc = pin.reshape(t, ncol, cgc, hg, 2 * p).transpose(1, 0, 2, 3, 4).reshape(ncol, t * cgc * hg, 2 * p)
    ct_re, ct_im = c_re.transpose(2, 0, 1)[None], c_im.transpose(2, 0, 1)[None]
    pt_re = pw_re[1:].transpose(0, 2, 1)[..., None]
    pt_im = pw_im[1:].transpose(0, 2, 1)[..., None]
    qg = jnp.concatenate([ct_re * pt_re - ct_im * pt_im,
                          -(ct_re * pt_im + ct_im * pt_re)], axis=1)
    qc = qg.reshape(t, 2 * p, ncol, cgc * hg).transpose(2, 0, 1, 3)
    nblk = g // bgc
    nsteps = int(np.log2(chunks_per_seq))
    sr, si = powers(t * 2.0 ** np.arange(nsteps))
    blk = lambda a: a.reshape(nsteps, nblk, bgc * 2 * p).transpose(1, 0, 2)
    ar = blk(jnp.concatenate([sr, sr], axis=2))
    ai = blk(jnp.concatenate([-si, si], axis=2))
    return kc, pc, qc, ar, ai


def _merge_body(x_ref, pin_ref, halo_ref, og_ref, r0_ref, r1_ref, yc_ref, sin_ref, g0_ref, g1_ref, g2_ref,
                pw_ref, ps_ref, gain_ref, d_ref, wglu_ref, wbp_ref, wbg0_ref, wbg1_ref, wbs_ref, wo_ref,
                o_ref, ys_ref, *, tiles_per_seq):
    t = pl.program_id(0) % tiles_per_seq
    yp = _pool_mix(pin_ref[...].astype(F32), halo_ref[...].astype(F32), t, pw_ref, ps_ref[...])
    r_refs = (r0_ref, r1_ref)
    heads_per_block = r0_ref.shape[1] // GLA_DV
    yg_heads = []
    for h in range(GLA_HEADS):
        hs = slice(h * GLA_DV, (h + 1) * GLA_DV)
        rs = slice((h % heads_per_block) * GLA_DV, (h % heads_per_block + 1) * GLA_DV)
        o = og_ref[:, hs].astype(F32)
        o = o * lax.rsqrt(jnp.mean(o * o, axis=-1, keepdims=True) + EPS) * gain_ref[:, hs]
        r = r_refs[h // heads_per_block][:, rs].astype(F32)
        yg_heads.append((o * (r * _sigmoid(r))).astype(BF16))
    yg = jnp.concatenate(yg_heads, axis=1)
    rows = yc_ref.shape[1]
    npairs = SSM_CHUNK // SSM_TPAIR
    for ob in range(yc_ref.shape[0]):
        c, v = divmod(ob, npairs)
        plane = yc_ref[ob].astype(F32)
        for tt in range(SSM_TPAIR):
            ys_ref[c, pl.ds(SSM_TPAIR * v + tt, rows, stride=SSM_CHUNK), :] = plane[:, tt * LANES:(tt + 1) * LANES]
    ys = jnp.concatenate([ys_ref[c] for c in range(ys_ref.shape[0])], axis=1)
    y = ys + d_ref[...] * sin_ref[...].astype(F32)
    y = jax.nn.gelu(y)
    y = y * _sigmoid(_dot(y.astype(BF16), wglu_ref[...]))
    m = g0_ref[...].astype(F32) * _dot(yp.astype(BF16), wbp_ref[...])
    half = wbg0_ref.shape[0]
    gla_proj = _dot(yg[:, :half], wbg0_ref[...]) + _dot(yg[:, half:], wbg1_ref[...])
    m = m + g1_ref[...].astype(F32) * gla_proj
    m = m + g2_ref[...].astype(F32) * _dot(y.astype(BF16), wbs_ref[...])
    o_ref[...] = x_ref[...] + _dot(m.astype(BF16), wo_ref[...])


def _merge(x, o_gla, ycat, z, r_col0, ssm_col0, gate_col0, seq, pool_w, pool_scale, gla_gain,
           d_skip, w_glu, wb, w_out, layer, *, tm=256):
    m, d = x.shape
    wp, wg, ws = pool_scale.shape[2], o_gla.shape[1], d_skip.shape[2]
    br = BRANCH_ROWS
    assert wp == br and ws == br and wg == 2 * br and seq % tm == 0 and tm % POOL_HALO == 0
    gb = gate_col0 // d
    hb = tm // POOL_HALO
    one = pl.Buffered(1)
    wspec = lambda shape, r: pl.BlockSpec((None,) + shape, lambda i: (layer, r, 0), pipeline_mode=one)
    return pl.pallas_call(
        functools.partial(_merge_body, tiles_per_seq=seq // tm),
        grid=(m // tm,),
        in_specs=[
            pl.BlockSpec((tm, d), lambda i: (i, 0)),
            pl.BlockSpec((tm, wp), lambda i: (i, 0)),
            pl.BlockSpec((POOL_HALO, wp), lambda i: (jnp.maximum(i * hb - 1, 0), 0)),
            pl.BlockSpec((tm, wg), lambda i: (i, 0)),
            pl.BlockSpec((tm, br), lambda i: (i, r_col0 // br)),
            pl.BlockSpec((tm, br), lambda i: (i, r_col0 // br + 1)),
            pl.BlockSpec((ycat.shape[0], tm // SSM_CHUNK, ycat.shape[2]), lambda i: (0, i, 0)),
            pl.BlockSpec((tm, ws), lambda i: (i, ssm_col0 // ws)),
            pl.BlockSpec((tm, d), lambda i: (i, gb)),
            pl.BlockSpec((tm, d), lambda i: (i, gb + 1)),
            pl.BlockSpec((tm, d), lambda i: (i, gb + 2)),
            pl.BlockSpec((None, len(POOL_WINDOWS), POOL_GROUP, POOL_GROUP), lambda i: (layer, 0, 0, 0),
                         pipeline_mode=one),
            wspec((1, wp), 0),
            wspec((1, wg), 0),
            wspec((1, ws), 0),
            wspec((ws, ws), 0),
            wspec((br, d), 0),
            wspec((br, d), 1),
            wspec((br, d), 2),
            wspec((br, d), 3),
            wspec((d, d), 0),
        ],
        out_specs=pl.BlockSpec((tm, d), lambda i: (i, 0)),
        out_shape=jax.ShapeDtypeStruct((m, d), F32),
        scratch_shapes=[pltpu.VMEM((ws // LANES, tm, LANES), F32)],
        compiler_params=_cparams(("parallel",)),
        name="merge",
    )(x, z, z, o_gla, z, z, ycat, z, z, z, z, pool_w, pool_scale, gla_gain, d_skip, w_glu,
      wb, wb, wb, wb, w_out)


def kernel(x, ffn1_norm, ffn1_w_gate, ffn1_w_up, ffn1_w_down, mix_norm, w_in, pool_w, pool_scale, gla_w_gate2, gla_gate_bias, gla_norm, ssm_a_re, ssm_a_im, ssm_log_dt, ssm_b_re, ssm_b_im, ssm_c_re, ssm_c_im, ssm_d, ssm_w_glu, w_branch, w_out, ffn2_norm, ffn2_w_gate, ffn2_w_up, ffn2_w_down, final_norm):
    batch, seq, d = x.shape
    depth = w_in.shape[0]
    m = batch * seq
    pool_width = pool_scale.shape[1]
    qk_width = GLA_HEADS * GLA_DK
    v_width = GLA_HEADS * GLA_DV
    ssm_width = ssm_d.shape[1]
    q0 = pool_width
    k0 = q0 + qk_width
    v0 = k0 + qk_width
    r0 = v0 + v_width
    glr0 = r0 + v_width
    ssm0 = glr0
    gate0 = ssm0 + ssm_width
    chunks_per_seq = seq // SSM_CHUNK

    bf = lambda a: a.astype(BF16)
    row3 = lambda a: a.reshape(depth, 1, a.shape[-1])
    w_in_t = jnp.swapaxes(w_in, 1, 2)
    ffn1_w = (ffn1_w_gate, ffn1_w_up, ffn1_w_down)
    ffn2_w = (ffn2_w_gate, ffn2_w_up, ffn2_w_down)
    f1 = [bf(w[0]) for w in ffn1_w]
    wbr, wo, wglu, pw = bf(w_branch), bf(w_out), bf(ssm_w_glu), bf(pool_w)
    n1, nm, n2 = row3(ffn1_norm), row3(mix_norm), row3(ffn2_norm)
    pscale, dskip = row3(pool_scale), row3(ssm_d)
    w2 = gla_w_gate2.reshape(depth, GLA_RANK, GLA_HEADS, GLA_DK).transpose(0, 2, 1, 3)
    w2 = bf(jnp.pad(w2, ((0, 0), (0, 0), (0, LANES - GLA_RANK), (0, 0))))
    gbias = gla_gate_bias.reshape(depth, GLA_HEADS, 1, GLA_DK)
    ggain = row3(gla_norm)
    ssm_ops = jax.vmap(functools.partial(_ssm_operators, chunks_per_seq=chunks_per_seq))(
        ssm_a_re, ssm_a_im, ssm_log_dt, ssm_b_re, ssm_b_im, ssm_c_re, ssm_c_im)

    xf = x.reshape(m, d)
    for l in range(depth):
        xf, *side = _ffn(xf, n1, l, *f1, cast=(l,) + ffn2_w, repack=(l, w_in_t, glr0, GLA_RANK))
        f2, (w_main, w_glr) = side[:3], side[3:]
        z, glr, xcat = _inproj(xf, nm, w_main, w_glr, l, ssm0, ssm_width, gate0)
        o_gla = _gla(z, glr, w2, gbias, l, batch, seq, (q0, k0, v0))
        ycat = _ssm_core(xcat, *ssm_ops, l, chunks_per_seq)
        xf = _merge(xf, o_gla, ycat, z, r0, ssm0, gate0, seq, pw, pscale, ggain, dskip, wglu, wbr, wo, l)
        last = l == depth - 1
        xf, *f1 = _ffn(xf, n2, l, *f2, cast=None if last else (l + 1,) + ffn1_w,
                       final_g=final_norm if last else None)
    return xf.reshape(batch, seq, d)
```

```python
import functools

import jax
import jax.numpy as jnp
import numpy as np
from jax import lax
from jax.experimental import pallas as pl
from jax.experimental.pallas import tpu as pltpu

F32 = jnp.float32
BF16 = jnp.bfloat16

EPS = 1e-6
POOL_WINDOWS = (2, 4, 8, 16)
POOL_GROUP = 128
POOL_HALO = 16
GLA_HEADS = 4
GLA_DK = 128
GLA_DV = 256
GLA_RANK = 16
GLA_TAU = 16.0
LOG2_E = 1.4426950408889634
GLA_CHUNK = 64
SSM_GROUP = 16
SSM_STATE = 64
SSM_CHUNK = 16
LANES = 128
FFN_DOWN_CHUNK = 512
FFN_FIRST_ROWS = 256
REPACK_ROWS = 64
INPROJ_FIRST_ROWS = 256
BRANCH_ROWS = 512

VMEM_LIMIT = 60 * 1024 * 1024


def _cparams(sem):
    return pltpu.CompilerParams(dimension_semantics=sem, vmem_limit_bytes=VMEM_LIMIT)


def _rms(xf, g):
    return xf * lax.rsqrt(jnp.mean(xf * xf, axis=-1, keepdims=True) + EPS) * g


def _dot(a, b):
    return jnp.dot(a, b, preferred_element_type=F32)


def _sigmoid(a):
    return 0.5 * jnp.tanh(0.5 * a) + 0.5


def _dot_nt(a, b_t):
    return lax.dot_general(a, b_t, (((1,), (1,)), ((), ())), preferred_element_type=F32)


def _ffn_body(x_ref, g_ref, wg_ref, wu_ref, wd_ref, *rest, final, n_cast, repack):
    rest = list(rest)
    cast_in = [rest.pop(0) for _ in range(n_cast)]
    rp_in = [rest.pop(0) for _ in range(2 if repack else 0)]
    fg_ref = rest.pop(0) if final else None
    o_ref = rest.pop(0)
    cast_out = [rest.pop(0) for _ in range(n_cast)]
    rp_out = [rest.pop(0) for _ in range(2 if repack else 0)]
    xn_ref, = rest
    j = pl.program_id(1)

    for src, dst in zip(cast_in, cast_out):
        dst[...] = src[...].astype(BF16)
    if repack:
        (rows_ref, drop_ref), (w_ref, wl_ref) = rp_in, rp_out
        w_ref[...] = rows_ref[0].astype(BF16)
        pad = jnp.zeros((wl_ref.shape[0] - drop_ref.shape[1], drop_ref.shape[2]), F32)
        wl_ref[...] = jnp.concatenate([drop_ref[0], pad], axis=0).astype(BF16)

    def swiglu_step(rs, xn, first):
        a = _dot(xn, wg_ref[...])
        b = _dot(xn, wu_ref[...])
        h = (a * _sigmoid(a) * b).astype(BF16)
        for c0 in range(0, o_ref.shape[1], FFN_DOWN_CHUNK):
            cs = slice(c0, c0 + FFN_DOWN_CHUNK)
            if first:
                o_ref[rs, cs] = _dot(h, wd_ref[:, cs])
            else:
                o_ref[rs, cs] += _dot(h, wd_ref[:, cs])

    @pl.when(j == 0)
    def _():
        for r0 in range(0, o_ref.shape[0], FFN_FIRST_ROWS):
            rs = slice(r0, r0 + FFN_FIRST_ROWS)
            xn = _rms(x_ref[rs, :], g_ref[...]).astype(BF16)
            xn_ref[rs, :] = xn
            swiglu_step(rs, xn, True)

    last = pl.num_programs(1) - 1

    @pl.when((j > 0) & (j < last))
    def _():
        swiglu_step(slice(None), xn_ref[...], False)

    @pl.when(j == last)
    def _():
        for r0 in range(0, o_ref.shape[0], FFN_FIRST_ROWS):
            rs = slice(r0, r0 + FFN_FIRST_ROWS)
            swiglu_step(rs, xn_ref[rs, :], False)
            y = x_ref[rs, :] + 0.5 * o_ref[rs, :]
            if final:
                y = _rms(y, fg_ref[...])
            o_ref[rs, :] = y


def _ffn(x, g, layer, wg, wu, wd, cast=None, repack=None, final_g=None, *, tm=1024, tf=512):
    m, d = x.shape
    f = wg.shape[1]
    ni, nj = m // tm, f // tf
    final = final_g is not None
    in_specs = [
        pl.BlockSpec((tm, d), lambda i, j: (i, 0)),
        pl.BlockSpec((None, 1, d), lambda i, j: (layer, 0, 0)),
        pl.BlockSpec((d, tf), lambda i, j: (0, j)),
        pl.BlockSpec((d, tf), lambda i, j: (0, j)),
        pl.BlockSpec((tf, d), lambda i, j: (j, 0)),
    ]
    args = [x, g, wg, wu, wd]
    out_specs = [pl.BlockSpec((tm, d), lambda i, j: (i, 0))]
    out_shape = [jax.ShapeDtypeStruct((m, d), F32)]
    n_cast = 0
    if cast is not None:
        cl, cg, cu, cd = cast
        n_cast = 3
        rows_up = d // ni
        rows_dn = f // (ni * nj)
        assert d % ni == 0 and f % (ni * nj) == 0 and rows_up % 16 == 0 and rows_dn % 16 == 0
        for w in (cg, cu):
            in_specs.append(pl.BlockSpec((None, rows_up, tf), lambda i, j: (cl, i, j)))
            out_specs.append(pl.BlockSpec((rows_up, tf), lambda i, j: (i, j)))
            out_shape.append(jax.ShapeDtypeStruct((d, f), BF16))
            args.append(w)
        in_specs.append(pl.BlockSpec((None, rows_dn, d), lambda i, j: (cl, i * nj + j, 0)))
        out_specs.append(pl.BlockSpec((rows_dn, d), lambda i, j: (i * nj + j, 0)))
        out_shape.append(jax.ShapeDtypeStruct((f, d), BF16))
        args.append(cd)
    if repack is not None:
        rl, w_in_t, drop0, drop_width = repack
        n = w_in_t.shape[1] - drop_width
        rb = REPACK_ROWS
        nblk = n // rb
        assert n % rb == 0 and nblk <= ni * nj and drop0 % rb == 0 and drop_width % 8 == 0
        blk = lambda i, j: jnp.minimum(i * nj + j, nblk - 1)
        src_row = lambda b: (b * (rb // 8) + jnp.where(b >= drop0 // rb, drop_width // 8, 0)) * 8
        in_specs.append(pl.BlockSpec((pl.Element(1), pl.Element(rb), pl.Element(d)),
                                     lambda i, j: (rl, src_row(blk(i, j)), 0)))
        in_specs.append(pl.BlockSpec((pl.Element(1), pl.Element(drop_width), pl.Element(d)),
                                     lambda i, j: (rl, drop0, 0)))
        out_specs.append(pl.BlockSpec((rb, d), lambda i, j: (blk(i, j), 0)))
        out_specs.append(pl.BlockSpec((LANES, d), lambda i, j: (0, 0)))
        out_shape.append(jax.ShapeDtypeStruct((n, d), BF16))
        out_shape.append(jax.ShapeDtypeStruct((LANES, d), BF16))
        args += [w_in_t, w_in_t]
    if final:
        in_specs.append(pl.BlockSpec((1, d), lambda i, j: (0, 0)))
        args.append(final_g.reshape(1, d))
    return pl.pallas_call(
        functools.partial(_ffn_body, final=final, n_cast=n_cast, repack=repack is not None),
        grid=(ni, nj),
        in_specs=in_specs,
        out_specs=out_specs,
        out_shape=out_shape,
        scratch_shapes=[pltpu.VMEM((tm, d), BF16)],
        compiler_params=_cparams(("arbitrary", "arbitrary")),
        name="ffn_final" if final else "ffn",
    )(*args)


def _inproj_body(x_ref, g_ref, w_ref, wl_ref, z_ref, glr_ref, xc_ref, un_ref, ph_ref, *,
                 gate_block0, ssm_block, ssm_off):
    j = pl.program_id(1)

    @pl.when(j == 0)
    def _():
        for r0 in range(0, z_ref.shape[0], INPROJ_FIRST_ROWS):
            rs = slice(r0, r0 + INPROJ_FIRST_ROWS)
            un = _rms(x_ref[rs, :], g_ref[...]).astype(BF16)
            un_ref[rs, :] = un
            glr_ref[rs, :] = _dot_nt(un, wl_ref[...])
            z_ref[rs, :] = _dot_nt(un, w_ref[...]).astype(BF16)

    @pl.when(j >= gate_block0)
    def _():
        z_ref[...] = _sigmoid(_dot_nt(un_ref[...], w_ref[...])).astype(BF16)

    @pl.when((j > 0) & (j < gate_block0))
    def _():
        acc = _dot_nt(un_ref[...], w_ref[...])
        z_ref[...] = acc.astype(BF16)

        @pl.when(j == ssm_block)
        def _():
            ncol = ph_ref.shape[0]
            for c in range(ncol):
                ph_ref[c] = acc[:, ssm_off + c * LANES:ssm_off + (c + 1) * LANES]
            rows = ph_ref.shape[1] // SSM_CHUNK
            for s in range(SSM_CHUNK):
                for c in range(ncol):
                    piece = ph_ref[c, pl.ds(s, rows, stride=SSM_CHUNK), :].astype(BF16)
                    xc_ref[c, :, s * LANES:(s + 1) * LANES] = piece


def _inproj(x, g, w_main, w_glr, layer, ssm_col0, ssm_width, gate_col0, *, tm=1024, tn=1024):
    m, d = x.shape
    n = w_main.shape[0]
    assert gate_col0 >= tn and tn <= ssm_col0 < gate_col0
    body = functools.partial(_inproj_body, gate_block0=gate_col0 // tn,
                             ssm_block=ssm_col0 // tn, ssm_off=ssm_col0 % tn)
    return pl.pallas_call(
        body,
        grid=(m // tm, n // tn),
        in_specs=[
            pl.BlockSpec((tm, d), lambda i, j: (i, 0)),
            pl.BlockSpec((None, 1, d), lambda i, j: (layer, 0, 0)),
            pl.BlockSpec((tn, d), lambda i, j: (j, 0)),
            pl.BlockSpec((LANES, d), lambda i, j: (0, 0)),
        ],
        out_specs=[
            pl.BlockSpec((tm, tn), lambda i, j: (i, j)),
            pl.BlockSpec((tm, LANES), lambda i, j: (i, 0)),
            pl.BlockSpec((ssm_width // LANES, tm // SSM_CHUNK, SSM_CHUNK * LANES), lambda i, j: (0, i, 0)),
        ],
        out_shape=[
            jax.ShapeDtypeStruct((m, n), BF16),
            jax.ShapeDtypeStruct((m, LANES), F32),
            jax.ShapeDtypeStruct((ssm_width // LANES, m // SSM_CHUNK, SSM_CHUNK * LANES), BF16),
        ],
        scratch_shapes=[pltpu.VMEM((tm, d), BF16), pltpu.VMEM((ssm_width // LANES, tm, LANES), F32)],
        compiler_params=_cparams(("parallel", "arbitrary")),
        name="inproj",
    )(x, g, w_main, w_glr)


def _pool_mix(x, halo, t, w_ref, scale):
    ts = x.shape[0]
    halo = jnp.where(t > 0, halo, 0.0)
    e = jnp.concatenate([halo, x], axis=0)
    s2 = e + pltpu.roll(e, 1, 0)
    s4 = s2 + pltpu.roll(s2, 2, 0)
    s8 = s4 + pltpu.roll(s4, 4, 0)
    s16 = s8 + pltpu.roll(s8, 8, 0)
    pos = (t * ts + 1 + lax.broadcasted_iota(jnp.int32, (ts, 1), 0)).astype(F32)
    outs = []
    for gi, (w, s) in enumerate(zip(POOL_WINDOWS, (s2, s4, s8, s16))):
        cols = slice(gi * POOL_GROUP, (gi + 1) * POOL_GROUP)
        mean = s[POOL_HALO:, cols] / jnp.minimum(pos, float(w))
        outs.append(_dot((mean - x[:, cols]).astype(BF16), w_ref[gi]))
    return jnp.concatenate(outs, axis=1) * scale


def _gla_streams(streams, tri, ts):
    c = GLA_CHUNK
    nc = ts // c
    causal = (lax.broadcasted_iota(jnp.int32, (c, c), 0)
              >= lax.broadcasted_iota(jnp.int32, (c, c), 1))
    qs, ks, vs, glrs, w2s, biases, sts = (list(a) for a in zip(*streams))
    logits = [_dot(glr, w2) + bias for glr, w2, bias in zip(glrs, w2s, biases)]
    bcs = [(jnp.minimum(l, 0.0) - jnp.log1p(jnp.exp(-jnp.abs(l)))) * (LOG2_E / GLA_TAU) for l in logits]
    his = [bc.astype(BF16) for bc in bcs]
    los = [(bc - hi.astype(F32)).astype(BF16) for bc, hi in zip(bcs, his)]
    sums = [_dot(tri, jnp.concatenate([hi, lo], axis=1)) for hi, lo in zip(his, los)]
    bcs = [s[:, :GLA_DK] + s[:, GLA_DK:] for s in sums]
    bc3 = [bc.reshape(nc, c, GLA_DK) for bc in bcs]
    b_last = [b[:, c - 1:c, :] for b in bc3]
    b_mid = [b[:, c // 2 - 1:c // 2, :] for b in bc3]
    q3 = [q.astype(F32).reshape(nc, c, GLA_DK) * (GLA_DK ** -0.5) for q in qs]
    k3 = [k.astype(F32).reshape(nc, c, GLA_DK) for k in ks]
    v3 = [v.reshape(nc, c, GLA_DV) for v in vs]
    qe = [(q * jnp.exp2(b - m)).astype(BF16) for q, b, m in zip(q3, bc3, b_mid)]
    ke = [(k * jnp.exp2(m - b)).astype(BF16) for k, b, m in zip(k3, bc3, b_mid)]
    scores = [jnp.einsum('nid,njd->nij', a, b, preferred_element_type=F32) for a, b in zip(qe, ke)]
    scores = [jnp.where(causal[None], s, 0.0).astype(BF16) for s in scores]
    o_intra = [jnp.einsum('nij,njv->niv', s, v, preferred_element_type=F32) for s, v in zip(scores, v3)]
    qd = [(q * jnp.exp2(b)).astype(BF16) for q, b in zip(q3, bc3)]
    kd = [(k * jnp.exp2(bl - b)).astype(BF16) for k, b, bl in zip(k3, bc3, b_last)]
    dec_t = [jnp.transpose(jnp.exp2(bl.reshape(nc, GLA_DK))) for bl in b_last]
    outs = [[] for _ in streams]
    for n in range(nc):
        for i in range(len(streams)):
            outs[i].append(o_intra[i][n] + _dot(qd[i][n], sts[i].astype(BF16)))
            kv = lax.dot_general(kd[i][n], v3[i][n], (((0,), (0,)), ((), ())), preferred_element_type=F32)
            sts[i] = dec_t[i][:, n:n + 1] * sts[i] + kv
    return [jnp.concatenate(o, axis=0).astype(BF16) for o in outs], sts


def _gla_body(q_ref, k_ref, v0_ref, v1_ref, glr_ref, w2_ref, b_ref, tri_ref, o_ref, st_ref, *, ts):
    @pl.when(pl.program_id(1) == 0)
    def _():
        st_ref[...] = jnp.zeros_like(st_ref)

    v_refs = (v0_ref, v1_ref)
    heads_per_block = v0_ref.shape[2] // GLA_DV
    streams, where = [], []
    for s in range(q_ref.shape[0]):
        glr = glr_ref[s].astype(BF16)
        for h in range(GLA_HEADS):
            v_ref = v_refs[h // heads_per_block]
            vs = slice((h % heads_per_block) * GLA_DV, (h % heads_per_block + 1) * GLA_DV)
            ks = slice(h * GLA_DK, (h + 1) * GLA_DK)
            streams.append((q_ref[s, :, ks], k_ref[s, :, ks], v_ref[s, :, vs], glr,
                            w2_ref[h], b_ref[h], st_ref[s, h]))
            where.append((s, h))
    outs, sts = _gla_streams(streams, tri_ref[...], ts)
    for (s, h), o, st in zip(where, outs, sts):
        st_ref[s, h] = st
        o_ref[s, :, h * GLA_DV:(h + 1) * GLA_DV] = o


def _gla(z, glr, w2, bias, layer, batch, seq, cols, *, ts=512, nseq=2):
    h = GLA_HEADS
    nt = seq // ts
    q0, k0, v0 = cols
    wb = h * GLA_DK
    z3 = z.reshape(batch, seq, z.shape[1])
    pos = np.arange(ts)
    tri = jnp.asarray((pos[:, None] // GLA_CHUNK == pos[None, :] // GLA_CHUNK) & (pos[:, None] >= pos[None, :]), BF16)
    zspec = lambda c0: pl.BlockSpec((nseq, ts, wb), lambda b, t: (b, t, c0 // wb))
    wspec = lambda last2: pl.BlockSpec((None, h) + last2, lambda b, t: (layer, 0, 0, 0))
    out = pl.pallas_call(
        functools.partial(_gla_body, ts=ts),
        grid=(batch // nseq, nt),
        in_specs=[
            zspec(q0), zspec(k0), zspec(v0), zspec(v0 + wb),
            pl.BlockSpec((nseq, ts, LANES), lambda b, t: (b, t, 0)),
            wspec((LANES, GLA_DK)), wspec((1, GLA_DK)),
            pl.BlockSpec((ts, ts), lambda b, t: (0, 0)),
        ],
        out_specs=pl.BlockSpec((nseq, ts, h * GLA_DV), lambda b, t: (b, t, 0)),
        out_shape=jax.ShapeDtypeStruct((batch, seq, h * GLA_DV), BF16),
        scratch_shapes=[pltpu.VMEM((nseq, h, GLA_DK, GLA_DV), F32)],
        compiler_params=_cparams(("parallel", "arbitrary")),
        name="gla",
    )(z3, z3, z3, z3, glr.reshape(batch, seq, LANES), w2, bias, tri)
    return out.reshape(batch * seq, h * GLA_DV)


SSM_COL_GROUPS = LANES // SSM_GROUP
SSM_BLOCK_GROUPS = 2
SSM_TPAIR = 2


def _ssm_body(x_ref, kc_ref, pc_ref, qc_ref, ar_ref, ai_ref, o_ref, hp_ref, kst_ref, v_ref, *,
              chunks_per_seq, n_state_blocks):
    u = pl.program_id(0)
    ncol, rows = x_ref.shape[0], x_ref.shape[1]
    blocks_per_col = n_state_blocks // ncol
    npairs = SSM_CHUNK // SSM_TPAIR
    ns = 2 * SSM_STATE
    bw = SSM_BLOCK_GROUPS * ns
    lane_group = lax.broadcasted_iota(jnp.int32, (1, LANES), 1) // SSM_GROUP

    def p_tile(n):
        pc = pc_ref[n // blocks_per_col]
        row_group = (lax.broadcasted_iota(jnp.int32, (pc.shape[0], 1), 0) // SSM_GROUP) % SSM_COL_GROUPS
        g0 = (n % blocks_per_col) * SSM_BLOCK_GROUPS
        return jnp.concatenate([jnp.where(row_group == g0 + gg, pc, 0.0)
                                for gg in range(SSM_BLOCK_GROUPS)], axis=1).astype(BF16)

    @pl.when(u == 0)
    def _():
        row_group = lax.broadcasted_iota(jnp.int32, (LANES, 1), 0) // SSM_GROUP

        def kbd(c, lag):
            kc = kc_ref[c, lag]
            tiled = jnp.broadcast_to(kc[None], (SSM_COL_GROUPS,) + kc.shape).reshape(LANES, LANES)
            return jnp.where(row_group == lane_group, tiled, 0.0).astype(BF16)

        for c in range(ncol):
            for r in range(SSM_CHUNK):
                left = kbd(c, SSM_CHUNK - 2 - r) if r <= SSM_CHUNK - 2 else jnp.zeros((LANES, LANES), BF16)
                kst_ref[c, r * LANES:(r + 1) * LANES, :] = jnp.concatenate(
                    [left, kbd(c, SSM_CHUNK - 1 - r)], axis=1)
        v_ref[...] = _dot(x_ref[0], p_tile(0))

    @pl.when(u < n_state_blocks)
    def _():
        rowc = lax.broadcasted_iota(jnp.int32, (rows, 1), 0) % chunks_per_seq
        hst = v_ref[...]
        nxt = jnp.minimum(u + 1, n_state_blocks - 1)
        v_next = _dot(x_ref[nxt // blocks_per_col], p_tile(nxt))
        k, si = 1, 0
        while k < chunks_per_seq:
            hs = jnp.where(rowc >= k, pltpu.roll(hst, k, 0), 0.0)
            sw = jnp.concatenate([pltpu.roll(hs[:, b0:b0 + ns], SSM_STATE, 1)
                                  for b0 in range(0, bw, ns)], axis=1)
            hst = hst + ar_ref[u, si:si + 1, :] * hs + ai_ref[u, si:si + 1, :] * sw
            k *= 2
            si += 1
        hp_ref[u] = jnp.where(rowc >= 1, pltpu.roll(hst, 1, 0), 0.0).astype(BF16)
        v_ref[...] = v_next

    @pl.when(u >= n_state_blocks)
    def _():
        ob = u - n_state_blocks
        c = ob // npairs
        v = ob % npairs
        grp = lax.broadcasted_iota(jnp.int32, (SSM_COL_GROUPS, 1, 1), 0)
        q_tile = jnp.concatenate(
            [jnp.where(grp == lane_group[None], qc_ref[c, SSM_TPAIR * v + tt][None], 0.0)
             .reshape(SSM_COL_GROUPS * ns, LANES) for tt in range(SSM_TPAIR)], axis=1).astype(BF16)
        yq = _dot(hp_ref[c * blocks_per_col], q_tile[0:bw, :])
        for kk in range(1, blocks_per_col):
            yq = yq + _dot(hp_ref[c * blocks_per_col + kk], q_tile[kk * bw:(kk + 1) * bw, :])
        for vs in range(npairs):
            kext = (vs + 1) * SSM_TPAIR * LANES

            @pl.when(v == vs)
            def _():
                start = (npairs - 1 - vs) * SSM_TPAIR * LANES
                intra = _dot(x_ref[c, :, 0:kext], kst_ref[c, start:start + kext, :])
                o_ref[...] = (yq + intra).astype(BF16)


def _ssm_core(xcat, kc, pc, qc, ar, ai, layer, chunks_per_seq):
    ncol, rows, xw = xcat.shape
    nb = ar.shape[1]
    bw = ar.shape[3]
    ow = SSM_TPAIR * LANES
    nout = ncol * SSM_CHUNK // SSM_TPAIR
    one = pl.Buffered(1)
    whole = lambda a: pl.BlockSpec((None,) + a.shape[1:], lambda u: (layer,) + (0,) * (a.ndim - 1),
                                   pipeline_mode=one)
    return pl.pallas_call(
        functools.partial(_ssm_body, chunks_per_seq=chunks_per_seq, n_state_blocks=nb),
        grid=(nb + nout,),
        in_specs=[
            pl.BlockSpec((ncol, rows, xw), lambda u: (0, 0, 0), pipeline_mode=one),
            whole(kc), whole(pc), whole(qc), whole(ar), whole(ai),
        ],
        out_specs=pl.BlockSpec((None, rows, ow), lambda u: (jnp.maximum(u - nb, 0), 0, 0)),
        out_shape=jax.ShapeDtypeStruct((nout, rows, ow), BF16),
        scratch_shapes=[
            pltpu.VMEM((nb, rows, bw), BF16),
            pltpu.VMEM((ncol, SSM_CHUNK * LANES, ow), BF16),
            pltpu.VMEM((rows, bw), F32),
        ],
        compiler_params=_cparams(("arbitrary",)),
        name="ssm_core",
    )(xcat, kc, pc, qc, ar, ai)


def _ssm_operators(a_re, a_im, log_dt, b_re, b_im, c_re, c_im, chunks_per_seq):
    t = SSM_CHUNK
    g, p = a_re.shape
    hg = b_re.shape[-1]
    cgc = SSM_COL_GROUPS
    ncol = g // cgc
    bgc = SSM_BLOCK_GROUPS
    dt = jnp.exp(log_dt)[:, None]
    lam_re, lam_im = dt * a_re, dt * a_im

    def powers(n):
        n = jnp.asarray(n, F32)[:, None, None]
        mag = jnp.exp(n * lam_re)
        return mag * jnp.cos(n * lam_im), mag * jnp.sin(n * lam_im)

    ab_re, ab_im = powers([1.0])
    ab_re, ab_im = ab_re[0], ab_im[0]
    den = a_re * a_re + a_im * a_im
    f_re = ((ab_re - 1.0) * a_re + ab_im * a_im) / den
    f_im = (ab_im * a_re - (ab_re - 1.0) * a_im) / den
    bb_re = f_re[..., None] * b_re - f_im[..., None] * b_im
    bb_im = f_re[..., None] * b_im + f_im[..., None] * b_re
    pw_re, pw_im = powers(np.arange(t + 1))
    cp_re = c_re[None] * pw_re[:, :, None, :] - c_im[None] * pw_im[:, :, None, :]
    cp_im = c_re[None] * pw_im[:, :, None, :] + c_im[None] * pw_re[:, :, None, :]
    ktau = (jnp.einsum('tghp,gpk->tkgh', cp_re[:t], bb_re)
            - jnp.einsum('tghp,gpk->tkgh', cp_im[:t], bb_im))
    kc = ktau.reshape(t, hg, ncol, cgc * hg).transpose(2, 0, 1, 3)
    bt_re, bt_im = bb_re.transpose(0, 2, 1), bb_im.transpose(0, 2, 1)
    rp_re = pw_re[t - 1 - np.arange(t)][:, :, None, :]
    rp_im = pw_im[t - 1 - np.arange(t)][:, :, None, :]
    pin = jnp.concatenate([rp_re * bt_re[None] - rp_im * bt_im[None],
                           rp_re * bt_im[None] + rp_im * bt_re[None]], axis=3)
    pc = pin.reshape(t, ncol, cgc, hg, 2 * p).transpose(1, 0, 2, 3, 4).reshape(ncol, t * cgc * hg, 2 * p)
    ct_re, ct_im = c_re.transpose(2, 0, 1)[None], c_im.transpose(2, 0, 1)[None]
    pt_re = pw_re[1:].transpose(0, 2, 1)[..., None]
    pt_im = pw_im[1:].transpose(0, 2, 1)[..., None]
    qg = jnp.concatenate([ct_re * pt_re - ct_im * pt_im,
                          -(ct_re * pt_im + ct_im * pt_re)], axis=1)
    qc = qg.reshape(t, 2 * p, ncol, cgc * hg).transpose(2, 0, 1, 3)
    nblk = g // bgc
    nsteps = int(np.log2(chunks_per_seq))
    sr, si = powers(t * 2.0 ** np.arange(nsteps))
    blk = lambda a: a.reshape(nsteps, nblk, bgc * 2 * p).transpose(1, 0, 2)
    ar = blk(jnp.concatenate([sr, sr], axis=2))
    ai = blk(jnp.concatenate([-si, si], axis=2))
    return kc, pc, qc, ar, ai


def _merge_body(x_ref, pin_ref, halo_ref, og_ref, r0_ref, r1_ref, yc_ref, sin_ref, g0_ref, g1_ref, g2_ref,
                pw_ref, ps_ref, gain_ref, d_ref, wglu_ref, wbp_ref, wbg0_ref, wbg1_ref, wbs_ref, wo_ref,
                o_ref, ys_ref, *, tiles_per_seq):
    t = pl.program_id(0) % tiles_per_seq
    yp = _pool_mix(pin_ref[...].astype(F32), halo_ref[...].astype(F32), t, pw_ref, ps_ref[...])
    r_refs = (r0_ref, r1_ref)
    heads_per_block = r0_ref.shape[1] // GLA_DV
    yg_heads = []
    for h in range(GLA_HEADS):
        hs = slice(h * GLA_DV, (h + 1) * GLA_DV)
        rs = slice((h % heads_per_block) * GLA_DV, (h % heads_per_block + 1) * GLA_DV)
        o = og_ref[:, hs].astype(F32)
        o = o * lax.rsqrt(jnp.mean(o * o, axis=-1, keepdims=True) + EPS) * gain_ref[:, hs]
        r = r_refs[h // heads_per_block][:, rs].astype(F32)
        yg_heads.append((o * (r * _sigmoid(r))).astype(BF16))
    yg = jnp.concatenate(yg_heads, axis=1)
    rows = yc_ref.shape[1]
    npairs = SSM_CHUNK // SSM_TPAIR
    for ob in range(yc_ref.shape[0]):
        c, v = divmod(ob, npairs)
        plane = yc_ref[ob].astype(F32)
        for tt in range(SSM_TPAIR):
            ys_ref[c, pl.ds(SSM_TPAIR * v + tt, rows, stride=SSM_CHUNK), :] = plane[:, tt * LANES:(tt + 1) * LANES]
    ys = jnp.concatenate([ys_ref[c] for c in range(ys_ref.shape[0])], axis=1)
    y = ys + d_ref[...] * sin_ref[...].astype(F32)
    y = jax.nn.gelu(y)
    y = y * _sigmoid(_dot(y.astype(BF16), wglu_ref[...]))
    m = g0_ref[...].astype(F32) * _dot(yp.astype(BF16), wbp_ref[...])
    half = wbg0_ref.shape[0]
    gla_proj = _dot(yg[:, :half], wbg0_ref[...]) + _dot(yg[:, half:], wbg1_ref[...])
    m = m + g1_ref[...].astype(F32) * gla_proj
    m = m + g2_ref[...].astype(F32) * _dot(y.astype(BF16), wbs_ref[...])
    o_ref[...] = x_ref[...] + _dot(m.astype(BF16), wo_ref[...])


def _merge(x, o_gla, ycat, z, r_col0, ssm_col0, gate_col0, seq, pool_w, pool_scale, gla_gain,
           d_skip, w_glu, wb, w_out, layer, *, tm=256):
    m, d = x.shape
    wp, wg, ws = pool_scale.shape[2], o_gla.shape[1], d_skip.shape[2]
    br = BRANCH_ROWS
    assert wp == br and ws == br and wg == 2 * br and seq % tm == 0 and tm % POOL_HALO == 0
    gb = gate_col0 // d
    hb = tm // POOL_HALO
    one = pl.Buffered(1)
    wspec = lambda shape, r: pl.BlockSpec((None,) + shape, lambda i: (layer, r, 0), pipeline_mode=one)
    return pl.pallas_call(
        functools.partial(_merge_body, tiles_per_seq=seq // tm),
        grid=(m // tm,),
        in_specs=[
            pl.BlockSpec((tm, d), lambda i: (i, 0)),
            pl.BlockSpec((tm, wp), lambda i: (i, 0)),
            pl.BlockSpec((POOL_HALO, wp), lambda i: (jnp.maximum(i * hb - 1, 0), 0)),
            pl.BlockSpec((tm, wg), lambda i: (i, 0)),
            pl.BlockSpec((tm, br), lambda i: (i, r_col0 // br)),
            pl.BlockSpec((tm, br), lambda i: (i, r_col0 // br + 1)),
            pl.BlockSpec((ycat.shape[0], tm // SSM_CHUNK, ycat.shape[2]), lambda i: (0, i, 0)),
            pl.BlockSpec((tm, ws), lambda i: (i, ssm_col0 // ws)),
            pl.BlockSpec((tm, d), lambda i: (i, gb)),
            pl.BlockSpec((tm, d), lambda i: (i, gb + 1)),
            pl.BlockSpec((tm, d), lambda i: (i, gb + 2)),
            pl.BlockSpec((None, len(POOL_WINDOWS), POOL_GROUP, POOL_GROUP), lambda i: (layer, 0, 0, 0),
                         pipeline_mode=one),
            wspec((1, wp), 0),
            wspec((1, wg), 0),
            wspec((1, ws), 0),
            wspec((ws, ws), 0),
            wspec((br, d), 0),
            wspec((br, d), 1),
            wspec((br, d), 2),
            wspec((br, d), 3),
            wspec((d, d), 0),
        ],
        out_specs=pl.BlockSpec((tm, d), lambda i: (i, 0)),
        out_shape=jax.ShapeDtypeStruct((m, d), F32),
        scratch_shapes=[pltpu.VMEM((ws // LANES, tm, LANES), F32)],
        compiler_params=_cparams(("parallel",)),
        name="merge",
    )(x, z, z, o_gla, z, z, ycat, z, z, z, z, pool_w, pool_scale, gla_gain, d_skip, w_glu,
      wb, wb, wb, wb, w_out)


def kernel(x, ffn1_norm, ffn1_w_gate, ffn1_w_up, ffn1_w_down, mix_norm, w_in, pool_w, pool_scale, gla_w_gate2, gla_gate_bias, gla_norm, ssm_a_re, ssm_a_im, ssm_log_dt, ssm_b_re, ssm_b_im, ssm_c_re, ssm_c_im, ssm_d, ssm_w_glu, w_branch, w_out, ffn2_norm, ffn2_w_gate, ffn2_w_up, ffn2_w_down, final_norm):
    batch, seq, d = x.shape
    depth = w_in.shape[0]
    m = batch * seq
    pool_width = pool_scale.shape[1]
    qk_width = GLA_HEADS * GLA_DK
    v_width = GLA_HEADS * GLA_DV
    ssm_width = ssm_d.shape[1]
    q0 = pool_width
    k0 = q0 + qk_width
    v0 = k0 + qk_width
    r0 = v0 + v_width
    glr0 = r0 + v_width
    ssm0 = glr0
    gate0 = ssm0 + ssm_width
    chunks_per_seq = seq // SSM_CHUNK

    bf = lambda a: a.astype(BF16)
    row3 = lambda a: a.reshape(depth, 1, a.shape[-1])
    w_in_t = jnp.swapaxes(w_in, 1, 2)
    ffn1_w = (ffn1_w_gate, ffn1_w_up, ffn1_w_down)
    ffn2_w = (ffn2_w_gate, ffn2_w_up, ffn2_w_down)
    f1 = [bf(w[0]) for w in ffn1_w]
    wbr, wo, wglu, pw = bf(w_branch), bf(w_out), bf(ssm_w_glu), bf(pool_w)
    n1, nm, n2 = row3(ffn1_norm), row3(mix_norm), row3(ffn2_norm)
    pscale, dskip = row3(pool_scale), row3(ssm_d)
    w2 = gla_w_gate2.reshape(depth, GLA_RANK, GLA_HEADS, GLA_DK).transpose(0, 2, 1, 3)
    w2 = bf(jnp.pad(w2, ((0, 0), (0, 0), (0, LANES - GLA_RANK), (0, 0))))
    gbias = gla_gate_bias.reshape(depth, GLA_HEADS, 1, GLA_DK)
    ggain = row3(gla_norm)
    ssm_ops = jax.vmap(functools.partial(_ssm_operators, chunks_per_seq=chunks_per_seq))(
        ssm_a_re, ssm_a_im, ssm_log_dt, ssm_b_re, ssm_b_im, ssm_c_re, ssm_c_im)

    xf = x.reshape(m, d)
    for l in range(depth):
        xf, *side = _ffn(xf, n1, l, *f1, cast=(l,) + ffn2_w, repack=(l, w_in_t, glr0, GLA_RANK))
        f2, (w_main, w_glr) = side[:3], side[3:]
        z, glr, xcat = _inproj(xf, nm, w_main, w_glr, l, ssm0, ssm_width, gate0)
        o_gla = _gla(z, glr, w2, gbias, l, batch, seq, (q0, k0, v0))
        ycat = _ssm_core(xcat, *ssm_ops, l, chunks_per_seq)
        xf = _merge(xf, o_gla, ycat, z, r0, ssm0, gate0, seq, pw, pscale, ggain, dskip, wglu, wbr, wo, l)
        last = l == depth - 1
        xf, *f1 = _ffn(xf, n2, l, *f2, cast=None if last else (l + 1,) + ffn1_w,
                       final_g=final_norm if last else None)
    return xf.reshape(batch, seq, d)
```

```python
import functools

import jax
import jax.numpy as jnp
import numpy as np
from jax import lax
from jax.experimental import pallas as pl
from jax.experimental.pallas import tpu as pltpu

F32 = jnp.float32
BF16 = jnp.bfloat16

EPS = 1e-6
POOL_WINDOWS = (2, 4, 8, 16)
POOL_GROUP = 128
POOL_HALO = 16
GLA_HEADS = 4
GLA_DK = 128
GLA_DV = 256
GLA_RANK = 16
GLA_TAU = 16.0
LOG2_E = 1.4426950408889634
GLA_CHUNK = 64
SSM_GROUP = 16
SSM_STATE = 64
SSM_CHUNK = 16
LANES = 128
FFN_DOWN_CHUNK = 512
FFN_FIRST_ROWS = 512
REPACK_ROWS = 64
INPROJ_FIRST_ROWS = 256
BRANCH_ROWS = 512

VMEM_LIMIT = 60 * 1024 * 1024


def _cparams(sem):
    return pltpu.CompilerParams(dimension_semantics=sem, vmem_limit_bytes=VMEM_LIMIT)


def _rms(xf, g):
    return xf * lax.rsqrt(jnp.mean(xf * xf, axis=-1, keepdims=True) + EPS) * g


def _dot(a, b):
    return jnp.dot(a, b, preferred_element_type=F32)


def _sigmoid(a):
    return 0.5 * jnp.tanh(0.5 * a) + 0.5


def _dot_nt(a, b_t):
    return lax.dot_general(a, b_t, (((1,), (1,)), ((), ())), preferred_element_type=F32)


def _ffn_body(x_ref, g_ref, wg_ref, wu_ref, wd_ref, *rest, final, n_cast, repack):
    rest = list(rest)
    cast_in = [rest.pop(0) for _ in range(n_cast)]
    rp_in = [rest.pop(0) for _ in range(2 if repack else 0)]
    fg_ref = rest.pop(0) if final else None
    o_ref = rest.pop(0)
    cast_out = [rest.pop(0) for _ in range(n_cast)]
    rp_out = [rest.pop(0) for _ in range(2 if repack else 0)]
    xn_ref, = rest
    j = pl.program_id(1)

    for src, dst in zip(cast_in, cast_out):
        dst[...] = src[...].astype(BF16)
    if repack:
        (rows_ref, drop_ref), (w_ref, wl_ref) = rp_in, rp_out
        w_ref[...] = rows_ref[0].astype(BF16)
        pad = jnp.zeros((wl_ref.shape[0] - drop_ref.shape[1], drop_ref.shape[2]), F32)
        wl_ref[...] = jnp.concatenate([drop_ref[0], pad], axis=0).astype(BF16)

    def swiglu_step(rs, xn, first):
        a = _dot(xn, wg_ref[...])
        b = _dot(xn, wu_ref[...])
        h = (a * _sigmoid(a) * b).astype(BF16)
        for c0 in range(0, o_ref.shape[1], FFN_DOWN_CHUNK):
            cs = slice(c0, c0 + FFN_DOWN_CHUNK)
            if first:
                o_ref[rs, cs] = _dot(h, wd_ref[:, cs])
            else:
                o_ref[rs, cs] += _dot(h, wd_ref[:, cs])

    @pl.when(j == 0)
    def _():
        for r0 in range(0, o_ref.shape[0], FFN_FIRST_ROWS):
            rs = slice(r0, r0 + FFN_FIRST_ROWS)
            xn = _rms(x_ref[rs, :], g_ref[...]).astype(BF16)
            xn_ref[rs, :] = xn
            swiglu_step(rs, xn, True)

    last = pl.num_programs(1) - 1

    @pl.when((j > 0) & (j < last))
    def _():
        swiglu_step(slice(None), xn_ref[...], False)

    @pl.when(j == last)
    def _():
        for r0 in range(0, o_ref.shape[0], FFN_FIRST_ROWS):
            rs = slice(r0, r0 + FFN_FIRST_ROWS)
            swiglu_step(rs, xn_ref[rs, :], False)
            y = x_ref[rs, :] + 0.5 * o_ref[rs, :]
            if final:
                y = _rms(y, fg_ref[...])
            o_ref[rs, :] = y


def _ffn(x, g, layer, wg, wu, wd, cast=None, repack=None, final_g=None, *, tm=1024, tf=512):
    m, d = x.shape
    f = wg.shape[1]
    ni, nj = m // tm, f // tf
    final = final_g is not None
    in_specs = [
        pl.BlockSpec((tm, d), lambda i, j: (i, 0)),
        pl.BlockSpec((None, 1, d), lambda i, j: (layer, 0, 0)),
        pl.BlockSpec((d, tf), lambda i, j: (0, j)),
        pl.BlockSpec((d, tf), lambda i, j: (0, j)),
        pl.BlockSpec((tf, d), lambda i, j: (j, 0)),
    ]
    args = [x, g, wg, wu, wd]
    out_specs = [pl.BlockSpec((tm, d), lambda i, j: (i, 0))]
    out_shape = [jax.ShapeDtypeStruct((m, d), F32)]
    n_cast = 0
    if cast is not None:
        cl, cg, cu, cd = cast
        n_cast = 3
        rows_up = d // ni
        rows_dn = f // (ni * nj)
        assert d % ni == 0 and f % (ni * nj) == 0 and rows_up % 16 == 0 and rows_dn % 16 == 0
        for w in (cg, cu):
            in_specs.append(pl.BlockSpec((None, rows_up, tf), lambda i, j: (cl, i, j)))
            out_specs.append(pl.BlockSpec((rows_up, tf), lambda i, j: (i, j)))
            out_shape.append(jax.ShapeDtypeStruct((d, f), BF16))
            args.append(w)
        in_specs.append(pl.BlockSpec((None, rows_dn, d), lambda i, j: (cl, i * nj + j, 0)))
        out_specs.append(pl.BlockSpec((rows_dn, d), lambda i, j: (i * nj + j, 0)))
        out_shape.append(jax.ShapeDtypeStruct((f, d), BF16))
        args.append(cd)
    if repack is not None:
        rl, w_in_t, drop0, drop_width = repack
        n = w_in_t.shape[1] - drop_width
        rb = REPACK_ROWS
        nblk = n // rb
        assert n % rb == 0 and nblk <= ni * nj and drop0 % rb == 0 and drop_width % 8 == 0
        blk = lambda i, j: jnp.minimum(i * nj + j, nblk - 1)
        src_row = lambda b: (b * (rb // 8) + jnp.where(b >= drop0 // rb, drop_width // 8, 0)) * 8
        in_specs.append(pl.BlockSpec((pl.Element(1), pl.Element(rb), pl.Element(d)),
                                     lambda i, j: (rl, src_row(blk(i, j)), 0)))
        in_specs.append(pl.BlockSpec((pl.Element(1), pl.Element(drop_width), pl.Element(d)),
                                     lambda i, j: (rl, drop0, 0)))
        out_specs.append(pl.BlockSpec((rb, d), lambda i, j: (blk(i, j), 0)))
        out_specs.append(pl.BlockSpec((LANES, d), lambda i, j: (0, 0)))
        out_shape.append(jax.ShapeDtypeStruct((n, d), BF16))
        out_shape.append(jax.ShapeDtypeStruct((LANES, d), BF16))
        args += [w_in_t, w_in_t]
    if final:
        in_specs.append(pl.BlockSpec((1, d), lambda i, j: (0, 0)))
        args.append(final_g.reshape(1, d))
    return pl.pallas_call(
        functools.partial(_ffn_body, final=final, n_cast=n_cast, repack=repack is not None),
        grid=(ni, nj),
        in_specs=in_specs,
        out_specs=out_specs,
        out_shape=out_shape,
        scratch_shapes=[pltpu.VMEM((tm, d), BF16)],
        compiler_params=_cparams(("arbitrary", "arbitrary")),
        name="ffn_final" if final else "ffn",
    )(*args)


def _inproj_body(x_ref, g_ref, w_ref, wl_ref, z_ref, glr_ref, xc_ref, un_ref, ph_ref, *,
                 gate_block0, ssm_block, ssm_off):
    j = pl.program_id(1)

    @pl.when(j == 0)
    def _():
        for r0 in range(0, z_ref.shape[0], INPROJ_FIRST_ROWS):
            rs = slice(r0, r0 + INPROJ_FIRST_ROWS)
            un = _rms(x_ref[rs, :], g_ref[...]).astype(BF16)
            un_ref[rs, :] = un
            glr_ref[rs, :] = _dot_nt(un, wl_ref[...])
            z_ref[rs, :] = _dot_nt(un, w_ref[...]).astype(BF16)

    @pl.when(j >= gate_block0)
    def _():
        z_ref[...] = _sigmoid(_dot_nt(un_ref[...], w_ref[...])).astype(BF16)

    @pl.when((j > 0) & (j < gate_block0))
    def _():
        acc = _dot_nt(un_ref[...], w_ref[...])
        z_ref[...] = acc.astype(BF16)

        @pl.when(j == ssm_block)
        def _():
            ncol = ph_ref.shape[0]
            for c in range(ncol):
                ph_ref[c] = acc[:, ssm_off + c * LANES:ssm_off + (c + 1) * LANES]
            rows = ph_ref.shape[1] // SSM_CHUNK
            for s in range(SSM_CHUNK):
                for c in range(ncol):
                    piece = ph_ref[c, pl.ds(s, rows, stride=SSM_CHUNK), :].astype(BF16)
                    xc_ref[c, :, s * LANES:(s + 1) * LANES] = piece


def _inproj(x, g, w_main, w_glr, layer, ssm_col0, ssm_width, gate_col0, *, tm=1024, tn=1024):
    m, d = x.shape
    n = w_main.shape[0]
    assert gate_col0 >= tn and tn <= ssm_col0 < gate_col0
    body = functools.partial(_inproj_body, gate_block0=gate_col0 // tn,
                             ssm_block=ssm_col0 // tn, ssm_off=ssm_col0 % tn)
    return pl.pallas_call(
        body,
        grid=(m // tm, n // tn),
        in_specs=[
            pl.BlockSpec((tm, d), lambda i, j: (i, 0)),
            pl.BlockSpec((None, 1, d), lambda i, j: (layer, 0, 0)),
            pl.BlockSpec((tn, d), lambda i, j: (j, 0)),
            pl.BlockSpec((LANES, d), lambda i, j: (0, 0)),
        ],
        out_specs=[
            pl.BlockSpec((tm, tn), lambda i, j: (i, j)),
            pl.BlockSpec((tm, LANES), lambda i, j: (i, 0)),
            pl.BlockSpec((ssm_width // LANES, tm // SSM_CHUNK, SSM_CHUNK * LANES), lambda i, j: (0, i, 0)),
        ],
        out_shape=[
            jax.ShapeDtypeStruct((m, n), BF16),
            jax.ShapeDtypeStruct((m, LANES), F32),
            jax.ShapeDtypeStruct((ssm_width // LANES, m // SSM_CHUNK, SSM_CHUNK * LANES), BF16),
        ],
        scratch_shapes=[pltpu.VMEM((tm, d), BF16), pltpu.VMEM((ssm_width // LANES, tm, LANES), F32)],
        compiler_params=_cparams(("parallel", "arbitrary")),
        name="inproj",
    )(x, g, w_main, w_glr)


def _pool_mix(x, halo, t, w_ref, scale):
    ts = x.shape[0]
    halo = jnp.where(t > 0, halo, 0.0)
    e = jnp.concatenate([halo, x], axis=0)
    s2 = e + pltpu.roll(e, 1, 0)
    s4 = s2 + pltpu.roll(s2, 2, 0)
    s8 = s4 + pltpu.roll(s4, 4, 0)
    s16 = s8 + pltpu.roll(s8, 8, 0)
    pos = (t * ts + 1 + lax.broadcasted_iota(jnp.int32, (ts, 1), 0)).astype(F32)
    outs = []
    for gi, (w, s) in enumerate(zip(POOL_WINDOWS, (s2, s4, s8, s16))):
        cols = slice(gi * POOL_GROUP, (gi + 1) * POOL_GROUP)
        mean = s[POOL_HALO:, cols] / jnp.minimum(pos, float(w))
        outs.append(_dot((mean - x[:, cols]).astype(BF16), w_ref[gi]))
    return jnp.concatenate(outs, axis=1) * scale


def _gla_streams(streams, tri, ts):
    c = GLA_CHUNK
    nc = ts // c
    causal = (lax.broadcasted_iota(jnp.int32, (c, c), 0)
              >= lax.broadcasted_iota(jnp.int32, (c, c), 1))
    qs, ks, vs, glrs, w2s, biases, sts = (list(a) for a in zip(*streams))
    logits = [_dot(glr, w2) + bias for glr, w2, bias in zip(glrs, w2s, biases)]
    bcs = [(jnp.minimum(l, 0.0) - jnp.log1p(jnp.exp(-jnp.abs(l)))) * (LOG2_E / GLA_TAU) for l in logits]
    his = [bc.astype(BF16) for bc in bcs]
    los = [(bc - hi.astype(F32)).astype(BF16) for bc, hi in zip(bcs, his)]
    sums = [_dot(tri, jnp.concatenate([hi, lo], axis=1)) for hi, lo in zip(his, los)]
    bcs = [s[:, :GLA_DK] + s[:, GLA_DK:] for s in sums]
    bc3 = [bc.reshape(nc, c, GLA_DK) for bc in bcs]
    b_last = [b[:, c - 1:c, :] for b in bc3]
    b_mid = [b[:, c // 2 - 1:c // 2, :] for b in bc3]
    q3 = [q.astype(F32).reshape(nc, c, GLA_DK) * (GLA_DK ** -0.5) for q in qs]
    k3 = [k.astype(F32).reshape(nc, c, GLA_DK) for k in ks]
    v3 = [v.reshape(nc, c, GLA_DV) for v in vs]
    qe = [(q * jnp.exp2(b - m)).astype(BF16) for q, b, m in zip(q3, bc3, b_mid)]
    ke = [(k * jnp.exp2(m - b)).astype(BF16) for k, b, m in zip(k3, bc3, b_mid)]
    scores = [jnp.einsum('nid,njd->nij', a, b, preferred_element_type=F32) for a, b in zip(qe, ke)]
    scores = [jnp.where(causal[None], s, 0.0).astype(BF16) for s in scores]
    o_intra = [jnp.einsum('nij,njv->niv', s, v, preferred_element_type=F32) for s, v in zip(scores, v3)]
    qd = [(q * jnp.exp2(b)).astype(BF16) for q, b in zip(q3, bc3)]
    kd = [(k * jnp.exp2(bl - b)).astype(BF16) for k, b, bl in zip(k3, bc3, b_last)]
    dec_t = [jnp.transpose(jnp.exp2(bl.reshape(nc, GLA_DK))) for bl in b_last]
    outs = [[] for _ in streams]
    for n in range(nc):
        for i in range(len(streams)):
            outs[i].append(o_intra[i][n] + _dot(qd[i][n], sts[i].astype(BF16)))
            kv = lax.dot_general(kd[i][n], v3[i][n], (((0,), (0,)), ((), ())), preferred_element_type=F32)
            sts[i] = dec_t[i][:, n:n + 1] * sts[i] + kv
    return [jnp.concatenate(o, axis=0).astype(BF16) for o in outs], sts


def _gla_body(q_ref, k_ref, v0_ref, v1_ref, glr_ref, w2_ref, b_ref, tri_ref, o_ref, st_ref, *, ts):
    @pl.when(pl.program_id(1) == 0)
    def _():
        st_ref[...] = jnp.zeros_like(st_ref)

    v_refs = (v0_ref, v1_ref)
    heads_per_block = v0_ref.shape[2] // GLA_DV
    streams, where = [], []
    for s in range(q_ref.shape[0]):
        glr = glr_ref[s].astype(BF16)
        for h in range(GLA_HEADS):
            v_ref = v_refs[h // heads_per_block]
            vs = slice((h % heads_per_block) * GLA_DV, (h % heads_per_block + 1) * GLA_DV)
            ks = slice(h * GLA_DK, (h + 1) * GLA_DK)
            streams.append((q_ref[s, :, ks], k_ref[s, :, ks], v_ref[s, :, vs], glr,
                            w2_ref[h], b_ref[h], st_ref[s, h]))
            where.append((s, h))
    outs, sts = _gla_streams(streams, tri_ref[...], ts)
    for (s, h), o, st in zip(where, outs, sts):
        st_ref[s, h] = st
        o_ref[s, :, h * GLA_DV:(h + 1) * GLA_DV] = o


def _gla(z, glr, w2, bias, layer, batch, seq, cols, *, ts=512, nseq=2):
    h = GLA_HEADS
    nt = seq // ts
    q0, k0, v0 = cols
    wb = h * GLA_DK
    z3 = z.reshape(batch, seq, z.shape[1])
    pos = np.arange(ts)
    tri = jnp.asarray((pos[:, None] // GLA_CHUNK == pos[None, :] // GLA_CHUNK) & (pos[:, None] >= pos[None, :]), BF16)
    zspec = lambda c0: pl.BlockSpec((nseq, ts, wb), lambda b, t: (b, t, c0 // wb))
    wspec = lambda last2: pl.BlockSpec((None, h) + last2, lambda b, t: (layer, 0, 0, 0))
    out = pl.pallas_call(
        functools.partial(_gla_body, ts=ts),
        grid=(batch // nseq, nt),
        in_specs=[
            zspec(q0), zspec(k0), zspec(v0), zspec(v0 + wb),
            pl.BlockSpec((nseq, ts, LANES), lambda b, t: (b, t, 0)),
            wspec((LANES, GLA_DK)), wspec((1, GLA_DK)),
            pl.BlockSpec((ts, ts), lambda b, t: (0, 0)),
        ],
        out_specs=pl.BlockSpec((nseq, ts, h * GLA_DV), lambda b, t: (b, t, 0)),
        out_shape=jax.ShapeDtypeStruct((batch, seq, h * GLA_DV), BF16),
        scratch_shapes=[pltpu.VMEM((nseq, h, GLA_DK, GLA_DV), F32)],
        compiler_params=_cparams(("parallel", "arbitrary")),
        name="gla",
    )(z3, z3, z3, z3, glr.reshape(batch, seq, LANES), w2, bias, tri)
    return out.reshape(batch * seq, h * GLA_DV)


SSM_COL_GROUPS = LANES // SSM_GROUP
SSM_BLOCK_GROUPS = 2
SSM_TPAIR = 2


def _ssm_body(x_ref, kc_ref, pc_ref, qc_ref, ar_ref, ai_ref, o_ref, hp_ref, kst_ref, v_ref, *,
              chunks_per_seq, n_state_blocks):
    u = pl.program_id(0)
    ncol, rows = x_ref.shape[0], x_ref.shape[1]
    blocks_per_col = n_state_blocks // ncol
    npairs = SSM_CHUNK // SSM_TPAIR
    ns = 2 * SSM_STATE
    bw = SSM_BLOCK_GROUPS * ns
    lane_group = lax.broadcasted_iota(jnp.int32, (1, LANES), 1) // SSM_GROUP

    def p_tile(n):
        pc = pc_ref[n // blocks_per_col]
        row_group = (lax.broadcasted_iota(jnp.int32, (pc.shape[0], 1), 0) // SSM_GROUP) % SSM_COL_GROUPS
        g0 = (n % blocks_per_col) * SSM_BLOCK_GROUPS
        return jnp.concatenate([jnp.where(row_group == g0 + gg, pc, 0.0)
                                for gg in range(SSM_BLOCK_GROUPS)], axis=1).astype(BF16)

    @pl.when(u == 0)
    def _():
        row_group = lax.broadcasted_iota(jnp.int32, (LANES, 1), 0) // SSM_GROUP

        def kbd(c, lag):
            kc = kc_ref[c, lag]
            tiled = jnp.broadcast_to(kc[None], (SSM_COL_GROUPS,) + kc.shape).reshape(LANES, LANES)
            return jnp.where(row_group == lane_group, tiled, 0.0).astype(BF16)

        for c in range(ncol):
            for r in range(SSM_CHUNK):
                left = kbd(c, SSM_CHUNK - 2 - r) if r <= SSM_CHUNK - 2 else jnp.zeros((LANES, LANES), BF16)
                kst_ref[c, r * LANES:(r + 1) * LANES, :] = jnp.concatenate(
                    [left, kbd(c, SSM_CHUNK - 1 - r)], axis=1)
        v_ref[...] = _dot(x_ref[0], p_tile(0))

    @pl.when(u < n_state_blocks)
    def _():
        rowc = lax.broadcasted_iota(jnp.int32, (rows, 1), 0) % chunks_per_seq
        hst = v_ref[...]
        nxt = jnp.minimum(u + 1, n_state_blocks - 1)
        v_next = _dot(x_ref[nxt // blocks_per_col], p_tile(nxt))
        k, si = 1, 0
        while k < chunks_per_seq:
            hs = jnp.where(rowc >= k, pltpu.roll(hst, k, 0), 0.0)
            sw = jnp.concatenate([pltpu.roll(hs[:, b0:b0 + ns], SSM_STATE, 1)
                                  for b0 in range(0, bw, ns)], axis=1)
            hst = hst + ar_ref[u, si:si + 1, :] * hs + ai_ref[u, si:si + 1, :] * sw
            k *= 2
            si += 1
        hp_ref[u] = jnp.where(rowc >= 1, pltpu.roll(hst, 1, 0), 0.0).astype(BF16)
        v_ref[...] = v_next

    @pl.when(u >= n_state_blocks)
    def _():
        ob = u - n_state_blocks
        c = ob // npairs
        v = ob % npairs
        grp = lax.broadcasted_iota(jnp.int32, (SSM_COL_GROUPS, 1, 1), 0)
        q_tile = jnp.concatenate(
            [jnp.where(grp == lane_group[None], qc_ref[c, SSM_TPAIR * v + tt][None], 0.0)
             .reshape(SSM_COL_GROUPS * ns, LANES) for tt in range(SSM_TPAIR)], axis=1).astype(BF16)
        yq = _dot(hp_ref[c * blocks_per_col], q_tile[0:bw, :])
        for kk in range(1, blocks_per_col):
            yq = yq + _dot(hp_ref[c * blocks_per_col + kk], q_tile[kk * bw:(kk + 1) * bw, :])
        for vs in range(npairs):
            kext = (vs + 1) * SSM_TPAIR * LANES

            @pl.when(v == vs)
            def _():
                start = (npairs - 1 - vs) * SSM_TPAIR * LANES
                intra = _dot(x_ref[c, :, 0:kext], kst_ref[c, start:start + kext, :])
                o_ref[...] = (yq + intra).astype(BF16)


def _ssm_core(xcat, kc, pc, qc, ar, ai, layer, chunks_per_seq):
    ncol, rows, xw = xcat.shape
    nb = ar.shape[1]
    bw = ar.shape[3]
    ow = SSM_TPAIR * LANES
    nout = ncol * SSM_CHUNK // SSM_TPAIR
    one = pl.Buffered(1)
    whole = lambda a: pl.BlockSpec((None,) + a.shape[1:], lambda u: (layer,) + (0,) * (a.ndim - 1),
                                   pipeline_mode=one)
    return pl.pallas_call(
        functools.partial(_ssm_body, chunks_per_seq=chunks_per_seq, n_state_blocks=nb),
        grid=(nb + nout,),
        in_specs=[
            pl.BlockSpec((ncol, rows, xw), lambda u: (0, 0, 0), pipeline_mode=one),
            whole(kc), whole(pc), whole(qc), whole(ar), whole(ai),
        ],
        out_specs=pl.BlockSpec((None, rows, ow), lambda u: (jnp.maximum(u - nb, 0), 0, 0)),
        out_shape=jax.ShapeDtypeStruct((nout, rows, ow), BF16),
        scratch_shapes=[
            pltpu.VMEM((nb, rows, bw), BF16),
            pltpu.VMEM((ncol, SSM_CHUNK * LANES, ow), BF16),
            pltpu.VMEM((rows, bw), F32),
        ],
        compiler_params=_cparams(("arbitrary",)),
        name="ssm_core",
    )(xcat, kc, pc, qc, ar, ai)


def _ssm_operators(a_re, a_im, log_dt, b_re, b_im, c_re, c_im, chunks_per_seq):
    t = SSM_CHUNK
    g, p = a_re.shape
    hg = b_re.shape[-1]
    cgc = SSM_COL_GROUPS
    ncol = g // cgc
    bgc = SSM_BLOCK_GROUPS
    dt = jnp.exp(log_dt)[:, None]
    lam_re, lam_im = dt * a_re, dt * a_im

    def powers(n):
        n = jnp.asarray(n, F32)[:, None, None]
        mag = jnp.exp(n * lam_re)
        return mag * jnp.cos(n * lam_im), mag * jnp.sin(n * lam_im)

    ab_re, ab_im = powers([1.0])
    ab_re, ab_im = ab_re[0], ab_im[0]
    den = a_re * a_re + a_im * a_im
    f_re = ((ab_re - 1.0) * a_re + ab_im * a_im) / den
    f_im = (ab_im * a_re - (ab_re - 1.0) * a_im) / den
    bb_re = f_re[..., None] * b_re - f_im[..., None] * b_im
    bb_im = f_re[..., None] * b_im + f_im[..., None] * b_re
    pw_re, pw_im = powers(np.arange(t + 1))
    cp_re = c_re[None] * pw_re[:, :, None, :] - c_im[None] * pw_im[:, :, None, :]
    cp_im = c_re[None] * pw_im[:, :, None, :] + c_im[None] * pw_re[:, :, None, :]
    ktau = (jnp.einsum('tghp,gpk->tkgh', cp_re[:t], bb_re)
            - jnp.einsum('tghp,gpk->tkgh', cp_im[:t], bb_im))
    kc = ktau.reshape(t, hg, ncol, cgc * hg).transpose(2, 0, 1, 3)
    bt_re, bt_im = bb_re.transpose(0, 2, 1), bb_im.transpose(0, 2, 1)
    rp_re = pw_re[t - 1 - np.arange(t)][:, :, None, :]
    rp_im = pw_im[t - 1 - np.arange(t)][:, :, None, :]
    pin = jnp.concatenate([rp_re * bt_re[None] - rp_im * bt_im[None],
                           rp_re * bt_im[None] + rp_im * bt_re[None]], axis=3)
    pc = pin.reshape(t, ncol, cgc, hg, 2 * p).transpose(1, 0, 2, 3, 4).reshape(ncol, t * cgc * hg, 2 * p)
    ct_re, ct_im = c_re.transpose(2, 0, 1)[None], c_im.transpose(2, 0, 1)[None]
    pt_re = pw_re[1:].transpose(0, 2, 1)[..., None]
    pt_im = pw_im[1:].transpose(0, 2, 1)[..., None]
    qg = jnp.concatenate([ct_re * pt_re - ct_im * pt_im,
                          -(ct_re * pt_im + ct_im * pt_re)], axis=1)
    qc = qg.reshape(t, 2 * p, ncol, cgc * hg).transpose(2, 0, 1, 3)
    nblk = g // bgc
    nsteps = int(np.log2(chunks_per_seq))
    sr, si = powers(t * 2.0 ** np.arange(nsteps))
    blk = lambda a: a.reshape(nsteps, nblk, bgc * 2 * p).transpose(1, 0, 2)
    ar = blk(jnp.concatenate([sr, sr], axis=2))
    ai = blk(jnp.concatenate([-si, si], axis=2))
    return kc, pc, qc, ar, ai


def _merge_body(x_ref, pin_ref, halo_ref, og_ref, r0_ref, r1_ref, yc_ref, sin_ref, g0_ref, g1_ref, g2_ref,
                pw_ref, ps_ref, gain_ref, d_ref, wglu_ref, wbp_ref, wbg0_ref, wbg1_ref, wbs_ref, wo_ref,
                o_ref, ys_ref, *, tiles_per_seq):
    t = pl.program_id(0) % tiles_per_seq
    yp = _pool_mix(pin_ref[...].astype(F32), halo_ref[...].astype(F32), t, pw_ref, ps_ref[...])
    r_refs = (r0_ref, r1_ref)
    heads_per_block = r0_ref.shape[1] // GLA_DV
    yg_heads = []
    for h in range(GLA_HEADS):
        hs = slice(h * GLA_DV, (h + 1) * GLA_DV)
        rs = slice((h % heads_per_block) * GLA_DV, (h % heads_per_block + 1) * GLA_DV)
        o = og_ref[:, hs].astype(F32)
        o = o * lax.rsqrt(jnp.mean(o * o, axis=-1, keepdims=True) + EPS) * gain_ref[:, hs]
        r = r_refs[h // heads_per_block][:, rs].astype(F32)
        yg_heads.append((o * (r * _sigmoid(r))).astype(BF16))
    yg = jnp.concatenate(yg_heads, axis=1)
    rows = yc_ref.shape[1]
    npairs = SSM_CHUNK // SSM_TPAIR
    for ob in range(yc_ref.shape[0]):
        c, v = divmod(ob, npairs)
        plane = yc_ref[ob].astype(F32)
        for tt in range(SSM_TPAIR):
            ys_ref[c, pl.ds(SSM_TPAIR * v + tt, rows, stride=SSM_CHUNK), :] = plane[:, tt * LANES:(tt + 1) * LANES]
    ys = jnp.concatenate([ys_ref[c] for c in range(ys_ref.shape[0])], axis=1)
    y = ys + d_ref[...] * sin_ref[...].astype(F32)
    y = jax.nn.gelu(y)
    y = y * _sigmoid(_dot(y.astype(BF16), wglu_ref[...]))
    m = g0_ref[...].astype(F32) * _dot(yp.astype(BF16), wbp_ref[...])
    half = wbg0_ref.shape[0]
    gla_proj = _dot(yg[:, :half], wbg0_ref[...]) + _dot(yg[:, half:], wbg1_ref[...])
    m = m + g1_ref[...].astype(F32) * gla_proj
    m = m + g2_ref[...].astype(F32) * _dot(y.astype(BF16), wbs_ref[...])
    o_ref[...] = x_ref[...] + _dot(m.astype(BF16), wo_ref[...])


def _merge(x, o_gla, ycat, z, r_col0, ssm_col0, gate_col0, seq, pool_w, pool_scale, gla_gain,
           d_skip, w_glu, wb, w_out, layer, *, tm=256):
    m, d = x.shape
    wp, wg, ws = pool_scale.shape[2], o_gla.shape[1], d_skip.shape[2]
    br = BRANCH_ROWS
    assert wp == br and ws == br and wg == 2 * br and seq % tm == 0 and tm % POOL_HALO == 0
    gb = gate_col0 // d
    hb = tm // POOL_HALO
    one = pl.Buffered(1)
    wspec = lambda shape, r: pl.BlockSpec((None,) + shape, lambda i: (layer, r, 0), pipeline_mode=one)
    return pl.pallas_call(
        functools.partial(_merge_body, tiles_per_seq=seq // tm),
        grid=(m // tm,),
        in_specs=[
            pl.BlockSpec((tm, d), lambda i: (i, 0)),
            pl.BlockSpec((tm, wp), lambda i: (i, 0)),
            pl.BlockSpec((POOL_HALO, wp), lambda i: (jnp.maximum(i * hb - 1, 0), 0)),
            pl.BlockSpec((tm, wg), lambda i: (i, 0)),
            pl.BlockSpec((tm, br), lambda i: (i, r_col0 // br)),
            pl.BlockSpec((tm, br), lambda i: (i, r_col0 // br + 1)),
            pl.BlockSpec((ycat.shape[0], tm // SSM_CHUNK, ycat.shape[2]), lambda i: (0, i, 0)),
            pl.BlockSpec((tm, ws), lambda i: (i, ssm_col0 // ws)),
            pl.BlockSpec((tm, d), lambda i: (i, gb)),
            pl.BlockSpec((tm, d), lambda i: (i, gb + 1)),
            pl.BlockSpec((tm, d), lambda i: (i, gb + 2)),
            pl.BlockSpec((None, len(POOL_WINDOWS), POOL_GROUP, POOL_GROUP), lambda i: (layer, 0, 0, 0),
                         pipeline_mode=one),
            wspec((1, wp), 0),
            wspec((1, wg), 0),
            wspec((1, ws), 0),
            wspec((ws, ws), 0),
            wspec((br, d), 0),
            wspec((br, d), 1),
            wspec((br, d), 2),
            wspec((br, d), 3),
            wspec((d, d), 0),
        ],
        out_specs=pl.BlockSpec((tm, d), lambda i: (i, 0)),
        out_shape=jax.ShapeDtypeStruct((m, d), F32),
        scratch_shapes=[pltpu.VMEM((ws // LANES, tm, LANES), F32)],
        compiler_params=_cparams(("parallel",)),
        name="merge",
    )(x, z, z, o_gla, z, z, ycat, z, z, z, z, pool_w, pool_scale, gla_gain, d_skip, w_glu,
      wb, wb, wb, wb, w_out)


def kernel(x, ffn1_norm, ffn1_w_gate, ffn1_w_up, ffn1_w_down, mix_norm, w_in, pool_w, pool_scale, gla_w_gate2, gla_gate_bias, gla_norm, ssm_a_re, ssm_a_im, ssm_log_dt, ssm_b_re, ssm_b_im, ssm_c_re, ssm_c_im, ssm_d, ssm_w_glu, w_branch, w_out, ffn2_norm, ffn2_w_gate, ffn2_w_up, ffn2_w_down, final_norm):
    batch, seq, d = x.shape
    depth = w_in.shape[0]
    m = batch * seq
    pool_width = pool_scale.shape[1]
    qk_width = GLA_HEADS * GLA_DK
    v_width = GLA_HEADS * GLA_DV
    ssm_width = ssm_d.shape[1]
    q0 = pool_width
    k0 = q0 + qk_width
    v0 = k0 + qk_width
    r0 = v0 + v_width
    glr0 = r0 + v_width
    ssm0 = glr0
    gate0 = ssm0 + ssm_width
    chunks_per_seq = seq // SSM_CHUNK

    bf = lambda a: a.astype(BF16)
    row3 = lambda a: a.reshape(depth, 1, a.shape[-1])
    w_in_t = jnp.swapaxes(w_in, 1, 2)
    ffn1_w = (ffn1_w_gate, ffn1_w_up, ffn1_w_down)
    ffn2_w = (ffn2_w_gate, ffn2_w_up, ffn2_w_down)
    f1 = [bf(w[0]) for w in ffn1_w]
    wbr, wo, wglu, pw = bf(w_branch), bf(w_out), bf(ssm_w_glu), bf(pool_w)
    n1, nm, n2 = row3(ffn1_norm), row3(mix_norm), row3(ffn2_norm)
    pscale, dskip = row3(pool_scale), row3(ssm_d)
    w2 = gla_w_gate2.reshape(depth, GLA_RANK, GLA_HEADS, GLA_DK).transpose(0, 2, 1, 3)
    w2 = bf(jnp.pad(w2, ((0, 0), (0, 0), (0, LANES - GLA_RANK), (0, 0))))
    gbias = gla_gate_bias.reshape(depth, GLA_HEADS, 1, GLA_DK)
    ggain = row3(gla_norm)
    ssm_ops = jax.vmap(functools.partial(_ssm_operators, chunks_per_seq=chunks_per_seq))(
        ssm_a_re, ssm_a_im, ssm_log_dt, ssm_b_re, ssm_b_im, ssm_c_re, ssm_c_im)

    xf = x.reshape(m, d)
    for l in range(depth):
        xf, *side = _ffn(xf, n1, l, *f1, cast=(l,) + ffn2_w, repack=(l, w_in_t, glr0, GLA_RANK))
        f2, (w_main, w_glr) = side[:3], side[3:]
        z, glr, xcat = _inproj(xf, nm, w_main, w_glr, l, ssm0, ssm_width, gate0)
        o_gla = _gla(z, glr, w2, gbias, l, batch, seq, (q0, k0, v0))
        ycat = _ssm_core(xcat, *ssm_ops, l, chunks_per_seq)
        xf = _merge(xf, o_gla, ycat, z, r0, ssm0, gate0, seq, pw, pscale, ggain, dskip, wglu, wbr, wo, l)
        last = l == depth - 1
        xf, *f1 = _ffn(xf, n2, l, *f2, cast=None if last else (l + 1,) + ffn1_w,
                       final_g=final_norm if last else None)
    return xf.reshape(batch, seq, d)
```

```python
import functools

import jax
import jax.numpy as jnp
import numpy as np
from jax import lax
from jax.experimental import pallas as pl
from jax.experimental.pallas import tpu as pltpu

F32 = jnp.float32
BF16 = jnp.bfloat16

EPS = 1e-6
POOL_WINDOWS = (2, 4, 8, 16)
POOL_GROUP = 128
POOL_HALO = 16
GLA_HEADS = 4
GLA_DK = 128
GLA_DV = 256
GLA_RANK = 16
GLA_TAU = 16.0
LOG2_E = 1.4426950408889634
GLA_CHUNK = 64
SSM_GROUP = 16
SSM_STATE = 64
SSM_CHUNK = 16
LANES = 128
FFN_DOWN_CHUNK = 512
FFN_FIRST_ROWS = 512
REPACK_ROWS = 64
INPROJ_FIRST_ROWS = 256
BRANCH_ROWS = 512

VMEM_LIMIT = 60 * 1024 * 1024


def _cparams(sem):
    return pltpu.CompilerParams(dimension_semantics=sem, vmem_limit_bytes=VMEM_LIMIT)


def _rms(xf, g):
    return xf * lax.rsqrt(jnp.mean(xf * xf, axis=-1, keepdims=True) + EPS) * g


def _dot(a, b):
    return jnp.dot(a, b, preferred_element_type=F32)


def _sigmoid(a):
    return 0.5 * jnp.tanh(0.5 * a) + 0.5


def _dot_nt(a, b_t):
    return lax.dot_general(a, b_t, (((1,), (1,)), ((), ())), preferred_element_type=F32)


def _ffn_body(x_ref, g_ref, wg_ref, wu_ref, wd_ref, *rest, final, n_cast, repack):
    rest = list(rest)
    cast_in = [rest.pop(0) for _ in range(n_cast)]
    rp_in = [rest.pop(0) for _ in range(2 if repack else 0)]
    fg_ref = rest.pop(0) if final else None
    o_ref = rest.pop(0)
    cast_out = [rest.pop(0) for _ in range(n_cast)]
    rp_out = [rest.pop(0) for _ in range(2 if repack else 0)]
    xn_ref, = rest
    j = pl.program_id(1)

    for src, dst in zip(cast_in, cast_out):
        dst[...] = src[...].astype(BF16)
    if repack:
        (rows_ref, drop_ref), (w_ref, wl_ref) = rp_in, rp_out
        w_ref[...] = rows_ref[0].astype(BF16)
        pad = jnp.zeros((wl_ref.shape[0] - drop_ref.shape[1], drop_ref.shape[2]), F32)
        wl_ref[...] = jnp.concatenate([drop_ref[0], pad], axis=0).astype(BF16)

    def swiglu_step(rs, xn, first):
        a = _dot(xn, wg_ref[...])
        b = _dot(xn, wu_ref[...])
        h = (a * _sigmoid(a) * b).astype(BF16)
        for c0 in range(0, o_ref.shape[1], FFN_DOWN_CHUNK):
            cs = slice(c0, c0 + FFN_DOWN_CHUNK)
            if first:
                o_ref[rs, cs] = _dot(h, wd_ref[:, cs])
            else:
                o_ref[rs, cs] += _dot(h, wd_ref[:, cs])

    @pl.when(j == 0)
    def _():
        for r0 in range(0, o_ref.shape[0], FFN_FIRST_ROWS):
            rs = slice(r0, r0 + FFN_FIRST_ROWS)
            xn = _rms(x_ref[rs, :], g_ref[...]).astype(BF16)
            xn_ref[rs, :] = xn
            swiglu_step(rs, xn, True)

    last = pl.num_programs(1) - 1

    @pl.when((j > 0) & (j < last))
    def _():
        swiglu_step(slice(None), xn_ref[...], False)

    @pl.when(j == last)
    def _():
        for r0 in range(0, o_ref.shape[0], FFN_FIRST_ROWS):
            rs = slice(r0, r0 + FFN_FIRST_ROWS)
            swiglu_step(rs, xn_ref[rs, :], False)
            y = x_ref[rs, :] + 0.5 * o_ref[rs, :]
            if final:
                y = _rms(y, fg_ref[...])
            o_ref[rs, :] = y


def _ffn(x, g, layer, wg, wu, wd, cast=None, repack=None, final_g=None, *, tm=1024, tf=512):
    m, d = x.shape
    f = wg.shape[1]
    ni, nj = m // tm, f // tf
    final = final_g is not None
    in_specs = [
        pl.BlockSpec((tm, d), lambda i, j: (i, 0)),
        pl.BlockSpec((None, 1, d), lambda i, j: (layer, 0, 0)),
        pl.BlockSpec((d, tf), lambda i, j: (0, j)),
        pl.BlockSpec((d, tf), lambda i, j: (0, j)),
        pl.BlockSpec((tf, d), lambda i, j: (j, 0)),
    ]
    args = [x, g, wg, wu, wd]
    out_specs = [pl.BlockSpec((tm, d), lambda i, j: (i, 0))]
    out_shape = [jax.ShapeDtypeStruct((m, d), F32)]
    n_cast = 0
    if cast is not None:
        cl, cg, cu, cd = cast
        n_cast = 3
        rows_up = d // ni
        rows_dn = f // (ni * nj)
        assert d % ni == 0 and f % (ni * nj) == 0 and rows_up % 16 == 0 and rows_dn % 16 == 0
        for w in (cg, cu):
            in_specs.append(pl.BlockSpec((None, rows_up, tf), lambda i, j: (cl, i, j)))
            out_specs.append(pl.BlockSpec((rows_up, tf), lambda i, j: (i, j)))
            out_shape.append(jax.ShapeDtypeStruct((d, f), BF16))
            args.append(w)
        in_specs.append(pl.BlockSpec((None, rows_dn, d), lambda i, j: (cl, i * nj + j, 0)))
        out_specs.append(pl.BlockSpec((rows_dn, d), lambda i, j: (i * nj + j, 0)))
        out_shape.append(jax.ShapeDtypeStruct((f, d), BF16))
        args.append(cd)
    if repack is not None:
        rl, w_in_t, drop0, drop_width = repack
        n = w_in_t.shape[1] - drop_width
        rb = REPACK_ROWS
        nblk = n // rb
        assert n % rb == 0 and nblk <= ni * nj and drop0 % rb == 0 and drop_width % 8 == 0
        blk = lambda i, j: jnp.minimum(i * nj + j, nblk - 1)
        src_row = lambda b: (b * (rb // 8) + jnp.where(b >= drop0 // rb, drop_width // 8, 0)) * 8
        in_specs.append(pl.BlockSpec((pl.Element(1), pl.Element(rb), pl.Element(d)),
                                     lambda i, j: (rl, src_row(blk(i, j)), 0)))
        in_specs.append(pl.BlockSpec((pl.Element(1), pl.Element(drop_width), pl.Element(d)),
                                     lambda i, j: (rl, drop0, 0)))
        out_specs.append(pl.BlockSpec((rb, d), lambda i, j: (blk(i, j), 0)))
        out_specs.append(pl.BlockSpec((LANES, d), lambda i, j: (0, 0)))
        out_shape.append(jax.ShapeDtypeStruct((n, d), BF16))
        out_shape.append(jax.ShapeDtypeStruct((LANES, d), BF16))
        args += [w_in_t, w_in_t]
    if final:
        in_specs.append(pl.BlockSpec((1, d), lambda i, j: (0, 0)))
        args.append(final_g.reshape(1, d))
    return pl.pallas_call(
        functools.partial(_ffn_body, final=final, n_cast=n_cast, repack=repack is not None),
        grid=(ni, nj),
        in_specs=in_specs,
        out_specs=out_specs,
        out_shape=out_shape,
        scratch_shapes=[pltpu.VMEM((tm, d), BF16)],
        compiler_params=_cparams(("arbitrary", "arbitrary")),
        name="ffn_final" if final else "ffn",
    )(*args)


def _inproj_body(x_ref, g_ref, w_ref, wl_ref, z_ref, glr_ref, xc_ref, un_ref, ph_ref, *,
                 gate_block0, ssm_block, ssm_off):
    j = pl.program_id(1)

    @pl.when(j == 0)
    def _():
        for r0 in range(0, z_ref.shape[0], INPROJ_FIRST_ROWS):
            rs = slice(r0, r0 + INPROJ_FIRST_ROWS)
            un = _rms(x_ref[rs, :], g_ref[...]).astype(BF16)
            un_ref[rs, :] = un
            glr_ref[rs, :] = _dot_nt(un, wl_ref[...])
            z_ref[rs, :] = _dot_nt(un, w_ref[...]).astype(BF16)

    @pl.when(j >= gate_block0)
    def _():
        z_ref[...] = _sigmoid(_dot_nt(un_ref[...], w_ref[...])).astype(BF16)

    @pl.when((j > 0) & (j < gate_block0))
    def _():
        acc = _dot_nt(un_ref[...], w_ref[...])
        z_ref[...] = acc.astype(BF16)

        @pl.when(j == ssm_block)
        def _():
            ncol = ph_ref.shape[0]
            for c in range(ncol):
                ph_ref[c] = acc[:, ssm_off + c * LANES:ssm_off + (c + 1) * LANES]
            rows = ph_ref.shape[1] // SSM_CHUNK
            for s in range(SSM_CHUNK):
                for c in range(ncol):
                    piece = ph_ref[c, pl.ds(s, rows, stride=SSM_CHUNK), :].astype(BF16)
                    xc_ref[c, :, s * LANES:(s + 1) * LANES] = piece


def _inproj(x, g, w_main, w_glr, layer, ssm_col0, ssm_width, gate_col0, *, tm=1024, tn=1024):
    m, d = x.shape
    n = w_main.shape[0]
    assert gate_col0 >= tn and tn <= ssm_col0 < gate_col0
    body = functools.partial(_inproj_body, gate_block0=gate_col0 // tn,
                             ssm_block=ssm_col0 // tn, ssm_off=ssm_col0 % tn)
    return pl.pallas_call(
        body,
        grid=(m // tm, n // tn),
        in_specs=[
            pl.BlockSpec((tm, d), lambda i, j: (i, 0)),
            pl.BlockSpec((None, 1, d), lambda i, j: (layer, 0, 0)),
            pl.BlockSpec((tn, d), lambda i, j: (j, 0)),
            pl.BlockSpec((LANES, d), lambda i, j: (0, 0)),
        ],
        out_specs=[
            pl.BlockSpec((tm, tn), lambda i, j: (i, j)),
            pl.BlockSpec((tm, LANES), lambda i, j: (i, 0)),
            pl.BlockSpec((ssm_width // LANES, tm // SSM_CHUNK, SSM_CHUNK * LANES), lambda i, j: (0, i, 0)),
        ],
        out_shape=[
            jax.ShapeDtypeStruct((m, n), BF16),
            jax.ShapeDtypeStruct((m, LANES), F32),
            jax.ShapeDtypeStruct((ssm_width // LANES, m // SSM_CHUNK, SSM_CHUNK * LANES), BF16),
        ],
        scratch_shapes=[pltpu.VMEM((tm, d), BF16), pltpu.VMEM((ssm_width // LANES, tm, LANES), F32)],
        compiler_params=_cparams(("parallel", "arbitrary")),
        name="inproj",
    )(x, g, w_main, w_glr)


def _pool_mix(x, halo, t, w_ref, scale):
    ts = x.shape[0]
    halo = jnp.where(t > 0, halo, 0.0)
    e = jnp.concatenate([halo, x], axis=0)
    s2 = e + pltpu.roll(e, 1, 0)
    s4 = s2 + pltpu.roll(s2, 2, 0)
    s8 = s4 + pltpu.roll(s4, 4, 0)
    s16 = s8 + pltpu.roll(s8, 8, 0)
    pos = (t * ts + 1 + lax.broadcasted_iota(jnp.int32, (ts, 1), 0)).astype(F32)
    outs = []
    for gi, (w, s) in enumerate(zip(POOL_WINDOWS, (s2, s4, s8, s16))):
        cols = slice(gi * POOL_GROUP, (gi + 1) * POOL_GROUP)
        mean = s[POOL_HALO:, cols] / jnp.minimum(pos, float(w))
        outs.append(_dot((mean - x[:, cols]).astype(BF16), w_ref[gi]))
    return jnp.concatenate(outs, axis=1) * scale


def _gla_streams(streams, tri, ts):
    c = GLA_CHUNK
    nc = ts // c
    causal = (lax.broadcasted_iota(jnp.int32, (c, c), 0)
              >= lax.broadcasted_iota(jnp.int32, (c, c), 1))
    qs, ks, vs, glrs, w2s, biases, sts = (list(a) for a in zip(*streams))
    logits = [_dot(glr, w2) + bias for glr, w2, bias in zip(glrs, w2s, biases)]
    bcs = [(jnp.minimum(l, 0.0) - jnp.log1p(jnp.exp(-jnp.abs(l)))) * (LOG2_E / GLA_TAU) for l in logits]
    his = [bc.astype(BF16) for bc in bcs]
    los = [(bc - hi.astype(F32)).astype(BF16) for bc, hi in zip(bcs, his)]
    sums = [_dot(tri, jnp.concatenate([hi, lo], axis=1)) for hi, lo in zip(his, los)]
    bcs = [s[:, :GLA_DK] + s[:, GLA_DK:] for s in sums]
    bc3 = [bc.reshape(nc, c, GLA_DK) for bc in bcs]
    b_last = [b[:, c - 1:c, :] for b in bc3]
    b_mid = [b[:, c // 2 - 1:c // 2, :] for b in bc3]
    q3 = [q.astype(F32).reshape(nc, c, GLA_DK) * (GLA_DK ** -0.5) for q in qs]
    k3 = [k.astype(F32).reshape(nc, c, GLA_DK) for k in ks]
    v3 = [v.reshape(nc, c, GLA_DV) for v in vs]
    qe = [(q * jnp.exp2(b - m)).astype(BF16) for q, b, m in zip(q3, bc3, b_mid)]
    ke = [(k * jnp.exp2(m - b)).astype(BF16) for k, b, m in zip(k3, bc3, b_mid)]
    scores = [jnp.einsum('nid,njd->nij', a, b, preferred_element_type=F32) for a, b in zip(qe, ke)]
    scores = [jnp.where(causal[None], s, 0.0).astype(BF16) for s in scores]
    o_intra = [jnp.einsum('nij,njv->niv', s, v, preferred_element_type=F32) for s, v in zip(scores, v3)]
    qd = [(q * jnp.exp2(b)).astype(BF16) for q, b in zip(q3, bc3)]
    kd = [(k * jnp.exp2(bl - b)).astype(BF16) for k, b, bl in zip(k3, bc3, b_last)]
    dec_t = [jnp.transpose(jnp.exp2(bl.reshape(nc, GLA_DK))) for bl in b_last]
    outs = [[] for _ in streams]
    for n in range(nc):
        for i in range(len(streams)):
            outs[i].append(o_intra[i][n] + _dot(qd[i][n], sts[i].astype(BF16)))
            kv = lax.dot_general(kd[i][n], v3[i][n], (((0,), (0,)), ((), ())), preferred_element_type=F32)
            sts[i] = dec_t[i][:, n:n + 1] * sts[i] + kv
    return [jnp.concatenate(o, axis=0).astype(BF16) for o in outs], sts


def _gla_body(q_ref, k_ref, v0_ref, v1_ref, glr_ref, w2_ref, b_ref, tri_ref, o_ref, st_ref, *, ts):
    @pl.when(pl.program_id(1) == 0)
    def _():
        st_ref[...] = jnp.zeros_like(st_ref)

    v_refs = (v0_ref, v1_ref)
    heads_per_block = v0_ref.shape[2] // GLA_DV
    streams, where = [], []
    for s in range(q_ref.shape[0]):
        glr = glr_ref[s].astype(BF16)
        for h in range(GLA_HEADS):
            v_ref = v_refs[h // heads_per_block]
            vs = slice((h % heads_per_block) * GLA_DV, (h % heads_per_block + 1) * GLA_DV)
            ks = slice(h * GLA_DK, (h + 1) * GLA_DK)
            streams.append((q_ref[s, :, ks], k_ref[s, :, ks], v_ref[s, :, vs], glr,
                            w2_ref[h], b_ref[h], st_ref[s, h]))
            where.append((s, h))
    outs, sts = _gla_streams(streams, tri_ref[...], ts)
    for (s, h), o, st in zip(where, outs, sts):
        st_ref[s, h] = st
        o_ref[s, :, h * GLA_DV:(h + 1) * GLA_DV] = o


def _gla(z, glr, w2, bias, layer, batch, seq, cols, *, ts=512, nseq=2):
    h = GLA_HEADS
    nt = seq // ts
    q0, k0, v0 = cols
    wb = h * GLA_DK
    z3 = z.reshape(batch, seq, z.shape[1])
    pos = np.arange(ts)
    tri = jnp.asarray((pos[:, None] // GLA_CHUNK == pos[None, :] // GLA_CHUNK) & (pos[:, None] >= pos[None, :]), BF16)
    zspec = lambda c0: pl.BlockSpec((nseq, ts, wb), lambda b, t: (b, t, c0 // wb))
    wspec = lambda last2: pl.BlockSpec((None, h) + last2, lambda b, t: (layer, 0, 0, 0))
    out = pl.pallas_call(
        functools.partial(_gla_body, ts=ts),
        grid=(batch // nseq, nt),
        in_specs=[
            zspec(q0), zspec(k0), zspec(v0), zspec(v0 + wb),
            pl.BlockSpec((nseq, ts, LANES), lambda b, t: (b, t, 0)),
            wspec((LANES, GLA_DK)), wspec((1, GLA_DK)),
            pl.BlockSpec((ts, ts), lambda b, t: (0, 0)),
        ],
        out_specs=pl.BlockSpec((nseq, ts, h * GLA_DV), lambda b, t: (b, t, 0)),
        out_shape=jax.ShapeDtypeStruct((batch, seq, h * GLA_DV), BF16),
        scratch_shapes=[pltpu.VMEM((nseq, h, GLA_DK, GLA_DV), F32)],
        compiler_params=_cparams(("parallel", "arbitrary")),
        name="gla",
    )(z3, z3, z3, z3, glr.reshape(batch, seq, LANES), w2, bias, tri)
    return out.reshape(batch * seq, h * GLA_DV)


SSM_COL_GROUPS = LANES // SSM_GROUP
SSM_BLOCK_GROUPS = 2
SSM_TPAIR = 2


def _ssm_body(x_ref, kc_ref, pc_ref, qc_ref, ar_ref, ai_ref, o_ref, hp_ref, kst_ref, v_ref, *,
              chunks_per_seq, n_state_blocks, ncol):
    u = pl.program_id(0)
    rows = x_ref.shape[0]
    blocks_per_col = n_state_blocks // ncol
    npairs = SSM_CHUNK // SSM_TPAIR
    ns = 2 * SSM_STATE
    bw = SSM_BLOCK_GROUPS * ns
    lane_group = lax.broadcasted_iota(jnp.int32, (1, LANES), 1) // SSM_GROUP

    def p_tile(n):
        pc = pc_ref[n // blocks_per_col]
        row_group = (lax.broadcasted_iota(jnp.int32, (pc.shape[0], 1), 0) // SSM_GROUP) % SSM_COL_GROUPS
        g0 = (n % blocks_per_col) * SSM_BLOCK_GROUPS
        return jnp.concatenate([jnp.where(row_group == g0 + gg, pc, 0.0)
                                for gg in range(SSM_BLOCK_GROUPS)], axis=1).astype(BF16)

    @pl.when(u == 0)
    def _():
        row_group = lax.broadcasted_iota(jnp.int32, (LANES, 1), 0) // SSM_GROUP

        def kbd(c, lag):
            kc = kc_ref[c, lag]
            tiled = jnp.broadcast_to(kc[None], (SSM_COL_GROUPS,) + kc.shape).reshape(LANES, LANES)
            return jnp.where(row_group == lane_group, tiled, 0.0).astype(BF16)

        for c in range(ncol):
            for r in range(SSM_CHUNK):
                left = kbd(c, SSM_CHUNK - 2 - r) if r <= SSM_CHUNK - 2 else jnp.zeros((LANES, LANES), BF16)
                kst_ref[c, r * LANES:(r + 1) * LANES, :] = jnp.concatenate(
                    [left, kbd(c, SSM_CHUNK - 1 - r)], axis=1)
        v_ref[...] = _dot(x_ref[...], p_tile(0))

    @pl.when(u < n_state_blocks)
    def _():
        rowc = lax.broadcasted_iota(jnp.int32, (rows, 1), 0) % chunks_per_seq
        hst = v_ref[...]
        nxt = jnp.minimum(u + 1, n_state_blocks - 1)
        v_next = _dot(x_ref[...], p_tile(nxt))
        k, si = 1, 0
        while k < chunks_per_seq:
            hs = jnp.where(rowc >= k, pltpu.roll(hst, k, 0), 0.0)
            sw = jnp.concatenate([pltpu.roll(hs[:, b0:b0 + ns], SSM_STATE, 1)
                                  for b0 in range(0, bw, ns)], axis=1)
            hst = hst + ar_ref[u, si:si + 1, :] * hs + ai_ref[u, si:si + 1, :] * sw
            k *= 2
            si += 1
        hp_ref[u] = jnp.where(rowc >= 1, pltpu.roll(hst, 1, 0), 0.0).astype(BF16)
        v_ref[...] = v_next

    @pl.when(u >= n_state_blocks)
    def _():
        ob = u - n_state_blocks
        c = ob // npairs
        v = ob % npairs
        grp = lax.broadcasted_iota(jnp.int32, (SSM_COL_GROUPS, 1, 1), 0)
        q_tile = jnp.concatenate(
            [jnp.where(grp == lane_group[None], qc_ref[c, SSM_TPAIR * v + tt][None], 0.0)
             .reshape(SSM_COL_GROUPS * ns, LANES) for tt in range(SSM_TPAIR)], axis=1).astype(BF16)
        yq = _dot(hp_ref[c * blocks_per_col], q_tile[0:bw, :])
        for kk in range(1, blocks_per_col):
            yq = yq + _dot(hp_ref[c * blocks_per_col + kk], q_tile[kk * bw:(kk + 1) * bw, :])
        for vs in range(npairs):
            kext = (vs + 1) * SSM_TPAIR * LANES

            @pl.when(v == vs)
            def _():
                start = (npairs - 1 - vs) * SSM_TPAIR * LANES
                intra = _dot(x_ref[:, 0:kext], kst_ref[c, start:start + kext, :])
                o_ref[...] = (yq + intra).astype(BF16)


def _ssm_core(xcat, kc, pc, qc, ar, ai, layer, chunks_per_seq):
    ncol, rows, xw = xcat.shape
    nb = ar.shape[1]
    bw = ar.shape[3]
    ow = SSM_TPAIR * LANES
    nout = ncol * SSM_CHUNK // SSM_TPAIR
    one = pl.Buffered(1)
    whole = lambda a: pl.BlockSpec((None,) + a.shape[1:], lambda u: (layer,) + (0,) * (a.ndim - 1),
                                   pipeline_mode=one)
    return pl.pallas_call(
        functools.partial(_ssm_body, chunks_per_seq=chunks_per_seq, n_state_blocks=nb, ncol=ncol),
        grid=(nb + nout,),
        in_specs=[
            pl.BlockSpec((None, rows, xw), lambda u: (
                jnp.where(u < nb, jnp.minimum(u + 1, nb - 1) // (nb // ncol),
                          (u - nb) // (SSM_CHUNK // SSM_TPAIR)), 0, 0)),
            whole(kc), whole(pc), whole(qc), whole(ar), whole(ai),
        ],
        out_specs=pl.BlockSpec((None, rows, ow), lambda u: (jnp.maximum(u - nb, 0), 0, 0)),
        out_shape=jax.ShapeDtypeStruct((nout, rows, ow), BF16),
        scratch_shapes=[
            pltpu.VMEM((nb, rows, bw), BF16),
            pltpu.VMEM((ncol, SSM_CHUNK * LANES, ow), BF16),
            pltpu.VMEM((rows, bw), F32),
        ],
        compiler_params=_cparams(("arbitrary",)),
        name="ssm_core",
    )(xcat, kc, pc, qc, ar, ai)


def _ssm_operators(a_re, a_im, log_dt, b_re, b_im, c_re, c_im, chunks_per_seq):
    t = SSM_CHUNK
    g, p = a_re.shape
    hg = b_re.shape[-1]
    cgc = SSM_COL_GROUPS
    ncol = g // cgc
    bgc = SSM_BLOCK_GROUPS
    dt = jnp.exp(log_dt)[:, None]
    lam_re, lam_im = dt * a_re, dt * a_im

    def powers(n):
        n = jnp.asarray(n, F32)[:, None, None]
        mag = jnp.exp(n * lam_re)
        return mag * jnp.cos(n * lam_im), mag * jnp.sin(n * lam_im)

    ab_re, ab_im = powers([1.0])
    ab_re, ab_im = ab_re[0], ab_im[0]
    den = a_re * a_re + a_im * a_im
    f_re = ((ab_re - 1.0) * a_re + ab_im * a_im) / den
    f_im = (ab_im * a_re - (ab_re - 1.0) * a_im) / den
    bb_re = f_re[..., None] * b_re - f_im[..., None] * b_im
    bb_im = f_re[..., None] * b_im + f_im[..., None] * b_re
    pw_re, pw_im = powers(np.arange(t + 1))
    cp_re = c_re[None] * pw_re[:, :, None, :] - c_im[None] * pw_im[:, :, None, :]
    cp_im = c_re[None] * pw_im[:, :, None, :] + c_im[None] * pw_re[:, :, None, :]
    ktau = (jnp.einsum('tghp,gpk->tkgh', cp_re[:t], bb_re)
            - jnp.einsum('tghp,gpk->tkgh', cp_im[:t], bb_im))
    kc = ktau.reshape(t, hg, ncol, cgc * hg).transpose(2, 0, 1, 3)
    bt_re, bt_im = bb_re.transpose(0, 2, 1), bb_im.transpose(0, 2, 1)
    rp_re = pw_re[t - 1 - np.arange(t)][:, :, None, :]
    rp_im = pw_im[t - 1 - np.arange(t)][:, :, None, :]
    pin = jnp.concatenate([rp_re * bt_re[None] - rp_im * bt_im[None],
                           rp_re * bt_im[None] + rp_im * bt_re[None]], axis=3)
    pc = pin.reshape(t, ncol, cgc, hg, 2 * p).transpose(1, 0, 2, 3, 4).reshape(ncol, t * cgc * hg, 2 * p)
    ct_re, ct_im = c_re.transpose(2, 0, 1)[None], c_im.transpose(2, 0, 1)[None]
    pt_re = pw_re[1:].transpose(0, 2, 1)[..., None]
    pt_im = pw_im[1:].transpose(0, 2, 1)[..., None]
    qg = jnp.concatenate([ct_re * pt_re - ct_im * pt_im,
                          -(ct_re * pt_im + ct_im * pt_re)], axis=1)
    qc = qg.reshape(t, 2 * p, ncol, cgc * hg).transpose(2, 0, 1, 3)
    nblk = g // bgc
    nsteps = int(np.log2(chunks_per_seq))
    sr, si = powers(t * 2.0 ** np.arange(nsteps))
    blk = lambda a: a.reshape(nsteps, nblk, bgc * 2 * p).transpose(1, 0, 2)
    ar = blk(jnp.concatenate([sr, sr], axis=2))
    ai = blk(jnp.concatenate([-si, si], axis=2))
    return kc, pc, qc, ar, ai


def _merge_body(x_ref, pin_ref, halo_ref, og_ref, r0_ref, r1_ref, yc_ref, sin_ref, g0_ref, g1_ref, g2_ref,
                pw_ref, ps_ref, gain_ref, d_ref, wglu_ref, wbp_ref, wbg0_ref, wbg1_ref, wbs_ref, wo_ref,
                o_ref, ys_ref, *, tiles_per_seq):
    t = pl.program_id(0) % tiles_per_seq
    yp = _pool_mix(pin_ref[...].astype(F32), halo_ref[...].astype(F32), t, pw_ref, ps_ref[...])
    r_refs = (r0_ref, r1_ref)
    heads_per_block = r0_ref.shape[1] // GLA_DV
    yg_heads = []
    for h in range(GLA_HEADS):
        hs = slice(h * GLA_DV, (h + 1) * GLA_DV)
        rs = slice((h % heads_per_block) * GLA_DV, (h % heads_per_block + 1) * GLA_DV)
        o = og_ref[:, hs].astype(F32)
        o = o * lax.rsqrt(jnp.mean(o * o, axis=-1, keepdims=True) + EPS) * gain_ref[:, hs]
        r = r_refs[h // heads_per_block][:, rs].astype(F32)
        yg_heads.append((o * (r * _sigmoid(r))).astype(BF16))
    yg = jnp.concatenate(yg_heads, axis=1)
    rows = yc_ref.shape[1]
    npairs = SSM_CHUNK // SSM_TPAIR
    for ob in range(yc_ref.shape[0]):
        c, v = divmod(ob, npairs)
        plane = yc_ref[ob].astype(F32)
        for tt in range(SSM_TPAIR):
            ys_ref[c, pl.ds(SSM_TPAIR * v + tt, rows, stride=SSM_CHUNK), :] = plane[:, tt * LANES:(tt + 1) * LANES]
    ys = jnp.concatenate([ys_ref[c] for c in range(ys_ref.shape[0])], axis=1)
    y = ys + d_ref[...] * sin_ref[...].astype(F32)
    y = jax.nn.gelu(y)
    y = y * _sigmoid(_dot(y.astype(BF16), wglu_ref[...]))
    m = g0_ref[...].astype(F32) * _dot(yp.astype(BF16), wbp_ref[...])
    half = wbg0_ref.shape[0]
    gla_proj = _dot(yg[:, :half], wbg0_ref[...]) + _dot(yg[:, half:], wbg1_ref[...])
    m = m + g1_ref[...].astype(F32) * gla_proj
    m = m + g2_ref[...].astype(F32) * _dot(y.astype(BF16), wbs_ref[...])
    o_ref[...] = x_ref[...] + _dot(m.astype(BF16), wo_ref[...])


def _merge(x, o_gla, ycat, z, r_col0, ssm_col0, gate_col0, seq, pool_w, pool_scale, gla_gain,
           d_skip, w_glu, wb, w_out, layer, *, tm=256):
    m, d = x.shape
    wp, wg, ws = pool_scale.shape[2], o_gla.shape[1], d_skip.shape[2]
    br = BRANCH_ROWS
    assert wp == br and ws == br and wg == 2 * br and seq % tm == 0 and tm % POOL_HALO == 0
    gb = gate_col0 // d
    hb = tm // POOL_HALO
    one = pl.Buffered(1)
    wspec = lambda shape, r: pl.BlockSpec((None,) + shape, lambda i: (layer, r, 0), pipeline_mode=one)
    return pl.pallas_call(
        functools.partial(_merge_body, tiles_per_seq=seq // tm),
        grid=(m // tm,),
        in_specs=[
            pl.BlockSpec((tm, d), lambda i: (i, 0)),
            pl.BlockSpec((tm, wp), lambda i: (i, 0)),
            pl.BlockSpec((POOL_HALO, wp), lambda i: (jnp.maximum(i * hb - 1, 0), 0)),
            pl.BlockSpec((tm, wg), lambda i: (i, 0)),
            pl.BlockSpec((tm, br), lambda i: (i, r_col0 // br)),
            pl.BlockSpec((tm, br), lambda i: (i, r_col0 // br + 1)),
            pl.BlockSpec((ycat.shape[0], tm // SSM_CHUNK, ycat.shape[2]), lambda i: (0, i, 0)),
            pl.BlockSpec((tm, ws), lambda i: (i, ssm_col0 // ws)),
            pl.BlockSpec((tm, d), lambda i: (i, gb)),
            pl.BlockSpec((tm, d), lambda i: (i, gb + 1)),
            pl.BlockSpec((tm, d), lambda i: (i, gb + 2)),
            pl.BlockSpec((None, len(POOL_WINDOWS), POOL_GROUP, POOL_GROUP), lambda i: (layer, 0, 0, 0),
                         pipeline_mode=one),
            wspec((1, wp), 0),
            wspec((1, wg), 0),
            wspec((1, ws), 0),
            wspec((ws, ws), 0),
            wspec((br, d), 0),
            wspec((br, d), 1),
            wspec((br, d), 2),
            wspec((br, d), 3),
            wspec((d, d), 0),
        ],
        out_specs=pl.BlockSpec((tm, d), lambda i: (i, 0)),
        out_shape=jax.ShapeDtypeStruct((m, d), F32),
        scratch_shapes=[pltpu.VMEM((ws // LANES, tm, LANES), F32)],
        compiler_params=_cparams(("parallel",)),
        name="merge",
    )(x, z, z, o_gla, z, z, ycat, z, z, z, z, pool_w, pool_scale, gla_gain, d_skip, w_glu,
      wb, wb, wb, wb, w_out)


def kernel(x, ffn1_norm, ffn1_w_gate, ffn1_w_up, ffn1_w_down, mix_norm, w_in, pool_w, pool_scale, gla_w_gate2, gla_gate_bias, gla_norm, ssm_a_re, ssm_a_im, ssm_log_dt, ssm_b_re, ssm_b_im, ssm_c_re, ssm_c_im, ssm_d, ssm_w_glu, w_branch, w_out, ffn2_norm, ffn2_w_gate, ffn2_w_up, ffn2_w_down, final_norm):
    batch, seq, d = x.shape
    depth = w_in.shape[0]
    m = batch * seq
    pool_width = pool_scale.shape[1]
    qk_width = GLA_HEADS * GLA_DK
    v_width = GLA_HEADS * GLA_DV
    ssm_width = ssm_d.shape[1]
    q0 = pool_width
    k0 = q0 + qk_width
    v0 = k0 + qk_width
    r0 = v0 + v_width
    glr0 = r0 + v_width
    ssm0 = glr0
    gate0 = ssm0 + ssm_width
    chunks_per_seq = seq // SSM_CHUNK

    bf = lambda a: a.astype(BF16)
    row3 = lambda a: a.reshape(depth, 1, a.shape[-1])
    w_in_t = jnp.swapaxes(w_in, 1, 2)
    ffn1_w = (ffn1_w_gate, ffn1_w_up, ffn1_w_down)
    ffn2_w = (ffn2_w_gate, ffn2_w_up, ffn2_w_down)
    f1 = [bf(w[0]) for w in ffn1_w]
    wbr, wo, wglu, pw = bf(w_branch), bf(w_out), bf(ssm_w_glu), bf(pool_w)
    n1, nm, n2 = row3(ffn1_norm), row3(mix_norm), row3(ffn2_norm)
    pscale, dskip = row3(pool_scale), row3(ssm_d)
    w2 = gla_w_gate2.reshape(depth, GLA_RANK, GLA_HEADS, GLA_DK).transpose(0, 2, 1, 3)
    w2 = bf(jnp.pad(w2, ((0, 0), (0, 0), (0, LANES - GLA_RANK), (0, 0))))
    gbias = gla_gate_bias.reshape(depth, GLA_HEADS, 1, GLA_DK)
    ggain = row3(gla_norm)
    ssm_ops = jax.vmap(functools.partial(_ssm_operators, chunks_per_seq=chunks_per_seq))(
        ssm_a_re, ssm_a_im, ssm_log_dt, ssm_b_re, ssm_b_im, ssm_c_re, ssm_c_im)

    xf = x.reshape(m, d)
    for l in range(depth):
        xf, *side = _ffn(xf, n1, l, *f1, cast=(l,) + ffn2_w, repack=(l, w_in_t, glr0, GLA_RANK))
        f2, (w_main, w_glr) = side[:3], side[3:]
        z, glr, xcat = _inproj(xf, nm, w_main, w_glr, l, ssm0, ssm_width, gate0)
        o_gla = _gla(z, glr, w2, gbias, l, batch, seq, (q0, k0, v0))
        ycat = _ssm_core(xcat, *ssm_ops, l, chunks_per_seq)
        xf = _merge(xf, o_gla, ycat, z, r0, ssm0, gate0, seq, pw, pscale, ggain, dskip, wglu, wbr, wo, l)
        last = l == depth - 1
        xf, *f1 = _ffn(xf, n2, l, *f2, cast=None if last else (l + 1,) + ffn1_w,
                       final_g=final_norm if last else None)
    return xf.reshape(batch, seq, d)
```
